```python
import jax, jax.numpy as jnp
from jax import lax
import numpy as np

D_MODEL = 1024
BATCH = 8
SEQ = 4096
DEPTH = 4

CHUNK = 64
Q_BLOCK = 128
N_MIXERS = 3
EXPAND = 2
D_INNER = EXPAND * D_MODEL
N_HEADS = 16
HEAD_DIM = D_INNER // N_HEADS
MLA_Q_RANK = 256
MLA_KV_RANK = 128
MLA_NOPE_DIM = 128
MLA_ROPE_DIM = 64
MLA_V_DIM = D_INNER // N_HEADS
ROPE_BASE = 10000.0
FORGET_BIAS = 3.0
EPS = 1e-6
NEG = -1e30

kernel_name = 'hybrid_stickbreak_mla_forgetting_trunk'


def rmsnorm(x, g):
    x32 = x.astype(jnp.float32)
    y = x32 * lax.rsqrt(jnp.mean(x32 * x32, axis=-1, keepdims=True) + EPS)
    return (y * g.astype(jnp.float32)).astype(x.dtype)


def split_heads(t, n_heads):
    b, s, _ = t.shape
    return t.reshape(b, s, n_heads, -1).transpose(0, 2, 1, 3)


def merge_heads(t):
    b, h, s, d = t.shape
    return t.transpose(0, 2, 1, 3).reshape(b, s, h * d)


def sweep_query_blocks(block_fn, seq):
    outs = [block_fn(start, start + Q_BLOCK) for start in range(0, seq, Q_BLOCK)]
    return jnp.concatenate(outs, axis=2)


def rope(x, positions):
    r = x.shape[-1]
    inv_freq = ROPE_BASE ** (-jnp.arange(0, r, 2, dtype=jnp.float32) / r)
    ang = positions.astype(jnp.float32)[:, :, None, None] * inv_freq
    cos, sin = jnp.cos(ang), jnp.sin(ang)
    x32 = x.astype(jnp.float32)
    x1, x2 = x32[..., : r // 2], x32[..., r // 2:]
    return jnp.concatenate([x1 * cos - x2 * sin, x1 * sin + x2 * cos], axis=-1).astype(x.dtype)


def stick_breaking_mixer(h, w_in, w_out):
    b, s, _ = h.shape
    q, k, v, gate = jnp.split(h @ w_in, 4, axis=-1)
    q, k, v = split_heads(q, N_HEADS), split_heads(k, N_HEADS), split_heads(v, N_HEADS)
    scale = HEAD_DIM ** -0.5

    def block(start, end):
        z = jnp.einsum('bhqd,bhkd->bhqk', q[:, :, start:end], k[:, :, :end],
                       preferred_element_type=jnp.float32) * scale
        strict = jnp.arange(end)[None, :] < jnp.arange(start, end)[:, None]
        log_skip = jnp.where(strict, jax.nn.log_sigmoid(-z), 0.0)
        later = lax.cumsum(log_skip, axis=3, reverse=True) - log_skip
        w = jnp.where(strict, jnp.exp(jax.nn.log_sigmoid(z) + later), 0.0)
        return jnp.einsum('bhqk,bhkd->bhqd', w.astype(v.dtype), v[:, :, :end])

    o = merge_heads(sweep_query_blocks(block, s))
    return (o * jax.nn.silu(gate)) @ w_out


def mla_mixer(h, positions, w_in, q_norm, w_qb, kv_norm, w_kvb, w_out):
    b, s, _ = h.shape
    i1 = MLA_Q_RANK
    i2 = i1 + MLA_KV_RANK
    i3 = i2 + MLA_ROPE_DIM
    proj = h @ w_in
    q_lat, kv_lat, k_rope, gate = proj[..., :i1], proj[..., i1:i2], proj[..., i2:i3], proj[..., i3:]
    q = (rmsnorm(q_lat, q_norm) @ w_qb).reshape(b, s, N_HEADS, MLA_NOPE_DIM + MLA_ROPE_DIM)
    q_nope = q[..., :MLA_NOPE_DIM].transpose(0, 2, 1, 3)
    q_rope = rope(q[..., MLA_NOPE_DIM:], positions).transpose(0, 2, 1, 3)
    kv = (rmsnorm(kv_lat, kv_norm) @ w_kvb).reshape(b, s, N_HEADS, MLA_NOPE_DIM + MLA_V_DIM)
    k_nope = kv[..., :MLA_NOPE_DIM].transpose(0, 2, 1, 3)
    v = kv[..., MLA_NOPE_DIM:].transpose(0, 2, 1, 3)
    k_rope = rope(k_rope[:, :, None, :], positions)[:, :, 0, :]
    scale = (MLA_NOPE_DIM + MLA_ROPE_DIM) ** -0.5

    def block(start, end):
        z = (jnp.einsum('bhqd,bhkd->bhqk', q_nope[:, :, start:end], k_nope[:, :, :end],
                        preferred_element_type=jnp.float32)
             + jnp.einsum('bhqr,bkr->bhqk', q_rope[:, :, start:end], k_rope[:, :end],
                          preferred_element_type=jnp.float32)) * scale
        allowed = (jnp.arange(end)[None, :] // CHUNK) <= (jnp.arange(start, end)[:, None] // CHUNK)
        p = jax.nn.softmax(jnp.where(allowed, z, NEG), axis=-1)
        return jnp.einsum('bhqk,bhkd->bhqd', p.astype(v.dtype), v[:, :, :end])

    o = merge_heads(sweep_query_blocks(block, s))
    return (o * jax.nn.silu(gate)) @ w_out


def forgetting_mixer(h, w_in, b_f, w_out):
    b, s, _ = h.shape
    proj = h @ w_in
    q = split_heads(proj[..., :D_INNER], N_HEADS)
    k = split_heads(proj[..., D_INNER:2 * D_INNER], N_HEADS)
    v = split_heads(proj[..., 2 * D_INNER:3 * D_INNER], N_HEADS)
    gate = proj[..., 3 * D_INNER:4 * D_INNER]
    f_logit = proj[..., 4 * D_INNER:].astype(jnp.float32) + b_f.astype(jnp.float32)
    cum_log_f = lax.cumsum(jax.nn.log_sigmoid(f_logit), axis=1).transpose(0, 2, 1)
    scale = HEAD_DIM ** -0.5

    def block(start, end):
        z = jnp.einsum('bhqd,bhkd->bhqk', q[:, :, start:end], k[:, :, :end],
                       preferred_element_type=jnp.float32) * scale
        z = z + cum_log_f[:, :, start:end, None] - cum_log_f[:, :, None, :end]
        causal = jnp.arange(end)[None, :] <= jnp.arange(start, end)[:, None]
        p = jax.nn.softmax(jnp.where(causal, z, NEG), axis=-1)
        return jnp.einsum('bhqk,bhkd->bhqd', p.astype(v.dtype), v[:, :, :end])

    o = merge_heads(sweep_query_blocks(block, s))
    return (o * jax.nn.silu(gate)) @ w_out


def _fwd_setup_inputs(seed: int = 0) -> dict:
    key = jax.random.key(seed)
    ks = iter(jax.random.split(key, 32))

    def w(shape):
        return jax.random.normal(next(ks), shape, jnp.float32) * shape[0] ** -0.5

    def gain(n):
        return 1.0 + 0.02 * jax.random.normal(next(ks), (n,), jnp.float32)

    x = jax.random.normal(next(ks), (BATCH, SEQ, D_MODEL), jnp.float32)
    offsets = jax.random.randint(next(ks), (BATCH,), 0, 64) * CHUNK
    positions = (offsets[:, None] + jnp.arange(SEQ)[None, :]).astype(jnp.int32)
    mla_in = MLA_Q_RANK + MLA_KV_RANK + MLA_ROPE_DIM + D_INNER
    return {
        'x': x,
        'positions': positions,
        'ln0': gain(D_MODEL),
        'w_in0': w((D_MODEL, 4 * D_INNER)),
        'w_out0': w((D_INNER, D_MODEL)),
        'ln1': gain(D_MODEL),
        'w_in1': w((D_MODEL, mla_in)),
        'q_norm1': gain(MLA_Q_RANK),
        'w_qb1': w((MLA_Q_RANK, N_HEADS * (MLA_NOPE_DIM + MLA_ROPE_DIM))),
        'kv_norm1': gain(MLA_KV_RANK),
        'w_kvb1': w((MLA_KV_RANK, N_HEADS * (MLA_NOPE_DIM + MLA_V_DIM))),
        'w_out1': w((D_INNER, D_MODEL)),
        'ln2': gain(D_MODEL),
        'w_in2': w((D_MODEL, 4 * D_INNER + N_HEADS)),
        'b_f2': FORGET_BIAS + 0.1 * jax.random.normal(next(ks), (N_HEADS,), jnp.float32),
        'w_out2': w((D_INNER, D_MODEL)),
        'ln3': gain(D_MODEL),
        'w_in3': w((D_MODEL, 4 * D_INNER)),
        'w_out3': w((D_INNER, D_MODEL)),
        'final_norm': gain(D_MODEL),
    }


def _fwd_reference(x, positions, ln0, w_in0, w_out0, ln1, w_in1, q_norm1, w_qb1, kv_norm1, w_kvb1,
              w_out1, ln2, w_in2, b_f2, w_out2, ln3, w_in3, w_out3, final_norm):
    layer_params = [
        (ln0, (w_in0, w_out0)),
        (ln1, (w_in1, q_norm1, w_qb1, kv_norm1, w_kvb1, w_out1)),
        (ln2, (w_in2, b_f2, w_out2)),
        (ln3, (w_in3, w_out3)),
    ]
    for i in range(DEPTH):
        ln, p = layer_params[i]
        h = rmsnorm(x, ln)
        kind = i % N_MIXERS
        if kind == 0:
            y = stick_breaking_mixer(h, *p)
        elif kind == 1:
            y = mla_mixer(h, positions, *p)
        else:
            y = forgetting_mixer(h, *p)
        x = x + y
    return rmsnorm(x, final_norm)


import jax as _jax
import jax.numpy as _jnp

TWIN_FORMAT = 'train_step'
FWD_PARAMS = ['x', 'positions', 'ln0', 'w_in0', 'w_out0', 'ln1', 'w_in1', 'q_norm1', 'w_qb1', 'kv_norm1', 'w_kvb1', 'w_out1', 'ln2', 'w_in2', 'b_f2', 'w_out2', 'ln3', 'w_in3', 'w_out3', 'final_norm']
TWIN_WEIGHTS = ['ln0', 'w_in0', 'w_out0', 'ln1', 'w_in1', 'q_norm1', 'w_qb1', 'kv_norm1', 'w_kvb1', 'w_out1', 'ln2', 'w_in2', 'b_f2', 'w_out2', 'ln3', 'w_in3', 'w_out3', 'final_norm']
TWIN_DIFF_INPUT = 'x'
TWIN_INPUTS = ['x', 'positions', 'ln0', 'w_in0', 'w_out0', 'ln1', 'w_in1', 'q_norm1', 'w_qb1', 'kv_norm1', 'w_kvb1', 'w_out1', 'ln2', 'w_in2', 'b_f2', 'w_out2', 'ln3', 'w_in3', 'w_out3', 'final_norm', 'loss_target', 'm_ln0', 'm_w_in0', 'm_w_out0', 'm_ln1', 'm_w_in1', 'm_q_norm1', 'm_w_qb1', 'm_kv_norm1', 'm_w_kvb1', 'm_w_out1', 'm_ln2', 'm_w_in2', 'm_b_f2', 'm_w_out2', 'm_ln3', 'm_w_in3', 'm_w_out3', 'm_final_norm', 'v_ln0', 'v_w_in0', 'v_w_out0', 'v_ln1', 'v_w_in1', 'v_q_norm1', 'v_w_qb1', 'v_kv_norm1', 'v_w_kvb1', 'v_w_out1', 'v_ln2', 'v_w_in2', 'v_b_f2', 'v_w_out2', 'v_ln3', 'v_w_in3', 'v_w_out3', 'v_final_norm']
TWIN_OUTPUTS = ['loss', 'grad_x', 'grad_ln0', 'grad_w_in0', 'grad_w_out0', 'grad_ln1', 'grad_w_in1', 'grad_q_norm1', 'grad_w_qb1', 'grad_kv_norm1', 'grad_w_kvb1', 'grad_w_out1', 'grad_ln2', 'grad_w_in2', 'grad_b_f2', 'grad_w_out2', 'grad_ln3', 'grad_w_in3', 'grad_w_out3', 'grad_final_norm', 'delta_ln0', 'delta_w_in0', 'delta_w_out0', 'delta_ln1', 'delta_w_in1', 'delta_q_norm1', 'delta_w_qb1', 'delta_kv_norm1', 'delta_w_kvb1', 'delta_w_out1', 'delta_ln2', 'delta_w_in2', 'delta_b_f2', 'delta_w_out2', 'delta_ln3', 'delta_w_in3', 'delta_w_out3', 'delta_final_norm', 'new_m_ln0', 'new_m_w_in0', 'new_m_w_out0', 'new_m_ln1', 'new_m_w_in1', 'new_m_q_norm1', 'new_m_w_qb1', 'new_m_kv_norm1', 'new_m_w_kvb1', 'new_m_w_out1', 'new_m_ln2', 'new_m_w_in2', 'new_m_b_f2', 'new_m_w_out2', 'new_m_ln3', 'new_m_w_in3', 'new_m_w_out3', 'new_m_final_norm', 'new_v_ln0', 'new_v_w_in0', 'new_v_w_out0', 'new_v_ln1', 'new_v_w_in1', 'new_v_q_norm1', 'new_v_w_qb1', 'new_v_kv_norm1', 'new_v_w_kvb1', 'new_v_w_out1', 'new_v_ln2', 'new_v_w_in2', 'new_v_b_f2', 'new_v_w_out2', 'new_v_ln3', 'new_v_w_in3', 'new_v_w_out3', 'new_v_final_norm']
TWIN_LEAF_KINDS = {'loss': 'loss', 'grad_x': 'grad_x', 'grad_ln0': 'grad_w', 'grad_w_in0': 'grad_w', 'grad_w_out0': 'grad_w', 'grad_ln1': 'grad_w', 'grad_w_in1': 'grad_w', 'grad_q_norm1': 'grad_w', 'grad_w_qb1': 'grad_w', 'grad_kv_norm1': 'grad_w', 'grad_w_kvb1': 'grad_w', 'grad_w_out1': 'grad_w', 'grad_ln2': 'grad_w', 'grad_w_in2': 'grad_w', 'grad_b_f2': 'grad_w', 'grad_w_out2': 'grad_w', 'grad_ln3': 'grad_w', 'grad_w_in3': 'grad_w', 'grad_w_out3': 'grad_w', 'grad_final_norm': 'grad_w', 'delta_ln0': 'delta_w', 'delta_w_in0': 'delta_w', 'delta_w_out0': 'delta_w', 'delta_ln1': 'delta_w', 'delta_w_in1': 'delta_w', 'delta_q_norm1': 'delta_w', 'delta_w_qb1': 'delta_w', 'delta_kv_norm1': 'delta_w', 'delta_w_kvb1': 'delta_w', 'delta_w_out1': 'delta_w', 'delta_ln2': 'delta_w', 'delta_w_in2': 'delta_w', 'delta_b_f2': 'delta_w', 'delta_w_out2': 'delta_w', 'delta_ln3': 'delta_w', 'delta_w_in3': 'delta_w', 'delta_w_out3': 'delta_w', 'delta_final_norm': 'delta_w', 'new_m_ln0': 'new_m', 'new_m_w_in0': 'new_m', 'new_m_w_out0': 'new_m', 'new_m_ln1': 'new_m', 'new_m_w_in1': 'new_m', 'new_m_q_norm1': 'new_m', 'new_m_w_qb1': 'new_m', 'new_m_kv_norm1': 'new_m', 'new_m_w_kvb1': 'new_m', 'new_m_w_out1': 'new_m', 'new_m_ln2': 'new_m', 'new_m_w_in2': 'new_m', 'new_m_b_f2': 'new_m', 'new_m_w_out2': 'new_m', 'new_m_ln3': 'new_m', 'new_m_w_in3': 'new_m', 'new_m_w_out3': 'new_m', 'new_m_final_norm': 'new_m', 'new_v_ln0': 'new_v', 'new_v_w_in0': 'new_v', 'new_v_w_out0': 'new_v', 'new_v_ln1': 'new_v', 'new_v_w_in1': 'new_v', 'new_v_q_norm1': 'new_v', 'new_v_w_qb1': 'new_v', 'new_v_kv_norm1': 'new_v', 'new_v_w_kvb1': 'new_v', 'new_v_w_out1': 'new_v', 'new_v_ln2': 'new_v', 'new_v_w_in2': 'new_v', 'new_v_b_f2': 'new_v', 'new_v_w_out2': 'new_v', 'new_v_ln3': 'new_v', 'new_v_w_in3': 'new_v', 'new_v_w_out3': 'new_v', 'new_v_final_norm': 'new_v'}


def _forward(args):
    return _fwd_reference(*[args[k] for k in FWD_PARAMS])


def _output_shape():
    def fwd():
        inp = _fwd_setup_inputs(0)
        return _fwd_reference(*[inp[k] for k in FWD_PARAMS])
    out = _jax.eval_shape(fwd)
    return out.shape, out.dtype

N_MICROBATCH = 1
ADAM_LR = 0.001
ADAM_B1 = 0.9
ADAM_B2 = 0.999
ADAM_EPS = 1e-08
ADAM_WD = 0.01
ADAM_STEP = 10
PER_EXAMPLE_BATCH_AXIS = {'x': 0, 'positions': 0, 'loss_target': 0}
SHARED_INPUTS = []
_WEIGHT_DTYPES = {'ln0': _jnp.float32, 'w_in0': _jnp.float32, 'w_out0': _jnp.float32, 'ln1': _jnp.float32, 'w_in1': _jnp.float32, 'q_norm1': _jnp.float32, 'w_qb1': _jnp.float32, 'kv_norm1': _jnp.float32, 'w_kvb1': _jnp.float32, 'w_out1': _jnp.float32, 'ln2': _jnp.float32, 'w_in2': _jnp.float32, 'b_f2': _jnp.float32, 'w_out2': _jnp.float32, 'ln3': _jnp.float32, 'w_in3': _jnp.float32, 'w_out3': _jnp.float32, 'final_norm': _jnp.float32}
MOMENT_SCALE = {'ln0': 1.124244e-01, 'w_in0': 3.988547e-02, 'w_out0': 7.139040e-02, 'ln1': 3.297059e-02, 'w_in1': 2.147721e-02, 'q_norm1': 3.262996e-02, 'w_qb1': 9.798033e-03, 'kv_norm1': 7.365135e-02, 'w_kvb1': 1.156403e-02, 'w_out1': 1.802098e-02, 'ln2': 6.195427e-02, 'w_in2': 2.173053e-02, 'b_f2': 1.449285e-01, 'w_out2': 3.321212e-02, 'ln3': 9.707437e-02, 'w_in3': 3.322508e-02, 'w_out3': 5.950292e-02, 'final_norm': 3.199327e+01}


def _to_microbatches(a, axis):
    t = _jnp.moveaxis(a, axis, 0)
    t = t.reshape((N_MICROBATCH, t.shape[0] // N_MICROBATCH) + t.shape[1:])
    return _jnp.moveaxis(t, 1, axis + 1)


def setup_inputs(seed: int = 0) -> dict:
    inp = _fwd_setup_inputs(seed)
    key = _jax.random.fold_in(_jax.random.key(seed), 7919)
    shape, _ = _output_shape()
    out = dict(inp)
    out["loss_target"] = _jax.random.normal(_jax.random.fold_in(key, 0), shape, _jnp.float32)
    for i, name in enumerate(TWIN_WEIGHTS):
        w = inp[name].astype(_jnp.float32)
        if MOMENT_SCALE is None:
            s = _jnp.sqrt(_jnp.mean(_jnp.square(w)) + 1e-30)
        else:
            s = MOMENT_SCALE[name]
        km, kv = _jax.random.split(_jax.random.fold_in(key, i + 1))
        out[name] = w
        out["m_" + name] = s * _jax.random.normal(km, w.shape, _jnp.float32)
        out["v_" + name] = (s * s) * _jax.random.uniform(kv, w.shape, _jnp.float32, 0.5, 1.5)
    if N_MICROBATCH > 1:
        for name, axis in PER_EXAMPLE_BATCH_AXIS.items():
            out[name] = _to_microbatches(out[name], axis)
    return {'x': out['x'], 'positions': out['positions'], 'ln0': out['ln0'], 'w_in0': out['w_in0'], 'w_out0': out['w_out0'], 'ln1': out['ln1'], 'w_in1': out['w_in1'], 'q_norm1': out['q_norm1'], 'w_qb1': out['w_qb1'], 'kv_norm1': out['kv_norm1'], 'w_kvb1': out['w_kvb1'], 'w_out1': out['w_out1'], 'ln2': out['ln2'], 'w_in2': out['w_in2'], 'b_f2': out['b_f2'], 'w_out2': out['w_out2'], 'ln3': out['ln3'], 'w_in3': out['w_in3'], 'w_out3': out['w_out3'], 'final_norm': out['final_norm'], 'loss_target': out['loss_target'], 'm_ln0': out['m_ln0'], 'm_w_in0': out['m_w_in0'], 'm_w_out0': out['m_w_out0'], 'm_ln1': out['m_ln1'], 'm_w_in1': out['m_w_in1'], 'm_q_norm1': out['m_q_norm1'], 'm_w_qb1': out['m_w_qb1'], 'm_kv_norm1': out['m_kv_norm1'], 'm_w_kvb1': out['m_w_kvb1'], 'm_w_out1': out['m_w_out1'], 'm_ln2': out['m_ln2'], 'm_w_in2': out['m_w_in2'], 'm_b_f2': out['m_b_f2'], 'm_w_out2': out['m_w_out2'], 'm_ln3': out['m_ln3'], 'm_w_in3': out['m_w_in3'], 'm_w_out3': out['m_w_out3'], 'm_final_norm': out['m_final_norm'], 'v_ln0': out['v_ln0'], 'v_w_in0': out['v_w_in0'], 'v_w_out0': out['v_w_out0'], 'v_ln1': out['v_ln1'], 'v_w_in1': out['v_w_in1'], 'v_q_norm1': out['v_q_norm1'], 'v_w_qb1': out['v_w_qb1'], 'v_kv_norm1': out['v_kv_norm1'], 'v_w_kvb1': out['v_w_kvb1'], 'v_w_out1': out['v_w_out1'], 'v_ln2': out['v_ln2'], 'v_w_in2': out['v_w_in2'], 'v_b_f2': out['v_b_f2'], 'v_w_out2': out['v_w_out2'], 'v_ln3': out['v_ln3'], 'v_w_in3': out['v_w_in3'], 'v_w_out3': out['v_w_out3'], 'v_final_norm': out['v_final_norm']}


def _loss(weights, diff, rest, loss_target):
    with _jax.named_scope("forward"):
        args = {**rest, TWIN_DIFF_INPUT: diff, **{k: w.astype(_WEIGHT_DTYPES[k]) for k, w in weights.items()}}
        y = _forward(args)
    with _jax.named_scope("loss_head"):
        err = _jnp.square(y.astype(_jnp.float32) - loss_target)
        return 0.5 * _jnp.sum(_jnp.mean(err, axis=-1)) if err.ndim else 0.5 * err


def _adamw(w, g, m, v):
    m = ADAM_B1 * m + (1.0 - ADAM_B1) * g
    v = ADAM_B2 * v + (1.0 - ADAM_B2) * _jnp.square(g)
    m_hat = m / (1.0 - ADAM_B1 ** ADAM_STEP)
    v_hat = v / (1.0 - ADAM_B2 ** ADAM_STEP)
    delta = -ADAM_LR * (m_hat / (_jnp.sqrt(v_hat) + ADAM_EPS) + ADAM_WD * w)
    return delta, m, v


def reference(x, positions, ln0, w_in0, w_out0, ln1, w_in1, q_norm1, w_qb1, kv_norm1, w_kvb1, w_out1, ln2, w_in2, b_f2, w_out2, ln3, w_in3, w_out3, final_norm, loss_target, m_ln0, m_w_in0, m_w_out0, m_ln1, m_w_in1, m_q_norm1, m_w_qb1, m_kv_norm1, m_w_kvb1, m_w_out1, m_ln2, m_w_in2, m_b_f2, m_w_out2, m_ln3, m_w_in3, m_w_out3, m_final_norm, v_ln0, v_w_in0, v_w_out0, v_ln1, v_w_in1, v_q_norm1, v_w_qb1, v_kv_norm1, v_w_kvb1, v_w_out1, v_ln2, v_w_in2, v_b_f2, v_w_out2, v_ln3, v_w_in3, v_w_out3, v_final_norm):
    given = dict(x=x, positions=positions, ln0=ln0, w_in0=w_in0, w_out0=w_out0, ln1=ln1, w_in1=w_in1, q_norm1=q_norm1, w_qb1=w_qb1, kv_norm1=kv_norm1, w_kvb1=w_kvb1, w_out1=w_out1, ln2=ln2, w_in2=w_in2, b_f2=b_f2, w_out2=w_out2, ln3=ln3, w_in3=w_in3, w_out3=w_out3, final_norm=final_norm, loss_target=loss_target, m_ln0=m_ln0, m_w_in0=m_w_in0, m_w_out0=m_w_out0, m_ln1=m_ln1, m_w_in1=m_w_in1, m_q_norm1=m_q_norm1, m_w_qb1=m_w_qb1, m_kv_norm1=m_kv_norm1, m_w_kvb1=m_w_kvb1, m_w_out1=m_w_out1, m_ln2=m_ln2, m_w_in2=m_w_in2, m_b_f2=m_b_f2, m_w_out2=m_w_out2, m_ln3=m_ln3, m_w_in3=m_w_in3, m_w_out3=m_w_out3, m_final_norm=m_final_norm, v_ln0=v_ln0, v_w_in0=v_w_in0, v_w_out0=v_w_out0, v_ln1=v_ln1, v_w_in1=v_w_in1, v_q_norm1=v_q_norm1, v_w_qb1=v_w_qb1, v_kv_norm1=v_kv_norm1, v_w_kvb1=v_w_kvb1, v_w_out1=v_w_out1, v_ln2=v_ln2, v_w_in2=v_w_in2, v_b_f2=v_b_f2, v_w_out2=v_w_out2, v_ln3=v_ln3, v_w_in3=v_w_in3, v_w_out3=v_w_out3, v_final_norm=v_final_norm)
    weights = {n: given[n] for n in TWIN_WEIGHTS}
    shared = {n: given[n] for n in SHARED_INPUTS}
    per_example = {n: given[n] for n in ['x', 'positions']}
    grad_fn = _jax.value_and_grad(_loss, argnums=(0, 1))

    def one_microbatch(ex, loss_target):
        ex = dict(ex)
        diff = ex.pop(TWIN_DIFF_INPUT)
        return grad_fn(weights, diff, {**shared, **ex}, loss_target)

    if N_MICROBATCH == 1:
        loss, (grad_w, grad_x) = one_microbatch(per_example, given["loss_target"])
    else:
        def body(carry, xs):
            loss_sum, grad_sum = carry
            l_k, (gw_k, gx_k) = one_microbatch(xs[0], xs[1])
            with _jax.named_scope("update"):
                return (loss_sum + l_k, _jax.tree.map(_jnp.add, grad_sum, gw_k)), gx_k

        init = (_jnp.zeros((), _jnp.float32), _jax.tree.map(_jnp.zeros_like, weights))
        (loss, grad_w), grad_x = _jax.lax.scan(body, init, (per_example, given["loss_target"]))
    with _jax.named_scope("update"):
        delta_w, new_m, new_v = {}, {}, {}
        for n in TWIN_WEIGHTS:
            delta_w[n], new_m[n], new_v[n] = _adamw(weights[n], grad_w[n], given["m_" + n], given["v_" + n])
    return (loss, grad_x, *[grad_w[n] for n in TWIN_WEIGHTS], *[delta_w[n] for n in TWIN_WEIGHTS],
            *[new_m[n] for n in TWIN_WEIGHTS], *[new_v[n] for n in TWIN_WEIGHTS])
```

```python
import functools

import jax
import jax.numpy as jnp
from jax import lax
from jax.experimental import pallas as pl
from jax.experimental.pallas import tpu as pltpu

F32 = jnp.float32
BF16 = jnp.bfloat16
MESH = pl.DeviceIdType.MESH

N_DEV = 8
N_HEADS = 16
HEAD_DIM = 128
D_INNER = N_HEADS * HEAD_DIM
MLA_Q_RANK = 256
MLA_KV_RANK = 128
MLA_NOPE = 128
MLA_ROPE = 64
MLA_QK_PAD = 256
MLA_CHUNK = 64
ROPE_BASE = 10000.0
EPS = 1e-6
NEG = -1e30
SB_CUT = 104.0

ADAM_LR = 0.001
ADAM_B1 = 0.9
ADAM_B2 = 0.999
ADAM_EPS = 1e-08
ADAM_WD = 0.01
ADAM_STEP = 10

PACK_COLS = 1024
VMEM_LIMIT = 56 * 1024 * 1024

BIG = ["w_in0", "w_out0", "w_in1", "w_qb1", "w_kvb1", "w_out1", "w_in2", "w_out2", "w_in3", "w_out3"]
COL_SHARDED = {"w_in0", "w_in1", "w_qb1", "w_kvb1", "w_in2", "w_in3"}
SMALL = ["ln0", "ln1", "ln2", "ln3", "final_norm", "q_norm1", "kv_norm1", "b_f2"]
ALL_W = ["ln0", "w_in0", "w_out0", "ln1", "w_in1", "q_norm1", "w_qb1", "kv_norm1", "w_kvb1", "w_out1",
         "ln2", "w_in2", "b_f2", "w_out2", "ln3", "w_in3", "w_out3", "final_norm"]


def _cparams(sem=None):
    return pltpu.CompilerParams(dimension_semantics=sem, vmem_limit_bytes=VMEM_LIMIT)


def _tile(dim, cap, align):
    if dim <= cap:
        return dim
    t = (cap // align) * align
    while t >= align:
        if dim % t == 0:
            return t
        t -= align
    return dim


def _dot(a, b, dims):
    return lax.dot_general(a, b, (dims, ((), ())), preferred_element_type=F32)


def _dot_nn(a, b):
    return _dot(a, b, ((1,), (0,)))


def _dot_nt(a, b):
    return _dot(a, b, ((1,), (1,)))


def _dot_tn(a, b):
    return _dot(a, b, ((0,), (0,)))


def _matmul(a, b, mode, n_cols=None, name="mm"):
    if mode == "nn":
        m, r = a.shape
        n = n_cols or b.shape[1]
        tm, tn, tr = _tile(m, 512, 8), _tile(n, 1024, 128), _tile(r, 1024, 128)
        a_spec = pl.BlockSpec((tm, tr), lambda i, j, k: (i, k))
        b_spec = pl.BlockSpec((tr, tn), lambda i, j, k: (k, j))
        dims = ((1,), (0,))
    elif mode == "nt":
        m = a.shape[0]
        n = b.shape[0]
        r = n_cols or a.shape[1]
        tm, tn, tr = _tile(m, 512, 8), _tile(n, 1024, 128), _tile(r, 1024, 128)
        a_spec = pl.BlockSpec((tm, tr), lambda i, j, k: (i, k))
        b_spec = pl.BlockSpec((tn, tr), lambda i, j, k: (j, k))
        dims = ((1,), (1,))
    else:
        r, m = a.shape
        n = b.shape[1]
        tm, tn, tr = _tile(m, 512, 128), _tile(n, 1024, 128), _tile(r, 512, 8)
        a_spec = pl.BlockSpec((tr, tm), lambda i, j, k: (k, i))
        b_spec = pl.BlockSpec((tr, tn), lambda i, j, k: (k, j))
        dims = ((0,), (0,))
    nr = r // tr

    def body(a_ref, b_ref, o_ref, acc_ref):
        k = pl.program_id(2)

        @pl.when(k == 0)
        def _():
            acc_ref[...] = jnp.zeros_like(acc_ref)

        acc_ref[...] += _dot(a_ref[...].astype(BF16), b_ref[...].astype(BF16), dims)

        @pl.when(k == nr - 1)
        def _():
            o_ref[...] = acc_ref[...]

    return pl.pallas_call(
        body,
        name=name,
        grid=(m // tm, n // tn, nr),
        in_specs=[a_spec, b_spec],
        out_specs=pl.BlockSpec((tm, tn), lambda i, j, k: (i, j)),
        out_shape=jax.ShapeDtypeStruct((m, n), F32),
        scratch_shapes=[pltpu.VMEM((tm, tn), F32)],
        compiler_params=_cparams(("parallel", "parallel", "arbitrary")),
    )(a, b)


def mm(a, w, carrier, n_cols=None, name="mm"):
    @jax.custom_vjp
    def f(a, w, carrier):
        return _matmul(a, w, "nn", n_cols, name + "_fwd")

    def fwd(a, w, carrier):
        return _matmul(a, w, "nn", n_cols, name + "_fwd"), (a, w)

    def bwd(res, g):
        a, w = res
        da = _matmul(g, w, "nt", n_cols, name + "_dx")
        dw = _matmul(a, g, "tn", None, name + "_dw")
        return da, jnp.zeros_like(w), dw

    f.defvjp(fwd, bwd)
    return f(a, w, carrier)


def _rms_fwd(x, g, name):
    s, d = x.shape
    tm = _tile(s, 512, 8)

    def body(x_ref, g_ref, y_ref):
        x = x_ref[...]
        r = lax.rsqrt(jnp.mean(x * x, axis=-1, keepdims=True) + EPS)
        y_ref[...] = x * r * g_ref[...]

    return pl.pallas_call(
        body, name=name, grid=(s // tm,),
        in_specs=[pl.BlockSpec((tm, d), lambda i: (i, 0)), pl.BlockSpec((1, d), lambda i: (0, 0))],
        out_specs=pl.BlockSpec((tm, d), lambda i: (i, 0)),
        out_shape=jax.ShapeDtypeStruct((s, d), F32),
        compiler_params=_cparams(("parallel",)),
    )(x, g)


def _rms_bwd(x, g, dy, name):
    s, d = x.shape
    tm = _tile(s, 512, 8)

    def body(x_ref, g_ref, dy_ref, dx_ref, dg_ref):
        @pl.when(pl.program_id(0) == 0)
        def _():
            dg_ref[...] = jnp.zeros_like(dg_ref)

        x = x_ref[...]
        dy = dy_ref[...]
        r = lax.rsqrt(jnp.mean(x * x, axis=-1, keepdims=True) + EPS)
        xh = x * r
        dg_ref[...] += jnp.sum(dy * xh, axis=0, keepdims=True)
        dxh = dy * g_ref[...]
        dx_ref[...] = r * (dxh - xh * jnp.mean(dxh * xh, axis=-1, keepdims=True))

    return pl.pallas_call(
        body, name=name, grid=(s // tm,),
        in_specs=[pl.BlockSpec((tm, d), lambda i: (i, 0)), pl.BlockSpec((1, d), lambda i: (0, 0)),
                  pl.BlockSpec((tm, d), lambda i: (i, 0))],
        out_specs=[pl.BlockSpec((tm, d), lambda i: (i, 0)), pl.BlockSpec((1, d), lambda i: (0, 0))],
        out_shape=[jax.ShapeDtypeStruct((s, d), F32), jax.ShapeDtypeStruct((1, d), F32)],
        compiler_params=_cparams(("arbitrary",)),
    )(x, g, dy)


def rmsnorm(x, g, name="rms"):
    @jax.custom_vjp
    def f(x, g):
        return _rms_fwd(x, g.reshape(1, -1), name + "_fwd")

    def fwd(x, g):
        return _rms_fwd(x, g.reshape(1, -1), name + "_fwd"), (x, g)

    def bwd(res, dy):
        x, g = res
        dx, dg = _rms_bwd(x, g.reshape(1, -1), dy, name + "_bwd")
        return dx, dg.reshape(-1)

    f.defvjp(fwd, bwd)
    return f(x, g)


def _gate_fwd(o, garr, gblk, name):
    s = o.shape[0]
    tm = 256

    def body(o_ref, g_ref, y_ref):
        g = g_ref[...]
        y_ref[...] = o_ref[...] * (g / (1.0 + jnp.exp(-g)))

    return pl.pallas_call(
        body, name=name, grid=(s // tm,),
        in_specs=[pl.BlockSpec((tm, D_INNER), lambda i: (i, 0)), pl.BlockSpec((tm, D_INNER), lambda i: (i, gblk))],
        out_specs=pl.BlockSpec((tm, D_INNER), lambda i: (i, 0)),
        out_shape=jax.ShapeDtypeStruct((s, D_INNER), F32),
        compiler_params=_cparams(("parallel",)),
    )(o, garr)


def _gate_bwd(dy, o, garr, gblk, name):
    s = o.shape[0]
    tm = 256

    def body(dy_ref, o_ref, g_ref, do_ref, dg_ref, dl_ref):
        g = g_ref[...]
        o = o_ref[...]
        dy = dy_ref[...]
        sg = 1.0 / (1.0 + jnp.exp(-g))
        do = dy * (g * sg)
        do_ref[...] = do
        dg_ref[...] = dy * o * (sg * (1.0 + g * (1.0 - sg)))
        prod = do * o
        for h in range(N_HEADS):
            dl_ref[h] = jnp.sum(prod[:, h * HEAD_DIM:(h + 1) * HEAD_DIM], axis=1, keepdims=True)

    row = pl.BlockSpec((tm, D_INNER), lambda i: (i, 0))
    return pl.pallas_call(
        body, name=name, grid=(s // tm,),
        in_specs=[row, row, pl.BlockSpec((tm, D_INNER), lambda i: (i, gblk))],
        out_specs=[row, row, pl.BlockSpec((N_HEADS, tm, 1), lambda i: (0, i, 0))],
        out_shape=[jax.ShapeDtypeStruct((s, D_INNER), F32), jax.ShapeDtypeStruct((s, D_INNER), F32),
                   jax.ShapeDtypeStruct((N_HEADS, s, 1), F32)],
        compiler_params=_cparams(("parallel",)),
    )(dy, o, garr)


SB_BLK = 128


def _softplus(z):
    return jnp.maximum(z, 0.0) + jnp.log(1.0 + jnp.exp(-jnp.abs(z)))


def _tri_sum(x, tri):
    hi = x.astype(BF16)
    lo = (x - hi.astype(F32)).astype(BF16)
    return _dot_nn(hi, tri) + _dot_nn(lo, tri)


def _sb_block(q, kb, i, j, a_run, row, col, tri_suffix, scale):
    z = _dot_nt(q, kb) * scale
    mask = (col + j * SB_BLK) < (row + i * SB_BLK)
    sp = _softplus(z)
    ls = jnp.where(mask, -sp, 0.0)
    suffix = _tri_sum(ls, tri_suffix)
    w = jnp.where(mask, jnp.exp(z + suffix + a_run), 0.0)
    return z, mask, sp, ls, w


def _sb_fwd(proj, name):
    s = proj.shape[0]
    b = SB_BLK
    nq = s // b
    scale = HEAD_DIM ** -0.5

    def body(q_ref, k_ref, v_ref, o_ref):
        i = pl.program_id(1)
        q = q_ref[...].astype(BF16)
        row = lax.broadcasted_iota(jnp.int32, (b, b), 0)
        col = lax.broadcasted_iota(jnp.int32, (b, b), 1)
        tri_suffix = (row >= col).astype(BF16)

        def cond(c):
            j, a_run, _ = c
            return jnp.logical_and(j >= 0, jnp.max(a_run) > -SB_CUT)

        def step(c):
            j, a_run, acc = c
            off = pl.multiple_of(j * b, b)
            kb = k_ref[pl.ds(off, b), :].astype(BF16)
            vb = v_ref[pl.ds(off, b), :].astype(BF16)
            _, _, _, ls, w = _sb_block(q, kb, i, j, a_run, row, col, tri_suffix, scale)
            acc = acc + _dot_nn(w.astype(BF16), vb)
            return j - 1, a_run + jnp.sum(ls, axis=1, keepdims=True), acc

        _, _, acc = lax.while_loop(cond, step, (i, jnp.zeros((b, 1), F32), jnp.zeros((b, HEAD_DIM), F32)))
        o_ref[...] = acc

    return pl.pallas_call(
        body, name=name, grid=(N_HEADS, nq),
        in_specs=[pl.BlockSpec((b, HEAD_DIM), lambda h, i: (i, h)),
                  pl.BlockSpec((s, HEAD_DIM), lambda h, i: (0, N_HEADS + h)),
                  pl.BlockSpec((s, HEAD_DIM), lambda h, i: (0, 2 * N_HEADS + h))],
        out_specs=pl.BlockSpec((b, HEAD_DIM), lambda h, i: (i, h)),
        out_shape=jax.ShapeDtypeStruct((s, D_INNER), F32),
        compiler_params=_cparams(("parallel", "arbitrary")),
    )(proj, proj, proj)


def _sb_bwd(proj, do, name):
    s = proj.shape[0]
    b = SB_BLK
    nq = s // b
    scale = HEAD_DIM ** -0.5

    def body(q_ref, k_ref, v_ref, do_ref, dq_ref, dk_ref, dv_ref, g_buf, sig_buf):
        i = pl.program_id(1)

        @pl.when(i == 0)
        def _():
            dk_ref[...] = jnp.zeros_like(dk_ref)
            dv_ref[...] = jnp.zeros_like(dv_ref)

        q = q_ref[...].astype(BF16)
        dob = do_ref[...].astype(BF16)
        row = lax.broadcasted_iota(jnp.int32, (b, b), 0)
        col = lax.broadcasted_iota(jnp.int32, (b, b), 1)
        tri_suffix = (row >= col).astype(BF16)
        tri_prefix = (row <= col).astype(BF16)

        def cond(c):
            j, a_run = c
            return jnp.logical_and(j >= 0, jnp.max(a_run) > -SB_CUT)

        def sweep(c):
            j, a_run = c
            off = pl.multiple_of(j * b, b)
            kb = k_ref[pl.ds(off, b), :].astype(BF16)
            vb = v_ref[pl.ds(off, b), :].astype(BF16)
            z, _, sp, ls, w = _sb_block(q, kb, i, j, a_run, row, col, tri_suffix, scale)
            g_buf[j] = w * _dot_nt(dob, vb)
            sig_buf[j] = jnp.exp(z - sp)
            dv_ref[pl.ds(off, b), :] += _dot_tn(w.astype(BF16), dob)
            return j - 1, a_run + jnp.sum(ls, axis=1, keepdims=True)

        j_end, _ = lax.while_loop(cond, sweep, (i, jnp.zeros((b, 1), F32)))

        def back(j, c):
            g_run, dq = c
            off = pl.multiple_of(j * b, b)
            kb = k_ref[pl.ds(off, b), :].astype(BF16)
            g = g_buf[j]
            g_incl = g_run + _tri_sum(g, tri_prefix)
            mask = (col + j * b) < (row + i * b)
            dz = jnp.where(mask, (g - sig_buf[j] * g_incl) * scale, 0.0).astype(BF16)
            dk_ref[pl.ds(off, b), :] += _dot_tn(dz, q)
            return g_run + jnp.sum(g, axis=1, keepdims=True), dq + _dot_nn(dz, kb)

        _, dq = lax.fori_loop(j_end + 1, i + 1, back, (jnp.zeros((b, 1), F32), jnp.zeros((b, HEAD_DIM), F32)))
        dq_ref[...] = dq

    blk = pl.BlockSpec((b, HEAD_DIM), lambda h, i: (i, h))
    head = pl.BlockSpec((s, HEAD_DIM), lambda h, i: (0, h))
    return pl.pallas_call(
        body, name=name, grid=(N_HEADS, nq),
        in_specs=[blk,
                  pl.BlockSpec((s, HEAD_DIM), lambda h, i: (0, N_HEADS + h)),
                  pl.BlockSpec((s, HEAD_DIM), lambda h, i: (0, 2 * N_HEADS + h)),
                  blk],
        out_specs=[blk, head, head],
        out_shape=[jax.ShapeDtypeStruct((s, D_INNER), F32)] * 3,
        scratch_shapes=[pltpu.VMEM((nq, b, b), F32), pltpu.VMEM((nq, b, b), F32)],
        compiler_params=_cparams(("arbitrary", "arbitrary")),
    )(proj, proj, proj, do)


def sb_core(proj, name):
    def run(proj):
        o = _sb_fwd(proj, name + "_fwd")
        return _gate_fwd(o, proj, 3, name + "_gate"), o

    @jax.custom_vjp
    def f(proj):
        return run(proj)[0]

    def fwd(proj):
        y, o = run(proj)
        return y, (proj, o)

    def bwd(res, dy):
        proj, o = res
        do, dgate, _ = _gate_bwd(dy, o, proj, 3, name + "_gate_bwd")
        dq, dk, dv = _sb_bwd(proj, do, name + "_bwd")
        return (jnp.concatenate([dq, dk, dv, dgate], axis=1),)

    f.defvjp(fwd, bwd)
    return f(proj)


SM_BLK = 256


def _sm_mask(i, j, row, col, chunk_shift):
    return ((col + j * SM_BLK) >> chunk_shift) <= ((row + i * SM_BLK) >> chunk_shift)


def _sm_fwd(qa, ka, va, ccol, crow, dqk, qo, ko, vo, chunk_shift, scale, name):
    s = qa.shape[0]
    b = SM_BLK
    nq = s // b
    has_bias = ccol is not None

    def body(*refs):
        if has_bias:
            q_ref, k_ref, v_ref, cc_ref, cr_ref, o_ref, lse_ref, m_s, l_s, acc_s = refs
        else:
            q_ref, k_ref, v_ref, o_ref, lse_ref, m_s, l_s, acc_s = refs
        i = pl.program_id(1)
        q = q_ref[...].astype(BF16)
        row = lax.broadcasted_iota(jnp.int32, (b, b), 0)
        col = lax.broadcasted_iota(jnp.int32, (b, b), 1)
        m_s[...] = jnp.full_like(m_s, NEG)
        l_s[...] = jnp.zeros_like(l_s)
        acc_s[...] = jnp.zeros_like(acc_s)

        def step(j, carry):
            off = pl.multiple_of(j * b, b)
            kb = k_ref[pl.ds(off, b), :].astype(BF16)
            vb = v_ref[pl.ds(off, b), :].astype(BF16)
            z = _dot_nt(q, kb) * scale
            if has_bias:
                z = z + cc_ref[...] - cr_ref[j]
            z = jnp.where(_sm_mask(i, j, row, col, chunk_shift), z, NEG)
            m_old = m_s[...]
            m_new = jnp.maximum(m_old, jnp.max(z, axis=1, keepdims=True))
            alpha = jnp.exp(m_old - m_new)
            p = jnp.exp(z - m_new)
            l_s[...] = alpha * l_s[...] + jnp.sum(p, axis=1, keepdims=True)
            acc_s[...] = alpha * acc_s[...] + _dot_nn(p.astype(BF16), vb)
            m_s[...] = m_new
            return carry

        lax.fori_loop(0, i + 1, step, 0)
        o_ref[...] = acc_s[...] / l_s[...]
        lse_ref[...] = m_s[...] + jnp.log(l_s[...])

    in_specs = [pl.BlockSpec((b, dqk), lambda h, i: (i, qo + h)),
                pl.BlockSpec((s, dqk), lambda h, i: (0, ko + h)),
                pl.BlockSpec((s, HEAD_DIM), lambda h, i: (0, vo + h))]
    args = [qa, ka, va]
    if has_bias:
        in_specs += [pl.BlockSpec((None, b, 1), lambda h, i: (h, i, 0)),
                     pl.BlockSpec((None, nq, 1, b), lambda h, i: (h, 0, 0, 0))]
        args += [ccol, crow]
    return pl.pallas_call(
        body, name=name, grid=(N_HEADS, nq),
        in_specs=in_specs,
        out_specs=[pl.BlockSpec((b, HEAD_DIM), lambda h, i: (i, h)),
                   pl.BlockSpec((None, b, 1), lambda h, i: (h, i, 0))],
        out_shape=[jax.ShapeDtypeStruct((s, D_INNER), F32), jax.ShapeDtypeStruct((N_HEADS, s, 1), F32)],
        scratch_shapes=[pltpu.VMEM((b, 1), F32), pltpu.VMEM((b, 1), F32), pltpu.VMEM((b, HEAD_DIM), F32)],
        compiler_params=_cparams(("parallel", "arbitrary")),
    )(*args)


def _sm_bwd(qa, ka, va, do, lse, delta, ccol, crow, dqk, qo, ko, vo, chunk_shift, scale, name):
    s = qa.shape[0]
    b = SM_BLK
    nq = s // b
    has_bias = ccol is not None

    def body(*refs):
        if has_bias:
            (q_ref, k_ref, v_ref, do_ref, lse_ref, dl_ref, cc_ref, cr_ref,
             dq_ref, dk_ref, dv_ref, dc_ref, dr_ref, dk_s, dv_s, dc_s) = refs
        else:
            (q_ref, k_ref, v_ref, do_ref, lse_ref, dl_ref,
             dq_ref, dk_ref, dv_ref, dk_s, dv_s) = refs
        j = pl.program_id(1)

        @pl.when(j == 0)
        def _():
            dq_ref[...] = jnp.zeros_like(dq_ref)
            if has_bias:
                dr_ref[...] = jnp.zeros_like(dr_ref)

        kb = k_ref[...].astype(BF16)
        vb = v_ref[...].astype(BF16)
        row = lax.broadcasted_iota(jnp.int32, (b, b), 0)
        col = lax.broadcasted_iota(jnp.int32, (b, b), 1)
        dk_s[...] = jnp.zeros_like(dk_s)
        dv_s[...] = jnp.zeros_like(dv_s)
        if has_bias:
            dc_s[...] = jnp.zeros_like(dc_s)

        def step(i, carry):
            off = pl.multiple_of(i * b, b)
            qb = q_ref[pl.ds(off, b), :].astype(BF16)
            dob = do_ref[pl.ds(off, b), :].astype(BF16)
            z = _dot_nt(qb, kb) * scale
            if has_bias:
                z = z + cc_ref[pl.ds(off, b), :] - cr_ref[...]
            p = jnp.where(_sm_mask(i, j, row, col, chunk_shift), jnp.exp(z - lse_ref[pl.ds(off, b), :]), 0.0)
            dv_s[...] += _dot_tn(p.astype(BF16), dob)
            dz = p * (_dot_nt(dob, vb) - dl_ref[pl.ds(off, b), :])
            if has_bias:
                dc_s[...] += jnp.sum(dz, axis=0, keepdims=True)
                dr_ref[pl.ds(off, b), :] += jnp.sum(dz, axis=1, keepdims=True)
            dzs = (dz * scale).astype(BF16)
            dk_s[...] += _dot_tn(dzs, qb)
            dq_ref[pl.ds(off, b), :] += _dot_nn(dzs, kb)
            return carry

        lax.fori_loop(j, nq, step, 0)
        dk_ref[...] = dk_s[...]
        dv_ref[...] = dv_s[...]
        if has_bias:
            dc_ref[...] = dc_s[...]

    vec = pl.BlockSpec((None, s, 1), lambda h, j: (h, 0, 0))
    in_specs = [pl.BlockSpec((s, dqk), lambda h, j: (0, qo + h)),
                pl.BlockSpec((b, dqk), lambda h, j: (j, ko + h)),
                pl.BlockSpec((b, HEAD_DIM), lambda h, j: (j, vo + h)),
                pl.BlockSpec((s, HEAD_DIM), lambda h, j: (0, h)),
                vec, vec]
    args = [qa, ka, va, do, lse, delta]
    out_specs = [pl.BlockSpec((s, dqk), lambda h, j: (0, h)),
                 pl.BlockSpec((b, dqk), lambda h, j: (j, h)),
                 pl.BlockSpec((b, HEAD_DIM), lambda h, j: (j, h))]
    out_shape = [jax.ShapeDtypeStruct((s, N_HEADS * dqk), F32), jax.ShapeDtypeStruct((s, N_HEADS * dqk), F32),
                 jax.ShapeDtypeStruct((s, D_INNER), F32)]
    scratch = [pltpu.VMEM((b, dqk), F32), pltpu.VMEM((b, HEAD_DIM), F32)]
    if has_bias:
        in_specs += [vec, pl.BlockSpec((None, None, 1, b), lambda h, j: (h, j, 0, 0))]
        args += [ccol, crow]
        out_specs += [pl.BlockSpec((None, None, 1, b), lambda h, j: (h, j, 0, 0)), vec]
        out_shape += [jax.ShapeDtypeStruct((N_HEADS, nq, 1, b), F32), jax.ShapeDtypeStruct((N_HEADS, s, 1), F32)]
        scratch += [pltpu.VMEM((1, b), F32)]
    return pl.pallas_call(
        body, name=name, grid=(N_HEADS, nq),
        in_specs=in_specs, out_specs=out_specs, out_shape=out_shape, scratch_shapes=scratch,
        compiler_params=_cparams(("arbitrary", "arbitrary")),
    )(*args)


def fox_core(proj, c, name):
    s = proj.shape[0]
    nq = s // SM_BLK
    scale = HEAD_DIM ** -0.5
    cfg = dict(dqk=HEAD_DIM, qo=0, ko=N_HEADS, vo=2 * N_HEADS, chunk_shift=0, scale=scale)

    def layouts(c):
        ct = c.T
        return ct.reshape(N_HEADS, s, 1), ct.reshape(N_HEADS, nq, 1, SM_BLK)

    def run(proj, c):
        ccol, crow = layouts(c)
        o, lse = _sm_fwd(proj, proj, proj, ccol, crow, name=name + "_fwd", **cfg)
        return _gate_fwd(o, proj, 3, name + "_gate"), o, lse

    @jax.custom_vjp
    def f(proj, c):
        return run(proj, c)[0]

    def fwd(proj, c):
        y, o, lse = run(proj, c)
        return y, (proj, c, o, lse)

    def bwd(res, dy):
        proj, c, o, lse = res
        ccol, crow = layouts(c)
        do, dgate, delta = _gate_bwd(dy, o, proj, 3, name + "_gate_bwd")
        dq, dk, dv, colsum, rowsum = _sm_bwd(proj, proj, proj, do, lse, delta, ccol, crow, name=name + "_bwd", **cfg)
        dc = (rowsum.reshape(N_HEADS, s) - colsum.reshape(N_HEADS, s)).T
        return jnp.concatenate([dq, dk, dv, dgate], axis=1), dc

    f.defvjp(fwd, bwd)
    return f(proj, c)


def mla_core(qc, kc, v, gate, name):
    scale = (MLA_NOPE + MLA_ROPE) ** -0.5
    cfg = dict(dqk=MLA_QK_PAD, qo=0, ko=0, vo=0, chunk_shift=MLA_CHUNK.bit_length() - 1, scale=scale)

    def run(qc, kc, v, gate):
        o, lse = _sm_fwd(qc, kc, v, None, None, name=name + "_fwd", **cfg)
        return _gate_fwd(o, gate, 0, name + "_gate"), o, lse

    @jax.custom_vjp
    def f(qc, kc, v, gate):
        return run(qc, kc, v, gate)[0]

    def fwd(qc, kc, v, gate):
        y, o, lse = run(qc, kc, v, gate)
        return y, (qc, kc, v, gate, o, lse)

    def bwd(res, dy):
        qc, kc, v, gate, o, lse = res
        do, dgate, delta = _gate_bwd(dy, o, gate, 0, name + "_gate_bwd")
        dq, dk, dv = _sm_bwd(qc, kc, v, do, lse, delta, None, None, name=name + "_bwd", **cfg)
        return dq, dk, dv, dgate

    f.defvjp(fwd, bwd)
    return f(qc, kc, v, gate)


def _sq_loss_call(y, t, name):
    s, d = y.shape
    tm = _tile(s, 512, 8)

    def body(y_ref, t_ref, l_ref, e_ref):
        @pl.when(pl.program_id(0) == 0)
        def _():
            l_ref[...] = jnp.zeros_like(l_ref)

        e = y_ref[...] - t_ref[...]
        e_ref[...] = e * (1.0 / d)
        part = jnp.sum(jnp.sum(e * e, axis=1, keepdims=True), axis=0, keepdims=True)
        l_ref[...] += jnp.broadcast_to(part * (0.5 / d), l_ref.shape)

    row = pl.BlockSpec((tm, d), lambda i: (i, 0))
    return pl.pallas_call(
        body, name=name, grid=(s // tm,),
        in_specs=[row, row],
        out_specs=[pl.BlockSpec((8, 128), lambda i: (0, 0)), row],
        out_shape=[jax.ShapeDtypeStruct((8, 128), F32), jax.ShapeDtypeStruct((s, d), F32)],
        compiler_params=_cparams(("arbitrary",)),
    )(y, t)


@jax.custom_vjp
def sq_loss(y, t):
    return _sq_loss_call(y, t, "loss_fwd")[0][0, 0]


def _sq_loss_fwd(y, t):
    l, e = _sq_loss_call(y, t, "loss_fwd")
    return l[0, 0], e


def _sq_loss_bwd(e, g):
    return g * e, jnp.zeros_like(e)


sq_loss.defvjp(_sq_loss_fwd, _sq_loss_bwd)


def _cast_bf16(x, name):
    r, c = x.shape
    tb = _tile(r, 512, 16)

    def body(x_ref, o_ref):
        o_ref[...] = x_ref[...].astype(BF16)

    return pl.pallas_call(
        body, name=name, grid=(r // tb,),
        in_specs=[pl.BlockSpec((tb, c), lambda i: (i, 0))],
        out_specs=pl.BlockSpec((tb, c), lambda i: (i, 0)),
        out_shape=jax.ShapeDtypeStruct((r, c), BF16),
        compiler_params=_cparams(("parallel",)),
    )(x)


def _pair_sum(core_idx, g, recv, name):
    _, r, c = g.shape
    tb = _tile(r, 512, 8)

    def body(c_ref, g_ref, r_ref, o_ref):
        o_ref[...] = g_ref[...] + r_ref[...]

    return pl.pallas_call(
        body, name=name,
        grid_spec=pltpu.PrefetchScalarGridSpec(
            num_scalar_prefetch=1, grid=(4, r // tb),
            in_specs=[pl.BlockSpec((None, tb, c), lambda q, i, c_ref: (2 * q + c_ref[0], i, 0)),
                      pl.BlockSpec((None, tb, c), lambda q, i, c_ref: (q, i, 0))],
            out_specs=pl.BlockSpec((None, tb, c), lambda q, i, c_ref: (q, i, 0))),
        out_shape=jax.ShapeDtypeStruct((4, r, c), F32),
        compiler_params=_cparams(("parallel", "parallel")),
    )(core_idx, g, recv)


def _adamw(w, parts, m, v, name):
    n, r, c = parts.shape
    tb = _tile(r, 256, 8)
    b1c = 1.0 - ADAM_B1 ** ADAM_STEP
    b2c = 1.0 - ADAM_B2 ** ADAM_STEP

    def body(w_ref, p_ref, m_ref, v_ref, g_ref, d_ref, nm_ref, nv_ref):
        g = p_ref[0]
        for k in range(1, n):
            g = g + p_ref[k]
        m_new = ADAM_B1 * m_ref[...] + (1.0 - ADAM_B1) * g
        v_new = ADAM_B2 * v_ref[...] + (1.0 - ADAM_B2) * (g * g)
        m_hat = m_new / b1c
        v_hat = v_new / b2c
        g_ref[...] = g
        d_ref[...] = -ADAM_LR * (m_hat / (jnp.sqrt(v_hat) + ADAM_EPS) + ADAM_WD * w_ref[...])
        nm_ref[...] = m_new
        nv_ref[...] = v_new

    row = pl.BlockSpec((tb, c), lambda i: (i, 0))
    return pl.pallas_call(
        body, name=name, grid=(r // tb,),
        in_specs=[row, pl.BlockSpec((n, tb, c), lambda i: (0, i, 0)), row, row],
        out_specs=[row] * 4,
        out_shape=[jax.ShapeDtypeStruct((r, c), F32)] * 4,
        compiler_params=_cparams(("parallel",)),
    )(w, parts, m, v)


ANY = pl.BlockSpec(memory_space=pl.ANY)


def _place():
    return lax.axis_index("x"), lax.axis_index("y"), lax.axis_index("c")


def _all_gather(shard, name):
    r, c = shard.shape

    def body(x_ref, out_ref, send_sems, recv_sems, local_sem):
        x, y, cc = _place()
        me, sibling = (x, y, cc), (x, y, 1 - cc)
        chips = [(1 - x, y), (x, 1 - y), (1 - x, 1 - y)]

        def slot(px, py, pc):
            return out_ref.at[4 * px + 2 * py + pc]

        def copy(k, block, to, src=None):
            return pltpu.make_async_remote_copy(
                src_ref=slot(*block) if src is None else src, dst_ref=slot(*block),
                send_sem=send_sems.at[k], recv_sem=recv_sems.at[k], device_id=to, device_id_type=MESH)

        mine = pltpu.make_async_copy(x_ref, slot(*me), local_sem)
        mine.start()
        first = [copy(0, me, sibling, src=x_ref)]
        first += [copy(1 + j, me, (*chip, cc), src=x_ref) for j, chip in enumerate(chips)]
        for cp in first:
            cp.start()
        passed = [copy(4 + j, (*chip, cc), sibling) for j, chip in enumerate(chips)]
        for j, chip in enumerate(chips):
            copy(1 + j, (*chip, cc), me).wait_recv()
            passed[j].start()
        copy(0, sibling, me).wait_recv()
        for j, chip in enumerate(chips):
            copy(4 + j, (*chip, 1 - cc), me).wait_recv()
        for cp in first + passed:
            cp.wait_send()
        mine.wait()

    return pl.pallas_call(
        body, name=name,
        out_shape=jax.ShapeDtypeStruct((N_DEV, r, c), shard.dtype),
        in_specs=[ANY], out_specs=ANY,
        scratch_shapes=[pltpu.SemaphoreType.DMA((7,)), pltpu.SemaphoreType.DMA((7,)), pltpu.SemaphoreType.DMA],
    )(shard)


def _pair_exchange(g, name):
    _, r, c = g.shape

    def body(g_ref, recv_ref, send_sems, recv_sems):
        x, y, cc = _place()
        sibling = (x, y, 1 - cc)
        copies = [pltpu.make_async_remote_copy(
            src_ref=g_ref.at[2 * q + (1 - cc)], dst_ref=recv_ref.at[q],
            send_sem=send_sems.at[q], recv_sem=recv_sems.at[q], device_id=sibling, device_id_type=MESH)
            for q in range(4)]
        for cp in copies:
            cp.start()
        for cp in copies:
            cp.wait_recv()
        for cp in copies:
            cp.wait_send()

    return pl.pallas_call(
        body, name=name,
        out_shape=jax.ShapeDtypeStruct((4, r, c), g.dtype),
        in_specs=[ANY], out_specs=ANY,
        scratch_shapes=[pltpu.SemaphoreType.DMA((4,)), pltpu.SemaphoreType.DMA((4,))],
    )(g)


def _chip_exchange(part, name):
    _, r, c = part.shape

    def body(p_ref, out_ref, send_sems, recv_sems, local_sem):
        x, y, cc = _place()
        mine = 2 * x + y
        others = [(1 - x, y), (x, 1 - y), (1 - x, 1 - y)]
        keep = pltpu.make_async_copy(p_ref.at[mine], out_ref.at[mine], local_sem)
        keep.start()
        sends = []
        for px, py in others:
            q = 2 * px + py
            sends.append(pltpu.make_async_remote_copy(
                src_ref=p_ref.at[q], dst_ref=out_ref.at[mine],
                send_sem=send_sems.at[q], recv_sem=recv_sems.at[mine], device_id=(px, py, cc), device_id_type=MESH))
        for cp in sends:
            cp.start()
        for px, py in others:
            q = 2 * px + py
            pltpu.make_async_remote_copy(
                src_ref=p_ref.at[q], dst_ref=out_ref.at[q],
                send_sem=send_sems.at[q], recv_sem=recv_sems.at[q], device_id=(px, py, cc), device_id_type=MESH
            ).wait_recv()
        for cp in sends:
            cp.wait_send()
        keep.wait()

    return pl.pallas_call(
        body, name=name,
        out_shape=jax.ShapeDtypeStruct((4, r, c), part.dtype),
        in_specs=[ANY], out_specs=ANY,
        scratch_shapes=[pltpu.SemaphoreType.DMA((4,)), pltpu.SemaphoreType.DMA((4,)), pltpu.SemaphoreType.DMA],
    )(part)


def _all_reduce_small(v, name):
    shape = v.shape

    def body(v_ref, out_ref, buf, send_sems, recv_sems):
        x, y, cc = _place()
        me = 4 * x + 2 * y + cc
        buf[me] = v_ref[...]
        flips = [(a, b, d) for a in (0, 1) for b in (0, 1) for d in (0, 1)][1:]
        copies = []
        for k, (a, b, d) in enumerate(flips):
            peer = (x ^ a, y ^ b, cc ^ d)
            copies.append(pltpu.make_async_remote_copy(
                src_ref=v_ref, dst_ref=buf.at[me],
                send_sem=send_sems.at[k], recv_sem=recv_sems.at[k], device_id=peer, device_id_type=MESH))
        for cp in copies:
            cp.start()
        for k, (a, b, d) in enumerate(flips):
            peer_id = 4 * (x ^ a) + 2 * (y ^ b) + (cc ^ d)
            pltpu.make_async_remote_copy(
                src_ref=v_ref, dst_ref=buf.at[peer_id],
                send_sem=send_sems.at[k], recv_sem=recv_sems.at[k], device_id=(x, y, cc), device_id_type=MESH
            ).wait_recv()
        for cp in copies:
            cp.wait_send()
        total = buf[0]
        for k in range(1, N_DEV):
            total = total + buf[k]
        out_ref[...] = total

    vm = pl.BlockSpec(memory_space=pltpu.VMEM)
    return pl.pallas_call(
        body, name=name,
        out_shape=jax.ShapeDtypeStruct(shape, F32),
        in_specs=[vm], out_specs=vm,
        scratch_shapes=[pltpu.VMEM((N_DEV,) + shape, F32), pltpu.SemaphoreType.DMA((7,)), pltpu.SemaphoreType.DMA((7,))],
    )(v)


def _pack_rows(shapes):
    offs, off = {}, 0
    for n in BIG:
        rows = shapes[n][0] * shapes[n][1] // PACK_COLS
        offs[n] = (off, rows)
        off += rows
    total = -(-off // 512) * 512
    return offs, total


def _pack(arrs, offs, total):
    parts = [arrs[n].reshape(-1, PACK_COLS) for n in BIG]
    used = sum(p.shape[0] for p in parts)
    parts.append(jnp.zeros((total - used, PACK_COLS), F32))
    return jnp.concatenate(parts, axis=0)


def _unpack_shard(packed, offs, shapes):
    return {n: packed[offs[n][0]:offs[n][0] + offs[n][1]].reshape(shapes[n]) for n in BIG}


def _unpack_full(gathered, offs, shapes):
    out = {}
    for n in BIG:
        o, rows = offs[n]
        kk, nn = shapes[n]
        blk = gathered[:, o:o + rows, :].reshape(N_DEV, kk, nn)
        if n in COL_SHARDED:
            out[n] = blk.transpose(1, 0, 2).reshape(kk, N_DEV * nn)
        else:
            out[n] = blk.reshape(N_DEV * kk, nn)
    return out


def _pack_grads(grads, offs, total, shapes):
    parts = []
    for n in BIG:
        kk, nn = shapes[n]
        g = grads[n]
        if n in COL_SHARDED:
            g = g.reshape(kk, N_DEV, nn).transpose(1, 0, 2)
        parts.append(g.reshape(N_DEV, -1, PACK_COLS))
    used = sum(p.shape[1] for p in parts)
    parts.append(jnp.zeros((N_DEV, total - used, PACK_COLS), F32))
    return jnp.concatenate(parts, axis=1)


SMALL_ROWS = 8


def _pack_small(arrs):
    rows = [arrs[n] for n in SMALL[:5]]
    last = jnp.concatenate([arrs["q_norm1"], arrs["kv_norm1"], arrs["b_f2"]])
    rows.append(jnp.pad(last, (0, PACK_COLS - last.shape[0])))
    rows += [jnp.zeros((PACK_COLS,), F32)] * (SMALL_ROWS - len(rows))
    return jnp.stack(rows)


def _unpack_small(p):
    out = {n: p[k] for k, n in enumerate(SMALL[:5])}
    out["q_norm1"] = p[5, :MLA_Q_RANK]
    out["kv_norm1"] = p[5, MLA_Q_RANK:MLA_Q_RANK + MLA_KV_RANK]
    out["b_f2"] = p[5, MLA_Q_RANK + MLA_KV_RANK:MLA_Q_RANK + MLA_KV_RANK + N_HEADS]
    return out


def _rope(x, pos):
    r = x.shape[-1]
    inv_freq = ROPE_BASE ** (-jnp.arange(0, r, 2, dtype=F32) / r)
    ang = pos.astype(F32)[:, None, None] * inv_freq
    cos, sin = jnp.cos(ang), jnp.sin(ang)
    x1, x2 = x[..., : r // 2], x[..., r // 2:]
    return jnp.concatenate([x1 * cos - x2 * sin, x1 * sin + x2 * cos], axis=-1)


def _forward_loss(carriers, small, x, wfull, pos, target):
    s = x.shape[0]
    n_main = 4 * D_INNER

    def sb_layer(x, ln, w_in, w_out, tag):
        h = rmsnorm(x, small[ln], tag + "_ln")
        proj = mm(h, wfull[w_in], carriers[w_in], name=tag + "_in")
        return x + mm(sb_core(proj, tag), wfull[w_out], carriers[w_out], name=tag + "_out")

    x = sb_layer(x, "ln0", "w_in0", "w_out0", "l0")

    h = rmsnorm(x, small["ln1"], "l1_ln")
    proj = mm(h, wfull["w_in1"], carriers["w_in1"], name="l1_in")
    i1, i2, i3 = MLA_Q_RANK, MLA_Q_RANK + MLA_KV_RANK, MLA_Q_RANK + MLA_KV_RANK + MLA_ROPE
    q = mm(rmsnorm(proj[:, :i1], small["q_norm1"], "l1_qn"), wfull["w_qb1"], carriers["w_qb1"], name="l1_qb")
    q = q.reshape(s, N_HEADS, MLA_NOPE + MLA_ROPE)
    kv = mm(rmsnorm(proj[:, i1:i2], small["kv_norm1"], "l1_kvn"), wfull["w_kvb1"], carriers["w_kvb1"], name="l1_kvb")
    kv = kv.reshape(s, N_HEADS, MLA_NOPE + HEAD_DIM)
    k_rope = _rope(proj[:, i2:i3][:, None, :], pos)
    pad = jnp.zeros((s, N_HEADS, MLA_QK_PAD - MLA_NOPE - MLA_ROPE), F32)
    qc = jnp.concatenate([q[..., :MLA_NOPE], _rope(q[..., MLA_NOPE:], pos), pad], axis=-1)
    kc = jnp.concatenate([kv[..., :MLA_NOPE], jnp.broadcast_to(k_rope, (s, N_HEADS, MLA_ROPE)), pad], axis=-1)
    y = mla_core(qc.reshape(s, -1), kc.reshape(s, -1), kv[..., MLA_NOPE:].reshape(s, -1), proj[:, i3:], "l1")
    x = x + mm(y, wfull["w_out1"], carriers["w_out1"], name="l1_out")

    h = rmsnorm(x, small["ln2"], "l2_ln")
    proj = mm(h, wfull["w_in2"], carriers["w_in2_main"], n_cols=n_main, name="l2_in")
    f_logit = mm(h, wfull["w_in2"][:, n_main:], carriers["w_in2_f"], name="l2_f") + small["b_f2"]
    c = jnp.cumsum(jax.nn.log_sigmoid(f_logit), axis=0)
    x = x + mm(fox_core(proj, c, "l2"), wfull["w_out2"], carriers["w_out2"], name="l2_out")

    x = sb_layer(x, "ln3", "w_in3", "w_out3", "l3")
    return sq_loss(rmsnorm(x, small["final_norm"], "final_ln"), target)


def kernel(x, positions, ln0, w_in0, w_out0, ln1, w_in1, q_norm1, w_qb1, kv_norm1, w_kvb1, w_out1, ln2, w_in2, b_f2, w_out2, ln3, w_in3, w_out3, final_norm, loss_target, m_ln0, m_w_in0, m_w_out0, m_ln1, m_w_in1, m_q_norm1, m_w_qb1, m_kv_norm1, m_w_kvb1, m_w_out1, m_ln2, m_w_in2, m_b_f2, m_w_out2, m_ln3, m_w_in3, m_w_out3, m_final_norm, v_ln0, v_w_in0, v_w_out0, v_ln1, v_w_in1, v_q_norm1, v_w_qb1, v_kv_norm1, v_w_kvb1, v_w_out1, v_ln2, v_w_in2, v_b_f2, v_w_out2, v_ln3, v_w_in3, v_w_out3, v_final_norm):
    args = dict(locals())
    w = {n: args[n] for n in ALL_W}
    m = {n: args["m_" + n] for n in ALL_W}
    v = {n: args["v_" + n] for n in ALL_W}
    shapes = {n: w[n].shape for n in BIG}
    offs, total = _pack_rows(shapes)

    w_pack = _pack(w, offs, total)
    gathered = _all_gather(_cast_bf16(w_pack, "cast_w"), "gather_w")
    wfull = _unpack_full(gathered, offs, shapes)

    full_shapes = {n: wfull[n].shape for n in BIG}
    n_main = 4 * D_INNER
    carriers = {n: jnp.zeros(full_shapes[n], F32) for n in BIG if n != "w_in2"}
    carriers["w_in2_main"] = jnp.zeros((full_shapes["w_in2"][0], n_main), F32)
    carriers["w_in2_f"] = jnp.zeros((full_shapes["w_in2"][0], full_shapes["w_in2"][1] - n_main), F32)
    small = {n: w[n] for n in SMALL}

    loss_local, (g_car, g_small, g_x) = jax.value_and_grad(_forward_loss, argnums=(0, 1, 2))(
        carriers, small, x[0], wfull, positions[0], loss_target[0])
    loss = lax.psum(loss_local, ("x", "y", "c"))

    g_full = {n: g_car[n] for n in BIG if n != "w_in2"}
    g_full["w_in2"] = jnp.concatenate([g_car["w_in2_main"], g_car["w_in2_f"]], axis=1)
    g_pack = _pack_grads(g_full, offs, total, shapes)
    core_idx = lax.axis_index("c").astype(jnp.int32).reshape(1)
    chip_part = _pair_sum(core_idx, g_pack, _pair_exchange(g_pack, "pair_exchange"), "pair_sum")
    by_chip = _chip_exchange(chip_part, "chip_exchange")
    g_sh, d_sh, m_sh, v_sh = _adamw(w_pack, by_chip, _pack(m, offs, total), _pack(v, offs, total), "adamw")
    big = [_unpack_shard(t, offs, shapes) for t in (g_sh, d_sh, m_sh, v_sh)]

    g_small_sum = _all_reduce_small(_pack_small(g_small), "reduce_small")
    sm = _adamw(_pack_small(w), g_small_sum[None], _pack_small(m), _pack_small(v), "adamw_small")
    small_out = [_unpack_small(t) for t in sm]

    outs = [loss, g_x[None]]
    for k in range(4):
        outs += [small_out[k][n] if n in small_out[k] else big[k][n] for n in ALL_W]
    return tuple(outs)
```

```python
import jax
import jax.numpy as jnp
from jax import lax
from jax.experimental import pallas as pl
from jax.experimental.pallas import tpu as pltpu

F32 = jnp.float32
BF16 = jnp.bfloat16
MESH = pl.DeviceIdType.MESH

N_DEV = 8
N_HEADS = 16
HEAD_DIM = 128
D_INNER = N_HEADS * HEAD_DIM
MLA_Q_RANK = 256
MLA_KV_RANK = 128
MLA_NOPE = 128
MLA_ROPE = 64
MLA_QK_PAD = 256
MLA_CHUNK = 64
ROPE_BASE = 10000.0
EPS = 1e-6
NEG = -1e30
SB_CUT = 104.0

ADAM_LR = 0.001
ADAM_B1 = 0.9
ADAM_B2 = 0.999
ADAM_EPS = 1e-08
ADAM_WD = 0.01
ADAM_STEP = 10

PACK_COLS = 1024
VMEM_LIMIT = 56 * 1024 * 1024

BIG = ["w_in0", "w_out0", "w_in1", "w_qb1", "w_kvb1", "w_out1", "w_in2", "w_out2", "w_in3", "w_out3"]
COL_SHARDED = {"w_in0", "w_in1", "w_qb1", "w_kvb1", "w_in2", "w_in3"}
SMALL = ["ln0", "ln1", "ln2", "ln3", "final_norm", "q_norm1", "kv_norm1", "b_f2"]
ALL_W = ["ln0", "w_in0", "w_out0", "ln1", "w_in1", "q_norm1", "w_qb1", "kv_norm1", "w_kvb1", "w_out1",
         "ln2", "w_in2", "b_f2", "w_out2", "ln3", "w_in3", "w_out3", "final_norm"]


def _cparams(sem=None):
    return pltpu.CompilerParams(dimension_semantics=sem, vmem_limit_bytes=VMEM_LIMIT)


def _tile(dim, cap, align):
    if dim <= cap:
        return dim
    t = (cap // align) * align
    while t >= align:
        if dim % t == 0:
            return t
        t -= align
    return dim


def _dot(a, b, dims):
    return lax.dot_general(a, b, (dims, ((), ())), preferred_element_type=F32)


def _dot_nn(a, b):
    return _dot(a, b, ((1,), (0,)))


def _dot_nt(a, b):
    return _dot(a, b, ((1,), (1,)))


def _dot_tn(a, b):
    return _dot(a, b, ((0,), (0,)))


def _matmul(a, b, mode, col0=0, n_cols=None, out_dtype=F32, name="mm"):
    if mode == "nn":
        m, r = a.shape
        n = n_cols or b.shape[1]
        tm, tn, tr = _tile(m, 512, 8), _tile(n, 1024, 128), _tile(r, 1024, 128)
        c0 = col0 // tn
        a_spec = pl.BlockSpec((tm, tr), lambda i, j, k: (i, k))
        b_spec = pl.BlockSpec((tr, tn), lambda i, j, k: (k, j + c0))
        dims = ((1,), (0,))
        assert col0 % tn == 0
    elif mode == "nt":
        m, r = a.shape
        n = b.shape[0]
        tm, tn, tr = _tile(m, 512, 8), _tile(n, 1024, 128), _tile(r, 1024, 128)
        c0 = col0 // tr
        a_spec = pl.BlockSpec((tm, tr), lambda i, j, k: (i, k))
        b_spec = pl.BlockSpec((tn, tr), lambda i, j, k: (j, k + c0))
        dims = ((1,), (1,))
        assert col0 % tr == 0
    else:
        r, m = a.shape
        n = b.shape[1]
        tm, tn, tr = _tile(m, 1024, 128), _tile(n, 1024, 128), _tile(r, 512, 16)
        a_spec = pl.BlockSpec((tr, tm), lambda i, j, k: (k, i))
        b_spec = pl.BlockSpec((tr, tn), lambda i, j, k: (k, j))
        dims = ((0,), (0,))
    nr = r // tr

    def body(a_ref, b_ref, o_ref, acc_ref):
        k = pl.program_id(2)

        @pl.when(k == 0)
        def _():
            acc_ref[...] = jnp.zeros_like(acc_ref)

        acc_ref[...] += _dot(a_ref[...].astype(BF16), b_ref[...].astype(BF16), dims)

        @pl.when(k == nr - 1)
        def _():
            o_ref[...] = acc_ref[...].astype(out_dtype)

    return pl.pallas_call(
        body,
        name=name,
        grid=(m // tm, n // tn, nr),
        in_specs=[a_spec, b_spec],
        out_specs=pl.BlockSpec((tm, tn), lambda i, j, k: (i, j)),
        out_shape=jax.ShapeDtypeStruct((m, n), out_dtype),
        scratch_shapes=[pltpu.VMEM((tm, tn), F32)],
        compiler_params=_cparams(("parallel", "parallel", "arbitrary")),
    )(a, b)


def mm(a, w, carrier, col0=0, n_cols=None, out_dtype=F32, name="mm"):
    @jax.custom_vjp
    def f(a, w, carrier):
        return _matmul(a, w, "nn", col0, n_cols, out_dtype, name + "_fwd")

    def fwd(a, w, carrier):
        return _matmul(a, w, "nn", col0, n_cols, out_dtype, name + "_fwd"), (a, w)

    def bwd(res, g):
        a, w = res
        da = _matmul(g, w, "nt", col0, None, F32, name + "_dx")
        dw = _matmul(a, g, "tn", 0, None, F32, name + "_dw")
        return da, jnp.zeros_like(w), dw

    f.defvjp(fwd, bwd)
    return f(a, w, carrier)


def _rms_fwd(x, g, name):
    s, d = x.shape
    tm = _tile(s, 512, 8)

    def body(x_ref, g_ref, y_ref):
        x = x_ref[...]
        r = lax.rsqrt(jnp.mean(x * x, axis=-1, keepdims=True) + EPS)
        y_ref[...] = x * r * g_ref[...]

    return pl.pallas_call(
        body, name=name, grid=(s // tm,),
        in_specs=[pl.BlockSpec((tm, d), lambda i: (i, 0)), pl.BlockSpec((1, d), lambda i: (0, 0))],
        out_specs=pl.BlockSpec((tm, d), lambda i: (i, 0)),
        out_shape=jax.ShapeDtypeStruct((s, d), F32),
        compiler_params=_cparams(("parallel",)),
    )(x, g)


def _rms_bwd(x, g, dy, name):
    s, d = x.shape
    tm = _tile(s, 512, 8)

    def body(x_ref, g_ref, dy_ref, dx_ref, dg_ref):
        @pl.when(pl.program_id(0) == 0)
        def _():
            dg_ref[...] = jnp.zeros_like(dg_ref)

        x = x_ref[...]
        dy = dy_ref[...]
        r = lax.rsqrt(jnp.mean(x * x, axis=-1, keepdims=True) + EPS)
        xh = x * r
        dg_ref[...] += jnp.sum(dy * xh, axis=0, keepdims=True)
        dxh = dy * g_ref[...]
        dx_ref[...] = r * (dxh - xh * jnp.mean(dxh * xh, axis=-1, keepdims=True))

    return pl.pallas_call(
        body, name=name, grid=(s // tm,),
        in_specs=[pl.BlockSpec((tm, d), lambda i: (i, 0)), pl.BlockSpec((1, d), lambda i: (0, 0)),
                  pl.BlockSpec((tm, d), lambda i: (i, 0))],
        out_specs=[pl.BlockSpec((tm, d), lambda i: (i, 0)), pl.BlockSpec((1, d), lambda i: (0, 0))],
        out_shape=[jax.ShapeDtypeStruct((s, d), F32), jax.ShapeDtypeStruct((1, d), F32)],
        compiler_params=_cparams(("arbitrary",)),
    )(x, g, dy)


def rmsnorm(x, g, name="rms"):
    @jax.custom_vjp
    def f(x, g):
        return _rms_fwd(x, g.reshape(1, -1), name + "_fwd")

    def fwd(x, g):
        return _rms_fwd(x, g.reshape(1, -1), name + "_fwd"), (x, g)

    def bwd(res, dy):
        x, g = res
        dx, dg = _rms_bwd(x, g.reshape(1, -1), dy, name + "_bwd")
        return dx, dg.reshape(-1)

    f.defvjp(fwd, bwd)
    return f(x, g)


def _gate_fwd(o, gate, name):
    s = o.shape[0]
    tm = 256

    def body(o_ref, g_ref, y_ref):
        g = g_ref[...]
        y_ref[...] = o_ref[...] * (g / (1.0 + jnp.exp(-g)))

    row = pl.BlockSpec((tm, D_INNER), lambda i: (i, 0))
    return pl.pallas_call(
        body, name=name, grid=(s // tm,),
        in_specs=[row, row], out_specs=row,
        out_shape=jax.ShapeDtypeStruct((s, D_INNER), F32),
        compiler_params=_cparams(("parallel",)),
    )(o, gate)


def _gate_bwd(dy, o, gate, name):
    s = o.shape[0]
    tm = 256

    def body(dy_ref, o_ref, g_ref, do_ref, dg_ref, dl_ref):
        g = g_ref[...]
        o = o_ref[...]
        dy = dy_ref[...]
        sg = 1.0 / (1.0 + jnp.exp(-g))
        do = dy * (g * sg)
        do_ref[...] = do.astype(BF16)
        dg_ref[...] = dy * o * (sg * (1.0 + g * (1.0 - sg)))
        prod = do * o
        for h in range(N_HEADS):
            dl_ref[h] = jnp.sum(prod[:, h * HEAD_DIM:(h + 1) * HEAD_DIM], axis=1, keepdims=True)

    row = pl.BlockSpec((tm, D_INNER), lambda i: (i, 0))
    return pl.pallas_call(
        body, name=name, grid=(s // tm,),
        in_specs=[row, row, row],
        out_specs=[row, row, pl.BlockSpec((N_HEADS, tm, 1), lambda i: (0, i, 0))],
        out_shape=[jax.ShapeDtypeStruct((s, D_INNER), BF16), jax.ShapeDtypeStruct((s, D_INNER), F32),
                   jax.ShapeDtypeStruct((N_HEADS, s, 1), F32)],
        compiler_params=_cparams(("parallel",)),
    )(dy, o, gate)


SB_BLK = 256


def _softplus(z):
    return jnp.maximum(z, 0.0) + jnp.log(1.0 + jnp.exp(-jnp.abs(z)))


def _tri_sum(x, tri):
    hi = x.astype(BF16)
    lo = (x - hi.astype(F32)).astype(BF16)
    return _dot_nn(hi, tri) + _dot_nn(lo, tri)


def _sb_block(q, kb, i, j, a_run, row, col, tri_suffix, scale):
    z = _dot_nt(q, kb) * scale
    mask = (col + j * SB_BLK) < (row + i * SB_BLK)
    sp = _softplus(z)
    ls = jnp.where(mask, -sp, 0.0)
    suffix = _tri_sum(ls, tri_suffix)
    w = jnp.where(mask, jnp.exp(z + suffix + a_run), 0.0)
    return z, mask, sp, ls, w


def _sb_fwd(qkv, name):
    s = qkv.shape[0]
    b = SB_BLK
    nq = s // b
    scale = HEAD_DIM ** -0.5

    def body(q_ref, k_ref, v_ref, o_ref):
        i = pl.program_id(1)
        q = q_ref[...]
        row = lax.broadcasted_iota(jnp.int32, (b, b), 0)
        col = lax.broadcasted_iota(jnp.int32, (b, b), 1)
        tri_suffix = (row >= col).astype(BF16)

        def cond(c):
            j, a_run, _ = c
            return jnp.logical_and(j >= 0, jnp.max(a_run) > -SB_CUT)

        def step(c):
            j, a_run, acc = c
            off = pl.multiple_of(j * b, b)
            kb = k_ref[pl.ds(off, b), :]
            vb = v_ref[pl.ds(off, b), :]
            _, _, _, ls, w = _sb_block(q, kb, i, j, a_run, row, col, tri_suffix, scale)
            acc = acc + _dot_nn(w.astype(BF16), vb)
            return j - 1, a_run + jnp.sum(ls, axis=1, keepdims=True), acc

        _, _, acc = lax.while_loop(cond, step, (i, jnp.zeros((b, 1), F32), jnp.zeros((b, HEAD_DIM), F32)))
        o_ref[...] = acc

    return pl.pallas_call(
        body, name=name, grid=(N_HEADS, nq),
        in_specs=[pl.BlockSpec((b, HEAD_DIM), lambda h, i: (i, h)),
                  pl.BlockSpec((s, HEAD_DIM), lambda h, i: (0, N_HEADS + h)),
                  pl.BlockSpec((s, HEAD_DIM), lambda h, i: (0, 2 * N_HEADS + h))],
        out_specs=pl.BlockSpec((b, HEAD_DIM), lambda h, i: (i, h)),
        out_shape=jax.ShapeDtypeStruct((s, D_INNER), F32),
        compiler_params=_cparams(("parallel", "arbitrary")),
    )(qkv, qkv, qkv)


def _sb_bwd(qkv, do, name):
    s = qkv.shape[0]
    b = SB_BLK
    nq = s // b
    scale = HEAD_DIM ** -0.5

    def body(q_ref, k_ref, v_ref, do_ref, dq_ref, dk_ref, dv_ref, dk_s, dv_s, g_buf, sig_buf):
        i = pl.program_id(1)

        @pl.when(i == 0)
        def _():
            dk_s[...] = jnp.zeros_like(dk_s)
            dv_s[...] = jnp.zeros_like(dv_s)

        q = q_ref[...]
        dob = do_ref[...]
        row = lax.broadcasted_iota(jnp.int32, (b, b), 0)
        col = lax.broadcasted_iota(jnp.int32, (b, b), 1)
        tri_suffix = (row >= col).astype(BF16)
        tri_prefix = (row <= col).astype(BF16)

        def cond(c):
            j, a_run = c
            return jnp.logical_and(j >= 0, jnp.max(a_run) > -SB_CUT)

        def sweep(c):
            j, a_run = c
            off = pl.multiple_of(j * b, b)
            kb = k_ref[pl.ds(off, b), :]
            vb = v_ref[pl.ds(off, b), :]
            z, _, sp, ls, w = _sb_block(q, kb, i, j, a_run, row, col, tri_suffix, scale)
            g_buf[j] = w * _dot_nt(dob, vb)
            sig_buf[j] = jnp.exp(z - sp)
            dv_s[pl.ds(off, b), :] += _dot_tn(w.astype(BF16), dob)
            return j - 1, a_run + jnp.sum(ls, axis=1, keepdims=True)

        j_end, _ = lax.while_loop(cond, sweep, (i, jnp.zeros((b, 1), F32)))

        def back(j, c):
            g_run, dq = c
            off = pl.multiple_of(j * b, b)
            kb = k_ref[pl.ds(off, b), :]
            g = g_buf[j]
            g_incl = g_run + _tri_sum(g, tri_prefix)
            mask = (col + j * b) < (row + i * b)
            dz = jnp.where(mask, (g - sig_buf[j] * g_incl) * scale, 0.0).astype(BF16)
            dk_s[pl.ds(off, b), :] += _dot_tn(dz, q)
            return g_run + jnp.sum(g, axis=1, keepdims=True), dq + _dot_nn(dz, kb)

        _, dq = lax.fori_loop(j_end + 1, i + 1, back, (jnp.zeros((b, 1), F32), jnp.zeros((b, HEAD_DIM), F32)))
        dq_ref[...] = dq.astype(BF16)

        @pl.when(i == nq - 1)
        def _():
            dk_ref[...] = dk_s[...].astype(BF16)
            dv_ref[...] = dv_s[...].astype(BF16)

    blk = pl.BlockSpec((b, HEAD_DIM), lambda h, i: (i, h))
    head = pl.BlockSpec((s, HEAD_DIM), lambda h, i: (0, h))
    return pl.pallas_call(
        body, name=name, grid=(N_HEADS, nq),
        in_specs=[blk,
                  pl.BlockSpec((s, HEAD_DIM), lambda h, i: (0, N_HEADS + h)),
                  pl.BlockSpec((s, HEAD_DIM), lambda h, i: (0, 2 * N_HEADS + h)),
                  blk],
        out_specs=[blk, head, head],
        out_shape=[jax.ShapeDtypeStruct((s, D_INNER), BF16)] * 3,
        scratch_shapes=[pltpu.VMEM((s, HEAD_DIM), F32), pltpu.VMEM((s, HEAD_DIM), F32),
                        pltpu.VMEM((nq, b, b), F32), pltpu.VMEM((nq, b, b), F32)],
        compiler_params=_cparams(("arbitrary", "arbitrary")),
    )(qkv, qkv, qkv, do)


def sb_core(qkv, gate, name):
    def run(qkv, gate):
        o = _sb_fwd(qkv, name + "_fwd")
        return _gate_fwd(o, gate, name + "_gate"), o

    @jax.custom_vjp
    def f(qkv, gate):
        return run(qkv, gate)[0]

    def fwd(qkv, gate):
        y, o = run(qkv, gate)
        return y, (qkv, gate, o)

    def bwd(res, dy):
        qkv, gate, o = res
        do, dgate, _ = _gate_bwd(dy, o, gate, name + "_gate_bwd")
        dq, dk, dv = _sb_bwd(qkv, do, name + "_bwd")
        return jnp.concatenate([dq, dk, dv], axis=1), dgate

    f.defvjp(fwd, bwd)
    return f(qkv, gate)


SM_BLK = 256


def _sm_mask(i, j, row, col, chunk_shift):
    return ((col + j * SM_BLK) >> chunk_shift) <= ((row + i * SM_BLK) >> chunk_shift)


def _sm_fwd(qa, ka, va, ccol, crow, dqk, qo, ko, vo, chunk_shift, scale, name):
    s = qa.shape[0]
    b = SM_BLK
    wide = 2 * b
    nq = s // b
    has_bias = ccol is not None

    def body(*refs):
        if has_bias:
            q_ref, k_ref, v_ref, cc_ref, cr_ref, o_ref, lse_ref, m_s, l_s, acc_s = refs
        else:
            q_ref, k_ref, v_ref, o_ref, lse_ref, m_s, l_s, acc_s = refs
        i = pl.program_id(1)
        q = q_ref[...]
        m_s[...] = jnp.full_like(m_s, NEG)
        l_s[...] = jnp.zeros_like(l_s)
        acc_s[...] = jnp.zeros_like(acc_s)

        def update(z, vb):
            m_old = m_s[...]
            m_new = jnp.maximum(m_old, jnp.max(z, axis=1, keepdims=True))
            alpha = jnp.exp(m_old - m_new)
            p = jnp.exp(z - m_new)
            l_s[...] = alpha * l_s[...] + jnp.sum(p, axis=1, keepdims=True)
            acc_s[...] = alpha * acc_s[...] + _dot_nn(p.astype(BF16), vb)
            m_s[...] = m_new

        def full(jw, carry):
            off = pl.multiple_of(jw * wide, wide)
            z = _dot_nt(q, k_ref[pl.ds(off, wide), :]) * scale
            if has_bias:
                z = z + cc_ref[...] - jnp.concatenate([cr_ref[2 * jw], cr_ref[2 * jw + 1]], axis=1)
            update(z, v_ref[pl.ds(off, wide), :])
            return carry

        lax.fori_loop(0, i // 2, full, 0)

        def tail(j, carry):
            off = pl.multiple_of(j * b, b)
            row = lax.broadcasted_iota(jnp.int32, (b, b), 0)
            col = lax.broadcasted_iota(jnp.int32, (b, b), 1)
            z = _dot_nt(q, k_ref[pl.ds(off, b), :]) * scale
            if has_bias:
                z = z + cc_ref[...] - cr_ref[j]
            update(jnp.where(_sm_mask(i, j, row, col, chunk_shift), z, NEG), v_ref[pl.ds(off, b), :])
            return carry

        lax.fori_loop(2 * (i // 2), i + 1, tail, 0)
        o_ref[...] = acc_s[...] / l_s[...]
        lse_ref[...] = m_s[...] + jnp.log(l_s[...])

    in_specs = [pl.BlockSpec((b, dqk), lambda h, i: (i, qo + h)),
                pl.BlockSpec((s, dqk), lambda h, i: (0, ko + h)),
                pl.BlockSpec((s, HEAD_DIM), lambda h, i: (0, vo + h))]
    args = [qa, ka, va]
    if has_bias:
        in_specs += [pl.BlockSpec((None, b, 1), lambda h, i: (h, i, 0)),
                     pl.BlockSpec((None, nq, 1, b), lambda h, i: (h, 0, 0, 0))]
        args += [ccol, crow]
    return pl.pallas_call(
        body, name=name, grid=(N_HEADS, nq),
        in_specs=in_specs,
        out_specs=[pl.BlockSpec((b, HEAD_DIM), lambda h, i: (i, h)),
                   pl.BlockSpec((None, b, 1), lambda h, i: (h, i, 0))],
        out_shape=[jax.ShapeDtypeStruct((s, D_INNER), F32), jax.ShapeDtypeStruct((N_HEADS, s, 1), F32)],
        scratch_shapes=[pltpu.VMEM((b, 1), F32), pltpu.VMEM((b, 1), F32), pltpu.VMEM((b, HEAD_DIM), F32)],
        compiler_params=_cparams(("parallel", "arbitrary")),
    )(*args)


def _sm_bwd(qa, ka, va, do, lse, delta, ccol, crow, dqk, qo, ko, vo, chunk_shift, scale, grad_dtype, name):
    s = qa.shape[0]
    b = SM_BLK
    nq = s // b
    has_bias = ccol is not None

    def body(*refs):
        if has_bias:
            (q_ref, k_ref, v_ref, do_ref, lse_ref, dl_ref, cc_ref, cr_ref,
             dq_ref, dk_ref, dv_ref, dc_ref, dr_ref, dq_s, dk_s, dv_s, dc_s) = refs
        else:
            (q_ref, k_ref, v_ref, do_ref, lse_ref, dl_ref,
             dq_ref, dk_ref, dv_ref, dq_s, dk_s, dv_s) = refs
        j = pl.program_id(1)

        @pl.when(j == 0)
        def _():
            dq_s[...] = jnp.zeros_like(dq_s)
            if has_bias:
                dr_ref[...] = jnp.zeros_like(dr_ref)

        kb = k_ref[...]
        vb = v_ref[...]
        dk_s[...] = jnp.zeros_like(dk_s)
        dv_s[...] = jnp.zeros_like(dv_s)
        if has_bias:
            dc_s[...] = jnp.zeros_like(dc_s)

        def pair(i, masked):
            off = pl.multiple_of(i * b, b)
            qb = q_ref[pl.ds(off, b), :]
            dob = do_ref[pl.ds(off, b), :]
            z = _dot_nt(qb, kb) * scale
            if has_bias:
                z = z + cc_ref[pl.ds(off, b), :] - cr_ref[...]
            p = jnp.exp(z - lse_ref[pl.ds(off, b), :])
            if masked:
                row = lax.broadcasted_iota(jnp.int32, (b, b), 0)
                col = lax.broadcasted_iota(jnp.int32, (b, b), 1)
                p = jnp.where(_sm_mask(i, j, row, col, chunk_shift), p, 0.0)
            dv_s[...] += _dot_tn(p.astype(BF16), dob)
            dz = p * (_dot_nt(dob, vb) - dl_ref[pl.ds(off, b), :])
            if has_bias:
                dc_s[...] += jnp.sum(dz, axis=0, keepdims=True)
                dr_ref[pl.ds(off, b), :] += jnp.sum(dz, axis=1, keepdims=True)
            dzs = (dz * scale).astype(BF16)
            dk_s[...] += _dot_tn(dzs, qb)
            dq_s[pl.ds(off, b), :] += _dot_nn(dzs, kb)

        pair(j, True)

        def rest(i, carry):
            pair(i, False)
            return carry

        lax.fori_loop(j + 1, nq, rest, 0)
        dk_ref[...] = dk_s[...].astype(grad_dtype)
        dv_ref[...] = dv_s[...].astype(grad_dtype)
        if has_bias:
            dc_ref[...] = dc_s[...]

        @pl.when(j == nq - 1)
        def _():
            dq_ref[...] = dq_s[...].astype(grad_dtype)

    vec = pl.BlockSpec((None, s, 1), lambda h, j: (h, 0, 0))
    in_specs = [pl.BlockSpec((s, dqk), lambda h, j: (0, qo + h)),
                pl.BlockSpec((b, dqk), lambda h, j: (j, ko + h)),
                pl.BlockSpec((b, HEAD_DIM), lambda h, j: (j, vo + h)),
                pl.BlockSpec((s, HEAD_DIM), lambda h, j: (0, h)),
                vec, vec]
    args = [qa, ka, va, do, lse, delta]
    out_specs = [pl.BlockSpec((s, dqk), lambda h, j: (0, h)),
                 pl.BlockSpec((b, dqk), lambda h, j: (j, h)),
                 pl.BlockSpec((b, HEAD_DIM), lambda h, j: (j, h))]
    out_shape = [jax.ShapeDtypeStruct((s, N_HEADS * dqk), grad_dtype),
                 jax.ShapeDtypeStruct((s, N_HEADS * dqk), grad_dtype),
                 jax.ShapeDtypeStruct((s, D_INNER), grad_dtype)]
    scratch = [pltpu.VMEM((s, dqk), F32), pltpu.VMEM((b, dqk), F32), pltpu.VMEM((b, HEAD_DIM), F32)]
    if has_bias:
        in_specs += [vec, pl.BlockSpec((None, None, 1, b), lambda h, j: (h, j, 0, 0))]
        args += [ccol, crow]
        out_specs += [pl.BlockSpec((None, None, 1, b), lambda h, j: (h, j, 0, 0)), vec]
        out_shape += [jax.ShapeDtypeStruct((N_HEADS, nq, 1, b), F32), jax.ShapeDtypeStruct((N_HEADS, s, 1), F32)]
        scratch += [pltpu.VMEM((1, b), F32)]
    return pl.pallas_call(
        body, name=name, grid=(N_HEADS, nq),
        in_specs=in_specs, out_specs=out_specs, out_shape=out_shape, scratch_shapes=scratch,
        compiler_params=_cparams(("arbitrary", "arbitrary")),
    )(*args)


def fox_core(qkv, gate, c, name):
    s = qkv.shape[0]
    nq = s // SM_BLK
    scale = HEAD_DIM ** -0.5
    cfg = dict(dqk=HEAD_DIM, qo=0, ko=N_HEADS, vo=2 * N_HEADS, chunk_shift=0, scale=scale)

    def layouts(c):
        ct = c.T
        return ct.reshape(N_HEADS, s, 1), ct.reshape(N_HEADS, nq, 1, SM_BLK)

    def run(qkv, gate, c):
        ccol, crow = layouts(c)
        o, lse = _sm_fwd(qkv, qkv, qkv, ccol, crow, name=name + "_fwd", **cfg)
        return _gate_fwd(o, gate, name + "_gate"), o, lse

    @jax.custom_vjp
    def f(qkv, gate, c):
        return run(qkv, gate, c)[0]

    def fwd(qkv, gate, c):
        y, o, lse = run(qkv, gate, c)
        return y, (qkv, gate, c, o, lse)

    def bwd(res, dy):
        qkv, gate, c, o, lse = res
        ccol, crow = layouts(c)
        do, dgate, delta = _gate_bwd(dy, o, gate, name + "_gate_bwd")
        dq, dk, dv, colsum, rowsum = _sm_bwd(qkv, qkv, qkv, do, lse, delta, ccol, crow,
                                             grad_dtype=BF16, name=name + "_bwd", **cfg)
        dc = (rowsum.reshape(N_HEADS, s) - colsum.reshape(N_HEADS, s)).T
        return jnp.concatenate([dq, dk, dv], axis=1), dgate, dc

    f.defvjp(fwd, bwd)
    return f(qkv, gate, c)


def mla_core(qc, kc, v, gate, name):
    scale = (MLA_NOPE + MLA_ROPE) ** -0.5
    cfg = dict(dqk=MLA_QK_PAD, qo=0, ko=0, vo=0, chunk_shift=MLA_CHUNK.bit_length() - 1, scale=scale)

    def run(qc, kc, v, gate):
        o, lse = _sm_fwd(qc, kc, v, None, None, name=name + "_fwd", **cfg)
        return _gate_fwd(o, gate, name + "_gate"), o, lse

    @jax.custom_vjp
    def f(qc, kc, v, gate):
        return run(qc.astype(BF16), kc.astype(BF16), v.astype(BF16), gate)[0]

    def fwd(qc, kc, v, gate):
        qc, kc, v = qc.astype(BF16), kc.astype(BF16), v.astype(BF16)
        y, o, lse = run(qc, kc, v, gate)
        return y, (qc, kc, v, gate, o, lse)

    def bwd(res, dy):
        qc, kc, v, gate, o, lse = res
        do, dgate, delta = _gate_bwd(dy, o, gate, name + "_gate_bwd")
        dq, dk, dv = _sm_bwd(qc, kc, v, do, lse, delta, None, None, grad_dtype=F32, name=name + "_bwd", **cfg)
        return dq, dk, dv, dgate

    f.defvjp(fwd, bwd)
    return f(qc, kc, v, gate)


def _sq_loss_call(y, t, name):
    s, d = y.shape
    tm = _tile(s, 512, 8)

    def body(y_ref, t_ref, l_ref, e_ref):
        @pl.when(pl.program_id(0) == 0)
        def _():
            l_ref[...] = jnp.zeros_like(l_ref)

        e = y_ref[...] - t_ref[...]
        e_ref[...] = e * (1.0 / d)
        part = jnp.sum(jnp.sum(e * e, axis=1, keepdims=True), axis=0, keepdims=True)
        l_ref[...] += jnp.broadcast_to(part * (0.5 / d), l_ref.shape)

    row = pl.BlockSpec((tm, d), lambda i: (i, 0))
    return pl.pallas_call(
        body, name=name, grid=(s // tm,),
        in_specs=[row, row],
        out_specs=[pl.BlockSpec((8, 128), lambda i: (0, 0)), row],
        out_shape=[jax.ShapeDtypeStruct((8, 128), F32), jax.ShapeDtypeStruct((s, d), F32)],
        compiler_params=_cparams(("arbitrary",)),
    )(y, t)


@jax.custom_vjp
def sq_loss(y, t):
    return _sq_loss_call(y, t, "loss_fwd")[0][0, 0]


def _sq_loss_fwd(y, t):
    l, e = _sq_loss_call(y, t, "loss_fwd")
    return l[0, 0], e


def _sq_loss_bwd(e, g):
    return g * e, jnp.zeros_like(e)


sq_loss.defvjp(_sq_loss_fwd, _sq_loss_bwd)


def _cast_bf16(x, name):
    r, c = x.shape
    tb = _tile(r, 512, 16)

    def body(x_ref, o_ref):
        o_ref[...] = x_ref[...].astype(BF16)

    return pl.pallas_call(
        body, name=name, grid=(r // tb,),
        in_specs=[pl.BlockSpec((tb, c), lambda i: (i, 0))],
        out_specs=pl.BlockSpec((tb, c), lambda i: (i, 0)),
        out_shape=jax.ShapeDtypeStruct((r, c), BF16),
        compiler_params=_cparams(("parallel",)),
    )(x)


def _pair_sum(core_idx, g, recv, name):
    _, r, c = g.shape
    tb = _tile(r, 512, 8)

    def body(c_ref, g_ref, r_ref, o_ref):
        o_ref[...] = g_ref[...] + r_ref[...]

    return pl.pallas_call(
        body, name=name,
        grid_spec=pltpu.PrefetchScalarGridSpec(
            num_scalar_prefetch=1, grid=(4, r // tb),
            in_specs=[pl.BlockSpec((None, tb, c), lambda q, i, c_ref: (2 * q + c_ref[0], i, 0)),
                      pl.BlockSpec((None, tb, c), lambda q, i, c_ref: (q, i, 0))],
            out_specs=pl.BlockSpec((None, tb, c), lambda q, i, c_ref: (q, i, 0))),
        out_shape=jax.ShapeDtypeStruct((4, r, c), F32),
        compiler_params=_cparams(("parallel", "parallel")),
    )(core_idx, g, recv)


def _adamw(w, parts, m, v, name):
    n, r, c = parts.shape
    tb = _tile(r, 256, 8)
    b1c = 1.0 - ADAM_B1 ** ADAM_STEP
    b2c = 1.0 - ADAM_B2 ** ADAM_STEP

    def body(w_ref, p_ref, m_ref, v_ref, g_ref, d_ref, nm_ref, nv_ref):
        g = p_ref[0]
        for k in range(1, n):
            g = g + p_ref[k]
        m_new = ADAM_B1 * m_ref[...] + (1.0 - ADAM_B1) * g
        v_new = ADAM_B2 * v_ref[...] + (1.0 - ADAM_B2) * (g * g)
        m_hat = m_new / b1c
        v_hat = v_new / b2c
        g_ref[...] = g
        d_ref[...] = -ADAM_LR * (m_hat / (jnp.sqrt(v_hat) + ADAM_EPS) + ADAM_WD * w_ref[...])
        nm_ref[...] = m_new
        nv_ref[...] = v_new

    row = pl.BlockSpec((tb, c), lambda i: (i, 0))
    return pl.pallas_call(
        body, name=name, grid=(r // tb,),
        in_specs=[row, pl.BlockSpec((n, tb, c), lambda i: (0, i, 0)), row, row],
        out_specs=[row] * 4,
        out_shape=[jax.ShapeDtypeStruct((r, c), F32)] * 4,
        compiler_params=_cparams(("parallel",)),
    )(w, parts, m, v)


ANY = pl.BlockSpec(memory_space=pl.ANY)


def _place():
    return lax.axis_index("x"), lax.axis_index("y"), lax.axis_index("c")


def _all_gather(shard, name):
    r, c = shard.shape

    def body(x_ref, out_ref, send_sems, recv_sems, local_sem):
        x, y, cc = _place()
        me, sibling = (x, y, cc), (x, y, 1 - cc)
        chips = [(1 - x, y), (x, 1 - y), (1 - x, 1 - y)]

        def slot(px, py, pc):
            return out_ref.at[4 * px + 2 * py + pc]

        def copy(k, block, to, src=None):
            return pltpu.make_async_remote_copy(
                src_ref=slot(*block) if src is None else src, dst_ref=slot(*block),
                send_sem=send_sems.at[k], recv_sem=recv_sems.at[k], device_id=to, device_id_type=MESH)

        mine = pltpu.make_async_copy(x_ref, slot(*me), local_sem)
        mine.start()
        first = [copy(0, me, sibling, src=x_ref)]
        first += [copy(1 + j, me, (*chip, cc), src=x_ref) for j, chip in enumerate(chips)]
        for cp in first:
            cp.start()
        passed = [copy(4 + j, (*chip, cc), sibling) for j, chip in enumerate(chips)]
        for j, chip in enumerate(chips):
            copy(1 + j, (*chip, cc), me).wait_recv()
            passed[j].start()
        copy(0, sibling, me).wait_recv()
        for j, chip in enumerate(chips):
            copy(4 + j, (*chip, 1 - cc), me).wait_recv()
        for cp in first + passed:
            cp.wait_send()
        mine.wait()

    return pl.pallas_call(
        body, name=name,
        out_shape=jax.ShapeDtypeStruct((N_DEV, r, c), shard.dtype),
        in_specs=[ANY], out_specs=ANY,
        scratch_shapes=[pltpu.SemaphoreType.DMA((7,)), pltpu.SemaphoreType.DMA((7,)), pltpu.SemaphoreType.DMA],
    )(shard)


def _pair_exchange(g, name):
    _, r, c = g.shape

    def body(g_ref, recv_ref, send_sems, recv_sems):
        x, y, cc = _place()
        sibling = (x, y, 1 - cc)
        copies = [pltpu.make_async_remote_copy(
            src_ref=g_ref.at[2 * q + (1 - cc)], dst_ref=recv_ref.at[q],
            send_sem=send_sems.at[q], recv_sem=recv_sems.at[q], device_id=sibling, device_id_type=MESH)
            for q in range(4)]
        for cp in copies:
            cp.start()
        for cp in copies:
            cp.wait_recv()
        for cp in copies:
            cp.wait_send()

    return pl.pallas_call(
        body, name=name,
        out_shape=jax.ShapeDtypeStruct((4, r, c), g.dtype),
        in_specs=[ANY], out_specs=ANY,
        scratch_shapes=[pltpu.SemaphoreType.DMA((4,)), pltpu.SemaphoreType.DMA((4,))],
    )(g)


def _chip_exchange(part, name):
    _, r, c = part.shape

    def body(p_ref, out_ref, send_sems, recv_sems, local_sem):
        x, y, cc = _place()
        mine = 2 * x + y
        others = [(1 - x, y), (x, 1 - y), (1 - x, 1 - y)]
        keep = pltpu.make_async_copy(p_ref.at[mine], out_ref.at[mine], local_sem)
        keep.start()
        sends = []
        for px, py in others:
            q = 2 * px + py
            sends.append(pltpu.make_async_remote_copy(
                src_ref=p_ref.at[q], dst_ref=out_ref.at[mine],
                send_sem=send_sems.at[q], recv_sem=recv_sems.at[mine], device_id=(px, py, cc), device_id_type=MESH))
        for cp in sends:
            cp.start()
        for px, py in others:
            q = 2 * px + py
            pltpu.make_async_remote_copy(
                src_ref=p_ref.at[q], dst_ref=out_ref.at[q],
                send_sem=send_sems.at[q], recv_sem=recv_sems.at[q], device_id=(px, py, cc), device_id_type=MESH
            ).wait_recv()
        for cp in sends:
            cp.wait_send()
        keep.wait()

    return pl.pallas_call(
        body, name=name,
        out_shape=jax.ShapeDtypeStruct((4, r, c), part.dtype),
        in_specs=[ANY], out_specs=ANY,
        scratch_shapes=[pltpu.SemaphoreType.DMA((4,)), pltpu.SemaphoreType.DMA((4,)), pltpu.SemaphoreType.DMA],
    )(part)


def _all_reduce_small(v, name):
    shape = v.shape

    def body(v_ref, out_ref, buf, send_sems, recv_sems):
        x, y, cc = _place()
        me = 4 * x + 2 * y + cc
        buf[me] = v_ref[...]
        flips = [(a, b, d) for a in (0, 1) for b in (0, 1) for d in (0, 1)][1:]
        copies = []
        for k, (a, b, d) in enumerate(flips):
            peer = (x ^ a, y ^ b, cc ^ d)
            copies.append(pltpu.make_async_remote_copy(
                src_ref=v_ref, dst_ref=buf.at[me],
                send_sem=send_sems.at[k], recv_sem=recv_sems.at[k], device_id=peer, device_id_type=MESH))
        for cp in copies:
            cp.start()
        for k, (a, b, d) in enumerate(flips):
            peer_id = 4 * (x ^ a) + 2 * (y ^ b) + (cc ^ d)
            pltpu.make_async_remote_copy(
                src_ref=v_ref, dst_ref=buf.at[peer_id],
                send_sem=send_sems.at[k], recv_sem=recv_sems.at[k], device_id=(x, y, cc), device_id_type=MESH
            ).wait_recv()
        for cp in copies:
            cp.wait_send()
        total = buf[0]
        for k in range(1, N_DEV):
            total = total + buf[k]
        out_ref[...] = total

    vm = pl.BlockSpec(memory_space=pltpu.VMEM)
    return pl.pallas_call(
        body, name=name,
        out_shape=jax.ShapeDtypeStruct(shape, F32),
        in_specs=[vm], out_specs=vm,
        scratch_shapes=[pltpu.VMEM((N_DEV,) + shape, F32), pltpu.SemaphoreType.DMA((7,)), pltpu.SemaphoreType.DMA((7,))],
    )(v)


def _pack_rows(shapes):
    offs, off = {}, 0
    for n in BIG:
        rows = shapes[n][0] * shapes[n][1] // PACK_COLS
        offs[n] = (off, rows)
        off += rows
    total = -(-off // 512) * 512
    return offs, total


def _pack(arrs, offs, total):
    parts = [arrs[n].reshape(-1, PACK_COLS) for n in BIG]
    used = sum(p.shape[0] for p in parts)
    parts.append(jnp.zeros((total - used, PACK_COLS), F32))
    return jnp.concatenate(parts, axis=0)


def _unpack_shard(packed, offs, shapes):
    return {n: packed[offs[n][0]:offs[n][0] + offs[n][1]].reshape(shapes[n]) for n in BIG}


def _unpack_full(gathered, offs, shapes):
    out = {}
    for n in BIG:
        o, rows = offs[n]
        kk, nn = shapes[n]
        blk = gathered[:, o:o + rows, :].reshape(N_DEV, kk, nn)
        if n in COL_SHARDED:
            out[n] = blk.transpose(1, 0, 2).reshape(kk, N_DEV * nn)
        else:
            out[n] = blk.reshape(N_DEV * kk, nn)
    return out


def _pack_grads(grads, offs, total, shapes):
    parts = []
    for n in BIG:
        kk, nn = shapes[n]
        g = grads[n]
        if n in COL_SHARDED:
            g = g.reshape(kk, N_DEV, nn).transpose(1, 0, 2)
        parts.append(g.reshape(N_DEV, -1, PACK_COLS))
    used = sum(p.shape[1] for p in parts)
    parts.append(jnp.zeros((N_DEV, total - used, PACK_COLS), F32))
    return jnp.concatenate(parts, axis=1)


SMALL_ROWS = 8


def _pack_small(arrs):
    rows = [arrs[n] for n in SMALL[:5]]
    last = jnp.concatenate([arrs["q_norm1"], arrs["kv_norm1"], arrs["b_f2"]])
    rows.append(jnp.pad(last, (0, PACK_COLS - last.shape[0])))
    rows += [jnp.zeros((PACK_COLS,), F32)] * (SMALL_ROWS - len(rows))
    return jnp.stack(rows)


def _unpack_small(p):
    out = {n: p[k] for k, n in enumerate(SMALL[:5])}
    out["q_norm1"] = p[5, :MLA_Q_RANK]
    out["kv_norm1"] = p[5, MLA_Q_RANK:MLA_Q_RANK + MLA_KV_RANK]
    out["b_f2"] = p[5, MLA_Q_RANK + MLA_KV_RANK:MLA_Q_RANK + MLA_KV_RANK + N_HEADS]
    return out


N_QKV = 3 * D_INNER
N_MAIN = 4 * D_INNER


def _rope(x, pos):
    r = x.shape[-1]
    inv_freq = ROPE_BASE ** (-jnp.arange(0, r, 2, dtype=F32) / r)
    ang = pos.astype(F32)[:, None, None] * inv_freq
    cos, sin = jnp.cos(ang), jnp.sin(ang)
    x1, x2 = x[..., : r // 2], x[..., r // 2:]
    return jnp.concatenate([x1 * cos - x2 * sin, x1 * sin + x2 * cos], axis=-1)


def _forward_loss(carriers, small, x, wfull, pos, target):
    s = x.shape[0]

    def in_proj(h, w_in, tag):
        qkv = mm(h, wfull[w_in], carriers[w_in + "_qkv"], 0, N_QKV, BF16, name=tag + "_qkv")
        gate = mm(h, wfull[w_in], carriers[w_in + "_gate"], N_QKV, D_INNER, F32, name=tag + "_g")
        return qkv, gate

    def sb_layer(x, ln, w_in, w_out, tag):
        h = rmsnorm(x, small[ln], tag + "_ln")
        qkv, gate = in_proj(h, w_in, tag)
        return x + mm(sb_core(qkv, gate, tag), wfull[w_out], carriers[w_out], name=tag + "_out")

    x = sb_layer(x, "ln0", "w_in0", "w_out0", "l0")

    h = rmsnorm(x, small["ln1"], "l1_ln")
    proj = mm(h, wfull["w_in1"], carriers["w_in1"], name="l1_in")
    i1, i2, i3 = MLA_Q_RANK, MLA_Q_RANK + MLA_KV_RANK, MLA_Q_RANK + MLA_KV_RANK + MLA_ROPE
    q = mm(rmsnorm(proj[:, :i1], small["q_norm1"], "l1_qn"), wfull["w_qb1"], carriers["w_qb1"], name="l1_qb")
    q = q.reshape(s, N_HEADS, MLA_NOPE + MLA_ROPE)
    kv = mm(rmsnorm(proj[:, i1:i2], small["kv_norm1"], "l1_kvn"), wfull["w_kvb1"], carriers["w_kvb1"], name="l1_kvb")
    kv = kv.reshape(s, N_HEADS, MLA_NOPE + HEAD_DIM)
    k_rope = _rope(proj[:, i2:i3][:, None, :], pos)
    pad = jnp.zeros((s, N_HEADS, MLA_QK_PAD - MLA_NOPE - MLA_ROPE), F32)
    qc = jnp.concatenate([q[..., :MLA_NOPE], _rope(q[..., MLA_NOPE:], pos), pad], axis=-1)
    kc = jnp.concatenate([kv[..., :MLA_NOPE], jnp.broadcast_to(k_rope, (s, N_HEADS, MLA_ROPE)), pad], axis=-1)
    y = mla_core(qc.reshape(s, -1), kc.reshape(s, -1), kv[..., MLA_NOPE:].reshape(s, -1), proj[:, i3:], "l1")
    x = x + mm(y, wfull["w_out1"], carriers["w_out1"], name="l1_out")

    h = rmsnorm(x, small["ln2"], "l2_ln")
    qkv, gate = in_proj(h, "w_in2", "l2")
    f_logit = mm(h, wfull["w_in2"][:, N_MAIN:], carriers["w_in2_f"], name="l2_f") + small["b_f2"]
    c = jnp.cumsum(jax.nn.log_sigmoid(f_logit), axis=0)
    x = x + mm(fox_core(qkv, gate, c, "l2"), wfull["w_out2"], carriers["w_out2"], name="l2_out")

    x = sb_layer(x, "ln3", "w_in3", "w_out3", "l3")
    return sq_loss(rmsnorm(x, small["final_norm"], "final_ln"), target)


def kernel(x, positions, ln0, w_in0, w_out0, ln1, w_in1, q_norm1, w_qb1, kv_norm1, w_kvb1, w_out1, ln2, w_in2, b_f2, w_out2, ln3, w_in3, w_out3, final_norm, loss_target, m_ln0, m_w_in0, m_w_out0, m_ln1, m_w_in1, m_q_norm1, m_w_qb1, m_kv_norm1, m_w_kvb1, m_w_out1, m_ln2, m_w_in2, m_b_f2, m_w_out2, m_ln3, m_w_in3, m_w_out3, m_final_norm, v_ln0, v_w_in0, v_w_out0, v_ln1, v_w_in1, v_q_norm1, v_w_qb1, v_kv_norm1, v_w_kvb1, v_w_out1, v_ln2, v_w_in2, v_b_f2, v_w_out2, v_ln3, v_w_in3, v_w_out3, v_final_norm):
    args = dict(locals())
    w = {n: args[n] for n in ALL_W}
    m = {n: args["m_" + n] for n in ALL_W}
    v = {n: args["v_" + n] for n in ALL_W}
    shapes = {n: w[n].shape for n in BIG}
    offs, total = _pack_rows(shapes)

    w_pack = _pack(w, offs, total)
    gathered = _all_gather(_cast_bf16(w_pack, "cast_w"), "gather_w")
    wfull = _unpack_full(gathered, offs, shapes)

    d_model = wfull["w_in0"].shape[0]
    carriers = {n: jnp.zeros(wfull[n].shape, F32) for n in BIG if n not in ("w_in0", "w_in2", "w_in3")}
    for n in ("w_in0", "w_in2", "w_in3"):
        carriers[n + "_qkv"] = jnp.zeros((d_model, N_QKV), F32)
        carriers[n + "_gate"] = jnp.zeros((d_model, D_INNER), F32)
    carriers["w_in2_f"] = jnp.zeros((d_model, wfull["w_in2"].shape[1] - N_MAIN), F32)
    small = {n: w[n] for n in SMALL}

    loss_local, (g_car, g_small, g_x) = jax.value_and_grad(_forward_loss, argnums=(0, 1, 2))(
        carriers, small, x[0], wfull, positions[0], loss_target[0])
    loss = lax.psum(loss_local, ("x", "y", "c"))

    g_full = {n: g_car[n] for n in BIG if n not in ("w_in0", "w_in2", "w_in3")}
    for n in ("w_in0", "w_in3"):
        g_full[n] = jnp.concatenate([g_car[n + "_qkv"], g_car[n + "_gate"]], axis=1)
    g_full["w_in2"] = jnp.concatenate([g_car["w_in2_qkv"], g_car["w_in2_gate"], g_car["w_in2_f"]], axis=1)
    g_pack = _pack_grads(g_full, offs, total, shapes)
    core_idx = lax.axis_index("c").astype(jnp.int32).reshape(1)
    chip_part = _pair_sum(core_idx, g_pack, _pair_exchange(g_pack, "pair_exchange"), "pair_sum")
    by_chip = _chip_exchange(chip_part, "chip_exchange")
    g_sh, d_sh, m_sh, v_sh = _adamw(w_pack, by_chip, _pack(m, offs, total), _pack(v, offs, total), "adamw")
    big = [_unpack_shard(t, offs, shapes) for t in (g_sh, d_sh, m_sh, v_sh)]

    g_small_sum = _all_reduce_small(_pack_small(g_small), "reduce_small")
    sm = _adamw(_pack_small(w), g_small_sum[None], _pack_small(m), _pack_small(v), "adamw_small")
    small_out = [_unpack_small(t) for t in sm]

    outs = [loss, g_x[None]]
    for k in range(4):
        outs += [small_out[k][n] if n in small_out[k] else big[k][n] for n in ALL_W]
    return tuple(outs)
```

```python
import jax
import jax.numpy as jnp
from jax import lax
from jax.experimental import pallas as pl
from jax.experimental.pallas import tpu as pltpu

F32 = jnp.float32
BF16 = jnp.bfloat16
MESH = pl.DeviceIdType.MESH

N_DEV = 8
N_HEADS = 16
HEAD_DIM = 128
D_INNER = N_HEADS * HEAD_DIM
MLA_Q_RANK = 256
MLA_KV_RANK = 128
MLA_NOPE = 128
MLA_ROPE = 64
MLA_QK_PAD = 256
MLA_CHUNK = 64
ROPE_BASE = 10000.0
EPS = 1e-6
NEG = -1e30
SB_CUT = 104.0

ADAM_LR = 0.001
ADAM_B1 = 0.9
ADAM_B2 = 0.999
ADAM_EPS = 1e-08
ADAM_WD = 0.01
ADAM_STEP = 10

PACK_COLS = 1024
VMEM_LIMIT = 56 * 1024 * 1024

BIG = ["w_in0", "w_out0", "w_in1", "w_qb1", "w_kvb1", "w_out1", "w_in2", "w_out2", "w_in3", "w_out3"]
COL_SHARDED = {"w_in0", "w_in1", "w_qb1", "w_kvb1", "w_in2", "w_in3"}
SMALL = ["ln0", "ln1", "ln2", "ln3", "final_norm", "q_norm1", "kv_norm1", "b_f2"]
ALL_W = ["ln0", "w_in0", "w_out0", "ln1", "w_in1", "q_norm1", "w_qb1", "kv_norm1", "w_kvb1", "w_out1",
         "ln2", "w_in2", "b_f2", "w_out2", "ln3", "w_in3", "w_out3", "final_norm"]


def _cparams(sem=None):
    return pltpu.CompilerParams(dimension_semantics=sem, vmem_limit_bytes=VMEM_LIMIT)


def _tile(dim, cap, align):
    if dim <= cap:
        return dim
    t = (cap // align) * align
    while t >= align:
        if dim % t == 0:
            return t
        t -= align
    return dim


def _dot(a, b, dims):
    return lax.dot_general(a, b, (dims, ((), ())), preferred_element_type=F32)


def _dot_nn(a, b):
    return _dot(a, b, ((1,), (0,)))


def _dot_nt(a, b):
    return _dot(a, b, ((1,), (1,)))


def _dot_tn(a, b):
    return _dot(a, b, ((0,), (0,)))


def _transpose_bf16(x):
    return x.astype(F32).T.astype(BF16)


def _matmul(a, b, mode, col0=0, n_cols=None, out_dtype=F32, name="mm"):
    if mode == "nn":
        m, r = a.shape
        n = n_cols or b.shape[1]
        tm, tn, tr = _tile(m, 512, 8), _tile(n, 1024, 128), _tile(r, 1024, 128)
        c0 = col0 // tn
        a_spec = pl.BlockSpec((tm, tr), lambda i, j, k: (i, k))
        b_spec = pl.BlockSpec((tr, tn), lambda i, j, k: (k, j + c0))
        dims = ((1,), (0,))
        assert col0 % tn == 0
    elif mode == "nt":
        m, r = a.shape
        n = b.shape[0]
        tm, tn, tr = _tile(m, 512, 8), _tile(n, 1024, 128), _tile(r, 1024, 128)
        c0 = col0 // tr
        a_spec = pl.BlockSpec((tm, tr), lambda i, j, k: (i, k))
        b_spec = pl.BlockSpec((tn, tr), lambda i, j, k: (j, k + c0))
        dims = ((1,), (1,))
        assert col0 % tr == 0
    else:
        r, m = a.shape
        n = b.shape[1]
        tm, tn, tr = _tile(m, 1024, 128), _tile(n, 1024, 128), _tile(r, 512, 16)
        a_spec = pl.BlockSpec((tr, tm), lambda i, j, k: (k, i))
        b_spec = pl.BlockSpec((tr, tn), lambda i, j, k: (k, j))
        dims = ((0,), (0,))
    nr = r // tr

    def body(a_ref, b_ref, o_ref, acc_ref):
        k = pl.program_id(2)

        @pl.when(k == 0)
        def _():
            acc_ref[...] = jnp.zeros_like(acc_ref)

        acc_ref[...] += _dot(a_ref[...].astype(BF16), b_ref[...].astype(BF16), dims)

        @pl.when(k == nr - 1)
        def _():
            o_ref[...] = acc_ref[...].astype(out_dtype)

    return pl.pallas_call(
        body,
        name=name,
        grid=(m // tm, n // tn, nr),
        in_specs=[a_spec, b_spec],
        out_specs=pl.BlockSpec((tm, tn), lambda i, j, k: (i, j)),
        out_shape=jax.ShapeDtypeStruct((m, n), out_dtype),
        scratch_shapes=[pltpu.VMEM((tm, tn), F32)],
        compiler_params=_cparams(("parallel", "parallel", "arbitrary")),
    )(a, b)


def mm(a, w, carrier, col0=0, n_cols=None, out_dtype=F32, name="mm"):
    @jax.custom_vjp
    def f(a, w, carrier):
        return _matmul(a, w, "nn", col0, n_cols, out_dtype, name + "_fwd")

    def fwd(a, w, carrier):
        return _matmul(a, w, "nn", col0, n_cols, out_dtype, name + "_fwd"), (a, w)

    def bwd(res, g):
        a, w = res
        da = _matmul(g, w, "nt", col0, None, F32, name + "_dx")
        dw = _matmul(a, g, "tn", 0, None, F32, name + "_dw")
        return da, jnp.zeros_like(w), dw

    f.defvjp(fwd, bwd)
    return f(a, w, carrier)


def _rms_fwd(x, g, name):
    s, d = x.shape
    tm = _tile(s, 512, 8)

    def body(x_ref, g_ref, y_ref):
        x = x_ref[...]
        r = lax.rsqrt(jnp.mean(x * x, axis=-1, keepdims=True) + EPS)
        y_ref[...] = x * r * g_ref[...]

    return pl.pallas_call(
        body, name=name, grid=(s // tm,),
        in_specs=[pl.BlockSpec((tm, d), lambda i: (i, 0)), pl.BlockSpec((1, d), lambda i: (0, 0))],
        out_specs=pl.BlockSpec((tm, d), lambda i: (i, 0)),
        out_shape=jax.ShapeDtypeStruct((s, d), F32),
        compiler_params=_cparams(("parallel",)),
    )(x, g)


def _rms_bwd(x, g, dy, name):
    s, d = x.shape
    tm = _tile(s, 512, 8)

    def body(x_ref, g_ref, dy_ref, dx_ref, dg_ref):
        @pl.when(pl.program_id(0) == 0)
        def _():
            dg_ref[...] = jnp.zeros_like(dg_ref)

        x = x_ref[...]
        dy = dy_ref[...]
        r = lax.rsqrt(jnp.mean(x * x, axis=-1, keepdims=True) + EPS)
        xh = x * r
        dg_ref[...] += jnp.sum(dy * xh, axis=0, keepdims=True)
        dxh = dy * g_ref[...]
        dx_ref[...] = r * (dxh - xh * jnp.mean(dxh * xh, axis=-1, keepdims=True))

    return pl.pallas_call(
        body, name=name, grid=(s // tm,),
        in_specs=[pl.BlockSpec((tm, d), lambda i: (i, 0)), pl.BlockSpec((1, d), lambda i: (0, 0)),
                  pl.BlockSpec((tm, d), lambda i: (i, 0))],
        out_specs=[pl.BlockSpec((tm, d), lambda i: (i, 0)), pl.BlockSpec((1, d), lambda i: (0, 0))],
        out_shape=[jax.ShapeDtypeStruct((s, d), F32), jax.ShapeDtypeStruct((1, d), F32)],
        compiler_params=_cparams(("arbitrary",)),
    )(x, g, dy)


def rmsnorm(x, g, name="rms"):
    @jax.custom_vjp
    def f(x, g):
        return _rms_fwd(x, g.reshape(1, -1), name + "_fwd")

    def fwd(x, g):
        return _rms_fwd(x, g.reshape(1, -1), name + "_fwd"), (x, g)

    def bwd(res, dy):
        x, g = res
        dx, dg = _rms_bwd(x, g.reshape(1, -1), dy, name + "_bwd")
        return dx, dg.reshape(-1)

    f.defvjp(fwd, bwd)
    return f(x, g)


def _gate_fwd(o, gate, name):
    s = o.shape[0]
    tm = 256

    def body(o_ref, g_ref, y_ref):
        g = g_ref[...]
        y_ref[...] = o_ref[...] * (g / (1.0 + jnp.exp(-g)))

    row = pl.BlockSpec((tm, D_INNER), lambda i: (i, 0))
    return pl.pallas_call(
        body, name=name, grid=(s // tm,),
        in_specs=[row, row], out_specs=row,
        out_shape=jax.ShapeDtypeStruct((s, D_INNER), F32),
        compiler_params=_cparams(("parallel",)),
    )(o, gate)


def _gate_bwd(dy, o, gate, name):
    s = o.shape[0]
    tm = 256

    def body(dy_ref, o_ref, g_ref, do_ref, dg_ref, dl_ref):
        g = g_ref[...]
        o = o_ref[...]
        dy = dy_ref[...]
        sg = 1.0 / (1.0 + jnp.exp(-g))
        do = dy * (g * sg)
        do_ref[...] = do.astype(BF16)
        dg_ref[...] = dy * o * (sg * (1.0 + g * (1.0 - sg)))
        prod = do * o
        for h in range(N_HEADS):
            dl_ref[h] = jnp.sum(prod[:, h * HEAD_DIM:(h + 1) * HEAD_DIM], axis=1, keepdims=True)

    row = pl.BlockSpec((tm, D_INNER), lambda i: (i, 0))
    return pl.pallas_call(
        body, name=name, grid=(s // tm,),
        in_specs=[row, row, row],
        out_specs=[row, row, pl.BlockSpec((N_HEADS, tm, 1), lambda i: (0, i, 0))],
        out_shape=[jax.ShapeDtypeStruct((s, D_INNER), BF16), jax.ShapeDtypeStruct((s, D_INNER), F32),
                   jax.ShapeDtypeStruct((N_HEADS, s, 1), F32)],
        compiler_params=_cparams(("parallel",)),
    )(dy, o, gate)


SB_BLK = 256


def _softplus(z):
    return jnp.maximum(z, 0.0) + jnp.log(1.0 + jnp.exp(-jnp.abs(z)))


def _tri_sum(x, tri):
    hi = x.astype(BF16)
    lo = (x - hi.astype(F32)).astype(BF16)
    return _dot_nn(hi, tri) + _dot_nn(lo, tri)


def _sb_block(q, kb, i, j, a_run, row, col, tri_suffix, scale):
    z = _dot_nt(q, kb) * scale
    mask = (col + j * SB_BLK) < (row + i * SB_BLK)
    sp = _softplus(z)
    ls = jnp.where(mask, -sp, 0.0)
    suffix = _tri_sum(ls, tri_suffix)
    w = jnp.where(mask, jnp.exp(z + suffix + a_run), 0.0)
    return z, mask, sp, ls, w


def _sb_fwd(qkv, name):
    s = qkv.shape[0]
    b = SB_BLK
    nq = s // b
    scale = HEAD_DIM ** -0.5

    def body(q_ref, k_ref, v_ref, o_ref):
        i = pl.program_id(1)
        q = q_ref[...]
        row = lax.broadcasted_iota(jnp.int32, (b, b), 0)
        col = lax.broadcasted_iota(jnp.int32, (b, b), 1)
        tri_suffix = (row >= col).astype(BF16)

        def cond(c):
            j, a_run, _ = c
            return jnp.logical_and(j >= 0, jnp.max(a_run) > -SB_CUT)

        def step(c):
            j, a_run, acc = c
            off = pl.multiple_of(j * b, b)
            kb = k_ref[pl.ds(off, b), :]
            vb = v_ref[pl.ds(off, b), :]
            _, _, _, ls, w = _sb_block(q, kb, i, j, a_run, row, col, tri_suffix, scale)
            acc = acc + _dot_nn(w.astype(BF16), vb)
            return j - 1, a_run + jnp.sum(ls, axis=1, keepdims=True), acc

        _, _, acc = lax.while_loop(cond, step, (i, jnp.zeros((b, 1), F32), jnp.zeros((b, HEAD_DIM), F32)))
        o_ref[...] = acc

    return pl.pallas_call(
        body, name=name, grid=(N_HEADS, nq),
        in_specs=[pl.BlockSpec((b, HEAD_DIM), lambda h, i: (i, h)),
                  pl.BlockSpec((s, HEAD_DIM), lambda h, i: (0, N_HEADS + h)),
                  pl.BlockSpec((s, HEAD_DIM), lambda h, i: (0, 2 * N_HEADS + h))],
        out_specs=pl.BlockSpec((b, HEAD_DIM), lambda h, i: (i, h)),
        out_shape=jax.ShapeDtypeStruct((s, D_INNER), F32),
        compiler_params=_cparams(("parallel", "arbitrary")),
    )(qkv, qkv, qkv)


def _sb_bwd(qkv, do, name):
    s = qkv.shape[0]
    b = SB_BLK
    nq = s // b
    scale = HEAD_DIM ** -0.5

    def body(q_ref, k_ref, v_ref, do_ref, dq_ref, dk_ref, dv_ref, dkt_s, dvt_s, g_buf, sig_buf):
        i = pl.program_id(1)

        @pl.when(i == 0)
        def _():
            dkt_s[...] = jnp.zeros_like(dkt_s)
            dvt_s[...] = jnp.zeros_like(dvt_s)

        q = q_ref[...]
        dob = do_ref[...]
        q_t = _transpose_bf16(q)
        do_t = _transpose_bf16(dob)
        row = lax.broadcasted_iota(jnp.int32, (b, b), 0)
        col = lax.broadcasted_iota(jnp.int32, (b, b), 1)
        tri_suffix = (row >= col).astype(BF16)
        tri_prefix = (row <= col).astype(BF16)

        def cond(c):
            j, a_run = c
            return jnp.logical_and(j >= 0, jnp.max(a_run) > -SB_CUT)

        def sweep(c):
            j, a_run = c
            off = pl.multiple_of(j * b, b)
            kb = k_ref[pl.ds(off, b), :]
            vb = v_ref[pl.ds(off, b), :]
            z, _, sp, ls, w = _sb_block(q, kb, i, j, a_run, row, col, tri_suffix, scale)
            g_buf[j] = w * _dot_nt(dob, vb)
            sig_buf[j] = jnp.exp(z - sp)
            dvt_s[j] += _dot_nn(do_t, w.astype(BF16))
            return j - 1, a_run + jnp.sum(ls, axis=1, keepdims=True)

        j_end, _ = lax.while_loop(cond, sweep, (i, jnp.zeros((b, 1), F32)))

        def back(j, c):
            g_run, dq = c
            off = pl.multiple_of(j * b, b)
            kb = k_ref[pl.ds(off, b), :]
            g = g_buf[j]
            g_incl = g_run + _tri_sum(g, tri_prefix)
            mask = (col + j * b) < (row + i * b)
            dz = jnp.where(mask, (g - sig_buf[j] * g_incl) * scale, 0.0).astype(BF16)
            dkt_s[j] += _dot_nn(q_t, dz)
            return g_run + jnp.sum(g, axis=1, keepdims=True), dq + _dot_nn(dz, kb)

        _, dq = lax.fori_loop(j_end + 1, i + 1, back, (jnp.zeros((b, 1), F32), jnp.zeros((b, HEAD_DIM), F32)))
        dq_ref[...] = dq.astype(BF16)

        @pl.when(i == nq - 1)
        def _():
            for jb in range(nq):
                dk_ref[jb * b:(jb + 1) * b, :] = dkt_s[jb].T.astype(BF16)
                dv_ref[jb * b:(jb + 1) * b, :] = dvt_s[jb].T.astype(BF16)

    blk = pl.BlockSpec((b, HEAD_DIM), lambda h, i: (i, h))
    head = pl.BlockSpec((s, HEAD_DIM), lambda h, i: (0, h))
    return pl.pallas_call(
        body, name=name, grid=(N_HEADS, nq),
        in_specs=[blk,
                  pl.BlockSpec((s, HEAD_DIM), lambda h, i: (0, N_HEADS + h)),
                  pl.BlockSpec((s, HEAD_DIM), lambda h, i: (0, 2 * N_HEADS + h)),
                  blk],
        out_specs=[blk, head, head],
        out_shape=[jax.ShapeDtypeStruct((s, D_INNER), BF16)] * 3,
        scratch_shapes=[pltpu.VMEM((nq, HEAD_DIM, b), F32), pltpu.VMEM((nq, HEAD_DIM, b), F32),
                        pltpu.VMEM((nq, b, b), F32), pltpu.VMEM((nq, b, b), F32)],
        compiler_params=_cparams(("arbitrary", "arbitrary")),
    )(qkv, qkv, qkv, do)


def sb_core(qkv, gate, name):
    def run(qkv, gate):
        o = _sb_fwd(qkv, name + "_fwd")
        return _gate_fwd(o, gate, name + "_gate"), o

    @jax.custom_vjp
    def f(qkv, gate):
        return run(qkv, gate)[0]

    def fwd(qkv, gate):
        y, o = run(qkv, gate)
        return y, (qkv, gate, o)

    def bwd(res, dy):
        qkv, gate, o = res
        do, dgate, _ = _gate_bwd(dy, o, gate, name + "_gate_bwd")
        dq, dk, dv = _sb_bwd(qkv, do, name + "_bwd")
        return jnp.concatenate([dq, dk, dv], axis=1), dgate

    f.defvjp(fwd, bwd)
    return f(qkv, gate)


SM_BLK = 256


SM_FWD_KEYS = 1024
SM_BWD_KEYS = 512


def _sm_mask(i, jw, keys, chunk_shift):
    row = lax.broadcasted_iota(jnp.int32, (SM_BLK, keys), 0) + i * SM_BLK
    col = lax.broadcasted_iota(jnp.int32, (SM_BLK, keys), 1) + jw * keys
    return (col >> chunk_shift) <= (row >> chunk_shift)


def _sm_fwd(qa, ka, va, ccol, crow, dqk, qo, ko, vo, chunk_shift, scale, name):
    s = qa.shape[0]
    b = SM_BLK
    keys = min(SM_FWD_KEYS, s)
    per = keys // b
    nq = s // b
    has_bias = ccol is not None

    def body(*refs):
        if has_bias:
            q_ref, k_ref, v_ref, cc_ref, cr_ref, o_ref, lse_ref, m_s, l_s, acc_s = refs
        else:
            q_ref, k_ref, v_ref, o_ref, lse_ref, m_s, l_s, acc_s = refs
        i = pl.program_id(1)
        q = q_ref[...]
        m_s[...] = jnp.full_like(m_s, NEG)
        l_s[...] = jnp.zeros_like(l_s)
        acc_s[...] = jnp.zeros_like(acc_s)

        def sweep(jw, masked):
            off = pl.multiple_of(jw * keys, keys)
            z = _dot_nt(q, k_ref[pl.ds(off, keys), :]) * scale
            if has_bias:
                z = z + cc_ref[...] - cr_ref[jw]
            if masked:
                z = jnp.where(_sm_mask(i, jw, keys, chunk_shift), z, NEG)
            m_old = m_s[...]
            m_new = jnp.maximum(m_old, jnp.max(z, axis=1, keepdims=True))
            alpha = jnp.exp(m_old - m_new)
            p = jnp.exp(z - m_new)
            l_s[...] = alpha * l_s[...] + jnp.sum(p, axis=1, keepdims=True)
            acc_s[...] = alpha * acc_s[...] + _dot_nn(p.astype(BF16), v_ref[pl.ds(off, keys), :])
            m_s[...] = m_new

        def full(jw, carry):
            sweep(jw, False)
            return carry

        lax.fori_loop(0, i // per, full, 0)
        sweep(i // per, True)
        o_ref[...] = acc_s[...] / l_s[...]
        lse_ref[...] = m_s[...] + jnp.log(l_s[...])

    in_specs = [pl.BlockSpec((b, dqk), lambda h, i: (i, qo + h)),
                pl.BlockSpec((s, dqk), lambda h, i: (0, ko + h)),
                pl.BlockSpec((s, HEAD_DIM), lambda h, i: (0, vo + h))]
    args = [qa, ka, va]
    if has_bias:
        in_specs += [pl.BlockSpec((None, b, 1), lambda h, i: (h, i, 0)),
                     pl.BlockSpec((None, s // keys, 1, keys), lambda h, i: (h, 0, 0, 0))]
        args += [ccol, crow]
    return pl.pallas_call(
        body, name=name, grid=(N_HEADS, nq),
        in_specs=in_specs,
        out_specs=[pl.BlockSpec((b, HEAD_DIM), lambda h, i: (i, h)),
                   pl.BlockSpec((None, b, 1), lambda h, i: (h, i, 0))],
        out_shape=[jax.ShapeDtypeStruct((s, D_INNER), F32), jax.ShapeDtypeStruct((N_HEADS, s, 1), F32)],
        scratch_shapes=[pltpu.VMEM((b, 1), F32), pltpu.VMEM((b, 1), F32), pltpu.VMEM((b, HEAD_DIM), F32)],
        compiler_params=_cparams(("parallel", "arbitrary")),
    )(*args)


def _sm_bwd(qa, ka, va, do, lse, delta, ccol, crow, dqk, qo, ko, vo, chunk_shift, scale, grad_dtype, name):
    s = qa.shape[0]
    b = SM_BLK
    keys = min(SM_BWD_KEYS, s)
    per = keys // b
    nq = s // b
    nk = s // keys
    has_bias = ccol is not None

    def body(*refs):
        if has_bias:
            (q_ref, k_ref, v_ref, do_ref, lse_ref, dl_ref, cc_ref, cr_ref,
             dq_ref, dk_ref, dv_ref, dc_ref, dr_ref, dq_s, dkt_s, dvt_s, dc_s, dr_s) = refs
        else:
            (q_ref, k_ref, v_ref, do_ref, lse_ref, dl_ref,
             dq_ref, dk_ref, dv_ref, dq_s, dkt_s, dvt_s) = refs
        i = pl.program_id(1)

        @pl.when(i == 0)
        def _():
            dkt_s[...] = jnp.zeros_like(dkt_s)
            dvt_s[...] = jnp.zeros_like(dvt_s)
            if has_bias:
                dc_s[...] = jnp.zeros_like(dc_s)

        q = q_ref[...]
        dob = do_ref[...]
        q_t = _transpose_bf16(q)
        do_t = _transpose_bf16(dob)
        lse = lse_ref[...]
        delta = dl_ref[...]
        dq_s[...] = jnp.zeros_like(dq_s)
        if has_bias:
            dr_s[...] = jnp.zeros_like(dr_s)

        def sweep(jw, masked):
            off = pl.multiple_of(jw * keys, keys)
            kb = k_ref[pl.ds(off, keys), :]
            z = _dot_nt(q, kb) * scale
            if has_bias:
                z = z + cc_ref[...] - cr_ref[jw]
            p = jnp.exp(z - lse)
            if masked:
                p = jnp.where(_sm_mask(i, jw, keys, chunk_shift), p, 0.0)
            dvt_s[jw] += _dot_nn(do_t, p.astype(BF16))
            dz = p * (_dot_nt(dob, v_ref[pl.ds(off, keys), :]) - delta)
            if has_bias:
                dc_s[jw] += jnp.sum(dz, axis=0, keepdims=True)
                dr_s[...] += jnp.sum(dz, axis=1, keepdims=True)
            dzs = (dz * scale).astype(BF16)
            dkt_s[jw] += _dot_nn(q_t, dzs)
            dq_s[...] += _dot_nn(dzs, kb)

        def full(jw, carry):
            sweep(jw, False)
            return carry

        lax.fori_loop(0, i // per, full, 0)
        sweep(i // per, True)
        dq_ref[...] = dq_s[...].astype(grad_dtype)
        if has_bias:
            dr_ref[...] = dr_s[...]

        @pl.when(i == nq - 1)
        def _():
            for jw in range(nk):
                dk_ref[jw * keys:(jw + 1) * keys, :] = dkt_s[jw].T.astype(grad_dtype)
                dv_ref[jw * keys:(jw + 1) * keys, :] = dvt_s[jw].T.astype(grad_dtype)
            if has_bias:
                dc_ref[...] = dc_s[...]

    vec = pl.BlockSpec((None, b, 1), lambda h, i: (h, i, 0))
    in_specs = [pl.BlockSpec((b, dqk), lambda h, i: (i, qo + h)),
                pl.BlockSpec((s, dqk), lambda h, i: (0, ko + h)),
                pl.BlockSpec((s, HEAD_DIM), lambda h, i: (0, vo + h)),
                pl.BlockSpec((b, HEAD_DIM), lambda h, i: (i, h)),
                vec, vec]
    args = [qa, ka, va, do, lse, delta]
    out_specs = [pl.BlockSpec((b, dqk), lambda h, i: (i, h)),
                 pl.BlockSpec((s, dqk), lambda h, i: (0, h)),
                 pl.BlockSpec((s, HEAD_DIM), lambda h, i: (0, h))]
    out_shape = [jax.ShapeDtypeStruct((s, N_HEADS * dqk), grad_dtype),
                 jax.ShapeDtypeStruct((s, N_HEADS * dqk), grad_dtype),
                 jax.ShapeDtypeStruct((s, D_INNER), grad_dtype)]
    scratch = [pltpu.VMEM((b, dqk), F32), pltpu.VMEM((nk, dqk, keys), F32), pltpu.VMEM((nk, HEAD_DIM, keys), F32)]
    if has_bias:
        key_vec = pl.BlockSpec((None, nk, 1, keys), lambda h, i: (h, 0, 0, 0))
        in_specs += [vec, key_vec]
        args += [ccol, crow]
        out_specs += [key_vec, vec]
        out_shape += [jax.ShapeDtypeStruct((N_HEADS, nk, 1, keys), F32), jax.ShapeDtypeStruct((N_HEADS, s, 1), F32)]
        scratch += [pltpu.VMEM((nk, 1, keys), F32), pltpu.VMEM((b, 1), F32)]
    return pl.pallas_call(
        body, name=name, grid=(N_HEADS, nq),
        in_specs=in_specs, out_specs=out_specs, out_shape=out_shape, scratch_shapes=scratch,
        compiler_params=_cparams(("arbitrary", "arbitrary")),
    )(*args)


def fox_core(qkv, gate, c, name):
    s = qkv.shape[0]
    scale = HEAD_DIM ** -0.5
    cfg = dict(dqk=HEAD_DIM, qo=0, ko=N_HEADS, vo=2 * N_HEADS, chunk_shift=0, scale=scale)

    def layouts(c, keys):
        ct = c.T
        keys = min(keys, s)
        return ct.reshape(N_HEADS, s, 1), ct.reshape(N_HEADS, s // keys, 1, keys)

    def run(qkv, gate, c):
        ccol, crow = layouts(c, SM_FWD_KEYS)
        o, lse = _sm_fwd(qkv, qkv, qkv, ccol, crow, name=name + "_fwd", **cfg)
        return _gate_fwd(o, gate, name + "_gate"), o, lse

    @jax.custom_vjp
    def f(qkv, gate, c):
        return run(qkv, gate, c)[0]

    def fwd(qkv, gate, c):
        y, o, lse = run(qkv, gate, c)
        return y, (qkv, gate, c, o, lse)

    def bwd(res, dy):
        qkv, gate, c, o, lse = res
        ccol, crow = layouts(c, SM_BWD_KEYS)
        do, dgate, delta = _gate_bwd(dy, o, gate, name + "_gate_bwd")
        dq, dk, dv, colsum, rowsum = _sm_bwd(qkv, qkv, qkv, do, lse, delta, ccol, crow,
                                             grad_dtype=BF16, name=name + "_bwd", **cfg)
        dc = (rowsum.reshape(N_HEADS, s) - colsum.reshape(N_HEADS, s)).T
        return jnp.concatenate([dq, dk, dv], axis=1), dgate, dc

    f.defvjp(fwd, bwd)
    return f(qkv, gate, c)


def mla_core(qc, kc, v, gate, name):
    scale = (MLA_NOPE + MLA_ROPE) ** -0.5
    cfg = dict(dqk=MLA_QK_PAD, qo=0, ko=0, vo=0, chunk_shift=MLA_CHUNK.bit_length() - 1, scale=scale)

    def run(qc, kc, v, gate):
        o, lse = _sm_fwd(qc, kc, v, None, None, name=name + "_fwd", **cfg)
        return _gate_fwd(o, gate, name + "_gate"), o, lse

    @jax.custom_vjp
    def f(qc, kc, v, gate):
        return run(qc.astype(BF16), kc.astype(BF16), v.astype(BF16), gate)[0]

    def fwd(qc, kc, v, gate):
        qc, kc, v = qc.astype(BF16), kc.astype(BF16), v.astype(BF16)
        y, o, lse = run(qc, kc, v, gate)
        return y, (qc, kc, v, gate, o, lse)

    def bwd(res, dy):
        qc, kc, v, gate, o, lse = res
        do, dgate, delta = _gate_bwd(dy, o, gate, name + "_gate_bwd")
        dq, dk, dv = _sm_bwd(qc, kc, v, do, lse, delta, None, None, grad_dtype=F32, name=name + "_bwd", **cfg)
        return dq, dk, dv, dgate

    f.defvjp(fwd, bwd)
    return f(qc, kc, v, gate)


def _sq_loss_call(y, t, name):
    s, d = y.shape
    tm = _tile(s, 512, 8)

    def body(y_ref, t_ref, l_ref, e_ref):
        @pl.when(pl.program_id(0) == 0)
        def _():
            l_ref[...] = jnp.zeros_like(l_ref)

        e = y_ref[...] - t_ref[...]
        e_ref[...] = e * (1.0 / d)
        part = jnp.sum(jnp.sum(e * e, axis=1, keepdims=True), axis=0, keepdims=True)
        l_ref[...] += jnp.broadcast_to(part * (0.5 / d), l_ref.shape)

    row = pl.BlockSpec((tm, d), lambda i: (i, 0))
    return pl.pallas_call(
        body, name=name, grid=(s // tm,),
        in_specs=[row, row],
        out_specs=[pl.BlockSpec((8, 128), lambda i: (0, 0)), row],
        out_shape=[jax.ShapeDtypeStruct((8, 128), F32), jax.ShapeDtypeStruct((s, d), F32)],
        compiler_params=_cparams(("arbitrary",)),
    )(y, t)


@jax.custom_vjp
def sq_loss(y, t):
    return _sq_loss_call(y, t, "loss_fwd")[0][0, 0]


def _sq_loss_fwd(y, t):
    l, e = _sq_loss_call(y, t, "loss_fwd")
    return l[0, 0], e


def _sq_loss_bwd(e, g):
    return g * e, jnp.zeros_like(e)


sq_loss.defvjp(_sq_loss_fwd, _sq_loss_bwd)


def _cast_bf16(x, name):
    r, c = x.shape
    tb = _tile(r, 512, 16)

    def body(x_ref, o_ref):
        o_ref[...] = x_ref[...].astype(BF16)

    return pl.pallas_call(
        body, name=name, grid=(r // tb,),
        in_specs=[pl.BlockSpec((tb, c), lambda i: (i, 0))],
        out_specs=pl.BlockSpec((tb, c), lambda i: (i, 0)),
        out_shape=jax.ShapeDtypeStruct((r, c), BF16),
        compiler_params=_cparams(("parallel",)),
    )(x)


def _pair_sum(core_idx, g, recv, name):
    _, r, c = g.shape
    tb = _tile(r, 512, 8)

    def body(c_ref, g_ref, r_ref, o_ref):
        o_ref[...] = g_ref[...] + r_ref[...]

    return pl.pallas_call(
        body, name=name,
        grid_spec=pltpu.PrefetchScalarGridSpec(
            num_scalar_prefetch=1, grid=(4, r // tb),
            in_specs=[pl.BlockSpec((None, tb, c), lambda q, i, c_ref: (2 * q + c_ref[0], i, 0)),
                      pl.BlockSpec((None, tb, c), lambda q, i, c_ref: (q, i, 0))],
            out_specs=pl.BlockSpec((None, tb, c), lambda q, i, c_ref: (q, i, 0))),
        out_shape=jax.ShapeDtypeStruct((4, r, c), F32),
        compiler_params=_cparams(("parallel", "parallel")),
    )(core_idx, g, recv)


def _adamw(w, parts, m, v, name):
    n, r, c = parts.shape
    tb = _tile(r, 256, 8)
    b1c = 1.0 - ADAM_B1 ** ADAM_STEP
    b2c = 1.0 - ADAM_B2 ** ADAM_STEP

    def body(w_ref, p_ref, m_ref, v_ref, g_ref, d_ref, nm_ref, nv_ref):
        g = p_ref[0]
        for k in range(1, n):
            g = g + p_ref[k]
        m_new = ADAM_B1 * m_ref[...] + (1.0 - ADAM_B1) * g
        v_new = ADAM_B2 * v_ref[...] + (1.0 - ADAM_B2) * (g * g)
        m_hat = m_new / b1c
        v_hat = v_new / b2c
        g_ref[...] = g
        d_ref[...] = -ADAM_LR * (m_hat / (jnp.sqrt(v_hat) + ADAM_EPS) + ADAM_WD * w_ref[...])
        nm_ref[...] = m_new
        nv_ref[...] = v_new

    row = pl.BlockSpec((tb, c), lambda i: (i, 0))
    return pl.pallas_call(
        body, name=name, grid=(r // tb,),
        in_specs=[row, pl.BlockSpec((n, tb, c), lambda i: (0, i, 0)), row, row],
        out_specs=[row] * 4,
        out_shape=[jax.ShapeDtypeStruct((r, c), F32)] * 4,
        compiler_params=_cparams(("parallel",)),
    )(w, parts, m, v)


ANY = pl.BlockSpec(memory_space=pl.ANY)


def _place():
    return lax.axis_index("x"), lax.axis_index("y"), lax.axis_index("c")


def _all_gather(shard, name):
    r, c = shard.shape

    def body(x_ref, out_ref, send_sems, recv_sems, local_sem):
        x, y, cc = _place()
        me, sibling = (x, y, cc), (x, y, 1 - cc)
        chips = [(1 - x, y), (x, 1 - y), (1 - x, 1 - y)]

        def slot(px, py, pc):
            return out_ref.at[4 * px + 2 * py + pc]

        def copy(k, block, to, src=None):
            return pltpu.make_async_remote_copy(
                src_ref=slot(*block) if src is None else src, dst_ref=slot(*block),
                send_sem=send_sems.at[k], recv_sem=recv_sems.at[k], device_id=to, device_id_type=MESH)

        mine = pltpu.make_async_copy(x_ref, slot(*me), local_sem)
        mine.start()
        first = [copy(0, me, sibling, src=x_ref)]
        first += [copy(1 + j, me, (*chip, cc), src=x_ref) for j, chip in enumerate(chips)]
        for cp in first:
            cp.start()
        passed = [copy(4 + j, (*chip, cc), sibling) for j, chip in enumerate(chips)]
        for j, chip in enumerate(chips):
            copy(1 + j, (*chip, cc), me).wait_recv()
            passed[j].start()
        copy(0, sibling, me).wait_recv()
        for j, chip in enumerate(chips):
            copy(4 + j, (*chip, 1 - cc), me).wait_recv()
        for cp in first + passed:
            cp.wait_send()
        mine.wait()

    return pl.pallas_call(
        body, name=name,
        out_shape=jax.ShapeDtypeStruct((N_DEV, r, c), shard.dtype),
        in_specs=[ANY], out_specs=ANY,
        scratch_shapes=[pltpu.SemaphoreType.DMA((7,)), pltpu.SemaphoreType.DMA((7,)), pltpu.SemaphoreType.DMA],
    )(shard)


def _pair_exchange(g, name):
    _, r, c = g.shape

    def body(g_ref, recv_ref, send_sems, recv_sems):
        x, y, cc = _place()
        sibling = (x, y, 1 - cc)
        copies = [pltpu.make_async_remote_copy(
            src_ref=g_ref.at[2 * q + (1 - cc)], dst_ref=recv_ref.at[q],
            send_sem=send_sems.at[q], recv_sem=recv_sems.at[q], device_id=sibling, device_id_type=MESH)
            for q in range(4)]
        for cp in copies:
            cp.start()
        for cp in copies:
            cp.wait_recv()
        for cp in copies:
            cp.wait_send()

    return pl.pallas_call(
        body, name=name,
        out_shape=jax.ShapeDtypeStruct((4, r, c), g.dtype),
        in_specs=[ANY], out_specs=ANY,
        scratch_shapes=[pltpu.SemaphoreType.DMA((4,)), pltpu.SemaphoreType.DMA((4,))],
    )(g)


def _chip_exchange(part, name):
    _, r, c = part.shape

    def body(p_ref, out_ref, send_sems, recv_sems, local_sem):
        x, y, cc = _place()
        mine = 2 * x + y
        others = [(1 - x, y), (x, 1 - y), (1 - x, 1 - y)]
        keep = pltpu.make_async_copy(p_ref.at[mine], out_ref.at[mine], local_sem)
        keep.start()
        sends = []
        for px, py in others:
            q = 2 * px + py
            sends.append(pltpu.make_async_remote_copy(
                src_ref=p_ref.at[q], dst_ref=out_ref.at[mine],
                send_sem=send_sems.at[q], recv_sem=recv_sems.at[mine], device_id=(px, py, cc), device_id_type=MESH))
        for cp in sends:
            cp.start()
        for px, py in others:
            q = 2 * px + py
            pltpu.make_async_remote_copy(
                src_ref=p_ref.at[q], dst_ref=out_ref.at[q],
                send_sem=send_sems.at[q], recv_sem=recv_sems.at[q], device_id=(px, py, cc), device_id_type=MESH
            ).wait_recv()
        for cp in sends:
            cp.wait_send()
        keep.wait()

    return pl.pallas_call(
        body, name=name,
        out_shape=jax.ShapeDtypeStruct((4, r, c), part.dtype),
        in_specs=[ANY], out_specs=ANY,
        scratch_shapes=[pltpu.SemaphoreType.DMA((4,)), pltpu.SemaphoreType.DMA((4,)), pltpu.SemaphoreType.DMA],
    )(part)


def _all_reduce_small(v, name):
    shape = v.shape

    def body(v_ref, out_ref, buf, send_sems, recv_sems):
        x, y, cc = _place()
        me = 4 * x + 2 * y + cc
        buf[me] = v_ref[...]
        flips = [(a, b, d) for a in (0, 1) for b in (0, 1) for d in (0, 1)][1:]
        copies = []
        for k, (a, b, d) in enumerate(flips):
            peer = (x ^ a, y ^ b, cc ^ d)
            copies.append(pltpu.make_async_remote_copy(
                src_ref=v_ref, dst_ref=buf.at[me],
                send_sem=send_sems.at[k], recv_sem=recv_sems.at[k], device_id=peer, device_id_type=MESH))
        for cp in copies:
            cp.start()
        for k, (a, b, d) in enumerate(flips):
            peer_id = 4 * (x ^ a) + 2 * (y ^ b) + (cc ^ d)
            pltpu.make_async_remote_copy(
                src_ref=v_ref, dst_ref=buf.at[peer_id],
                send_sem=send_sems.at[k], recv_sem=recv_sems.at[k], device_id=(x, y, cc), device_id_type=MESH
            ).wait_recv()
        for cp in copies:
            cp.wait_send()
        total = buf[0]
        for k in range(1, N_DEV):
            total = total + buf[k]
        out_ref[...] = total

    vm = pl.BlockSpec(memory_space=pltpu.VMEM)
    return pl.pallas_call(
        body, name=name,
        out_shape=jax.ShapeDtypeStruct(shape, F32),
        in_specs=[vm], out_specs=vm,
        scratch_shapes=[pltpu.VMEM((N_DEV,) + shape, F32), pltpu.SemaphoreType.DMA((7,)), pltpu.SemaphoreType.DMA((7,))],
    )(v)


def _pack_rows(shapes):
    offs, off = {}, 0
    for n in BIG:
        rows = shapes[n][0] * shapes[n][1] // PACK_COLS
        offs[n] = (off, rows)
        off += rows
    total = -(-off // 512) * 512
    return offs, total


def _pack(arrs, offs, total):
    parts = [arrs[n].reshape(-1, PACK_COLS) for n in BIG]
    used = sum(p.shape[0] for p in parts)
    parts.append(jnp.zeros((total - used, PACK_COLS), F32))
    return jnp.concatenate(parts, axis=0)


def _unpack_shard(packed, offs, shapes):
    return {n: packed[offs[n][0]:offs[n][0] + offs[n][1]].reshape(shapes[n]) for n in BIG}


def _unpack_full(gathered, offs, shapes):
    out = {}
    for n in BIG:
        o, rows = offs[n]
        kk, nn = shapes[n]
        blk = gathered[:, o:o + rows, :].reshape(N_DEV, kk, nn)
        if n in COL_SHARDED:
            out[n] = blk.transpose(1, 0, 2).reshape(kk, N_DEV * nn)
        else:
            out[n] = blk.reshape(N_DEV * kk, nn)
    return out


def _pack_grads(grads, offs, total, shapes):
    parts = []
    for n in BIG:
        kk, nn = shapes[n]
        g = grads[n]
        if n in COL_SHARDED:
            g = g.reshape(kk, N_DEV, nn).transpose(1, 0, 2)
        parts.append(g.reshape(N_DEV, -1, PACK_COLS))
    used = sum(p.shape[1] for p in parts)
    parts.append(jnp.zeros((N_DEV, total - used, PACK_COLS), F32))
    return jnp.concatenate(parts, axis=1)


SMALL_ROWS = 8


def _pack_small(arrs):
    rows = [arrs[n] for n in SMALL[:5]]
    last = jnp.concatenate([arrs["q_norm1"], arrs["kv_norm1"], arrs["b_f2"]])
    rows.append(jnp.pad(last, (0, PACK_COLS - last.shape[0])))
    rows += [jnp.zeros((PACK_COLS,), F32)] * (SMALL_ROWS - len(rows))
    return jnp.stack(rows)


def _unpack_small(p):
    out = {n: p[k] for k, n in enumerate(SMALL[:5])}
    out["q_norm1"] = p[5, :MLA_Q_RANK]
    out["kv_norm1"] = p[5, MLA_Q_RANK:MLA_Q_RANK + MLA_KV_RANK]
    out["b_f2"] = p[5, MLA_Q_RANK + MLA_KV_RANK:MLA_Q_RANK + MLA_KV_RANK + N_HEADS]
    return out


N_QKV = 3 * D_INNER
N_MAIN = 4 * D_INNER


def _rope(x, pos):
    r = x.shape[-1]
    inv_freq = ROPE_BASE ** (-jnp.arange(0, r, 2, dtype=F32) / r)
    ang = pos.astype(F32)[:, None, None] * inv_freq
    cos, sin = jnp.cos(ang), jnp.sin(ang)
    x1, x2 = x[..., : r // 2], x[..., r // 2:]
    return jnp.concatenate([x1 * cos - x2 * sin, x1 * sin + x2 * cos], axis=-1)


def _forward_loss(carriers, small, x, wfull, pos, target):
    s = x.shape[0]

    def in_proj(h, w_in, tag):
        qkv = mm(h, wfull[w_in], carriers[w_in + "_qkv"], 0, N_QKV, BF16, name=tag + "_qkv")
        gate = mm(h, wfull[w_in], carriers[w_in + "_gate"], N_QKV, D_INNER, F32, name=tag + "_g")
        return qkv, gate

    def sb_layer(x, ln, w_in, w_out, tag):
        h = rmsnorm(x, small[ln], tag + "_ln")
        qkv, gate = in_proj(h, w_in, tag)
        return x + mm(sb_core(qkv, gate, tag), wfull[w_out], carriers[w_out], name=tag + "_out")

    x = sb_layer(x, "ln0", "w_in0", "w_out0", "l0")

    h = rmsnorm(x, small["ln1"], "l1_ln")
    proj = mm(h, wfull["w_in1"], carriers["w_in1"], name="l1_in")
    i1, i2, i3 = MLA_Q_RANK, MLA_Q_RANK + MLA_KV_RANK, MLA_Q_RANK + MLA_KV_RANK + MLA_ROPE
    q = mm(rmsnorm(proj[:, :i1], small["q_norm1"], "l1_qn"), wfull["w_qb1"], carriers["w_qb1"], name="l1_qb")
    q = q.reshape(s, N_HEADS, MLA_NOPE + MLA_ROPE)
    kv = mm(rmsnorm(proj[:, i1:i2], small["kv_norm1"], "l1_kvn"), wfull["w_kvb1"], carriers["w_kvb1"], name="l1_kvb")
    kv = kv.reshape(s, N_HEADS, MLA_NOPE + HEAD_DIM)
    k_rope = _rope(proj[:, i2:i3][:, None, :], pos)
    pad = jnp.zeros((s, N_HEADS, MLA_QK_PAD - MLA_NOPE - MLA_ROPE), F32)
    qc = jnp.concatenate([q[..., :MLA_NOPE], _rope(q[..., MLA_NOPE:], pos), pad], axis=-1)
    kc = jnp.concatenate([kv[..., :MLA_NOPE], jnp.broadcast_to(k_rope, (s, N_HEADS, MLA_ROPE)), pad], axis=-1)
    y = mla_core(qc.reshape(s, -1), kc.reshape(s, -1), kv[..., MLA_NOPE:].reshape(s, -1), proj[:, i3:], "l1")
    x = x + mm(y, wfull["w_out1"], carriers["w_out1"], name="l1_out")

    h = rmsnorm(x, small["ln2"], "l2_ln")
    qkv, gate = in_proj(h, "w_in2", "l2")
    f_logit = mm(h, wfull["w_in2"][:, N_MAIN:], carriers["w_in2_f"], name="l2_f") + small["b_f2"]
    c = jnp.cumsum(jax.nn.log_sigmoid(f_logit), axis=0)
    x = x + mm(fox_core(qkv, gate, c, "l2"), wfull["w_out2"], carriers["w_out2"], name="l2_out")

    x = sb_layer(x, "ln3", "w_in3", "w_out3", "l3")
    return sq_loss(rmsnorm(x, small["final_norm"], "final_ln"), target)


def kernel(x, positions, ln0, w_in0, w_out0, ln1, w_in1, q_norm1, w_qb1, kv_norm1, w_kvb1, w_out1, ln2, w_in2, b_f2, w_out2, ln3, w_in3, w_out3, final_norm, loss_target, m_ln0, m_w_in0, m_w_out0, m_ln1, m_w_in1, m_q_norm1, m_w_qb1, m_kv_norm1, m_w_kvb1, m_w_out1, m_ln2, m_w_in2, m_b_f2, m_w_out2, m_ln3, m_w_in3, m_w_out3, m_final_norm, v_ln0, v_w_in0, v_w_out0, v_ln1, v_w_in1, v_q_norm1, v_w_qb1, v_kv_norm1, v_w_kvb1, v_w_out1, v_ln2, v_w_in2, v_b_f2, v_w_out2, v_ln3, v_w_in3, v_w_out3, v_final_norm):
    args = dict(locals())
    w = {n: args[n] for n in ALL_W}
    m = {n: args["m_" + n] for n in ALL_W}
    v = {n: args["v_" + n] for n in ALL_W}
    shapes = {n: w[n].shape for n in BIG}
    offs, total = _pack_rows(shapes)

    w_pack = _pack(w, offs, total)
    gathered = _all_gather(_cast_bf16(w_pack, "cast_w"), "gather_w")
    wfull = _unpack_full(gathered, offs, shapes)

    d_model = wfull["w_in0"].shape[0]
    carriers = {n: jnp.zeros(wfull[n].shape, F32) for n in BIG if n not in ("w_in0", "w_in2", "w_in3")}
    for n in ("w_in0", "w_in2", "w_in3"):
        carriers[n + "_qkv"] = jnp.zeros((d_model, N_QKV), F32)
        carriers[n + "_gate"] = jnp.zeros((d_model, D_INNER), F32)
    carriers["w_in2_f"] = jnp.zeros((d_model, wfull["w_in2"].shape[1] - N_MAIN), F32)
    small = {n: w[n] for n in SMALL}

    loss_local, (g_car, g_small, g_x) = jax.value_and_grad(_forward_loss, argnums=(0, 1, 2))(
        carriers, small, x[0], wfull, positions[0], loss_target[0])
    loss = lax.psum(loss_local, ("x", "y", "c"))

    g_full = {n: g_car[n] for n in BIG if n not in ("w_in0", "w_in2", "w_in3")}
    for n in ("w_in0", "w_in3"):
        g_full[n] = jnp.concatenate([g_car[n + "_qkv"], g_car[n + "_gate"]], axis=1)
    g_full["w_in2"] = jnp.concatenate([g_car["w_in2_qkv"], g_car["w_in2_gate"], g_car["w_in2_f"]], axis=1)
    g_pack = _pack_grads(g_full, offs, total, shapes)
    core_idx = lax.axis_index("c").astype(jnp.int32).reshape(1)
    chip_part = _pair_sum(core_idx, g_pack, _pair_exchange(g_pack, "pair_exchange"), "pair_sum")
    by_chip = _chip_exchange(chip_part, "chip_exchange")
    g_sh, d_sh, m_sh, v_sh = _adamw(w_pack, by_chip, _pack(m, offs, total), _pack(v, offs, total), "adamw")
    big = [_unpack_shard(t, offs, shapes) for t in (g_sh, d_sh, m_sh, v_sh)]

    g_small_sum = _all_reduce_small(_pack_small(g_small), "reduce_small")
    sm = _adamw(_pack_small(w), g_small_sum[None], _pack_small(m), _pack_small(v), "adamw_small")
    small_out = [_unpack_small(t) for t in sm]

    outs = [loss, g_x[None]]
    for k in range(4):
        outs += [small_out[k][n] if n in small_out[k] else big[k][n] for n in ALL_W]
    return tuple(outs)
```

```python
import jax
import jax.numpy as jnp
from jax import lax
from jax.experimental import pallas as pl
from jax.experimental.pallas import tpu as pltpu

F32 = jnp.float32
BF16 = jnp.bfloat16
MESH = pl.DeviceIdType.MESH

N_DEV = 8
N_HEADS = 16
HEAD_DIM = 128
D_INNER = N_HEADS * HEAD_DIM
MLA_Q_RANK = 256
MLA_KV_RANK = 128
MLA_NOPE = 128
MLA_ROPE = 64
MLA_QK_PAD = 256
MLA_CHUNK = 64
ROPE_BASE = 10000.0
EPS = 1e-6
NEG = -1e30
SB_CUT = 104.0

ADAM_LR = 0.001
ADAM_B1 = 0.9
ADAM_B2 = 0.999
ADAM_EPS = 1e-08
ADAM_WD = 0.01
ADAM_STEP = 10

PACK_COLS = 1024
VMEM_LIMIT = 56 * 1024 * 1024

BIG = ["w_in0", "w_out0", "w_in1", "w_qb1", "w_kvb1", "w_out1", "w_in2", "w_out2", "w_in3", "w_out3"]
COL_SHARDED = {"w_in0", "w_in1", "w_qb1", "w_kvb1", "w_in2", "w_in3"}
SMALL = ["ln0", "ln1", "ln2", "ln3", "final_norm", "q_norm1", "kv_norm1", "b_f2"]
ALL_W = ["ln0", "w_in0", "w_out0", "ln1", "w_in1", "q_norm1", "w_qb1", "kv_norm1", "w_kvb1", "w_out1",
         "ln2", "w_in2", "b_f2", "w_out2", "ln3", "w_in3", "w_out3", "final_norm"]


def _cparams(sem=None):
    return pltpu.CompilerParams(dimension_semantics=sem, vmem_limit_bytes=VMEM_LIMIT)


def _tile(dim, cap, align):
    if dim <= cap:
        return dim
    t = (cap // align) * align
    while t >= align:
        if dim % t == 0:
            return t
        t -= align
    return dim


def _dot(a, b, dims):
    return lax.dot_general(a, b, (dims, ((), ())), preferred_element_type=F32)


def _dot_nn(a, b):
    return _dot(a, b, ((1,), (0,)))


def _dot_nt(a, b):
    return _dot(a, b, ((1,), (1,)))


def _dot_tn(a, b):
    return _dot(a, b, ((0,), (0,)))


def _transpose_bf16(x):
    return x.astype(F32).T.astype(BF16)


def _matmul(a, b, mode, col0=0, n_cols=None, out_dtype=F32, name="mm"):
    if mode == "nn":
        m, r = a.shape
        n = n_cols or b.shape[1]
        tm, tn, tr = _tile(m, 512, 8), _tile(n, 1024, 128), _tile(r, 1024, 128)
        c0 = col0 // tn
        a_spec = pl.BlockSpec((tm, tr), lambda i, j, k: (i, k))
        b_spec = pl.BlockSpec((tr, tn), lambda i, j, k: (k, j + c0))
        dims = ((1,), (0,))
        assert col0 % tn == 0
    elif mode == "nt":
        m, r = a.shape
        n = b.shape[0]
        tm, tn, tr = _tile(m, 512, 8), _tile(n, 1024, 128), _tile(r, 1024, 128)
        c0 = col0 // tr
        a_spec = pl.BlockSpec((tm, tr), lambda i, j, k: (i, k))
        b_spec = pl.BlockSpec((tn, tr), lambda i, j, k: (j, k + c0))
        dims = ((1,), (1,))
        assert col0 % tr == 0
    else:
        r, m = a.shape
        n = b.shape[1]
        tm, tn, tr = _tile(m, 1024, 128), _tile(n, 1024, 128), _tile(r, 512, 16)
        a_spec = pl.BlockSpec((tr, tm), lambda i, j, k: (k, i))
        b_spec = pl.BlockSpec((tr, tn), lambda i, j, k: (k, j))
        dims = ((0,), (0,))
    nr = r // tr

    def body(a_ref, b_ref, o_ref, acc_ref):
        k = pl.program_id(2)

        @pl.when(k == 0)
        def _():
            acc_ref[...] = jnp.zeros_like(acc_ref)

        acc_ref[...] += _dot(a_ref[...].astype(BF16), b_ref[...].astype(BF16), dims)

        @pl.when(k == nr - 1)
        def _():
            o_ref[...] = acc_ref[...].astype(out_dtype)

    return pl.pallas_call(
        body,
        name=name,
        grid=(m // tm, n // tn, nr),
        in_specs=[a_spec, b_spec],
        out_specs=pl.BlockSpec((tm, tn), lambda i, j, k: (i, j)),
        out_shape=jax.ShapeDtypeStruct((m, n), out_dtype),
        scratch_shapes=[pltpu.VMEM((tm, tn), F32)],
        compiler_params=_cparams(("parallel", "parallel", "arbitrary")),
    )(a, b)


def _matmul_dw(a, b1, b2, slab, name):
    r, m = a.shape
    n1 = b1.shape[1]
    n = n1 + (b2.shape[1] if b2 is not None else 0)
    tr = _tile(r, 512, 16)
    if slab is None:
        tm, tn = _tile(m, 1024, 128), _tile(n1, 1024, 128)
        out_spec = pl.BlockSpec((tm, tn), lambda i, j, k: (i, j))
        out_shape = (m, n)
    elif slab[0] == "col":
        tm, tn = _tile(m, 1024, 128), slab[1]
        out_spec = pl.BlockSpec((None, None, tm, tn), lambda i, j, k: (j % 2, j // 2, i, 0))
        out_shape = (2, 4, m, tn)
        assert n == N_DEV * tn
    else:
        tm, tn = slab[1], _tile(n, 1024, 128)
        out_spec = pl.BlockSpec((None, None, tm, tn), lambda i, j, k: (i % 2, i // 2, 0, j))
        out_shape = (2, 4, tm, n)
        assert m == N_DEV * tm
    assert n1 % tn == 0 and n % tn == 0
    n1b = n1 // tn
    nr = r // tr

    def body(*refs):
        a_ref, b_refs, o_ref, acc_ref = refs[0], refs[1:-2], refs[-2], refs[-1]
        j = pl.program_id(1)
        k = pl.program_id(2)

        @pl.when(k == 0)
        def _():
            acc_ref[...] = jnp.zeros_like(acc_ref)

        at = a_ref[...].astype(BF16)
        if b2 is None:
            acc_ref[...] += _dot_tn(at, b_refs[0][...].astype(BF16))
        else:
            @pl.when(j < n1b)
            def _():
                acc_ref[...] += _dot_tn(at, b_refs[0][...].astype(BF16))

            @pl.when(j >= n1b)
            def _():
                acc_ref[...] += _dot_tn(at, b_refs[1][...].astype(BF16))

        @pl.when(k == nr - 1)
        def _():
            o_ref[...] = acc_ref[...]

    in_specs = [pl.BlockSpec((tr, tm), lambda i, j, k: (k, i))]
    args = [a, b1]
    if b2 is None:
        in_specs.append(pl.BlockSpec((tr, tn), lambda i, j, k: (k, j)))
    else:
        in_specs.append(pl.BlockSpec((tr, tn), lambda i, j, k: (k, jnp.minimum(j, n1b - 1))))
        in_specs.append(pl.BlockSpec((tr, tn), lambda i, j, k: (k, jnp.maximum(j - n1b, 0))))
        args.append(b2)
    return pl.pallas_call(
        body, name=name, grid=(m // tm, n // tn, nr),
        in_specs=in_specs, out_specs=out_spec,
        out_shape=jax.ShapeDtypeStruct(out_shape, F32),
        scratch_shapes=[pltpu.VMEM((tm, tn), F32)],
        compiler_params=_cparams(("parallel", "parallel", "arbitrary")),
    )(*args)


def mm(a, w, carrier, slab=None, name="mm"):
    @jax.custom_vjp
    def f(a, w, carrier):
        return _matmul(a, w, "nn", 0, None, F32, name + "_fwd")

    def fwd(a, w, carrier):
        return _matmul(a, w, "nn", 0, None, F32, name + "_fwd"), (a, w)

    def bwd(res, g):
        a, w = res
        da = _matmul(g, w, "nt", 0, None, F32, name + "_dx")
        return da, jnp.zeros_like(w), _matmul_dw(a, g, None, slab, name + "_dw")

    f.defvjp(fwd, bwd)
    return f(a, w, carrier)


def in_proj(h, w, carriers, slab, name):
    def run(h, w):
        return (_matmul(h, w, "nn", 0, N_QKV, BF16, name + "_qkv"),
                _matmul(h, w, "nn", N_QKV, D_INNER, F32, name + "_g"))

    @jax.custom_vjp
    def f(h, w, *cars):
        return run(h, w)

    def fwd(h, w, *cars):
        return run(h, w), (h, w)

    def bwd(res, g):
        h, w = res
        g_qkv, g_gate = g
        dh = (_matmul(g_qkv, w, "nt", 0, None, F32, name + "_qkv_dx")
              + _matmul(g_gate, w, "nt", N_QKV, None, F32, name + "_g_dx"))
        if slab:
            dws = (_matmul_dw(h, g_qkv, g_gate, ("col", PACK_COLS), name + "_dw"),)
        else:
            dws = (_matmul_dw(h, g_qkv, None, None, name + "_qkv_dw"), _matmul_dw(h, g_gate, None, None, name + "_g_dw"))
        return (dh, jnp.zeros_like(w)) + dws

    f.defvjp(fwd, bwd)
    return f(h, w, *carriers)


def _rms_fwd(x, g, name):
    s, d = x.shape
    tm = _tile(s, 512, 8)

    def body(x_ref, g_ref, y_ref):
        x = x_ref[...]
        r = lax.rsqrt(jnp.mean(x * x, axis=-1, keepdims=True) + EPS)
        y_ref[...] = x * r * g_ref[...]

    return pl.pallas_call(
        body, name=name, grid=(s // tm,),
        in_specs=[pl.BlockSpec((tm, d), lambda i: (i, 0)), pl.BlockSpec((1, d), lambda i: (0, 0))],
        out_specs=pl.BlockSpec((tm, d), lambda i: (i, 0)),
        out_shape=jax.ShapeDtypeStruct((s, d), F32),
        compiler_params=_cparams(("parallel",)),
    )(x, g)


def _rms_bwd(x, g, dy, name):
    s, d = x.shape
    tm = _tile(s, 512, 8)

    def body(x_ref, g_ref, dy_ref, dx_ref, dg_ref):
        @pl.when(pl.program_id(0) == 0)
        def _():
            dg_ref[...] = jnp.zeros_like(dg_ref)

        x = x_ref[...]
        dy = dy_ref[...]
        r = lax.rsqrt(jnp.mean(x * x, axis=-1, keepdims=True) + EPS)
        xh = x * r
        dg_ref[...] += jnp.sum(dy * xh, axis=0, keepdims=True)
        dxh = dy * g_ref[...]
        dx_ref[...] = r * (dxh - xh * jnp.mean(dxh * xh, axis=-1, keepdims=True))

    return pl.pallas_call(
        body, name=name, grid=(s // tm,),
        in_specs=[pl.BlockSpec((tm, d), lambda i: (i, 0)), pl.BlockSpec((1, d), lambda i: (0, 0)),
                  pl.BlockSpec((tm, d), lambda i: (i, 0))],
        out_specs=[pl.BlockSpec((tm, d), lambda i: (i, 0)), pl.BlockSpec((1, d), lambda i: (0, 0))],
        out_shape=[jax.ShapeDtypeStruct((s, d), F32), jax.ShapeDtypeStruct((1, d), F32)],
        compiler_params=_cparams(("arbitrary",)),
    )(x, g, dy)


def rmsnorm(x, g, name="rms"):
    @jax.custom_vjp
    def f(x, g):
        return _rms_fwd(x, g.reshape(1, -1), name + "_fwd")

    def fwd(x, g):
        return _rms_fwd(x, g.reshape(1, -1), name + "_fwd"), (x, g)

    def bwd(res, dy):
        x, g = res
        dx, dg = _rms_bwd(x, g.reshape(1, -1), dy, name + "_bwd")
        return dx, dg.reshape(-1)

    f.defvjp(fwd, bwd)
    return f(x, g)


def _gate_fwd(o, gate, name):
    s = o.shape[0]
    tm = 256

    def body(o_ref, g_ref, y_ref):
        g = g_ref[...]
        y_ref[...] = o_ref[...] * (g / (1.0 + jnp.exp(-g)))

    row = pl.BlockSpec((tm, D_INNER), lambda i: (i, 0))
    return pl.pallas_call(
        body, name=name, grid=(s // tm,),
        in_specs=[row, row], out_specs=row,
        out_shape=jax.ShapeDtypeStruct((s, D_INNER), F32),
        compiler_params=_cparams(("parallel",)),
    )(o, gate)


def _gate_bwd(dy, o, gate, name):
    s = o.shape[0]
    tm = 256

    def body(dy_ref, o_ref, g_ref, do_ref, dg_ref, dl_ref):
        g = g_ref[...]
        o = o_ref[...]
        dy = dy_ref[...]
        sg = 1.0 / (1.0 + jnp.exp(-g))
        do = dy * (g * sg)
        do_ref[...] = do.astype(BF16)
        dg_ref[...] = dy * o * (sg * (1.0 + g * (1.0 - sg)))
        prod = do * o
        for h in range(N_HEADS):
            dl_ref[h] = jnp.sum(prod[:, h * HEAD_DIM:(h + 1) * HEAD_DIM], axis=1, keepdims=True)

    row = pl.BlockSpec((tm, D_INNER), lambda i: (i, 0))
    return pl.pallas_call(
        body, name=name, grid=(s // tm,),
        in_specs=[row, row, row],
        out_specs=[row, row, pl.BlockSpec((N_HEADS, tm, 1), lambda i: (0, i, 0))],
        out_shape=[jax.ShapeDtypeStruct((s, D_INNER), BF16), jax.ShapeDtypeStruct((s, D_INNER), F32),
                   jax.ShapeDtypeStruct((N_HEADS, s, 1), F32)],
        compiler_params=_cparams(("parallel",)),
    )(dy, o, gate)


SB_BLK = 256


def _softplus(z):
    return jnp.maximum(z, 0.0) + jnp.log(1.0 + jnp.exp(-jnp.abs(z)))


def _tri_sum(x, tri):
    hi = x.astype(BF16)
    lo = (x - hi.astype(F32)).astype(BF16)
    return _dot_nn(hi, tri) + _dot_nn(lo, tri)


def _sb_block(q, kb, i, j, a_run, row, col, tri_suffix, scale):
    z = _dot_nt(q, kb) * scale
    mask = (col + j * SB_BLK) < (row + i * SB_BLK)
    sp = _softplus(z)
    ls = jnp.where(mask, -sp, 0.0)
    suffix = _tri_sum(ls, tri_suffix)
    w = jnp.where(mask, jnp.exp(z + suffix + a_run), 0.0)
    return z, mask, sp, ls, w


def _sb_fwd(qkv, name):
    s = qkv.shape[0]
    b = SB_BLK
    nq = s // b
    scale = HEAD_DIM ** -0.5

    def body(q_ref, k_ref, v_ref, o_ref):
        i = pl.program_id(1)
        q = q_ref[...]
        row = lax.broadcasted_iota(jnp.int32, (b, b), 0)
        col = lax.broadcasted_iota(jnp.int32, (b, b), 1)
        tri_suffix = (row >= col).astype(BF16)

        def cond(c):
            j, a_run, _ = c
            return jnp.logical_and(j >= 0, jnp.max(a_run) > -SB_CUT)

        def step(c):
            j, a_run, acc = c
            off = pl.multiple_of(j * b, b)
            kb = k_ref[pl.ds(off, b), :]
            vb = v_ref[pl.ds(off, b), :]
            _, _, _, ls, w = _sb_block(q, kb, i, j, a_run, row, col, tri_suffix, scale)
            acc = acc + _dot_nn(w.astype(BF16), vb)
            return j - 1, a_run + jnp.sum(ls, axis=1, keepdims=True), acc

        _, _, acc = lax.while_loop(cond, step, (i, jnp.zeros((b, 1), F32), jnp.zeros((b, HEAD_DIM), F32)))
        o_ref[...] = acc

    return pl.pallas_call(
        body, name=name, grid=(N_HEADS, nq),
        in_specs=[pl.BlockSpec((b, HEAD_DIM), lambda h, i: (i, h)),
                  pl.BlockSpec((s, HEAD_DIM), lambda h, i: (0, N_HEADS + h)),
                  pl.BlockSpec((s, HEAD_DIM), lambda h, i: (0, 2 * N_HEADS + h))],
        out_specs=pl.BlockSpec((b, HEAD_DIM), lambda h, i: (i, h)),
        out_shape=jax.ShapeDtypeStruct((s, D_INNER), F32),
        compiler_params=_cparams(("parallel", "arbitrary")),
    )(qkv, qkv, qkv)


def _sb_bwd(qkv, do, name):
    s = qkv.shape[0]
    b = SB_BLK
    nq = s // b
    scale = HEAD_DIM ** -0.5

    def body(q_ref, k_ref, v_ref, do_ref, dq_ref, dk_ref, dv_ref, dkt_s, dvt_s, g_buf, sig_buf):
        i = pl.program_id(1)

        @pl.when(i == 0)
        def _():
            dkt_s[...] = jnp.zeros_like(dkt_s)
            dvt_s[...] = jnp.zeros_like(dvt_s)

        q = q_ref[...]
        dob = do_ref[...]
        q_t = _transpose_bf16(q)
        do_t = _transpose_bf16(dob)
        row = lax.broadcasted_iota(jnp.int32, (b, b), 0)
        col = lax.broadcasted_iota(jnp.int32, (b, b), 1)
        tri_suffix = (row >= col).astype(BF16)
        tri_prefix = (row <= col).astype(BF16)

        def cond(c):
            j, a_run = c
            return jnp.logical_and(j >= 0, jnp.max(a_run) > -SB_CUT)

        def sweep(c):
            j, a_run = c
            off = pl.multiple_of(j * b, b)
            kb = k_ref[pl.ds(off, b), :]
            vb = v_ref[pl.ds(off, b), :]
            z, _, sp, ls, w = _sb_block(q, kb, i, j, a_run, row, col, tri_suffix, scale)
            g_buf[j] = w * _dot_nt(dob, vb)
            sig_buf[j] = jnp.exp(z - sp)
            dvt_s[j] += _dot_nn(do_t, w.astype(BF16))
            return j - 1, a_run + jnp.sum(ls, axis=1, keepdims=True)

        j_end, _ = lax.while_loop(cond, sweep, (i, jnp.zeros((b, 1), F32)))

        def back(j, c):
            g_run, dq = c
            off = pl.multiple_of(j * b, b)
            kb = k_ref[pl.ds(off, b), :]
            g = g_buf[j]
            g_incl = g_run + _tri_sum(g, tri_prefix)
            mask = (col + j * b) < (row + i * b)
            dz = jnp.where(mask, (g - sig_buf[j] * g_incl) * scale, 0.0).astype(BF16)
            dkt_s[j] += _dot_nn(q_t, dz)
            return g_run + jnp.sum(g, axis=1, keepdims=True), dq + _dot_nn(dz, kb)

        _, dq = lax.fori_loop(j_end + 1, i + 1, back, (jnp.zeros((b, 1), F32), jnp.zeros((b, HEAD_DIM), F32)))
        dq_ref[...] = dq.astype(BF16)

        @pl.when(i == nq - 1)
        def _():
            for jb in range(nq):
                dk_ref[jb * b:(jb + 1) * b, :] = dkt_s[jb].T.astype(BF16)
                dv_ref[jb * b:(jb + 1) * b, :] = dvt_s[jb].T.astype(BF16)

    blk = pl.BlockSpec((b, HEAD_DIM), lambda h, i: (i, h))
    head = pl.BlockSpec((s, HEAD_DIM), lambda h, i: (0, h))
    return pl.pallas_call(
        body, name=name, grid=(N_HEADS, nq),
        in_specs=[blk,
                  pl.BlockSpec((s, HEAD_DIM), lambda h, i: (0, N_HEADS + h)),
                  pl.BlockSpec((s, HEAD_DIM), lambda h, i: (0, 2 * N_HEADS + h)),
                  blk],
        out_specs=[blk, head, head],
        out_shape=[jax.ShapeDtypeStruct((s, D_INNER), BF16)] * 3,
        scratch_shapes=[pltpu.VMEM((nq, HEAD_DIM, b), F32), pltpu.VMEM((nq, HEAD_DIM, b), F32),
                        pltpu.VMEM((nq, b, b), F32), pltpu.VMEM((nq, b, b), F32)],
        compiler_params=_cparams(("arbitrary", "arbitrary")),
    )(qkv, qkv, qkv, do)


def sb_core(qkv, gate, name):
    def run(qkv, gate):
        o = _sb_fwd(qkv, name + "_fwd")
        return _gate_fwd(o, gate, name + "_gate"), o

    @jax.custom_vjp
    def f(qkv, gate):
        return run(qkv, gate)[0]

    def fwd(qkv, gate):
        y, o = run(qkv, gate)
        return y, (qkv, gate, o)

    def bwd(res, dy):
        qkv, gate, o = res
        do, dgate, _ = _gate_bwd(dy, o, gate, name + "_gate_bwd")
        dq, dk, dv = _sb_bwd(qkv, do, name + "_bwd")
        return jnp.concatenate([dq, dk, dv], axis=1), dgate

    f.defvjp(fwd, bwd)
    return f(qkv, gate)


SM_BLK = 256


SM_FWD_KEYS = 1024
SM_BWD_KEYS = 512


def _sm_mask(i, jw, keys, chunk_shift):
    row = lax.broadcasted_iota(jnp.int32, (SM_BLK, keys), 0) + i * SM_BLK
    col = lax.broadcasted_iota(jnp.int32, (SM_BLK, keys), 1) + jw * keys
    return (col >> chunk_shift) <= (row >> chunk_shift)


def _sm_fwd(qa, ka, va, ccol, crow, dqk, qo, ko, vo, chunk_shift, scale, name):
    s = qa.shape[0]
    b = SM_BLK
    keys = min(SM_FWD_KEYS, s)
    per = keys // b
    nq = s // b
    has_bias = ccol is not None

    def body(*refs):
        if has_bias:
            q_ref, k_ref, v_ref, cc_ref, cr_ref, o_ref, lse_ref, m_s, l_s, acc_s = refs
        else:
            q_ref, k_ref, v_ref, o_ref, lse_ref, m_s, l_s, acc_s = refs
        i = pl.program_id(1)
        q = q_ref[...]
        m_s[...] = jnp.full_like(m_s, NEG)
        l_s[...] = jnp.zeros_like(l_s)
        acc_s[...] = jnp.zeros_like(acc_s)

        def sweep(jw, masked):
            off = pl.multiple_of(jw * keys, keys)
            z = _dot_nt(q, k_ref[pl.ds(off, keys), :]) * scale
            if has_bias:
                z = z + cc_ref[...] - cr_ref[jw]
            if masked:
                z = jnp.where(_sm_mask(i, jw, keys, chunk_shift), z, NEG)
            m_old = m_s[...]
            m_new = jnp.maximum(m_old, jnp.max(z, axis=1, keepdims=True))
            alpha = jnp.exp(m_old - m_new)
            p = jnp.exp(z - m_new)
            l_s[...] = alpha * l_s[...] + jnp.sum(p, axis=1, keepdims=True)
            acc_s[...] = alpha * acc_s[...] + _dot_nn(p.astype(BF16), v_ref[pl.ds(off, keys), :])
            m_s[...] = m_new

        def full(jw, carry):
            sweep(jw, False)
            return carry

        lax.fori_loop(0, i // per, full, 0)
        sweep(i // per, True)
        o_ref[...] = acc_s[...] / l_s[...]
        lse_ref[...] = m_s[...] + jnp.log(l_s[...])

    in_specs = [pl.BlockSpec((b, dqk), lambda h, i: (i, qo + h)),
                pl.BlockSpec((s, dqk), lambda h, i: (0, ko + h)),
                pl.BlockSpec((s, HEAD_DIM), lambda h, i: (0, vo + h))]
    args = [qa, ka, va]
    if has_bias:
        in_specs += [pl.BlockSpec((None, b, 1), lambda h, i: (h, i, 0)),
                     pl.BlockSpec((None, s // keys, 1, keys), lambda h, i: (h, 0, 0, 0))]
        args += [ccol, crow]
    return pl.pallas_call(
        body, name=name, grid=(N_HEADS, nq),
        in_specs=in_specs,
        out_specs=[pl.BlockSpec((b, HEAD_DIM), lambda h, i: (i, h)),
                   pl.BlockSpec((None, b, 1), lambda h, i: (h, i, 0))],
        out_shape=[jax.ShapeDtypeStruct((s, D_INNER), F32), jax.ShapeDtypeStruct((N_HEADS, s, 1), F32)],
        scratch_shapes=[pltpu.VMEM((b, 1), F32), pltpu.VMEM((b, 1), F32), pltpu.VMEM((b, HEAD_DIM), F32)],
        compiler_params=_cparams(("parallel", "arbitrary")),
    )(*args)


def _sm_bwd(qa, ka, va, do, lse, delta, ccol, crow, dqk, qo, ko, vo, chunk_shift, scale, grad_dtype, name):
    s = qa.shape[0]
    b = SM_BLK
    keys = min(SM_BWD_KEYS, s)
    per = keys // b
    nq = s // b
    nk = s // keys
    has_bias = ccol is not None

    def body(*refs):
        if has_bias:
            (q_ref, k_ref, v_ref, do_ref, lse_ref, dl_ref, cc_ref, cr_ref,
             dq_ref, dk_ref, dv_ref, dc_ref, dr_ref, dq_s, dkt_s, dvt_s, dc_s, dr_s) = refs
        else:
            (q_ref, k_ref, v_ref, do_ref, lse_ref, dl_ref,
             dq_ref, dk_ref, dv_ref, dq_s, dkt_s, dvt_s) = refs
        i = pl.program_id(1)

        @pl.when(i == 0)
        def _():
            dkt_s[...] = jnp.zeros_like(dkt_s)
            dvt_s[...] = jnp.zeros_like(dvt_s)
            if has_bias:
                dc_s[...] = jnp.zeros_like(dc_s)

        q = q_ref[...]
        dob = do_ref[...]
        q_t = _transpose_bf16(q)
        do_t = _transpose_bf16(dob)
        lse = lse_ref[...]
        delta = dl_ref[...]
        dq_s[...] = jnp.zeros_like(dq_s)
        if has_bias:
            dr_s[...] = jnp.zeros_like(dr_s)

        def sweep(jw, masked):
            off = pl.multiple_of(jw * keys, keys)
            kb = k_ref[pl.ds(off, keys), :]
            z = _dot_nt(q, kb) * scale
            if has_bias:
                z = z + cc_ref[...] - cr_ref[jw]
            p = jnp.exp(z - lse)
            if masked:
                p = jnp.where(_sm_mask(i, jw, keys, chunk_shift), p, 0.0)
            dvt_s[jw] += _dot_nn(do_t, p.astype(BF16))
            dz = p * (_dot_nt(dob, v_ref[pl.ds(off, keys), :]) - delta)
            if has_bias:
                dc_s[jw] += jnp.sum(dz, axis=0, keepdims=True)
                dr_s[...] += jnp.sum(dz, axis=1, keepdims=True)
            dzs = (dz * scale).astype(BF16)
            dkt_s[jw] += _dot_nn(q_t, dzs)
            dq_s[...] += _dot_nn(dzs, kb)

        def full(jw, carry):
            sweep(jw, False)
            return carry

        lax.fori_loop(0, i // per, full, 0)
        sweep(i // per, True)
        dq_ref[...] = dq_s[...].astype(grad_dtype)
        if has_bias:
            dr_ref[...] = dr_s[...]

        @pl.when(i == nq - 1)
        def _():
            for jw in range(nk):
                dk_ref[jw * keys:(jw + 1) * keys, :] = dkt_s[jw].T.astype(grad_dtype)
                dv_ref[jw * keys:(jw + 1) * keys, :] = dvt_s[jw].T.astype(grad_dtype)
            if has_bias:
                dc_ref[...] = dc_s[...]

    vec = pl.BlockSpec((None, b, 1), lambda h, i: (h, i, 0))
    in_specs = [pl.BlockSpec((b, dqk), lambda h, i: (i, qo + h)),
                pl.BlockSpec((s, dqk), lambda h, i: (0, ko + h)),
                pl.BlockSpec((s, HEAD_DIM), lambda h, i: (0, vo + h)),
                pl.BlockSpec((b, HEAD_DIM), lambda h, i: (i, h)),
                vec, vec]
    args = [qa, ka, va, do, lse, delta]
    out_specs = [pl.BlockSpec((b, dqk), lambda h, i: (i, h)),
                 pl.BlockSpec((s, dqk), lambda h, i: (0, h)),
                 pl.BlockSpec((s, HEAD_DIM), lambda h, i: (0, h))]
    out_shape = [jax.ShapeDtypeStruct((s, N_HEADS * dqk), grad_dtype),
                 jax.ShapeDtypeStruct((s, N_HEADS * dqk), grad_dtype),
                 jax.ShapeDtypeStruct((s, D_INNER), grad_dtype)]
    scratch = [pltpu.VMEM((b, dqk), F32), pltpu.VMEM((nk, dqk, keys), F32), pltpu.VMEM((nk, HEAD_DIM, keys), F32)]
    if has_bias:
        key_vec = pl.BlockSpec((None, nk, 1, keys), lambda h, i: (h, 0, 0, 0))
        in_specs += [vec, key_vec]
        args += [ccol, crow]
        out_specs += [key_vec, vec]
        out_shape += [jax.ShapeDtypeStruct((N_HEADS, nk, 1, keys), F32), jax.ShapeDtypeStruct((N_HEADS, s, 1), F32)]
        scratch += [pltpu.VMEM((nk, 1, keys), F32), pltpu.VMEM((b, 1), F32)]
    return pl.pallas_call(
        body, name=name, grid=(N_HEADS, nq),
        in_specs=in_specs, out_specs=out_specs, out_shape=out_shape, scratch_shapes=scratch,
        compiler_params=_cparams(("arbitrary", "arbitrary")),
    )(*args)


def fox_core(qkv, gate, c, name):
    s = qkv.shape[0]
    scale = HEAD_DIM ** -0.5
    cfg = dict(dqk=HEAD_DIM, qo=0, ko=N_HEADS, vo=2 * N_HEADS, chunk_shift=0, scale=scale)

    def layouts(c, keys):
        ct = c.T
        keys = min(keys, s)
        return ct.reshape(N_HEADS, s, 1), ct.reshape(N_HEADS, s // keys, 1, keys)

    def run(qkv, gate, c):
        ccol, crow = layouts(c, SM_FWD_KEYS)
        o, lse = _sm_fwd(qkv, qkv, qkv, ccol, crow, name=name + "_fwd", **cfg)
        return _gate_fwd(o, gate, name + "_gate"), o, lse

    @jax.custom_vjp
    def f(qkv, gate, c):
        return run(qkv, gate, c)[0]

    def fwd(qkv, gate, c):
        y, o, lse = run(qkv, gate, c)
        return y, (qkv, gate, c, o, lse)

    def bwd(res, dy):
        qkv, gate, c, o, lse = res
        ccol, crow = layouts(c, SM_BWD_KEYS)
        do, dgate, delta = _gate_bwd(dy, o, gate, name + "_gate_bwd")
        dq, dk, dv, colsum, rowsum = _sm_bwd(qkv, qkv, qkv, do, lse, delta, ccol, crow,
                                             grad_dtype=BF16, name=name + "_bwd", **cfg)
        dc = (rowsum.reshape(N_HEADS, s) - colsum.reshape(N_HEADS, s)).T
        return jnp.concatenate([dq, dk, dv], axis=1), dgate, dc

    f.defvjp(fwd, bwd)
    return f(qkv, gate, c)


def mla_core(qc, kc, v, gate, name):
    scale = (MLA_NOPE + MLA_ROPE) ** -0.5
    cfg = dict(dqk=MLA_QK_PAD, qo=0, ko=0, vo=0, chunk_shift=MLA_CHUNK.bit_length() - 1, scale=scale)

    def run(qc, kc, v, gate):
        o, lse = _sm_fwd(qc, kc, v, None, None, name=name + "_fwd", **cfg)
        return _gate_fwd(o, gate, name + "_gate"), o, lse

    @jax.custom_vjp
    def f(qc, kc, v, gate):
        return run(qc.astype(BF16), kc.astype(BF16), v.astype(BF16), gate)[0]

    def fwd(qc, kc, v, gate):
        qc, kc, v = qc.astype(BF16), kc.astype(BF16), v.astype(BF16)
        y, o, lse = run(qc, kc, v, gate)
        return y, (qc, kc, v, gate, o, lse)

    def bwd(res, dy):
        qc, kc, v, gate, o, lse = res
        do, dgate, delta = _gate_bwd(dy, o, gate, name + "_gate_bwd")
        dq, dk, dv = _sm_bwd(qc, kc, v, do, lse, delta, None, None, grad_dtype=F32, name=name + "_bwd", **cfg)
        return dq, dk, dv, dgate

    f.defvjp(fwd, bwd)
    return f(qc, kc, v, gate)


def _sq_loss_call(y, t, name):
    s, d = y.shape
    tm = _tile(s, 512, 8)

    def body(y_ref, t_ref, l_ref, e_ref):
        @pl.when(pl.program_id(0) == 0)
        def _():
            l_ref[...] = jnp.zeros_like(l_ref)

        e = y_ref[...] - t_ref[...]
        e_ref[...] = e * (1.0 / d)
        part = jnp.sum(jnp.sum(e * e, axis=1, keepdims=True), axis=0, keepdims=True)
        l_ref[...] += jnp.broadcast_to(part * (0.5 / d), l_ref.shape)

    row = pl.BlockSpec((tm, d), lambda i: (i, 0))
    return pl.pallas_call(
        body, name=name, grid=(s // tm,),
        in_specs=[row, row],
        out_specs=[pl.BlockSpec((8, 128), lambda i: (0, 0)), row],
        out_shape=[jax.ShapeDtypeStruct((8, 128), F32), jax.ShapeDtypeStruct((s, d), F32)],
        compiler_params=_cparams(("arbitrary",)),
    )(y, t)


@jax.custom_vjp
def sq_loss(y, t):
    return _sq_loss_call(y, t, "loss_fwd")[0][0, 0]


def _sq_loss_fwd(y, t):
    l, e = _sq_loss_call(y, t, "loss_fwd")
    return l[0, 0], e


def _sq_loss_bwd(e, g):
    return g * e, jnp.zeros_like(e)


sq_loss.defvjp(_sq_loss_fwd, _sq_loss_bwd)


def _cast_bf16(x, name):
    r, c = x.shape
    tb = _tile(r, 512, 16)

    def body(x_ref, o_ref):
        o_ref[...] = x_ref[...].astype(BF16)

    return pl.pallas_call(
        body, name=name, grid=(r // tb,),
        in_specs=[pl.BlockSpec((tb, c), lambda i: (i, 0))],
        out_specs=pl.BlockSpec((tb, c), lambda i: (i, 0)),
        out_shape=jax.ShapeDtypeStruct((r, c), BF16),
        compiler_params=_cparams(("parallel",)),
    )(x)


def _pair_sum(core_idx, g, recv, name):
    _, _, r, c = g.shape
    tb = _tile(r, 512, 8)

    def body(c_ref, g_ref, r_ref, o_ref):
        o_ref[...] = g_ref[...] + r_ref[...]

    return pl.pallas_call(
        body, name=name,
        grid_spec=pltpu.PrefetchScalarGridSpec(
            num_scalar_prefetch=1, grid=(4, r // tb),
            in_specs=[pl.BlockSpec((None, None, tb, c), lambda q, i, c_ref: (c_ref[0], q, i, 0)),
                      pl.BlockSpec((None, tb, c), lambda q, i, c_ref: (q, i, 0))],
            out_specs=pl.BlockSpec((None, tb, c), lambda q, i, c_ref: (q, i, 0))),
        out_shape=jax.ShapeDtypeStruct((4, r, c), F32),
        compiler_params=_cparams(("parallel", "parallel")),
    )(core_idx, g, recv)


def _adamw(w, parts, m, v, name):
    n, r, c = parts.shape
    tb = _tile(r, 256, 8)
    b1c = 1.0 - ADAM_B1 ** ADAM_STEP
    b2c = 1.0 - ADAM_B2 ** ADAM_STEP

    def body(w_ref, p_ref, m_ref, v_ref, g_ref, d_ref, nm_ref, nv_ref):
        g = p_ref[0]
        for k in range(1, n):
            g = g + p_ref[k]
        m_new = ADAM_B1 * m_ref[...] + (1.0 - ADAM_B1) * g
        v_new = ADAM_B2 * v_ref[...] + (1.0 - ADAM_B2) * (g * g)
        m_hat = m_new / b1c
        v_hat = v_new / b2c
        g_ref[...] = g
        d_ref[...] = -ADAM_LR * (m_hat / (jnp.sqrt(v_hat) + ADAM_EPS) + ADAM_WD * w_ref[...])
        nm_ref[...] = m_new
        nv_ref[...] = v_new

    row = pl.BlockSpec((tb, c), lambda i: (i, 0))
    return pl.pallas_call(
        body, name=name, grid=(r // tb,),
        in_specs=[row, pl.BlockSpec((n, tb, c), lambda i: (0, i, 0)), row, row],
        out_specs=[row] * 4,
        out_shape=[jax.ShapeDtypeStruct((r, c), F32)] * 4,
        compiler_params=_cparams(("parallel",)),
    )(w, parts, m, v)


ANY = pl.BlockSpec(memory_space=pl.ANY)


def _place():
    return lax.axis_index("x"), lax.axis_index("y"), lax.axis_index("c")


def _all_gather(shards, kinds, name):
    nm = len(shards)

    def out_shape(sh, kind):
        a, b = sh.shape
        return {"row": (N_DEV * a, b), "col": (a, N_DEV * b), "stack": (N_DEV, a, b)}[kind]

    def body(*refs):
        x_refs, out_refs = refs[:nm], refs[nm:2 * nm]
        send_sems, recv_sems, local_sems = refs[2 * nm:]
        x, y, cc = _place()
        me, sibling = (x, y, cc), (x, y, 1 - cc)
        chips = [(1 - x, y), (x, 1 - y), (1 - x, 1 - y)]

        def slot(mi, px, py, pc):
            d = 4 * px + 2 * py + pc
            a, b = x_refs[mi].shape
            if kinds[mi] == "row":
                return out_refs[mi].at[pl.ds(pl.multiple_of(d * a, a), a), :]
            if kinds[mi] == "col":
                return out_refs[mi].at[:, pl.ds(pl.multiple_of(d * b, 128), b)]
            return out_refs[mi].at[d]

        def copy(mi, k, block, to, src=None):
            return pltpu.make_async_remote_copy(
                src_ref=slot(mi, *block) if src is None else src, dst_ref=slot(mi, *block),
                send_sem=send_sems.at[7 * mi + k], recv_sem=recv_sems.at[7 * mi + k],
                device_id=to, device_id_type=MESH)

        mine = [pltpu.make_async_copy(x_refs[mi], slot(mi, *me), local_sems.at[mi]) for mi in range(nm)]
        for cp in mine:
            cp.start()
        first = []
        for mi in range(nm):
            first.append(copy(mi, 0, me, sibling, src=x_refs[mi]))
            first += [copy(mi, 1 + j, me, (*chip, cc), src=x_refs[mi]) for j, chip in enumerate(chips)]
        for cp in first:
            cp.start()
        passed = []
        for j, chip in enumerate(chips):
            for mi in range(nm):
                copy(mi, 1 + j, (*chip, cc), me).wait_recv()
                passed.append(copy(mi, 4 + j, (*chip, cc), sibling))
                passed[-1].start()
        for mi in range(nm):
            copy(mi, 0, sibling, me).wait_recv()
            for j, chip in enumerate(chips):
                copy(mi, 4 + j, (*chip, 1 - cc), me).wait_recv()
        for cp in first + passed:
            cp.wait_send()
        for cp in mine:
            cp.wait()

    return pl.pallas_call(
        body, name=name,
        out_shape=[jax.ShapeDtypeStruct(out_shape(sh, kd), sh.dtype) for sh, kd in zip(shards, kinds)],
        in_specs=[ANY] * nm, out_specs=[ANY] * nm,
        scratch_shapes=[pltpu.SemaphoreType.DMA((7 * nm,)), pltpu.SemaphoreType.DMA((7 * nm,)),
                        pltpu.SemaphoreType.DMA((nm,))],
    )(*shards)


def _pair_exchange(gs, name):
    nm = len(gs)

    def body(*refs):
        g_refs, recv_refs = refs[:nm], refs[nm:2 * nm]
        send_sems, recv_sems = refs[2 * nm:]
        x, y, cc = _place()
        copies = [pltpu.make_async_remote_copy(
            src_ref=g_refs[mi].at[1 - cc], dst_ref=recv_refs[mi],
            send_sem=send_sems.at[mi], recv_sem=recv_sems.at[mi], device_id=(x, y, 1 - cc), device_id_type=MESH)
            for mi in range(nm)]
        for cp in copies:
            cp.start()
        for cp in copies:
            cp.wait_recv()
        for cp in copies:
            cp.wait_send()

    return pl.pallas_call(
        body, name=name,
        out_shape=[jax.ShapeDtypeStruct(g.shape[1:], g.dtype) for g in gs],
        in_specs=[ANY] * nm, out_specs=[ANY] * nm,
        scratch_shapes=[pltpu.SemaphoreType.DMA((nm,)), pltpu.SemaphoreType.DMA((nm,))],
    )(*gs)


def _chip_exchange(parts, name):
    nm = len(parts)

    def body(*refs):
        p_refs, out_refs = refs[:nm], refs[nm:2 * nm]
        send_sems, recv_sems, local_sems = refs[2 * nm:]
        x, y, cc = _place()
        mine = 2 * x + y
        others = [(1 - x, y), (x, 1 - y), (1 - x, 1 - y)]
        keeps = [pltpu.make_async_copy(p_refs[mi].at[mine], out_refs[mi].at[mine], local_sems.at[mi]) for mi in range(nm)]
        for cp in keeps:
            cp.start()
        sends = []
        for px, py in others:
            q = 2 * px + py
            for mi in range(nm):
                sends.append(pltpu.make_async_remote_copy(
                    src_ref=p_refs[mi].at[q], dst_ref=out_refs[mi].at[mine],
                    send_sem=send_sems.at[4 * mi + q], recv_sem=recv_sems.at[4 * mi + mine],
                    device_id=(px, py, cc), device_id_type=MESH))
        for cp in sends:
            cp.start()
        for px, py in others:
            q = 2 * px + py
            for mi in range(nm):
                pltpu.make_async_remote_copy(
                    src_ref=p_refs[mi].at[q], dst_ref=out_refs[mi].at[q],
                    send_sem=send_sems.at[4 * mi + q], recv_sem=recv_sems.at[4 * mi + q],
                    device_id=(px, py, cc), device_id_type=MESH).wait_recv()
        for cp in sends:
            cp.wait_send()
        for cp in keeps:
            cp.wait()

    return pl.pallas_call(
        body, name=name,
        out_shape=[jax.ShapeDtypeStruct(p.shape, p.dtype) for p in parts],
        in_specs=[ANY] * nm, out_specs=[ANY] * nm,
        scratch_shapes=[pltpu.SemaphoreType.DMA((4 * nm,)), pltpu.SemaphoreType.DMA((4 * nm,)),
                        pltpu.SemaphoreType.DMA((nm,))],
    )(*parts)


def _all_reduce_small(v, name):
    shape = v.shape

    def body(v_ref, out_ref, buf, send_sems, recv_sems):
        x, y, cc = _place()
        me = 4 * x + 2 * y + cc
        buf[me] = v_ref[...]
        flips = [(a, b, d) for a in (0, 1) for b in (0, 1) for d in (0, 1)][1:]
        copies = []
        for k, (a, b, d) in enumerate(flips):
            peer = (x ^ a, y ^ b, cc ^ d)
            copies.append(pltpu.make_async_remote_copy(
                src_ref=v_ref, dst_ref=buf.at[me],
                send_sem=send_sems.at[k], recv_sem=recv_sems.at[k], device_id=peer, device_id_type=MESH))
        for cp in copies:
            cp.start()
        for k, (a, b, d) in enumerate(flips):
            peer_id = 4 * (x ^ a) + 2 * (y ^ b) + (cc ^ d)
            pltpu.make_async_remote_copy(
                src_ref=v_ref, dst_ref=buf.at[peer_id],
                send_sem=send_sems.at[k], recv_sem=recv_sems.at[k], device_id=(x, y, cc), device_id_type=MESH
            ).wait_recv()
        for cp in copies:
            cp.wait_send()
        total = buf[0]
        for k in range(1, N_DEV):
            total = total + buf[k]
        out_ref[...] = total

    vm = pl.BlockSpec(memory_space=pltpu.VMEM)
    return pl.pallas_call(
        body, name=name,
        out_shape=jax.ShapeDtypeStruct(shape, F32),
        in_specs=[vm], out_specs=vm,
        scratch_shapes=[pltpu.VMEM((N_DEV,) + shape, F32), pltpu.SemaphoreType.DMA((7,)), pltpu.SemaphoreType.DMA((7,))],
    )(v)


def _gather_kind(name, shape):
    if name not in COL_SHARDED:
        return "row"
    return "col" if shape[1] % 128 == 0 else "stack"


def _slab_of(name, shape):
    if name not in COL_SHARDED:
        return ("row", shape[0])
    return ("col", shape[1]) if shape[1] % 128 == 0 else None


def _to_slabs(g, shape):
    kk, nn = shape
    return g.reshape(kk, 4, 2, nn).transpose(2, 1, 0, 3)


SMALL_ROWS = 8


def _pack_small(arrs):
    rows = [arrs[n] for n in SMALL[:5]]
    last = jnp.concatenate([arrs["q_norm1"], arrs["kv_norm1"], arrs["b_f2"]])
    rows.append(jnp.pad(last, (0, PACK_COLS - last.shape[0])))
    rows += [jnp.zeros((PACK_COLS,), F32)] * (SMALL_ROWS - len(rows))
    return jnp.stack(rows)


def _unpack_small(p):
    out = {n: p[k] for k, n in enumerate(SMALL[:5])}
    out["q_norm1"] = p[5, :MLA_Q_RANK]
    out["kv_norm1"] = p[5, MLA_Q_RANK:MLA_Q_RANK + MLA_KV_RANK]
    out["b_f2"] = p[5, MLA_Q_RANK + MLA_KV_RANK:MLA_Q_RANK + MLA_KV_RANK + N_HEADS]
    return out


N_QKV = 3 * D_INNER
N_MAIN = 4 * D_INNER


def _rope(x, pos):
    r = x.shape[-1]
    inv_freq = ROPE_BASE ** (-jnp.arange(0, r, 2, dtype=F32) / r)
    ang = pos.astype(F32)[:, None, None] * inv_freq
    cos, sin = jnp.cos(ang), jnp.sin(ang)
    x1, x2 = x[..., : r // 2], x[..., r // 2:]
    return jnp.concatenate([x1 * cos - x2 * sin, x1 * sin + x2 * cos], axis=-1)


def _forward_loss(carriers, small, x, wfull, slabs, pos, target):
    s = x.shape[0]

    def out_proj(y, w_out, tag):
        return mm(y, wfull[w_out], carriers[w_out], slabs[w_out], name=tag + "_out")

    def sb_layer(x, ln, w_in, w_out, tag):
        h = rmsnorm(x, small[ln], tag + "_ln")
        qkv, gate = in_proj(h, wfull[w_in], (carriers[w_in],), True, tag)
        return x + out_proj(sb_core(qkv, gate, tag), w_out, tag)

    x = sb_layer(x, "ln0", "w_in0", "w_out0", "l0")

    h = rmsnorm(x, small["ln1"], "l1_ln")
    proj = mm(h, wfull["w_in1"], carriers["w_in1"], slabs["w_in1"], name="l1_in")
    i1, i2, i3 = MLA_Q_RANK, MLA_Q_RANK + MLA_KV_RANK, MLA_Q_RANK + MLA_KV_RANK + MLA_ROPE
    q = mm(rmsnorm(proj[:, :i1], small["q_norm1"], "l1_qn"), wfull["w_qb1"], carriers["w_qb1"], slabs["w_qb1"],
           name="l1_qb")
    q = q.reshape(s, N_HEADS, MLA_NOPE + MLA_ROPE)
    kv = mm(rmsnorm(proj[:, i1:i2], small["kv_norm1"], "l1_kvn"), wfull["w_kvb1"], carriers["w_kvb1"],
            slabs["w_kvb1"], name="l1_kvb")
    kv = kv.reshape(s, N_HEADS, MLA_NOPE + HEAD_DIM)
    k_rope = _rope(proj[:, i2:i3][:, None, :], pos)
    pad = jnp.zeros((s, N_HEADS, MLA_QK_PAD - MLA_NOPE - MLA_ROPE), F32)
    qc = jnp.concatenate([q[..., :MLA_NOPE], _rope(q[..., MLA_NOPE:], pos), pad], axis=-1)
    kc = jnp.concatenate([kv[..., :MLA_NOPE], jnp.broadcast_to(k_rope, (s, N_HEADS, MLA_ROPE)), pad], axis=-1)
    y = mla_core(qc.reshape(s, -1), kc.reshape(s, -1), kv[..., MLA_NOPE:].reshape(s, -1), proj[:, i3:], "l1")
    x = x + out_proj(y, "w_out1", "l1")

    h = rmsnorm(x, small["ln2"], "l2_ln")
    qkv, gate = in_proj(h, wfull["w_in2"], (carriers["w_in2_qkv"], carriers["w_in2_gate"]), False, "l2")
    f_logit = mm(h, wfull["w_in2"][:, N_MAIN:], carriers["w_in2_f"], None, name="l2_f") + small["b_f2"]
    c = jnp.cumsum(jax.nn.log_sigmoid(f_logit), axis=0)
    x = x + out_proj(fox_core(qkv, gate, c, "l2"), "w_out2", "l2")

    x = sb_layer(x, "ln3", "w_in3", "w_out3", "l3")
    return sq_loss(rmsnorm(x, small["final_norm"], "final_ln"), target)


def kernel(x, positions, ln0, w_in0, w_out0, ln1, w_in1, q_norm1, w_qb1, kv_norm1, w_kvb1, w_out1, ln2, w_in2, b_f2, w_out2, ln3, w_in3, w_out3, final_norm, loss_target, m_ln0, m_w_in0, m_w_out0, m_ln1, m_w_in1, m_q_norm1, m_w_qb1, m_kv_norm1, m_w_kvb1, m_w_out1, m_ln2, m_w_in2, m_b_f2, m_w_out2, m_ln3, m_w_in3, m_w_out3, m_final_norm, v_ln0, v_w_in0, v_w_out0, v_ln1, v_w_in1, v_q_norm1, v_w_qb1, v_kv_norm1, v_w_kvb1, v_w_out1, v_ln2, v_w_in2, v_b_f2, v_w_out2, v_ln3, v_w_in3, v_w_out3, v_final_norm):
    args = dict(locals())
    w = {n: args[n] for n in ALL_W}
    m = {n: args["m_" + n] for n in ALL_W}
    v = {n: args["v_" + n] for n in ALL_W}
    shapes = {n: w[n].shape for n in BIG}
    kinds = [_gather_kind(n, shapes[n]) for n in BIG]
    slabs = {n: _slab_of(n, shapes[n]) for n in BIG}

    gathered = _all_gather([w[n].astype(BF16) for n in BIG], kinds, "gather_w")
    wfull = {}
    for n, kind, full in zip(BIG, kinds, gathered):
        wfull[n] = full.transpose(1, 0, 2).reshape(shapes[n][0], -1) if kind == "stack" else full

    d_model = wfull["w_in0"].shape[0]
    carriers = {}
    for n in BIG:
        if n == "w_in2":
            continue
        if slabs[n] is None:
            carriers[n] = jnp.zeros(wfull[n].shape, F32)
        else:
            carriers[n] = jnp.zeros((2, 4) + shapes[n], F32)
    carriers["w_in2_qkv"] = jnp.zeros((d_model, N_QKV), F32)
    carriers["w_in2_gate"] = jnp.zeros((d_model, D_INNER), F32)
    carriers["w_in2_f"] = jnp.zeros((d_model, wfull["w_in2"].shape[1] - N_MAIN), F32)
    small = {n: w[n] for n in SMALL}

    def local_loss(carriers, small, x_seq):
        return _forward_loss(carriers, small, x_seq, wfull, slabs, positions[0], loss_target[0])

    loss_local, (g_car, g_small, g_x) = jax.value_and_grad(local_loss, argnums=(0, 1, 2))(carriers, small, x[0])
    loss = lax.psum(loss_local, ("x", "y", "c"))

    g_slab = {}
    for n in BIG:
        if n == "w_in2":
            g = jnp.concatenate([g_car["w_in2_qkv"], g_car["w_in2_gate"], g_car["w_in2_f"]], axis=1)
            g_slab[n] = _to_slabs(g, shapes[n])
        elif slabs[n] is None:
            g_slab[n] = _to_slabs(g_car[n], shapes[n])
        else:
            g_slab[n] = g_car[n]
    core_idx = lax.axis_index("c").astype(jnp.int32).reshape(1)
    from_sibling = _pair_exchange([g_slab[n] for n in BIG], "pair_exchange")
    chip_part = [_pair_sum(core_idx, g_slab[n], r, "pair_sum_" + n) for n, r in zip(BIG, from_sibling)]
    by_chip = _chip_exchange(chip_part, "chip_exchange")
    big = [{}, {}, {}, {}]
    for n, parts in zip(BIG, by_chip):
        for k, t in enumerate(_adamw(w[n], parts, m[n], v[n], "adamw_" + n)):
            big[k][n] = t

    g_small_sum = _all_reduce_small(_pack_small(g_small), "reduce_small")
    sm = _adamw(_pack_small(w), g_small_sum[None], _pack_small(m), _pack_small(v), "adamw_small")
    small_out = [_unpack_small(t) for t in sm]

    outs = [loss, g_x[None]]
    for k in range(4):
        outs += [small_out[k][n] if n in small_out[k] else big[k][n] for n in ALL_W]
    return tuple(outs)
```

```python
import jax
import jax.numpy as jnp
from jax import lax
from jax.experimental import pallas as pl
from jax.experimental.pallas import tpu as pltpu

F32 = jnp.float32
BF16 = jnp.bfloat16
MESH = pl.DeviceIdType.MESH

N_DEV = 8
N_HEADS = 16
HEAD_DIM = 128
D_INNER = N_HEADS * HEAD_DIM
MLA_Q_RANK = 256
MLA_KV_RANK = 128
MLA_NOPE = 128
MLA_ROPE = 64
MLA_QK_PAD = 256
MLA_CHUNK = 64
ROPE_BASE = 10000.0
EPS = 1e-6
NEG = -1e30
SB_CUT = 104.0

ADAM_LR = 0.001
ADAM_B1 = 0.9
ADAM_B2 = 0.999
ADAM_EPS = 1e-08
ADAM_WD = 0.01
ADAM_STEP = 10

PACK_COLS = 1024
VMEM_LIMIT = 56 * 1024 * 1024

BIG = ["w_in0", "w_out0", "w_in1", "w_qb1", "w_kvb1", "w_out1", "w_in2", "w_out2", "w_in3", "w_out3"]
COL_SHARDED = {"w_in0", "w_in1", "w_qb1", "w_kvb1", "w_in2", "w_in3"}
SMALL = ["ln0", "ln1", "ln2", "ln3", "final_norm", "q_norm1", "kv_norm1", "b_f2"]
ALL_W = ["ln0", "w_in0", "w_out0", "ln1", "w_in1", "q_norm1", "w_qb1", "kv_norm1", "w_kvb1", "w_out1",
         "ln2", "w_in2", "b_f2", "w_out2", "ln3", "w_in3", "w_out3", "final_norm"]


def _cparams(sem=None):
    return pltpu.CompilerParams(dimension_semantics=sem, vmem_limit_bytes=VMEM_LIMIT)


def _tile(dim, cap, align):
    if dim <= cap:
        return dim
    t = (cap // align) * align
    while t >= align:
        if dim % t == 0:
            return t
        t -= align
    return dim


def _dot(a, b, dims):
    return lax.dot_general(a, b, (dims, ((), ())), preferred_element_type=F32)


def _dot_nn(a, b):
    return _dot(a, b, ((1,), (0,)))


def _dot_nt(a, b):
    return _dot(a, b, ((1,), (1,)))


def _dot_tn(a, b):
    return _dot(a, b, ((0,), (0,)))


def _transpose_bf16(x):
    return x.astype(F32).T.astype(BF16)


def _matmul(a, b, mode, col0=0, n_cols=None, out_dtype=F32, name="mm"):
    if mode == "nn":
        m, r = a.shape
        n = n_cols or b.shape[1]
        tn, tr = _tile(n, 1024, 128), _tile(r, 1024, 128)
        tm = _tile(m, 1024 if tn <= 1024 else 512, 8)
        c0 = col0 // tn
        a_spec = pl.BlockSpec((tm, tr), lambda i, j, k: (i, k))
        b_spec = pl.BlockSpec((tr, tn), lambda i, j, k: (k, j + c0))
        dims = ((1,), (0,))
        assert col0 % tn == 0
    elif mode == "nt":
        m, r = a.shape
        n = b.shape[0]
        tn, tr = _tile(n, 1024, 128), _tile(r, 1024, 128)
        tm = _tile(m, 1024 if tr <= 1024 else 512, 8)
        c0 = col0 // tr
        a_spec = pl.BlockSpec((tm, tr), lambda i, j, k: (i, k))
        b_spec = pl.BlockSpec((tn, tr), lambda i, j, k: (j, k + c0))
        dims = ((1,), (1,))
        assert col0 % tr == 0
    else:
        r, m = a.shape
        n = b.shape[1]
        tm, tn, tr = _tile(m, 1024, 128), _tile(n, 1024, 128), _tile(r, 512, 16)
        a_spec = pl.BlockSpec((tr, tm), lambda i, j, k: (k, i))
        b_spec = pl.BlockSpec((tr, tn), lambda i, j, k: (k, j))
        dims = ((0,), (0,))
    nr = r // tr

    def body(a_ref, b_ref, o_ref, acc_ref):
        k = pl.program_id(2)

        @pl.when(k == 0)
        def _():
            acc_ref[...] = jnp.zeros_like(acc_ref)

        acc_ref[...] += _dot(a_ref[...].astype(BF16), b_ref[...].astype(BF16), dims)

        @pl.when(k == nr - 1)
        def _():
            o_ref[...] = acc_ref[...].astype(out_dtype)

    return pl.pallas_call(
        body,
        name=name,
        grid=(m // tm, n // tn, nr),
        in_specs=[a_spec, b_spec],
        out_specs=pl.BlockSpec((tm, tn), lambda i, j, k: (i, j)),
        out_shape=jax.ShapeDtypeStruct((m, n), out_dtype),
        scratch_shapes=[pltpu.VMEM((tm, tn), F32)],
        compiler_params=_cparams(("parallel", "parallel", "arbitrary")),
    )(a, b)


def _matmul_dw(a, b1, b2, slab, name):
    a = a.T.astype(BF16)
    m, r = a.shape
    n1 = b1.shape[1]
    n = n1 + (b2.shape[1] if b2 is not None else 0)
    tr = _tile(r, 1024, 128)
    if slab is None:
        tm, tn = _tile(m, 1024, 128), _tile(n1, 1024, 128)
        out_spec = pl.BlockSpec((tm, tn), lambda i, j, k: (i, j))
        out_shape = (m, n)
    elif slab[0] == "col":
        tm, tn = _tile(m, 1024, 128), slab[1]
        out_spec = pl.BlockSpec((None, None, tm, tn), lambda i, j, k: (j % 2, j // 2, i, 0))
        out_shape = (2, 4, m, tn)
        assert n == N_DEV * tn
    else:
        tm, tn = slab[1], _tile(n, 1024, 128)
        out_spec = pl.BlockSpec((None, None, tm, tn), lambda i, j, k: (i % 2, i // 2, 0, j))
        out_shape = (2, 4, tm, n)
        assert m == N_DEV * tm
    assert n1 % tn == 0 and n % tn == 0
    if tn > 1024:
        tr = _tile(r, 256, 128)
    n1b = n1 // tn
    nr = r // tr

    def body(*refs):
        a_ref, b_refs, o_ref, acc_ref = refs[0], refs[1:-2], refs[-2], refs[-1]
        j = pl.program_id(1)
        k = pl.program_id(2)

        @pl.when(k == 0)
        def _():
            acc_ref[...] = jnp.zeros_like(acc_ref)

        at = a_ref[...]
        if b2 is None:
            acc_ref[...] += _dot_nn(at, b_refs[0][...].astype(BF16))
        else:
            @pl.when(j < n1b)
            def _():
                acc_ref[...] += _dot_nn(at, b_refs[0][...].astype(BF16))

            @pl.when(j >= n1b)
            def _():
                acc_ref[...] += _dot_nn(at, b_refs[1][...].astype(BF16))

        @pl.when(k == nr - 1)
        def _():
            o_ref[...] = acc_ref[...]

    in_specs = [pl.BlockSpec((tm, tr), lambda i, j, k: (i, k))]
    args = [a, b1]
    if b2 is None:
        in_specs.append(pl.BlockSpec((tr, tn), lambda i, j, k: (k, j)))
    else:
        in_specs.append(pl.BlockSpec((tr, tn), lambda i, j, k: (jnp.where(j < n1b, k, nr - 1), jnp.minimum(j, n1b - 1))))
        in_specs.append(pl.BlockSpec((tr, tn), lambda i, j, k: (jnp.where(j < n1b, 0, k), jnp.maximum(j - n1b, 0))))
        args.append(b2)
    return pl.pallas_call(
        body, name=name, grid=(m // tm, n // tn, nr),
        in_specs=in_specs, out_specs=out_spec,
        out_shape=jax.ShapeDtypeStruct(out_shape, F32),
        scratch_shapes=[pltpu.VMEM((tm, tn), F32)],
        compiler_params=_cparams(("parallel", "parallel", "arbitrary")),
    )(*args)


def mm(a, w, carrier, slab=None, name="mm"):
    @jax.custom_vjp
    def f(a, w, carrier):
        return _matmul(a, w, "nn", 0, None, F32, name + "_fwd")

    def fwd(a, w, carrier):
        return _matmul(a, w, "nn", 0, None, F32, name + "_fwd"), (a, w)

    def bwd(res, g):
        a, w = res
        da = _matmul(g, w, "nt", 0, None, F32, name + "_dx")
        return da, jnp.zeros_like(w), _matmul_dw(a, g, None, slab, name + "_dw")

    f.defvjp(fwd, bwd)
    return f(a, w, carrier)


def in_proj(h, w, carriers, slab, name):
    def run(h, w):
        return (_matmul(h, w, "nn", 0, N_QKV, BF16, name + "_qkv"),
                _matmul(h, w, "nn", N_QKV, D_INNER, F32, name + "_g"))

    @jax.custom_vjp
    def f(h, w, *cars):
        return run(h, w)

    def fwd(h, w, *cars):
        return run(h, w), (h, w)

    def bwd(res, g):
        h, w = res
        g_qkv, g_gate = g
        dh = (_matmul(g_qkv, w, "nt", 0, None, F32, name + "_qkv_dx")
              + _matmul(g_gate, w, "nt", N_QKV, None, F32, name + "_g_dx"))
        if slab:
            dws = (_matmul_dw(h, g_qkv, g_gate, ("col", PACK_COLS), name + "_dw"),)
        else:
            dws = (_matmul_dw(h, g_qkv, None, None, name + "_qkv_dw"), _matmul_dw(h, g_gate, None, None, name + "_g_dw"))
        return (dh, jnp.zeros_like(w)) + dws

    f.defvjp(fwd, bwd)
    return f(h, w, *carriers)


def _rms_fwd(x, g, name):
    s, d = x.shape
    tm = _tile(s, 512, 8)

    def body(x_ref, g_ref, y_ref):
        x = x_ref[...]
        r = lax.rsqrt(jnp.mean(x * x, axis=-1, keepdims=True) + EPS)
        y_ref[...] = x * r * g_ref[...]

    return pl.pallas_call(
        body, name=name, grid=(s // tm,),
        in_specs=[pl.BlockSpec((tm, d), lambda i: (i, 0)), pl.BlockSpec((1, d), lambda i: (0, 0))],
        out_specs=pl.BlockSpec((tm, d), lambda i: (i, 0)),
        out_shape=jax.ShapeDtypeStruct((s, d), F32),
        compiler_params=_cparams(("parallel",)),
    )(x, g)


def _rms_bwd(x, g, dy, name):
    s, d = x.shape
    tm = _tile(s, 512, 8)

    def body(x_ref, g_ref, dy_ref, dx_ref, dg_ref):
        @pl.when(pl.program_id(0) == 0)
        def _():
            dg_ref[...] = jnp.zeros_like(dg_ref)

        x = x_ref[...]
        dy = dy_ref[...]
        r = lax.rsqrt(jnp.mean(x * x, axis=-1, keepdims=True) + EPS)
        xh = x * r
        dg_ref[...] += jnp.sum(dy * xh, axis=0, keepdims=True)
        dxh = dy * g_ref[...]
        dx_ref[...] = r * (dxh - xh * jnp.mean(dxh * xh, axis=-1, keepdims=True))

    return pl.pallas_call(
        body, name=name, grid=(s // tm,),
        in_specs=[pl.BlockSpec((tm, d), lambda i: (i, 0)), pl.BlockSpec((1, d), lambda i: (0, 0)),
                  pl.BlockSpec((tm, d), lambda i: (i, 0))],
        out_specs=[pl.BlockSpec((tm, d), lambda i: (i, 0)), pl.BlockSpec((1, d), lambda i: (0, 0))],
        out_shape=[jax.ShapeDtypeStruct((s, d), F32), jax.ShapeDtypeStruct((1, d), F32)],
        compiler_params=_cparams(("arbitrary",)),
    )(x, g, dy)


def rmsnorm(x, g, name="rms"):
    @jax.custom_vjp
    def f(x, g):
        return _rms_fwd(x, g.reshape(1, -1), name + "_fwd")

    def fwd(x, g):
        return _rms_fwd(x, g.reshape(1, -1), name + "_fwd"), (x, g)

    def bwd(res, dy):
        x, g = res
        dx, dg = _rms_bwd(x, g.reshape(1, -1), dy, name + "_bwd")
        return dx, dg.reshape(-1)

    f.defvjp(fwd, bwd)
    return f(x, g)


def _gate_fwd(o, gate, name):
    s = o.shape[0]
    tm = 256

    def body(o_ref, g_ref, y_ref):
        g = g_ref[...]
        y_ref[...] = o_ref[...] * (g / (1.0 + jnp.exp(-g)))

    row = pl.BlockSpec((tm, D_INNER), lambda i: (i, 0))
    return pl.pallas_call(
        body, name=name, grid=(s // tm,),
        in_specs=[row, row], out_specs=row,
        out_shape=jax.ShapeDtypeStruct((s, D_INNER), F32),
        compiler_params=_cparams(("parallel",)),
    )(o, gate)


def _gate_bwd(dy, o, gate, name):
    s = o.shape[0]
    tm = 256

    def body(dy_ref, o_ref, g_ref, do_ref, dg_ref, dl_ref):
        g = g_ref[...]
        o = o_ref[...]
        dy = dy_ref[...]
        sg = 1.0 / (1.0 + jnp.exp(-g))
        do = dy * (g * sg)
        do_ref[...] = do.astype(BF16)
        dg_ref[...] = dy * o * (sg * (1.0 + g * (1.0 - sg)))
        prod = do * o
        for h in range(N_HEADS):
            dl_ref[h] = jnp.sum(prod[:, h * HEAD_DIM:(h + 1) * HEAD_DIM], axis=1, keepdims=True)

    row = pl.BlockSpec((tm, D_INNER), lambda i: (i, 0))
    return pl.pallas_call(
        body, name=name, grid=(s // tm,),
        in_specs=[row, row, row],
        out_specs=[row, row, pl.BlockSpec((N_HEADS, tm, 1), lambda i: (0, i, 0))],
        out_shape=[jax.ShapeDtypeStruct((s, D_INNER), BF16), jax.ShapeDtypeStruct((s, D_INNER), F32),
                   jax.ShapeDtypeStruct((N_HEADS, s, 1), F32)],
        compiler_params=_cparams(("parallel",)),
    )(dy, o, gate)


SB_BLK = 256


def _softplus(z):
    return jnp.maximum(z, 0.0) + jnp.log(1.0 + jnp.exp(-jnp.abs(z)))


def _tri_sum(x, tri):
    hi = x.astype(BF16)
    lo = (x - hi.astype(F32)).astype(BF16)
    return _dot_nn(hi, tri) + _dot_nn(lo, tri)


def _sb_block(q, kb, i, j, a_run, row, col, tri_suffix, scale):
    z = _dot_nt(q, kb) * scale
    mask = (col + j * SB_BLK) < (row + i * SB_BLK)
    sp = _softplus(z)
    ls = jnp.where(mask, -sp, 0.0)
    suffix = _tri_sum(ls, tri_suffix)
    w = jnp.where(mask, jnp.exp(z + suffix + a_run), 0.0)
    return z, mask, sp, ls, w


def _sb_fwd(qkv, name):
    s = qkv.shape[0]
    b = SB_BLK
    nq = s // b
    scale = HEAD_DIM ** -0.5

    def body(q_ref, k_ref, v_ref, o_ref):
        i = pl.program_id(1)
        q = q_ref[...]
        row = lax.broadcasted_iota(jnp.int32, (b, b), 0)
        col = lax.broadcasted_iota(jnp.int32, (b, b), 1)
        tri_suffix = (row >= col).astype(BF16)

        def cond(c):
            j, a_run, _ = c
            return jnp.logical_and(j >= 0, jnp.max(a_run) > -SB_CUT)

        def step(c):
            j, a_run, acc = c
            off = pl.multiple_of(j * b, b)
            kb = k_ref[pl.ds(off, b), :]
            vb = v_ref[pl.ds(off, b), :]
            _, _, _, ls, w = _sb_block(q, kb, i, j, a_run, row, col, tri_suffix, scale)
            acc = acc + _dot_nn(w.astype(BF16), vb)
            return j - 1, a_run + jnp.sum(ls, axis=1, keepdims=True), acc

        _, _, acc = lax.while_loop(cond, step, (i, jnp.zeros((b, 1), F32), jnp.zeros((b, HEAD_DIM), F32)))
        o_ref[...] = acc

    return pl.pallas_call(
        body, name=name, grid=(N_HEADS, nq),
        in_specs=[pl.BlockSpec((b, HEAD_DIM), lambda h, i: (i, h)),
                  pl.BlockSpec((s, HEAD_DIM), lambda h, i: (0, N_HEADS + h)),
                  pl.BlockSpec((s, HEAD_DIM), lambda h, i: (0, 2 * N_HEADS + h))],
        out_specs=pl.BlockSpec((b, HEAD_DIM), lambda h, i: (i, h)),
        out_shape=jax.ShapeDtypeStruct((s, D_INNER), F32),
        compiler_params=_cparams(("parallel", "arbitrary")),
    )(qkv, qkv, qkv)


def _sb_bwd(qkv, do, name):
    s = qkv.shape[0]
    b = SB_BLK
    nq = s // b
    scale = HEAD_DIM ** -0.5

    def body(q_ref, k_ref, v_ref, do_ref, dq_ref, dk_ref, dv_ref, dkt_s, dvt_s, g_buf, sig_buf):
        i = pl.program_id(1)

        @pl.when(i == 0)
        def _():
            dkt_s[...] = jnp.zeros_like(dkt_s)
            dvt_s[...] = jnp.zeros_like(dvt_s)

        q = q_ref[...]
        dob = do_ref[...]
        q_t = _transpose_bf16(q)
        do_t = _transpose_bf16(dob)
        row = lax.broadcasted_iota(jnp.int32, (b, b), 0)
        col = lax.broadcasted_iota(jnp.int32, (b, b), 1)
        tri_suffix = (row >= col).astype(BF16)
        tri_prefix = (row <= col).astype(BF16)

        def cond(c):
            j, a_run = c
            return jnp.logical_and(j >= 0, jnp.max(a_run) > -SB_CUT)

        def sweep(c):
            j, a_run = c
            off = pl.multiple_of(j * b, b)
            kb = k_ref[pl.ds(off, b), :]
            vb = v_ref[pl.ds(off, b), :]
            z, _, sp, ls, w = _sb_block(q, kb, i, j, a_run, row, col, tri_suffix, scale)
            g_buf[j] = w * _dot_nt(dob, vb)
            sig_buf[j] = jnp.exp(z - sp)
            dvt_s[j] += _dot_nn(do_t, w.astype(BF16))
            return j - 1, a_run + jnp.sum(ls, axis=1, keepdims=True)

        j_end, _ = lax.while_loop(cond, sweep, (i, jnp.zeros((b, 1), F32)))

        def back(j, c):
            g_run, dq = c
            off = pl.multiple_of(j * b, b)
            kb = k_ref[pl.ds(off, b), :]
            g = g_buf[j]
            g_incl = g_run + _tri_sum(g, tri_prefix)
            mask = (col + j * b) < (row + i * b)
            dz = jnp.where(mask, (g - sig_buf[j] * g_incl) * scale, 0.0).astype(BF16)
            dkt_s[j] += _dot_nn(q_t, dz)
            return g_run + jnp.sum(g, axis=1, keepdims=True), dq + _dot_nn(dz, kb)

        _, dq = lax.fori_loop(j_end + 1, i + 1, back, (jnp.zeros((b, 1), F32), jnp.zeros((b, HEAD_DIM), F32)))
        dq_ref[...] = dq.astype(BF16)

        @pl.when(i == nq - 1)
        def _():
            for jb in range(nq):
                dk_ref[jb * b:(jb + 1) * b, :] = dkt_s[jb].T.astype(BF16)
                dv_ref[jb * b:(jb + 1) * b, :] = dvt_s[jb].T.astype(BF16)

    blk = pl.BlockSpec((b, HEAD_DIM), lambda h, i: (i, h))
    head = pl.BlockSpec((s, HEAD_DIM), lambda h, i: (0, h))
    return pl.pallas_call(
        body, name=name, grid=(N_HEADS, nq),
        in_specs=[blk,
                  pl.BlockSpec((s, HEAD_DIM), lambda h, i: (0, N_HEADS + h)),
                  pl.BlockSpec((s, HEAD_DIM), lambda h, i: (0, 2 * N_HEADS + h)),
                  blk],
        out_specs=[blk, head, head],
        out_shape=[jax.ShapeDtypeStruct((s, D_INNER), BF16)] * 3,
        scratch_shapes=[pltpu.VMEM((nq, HEAD_DIM, b), F32), pltpu.VMEM((nq, HEAD_DIM, b), F32),
                        pltpu.VMEM((nq, b, b), F32), pltpu.VMEM((nq, b, b), F32)],
        compiler_params=_cparams(("arbitrary", "arbitrary")),
    )(qkv, qkv, qkv, do)


def sb_core(qkv, gate, name):
    def run(qkv, gate):
        o = _sb_fwd(qkv, name + "_fwd")
        return _gate_fwd(o, gate, name + "_gate"), o

    @jax.custom_vjp
    def f(qkv, gate):
        return run(qkv, gate)[0]

    def fwd(qkv, gate):
        y, o = run(qkv, gate)
        return y, (qkv, gate, o)

    def bwd(res, dy):
        qkv, gate, o = res
        do, dgate, _ = _gate_bwd(dy, o, gate, name + "_gate_bwd")
        dq, dk, dv = _sb_bwd(qkv, do, name + "_bwd")
        return jnp.concatenate([dq, dk, dv], axis=1), dgate

    f.defvjp(fwd, bwd)
    return f(qkv, gate)


SM_BLK = 256


SM_FWD_KEYS = 1024
SM_BWD_KEYS = 512


def _sm_mask(i, jw, keys, chunk_shift):
    row = lax.broadcasted_iota(jnp.int32, (SM_BLK, keys), 0) + i * SM_BLK
    col = lax.broadcasted_iota(jnp.int32, (SM_BLK, keys), 1) + jw * keys
    return (col >> chunk_shift) <= (row >> chunk_shift)


def _sm_fwd(qa, ka, va, ccol, crow, dqk, qo, ko, vo, chunk_shift, scale, name):
    s = qa.shape[0]
    b = SM_BLK
    keys = min(SM_FWD_KEYS, s)
    per = keys // b
    nq = s // b
    has_bias = ccol is not None

    def body(*refs):
        if has_bias:
            q_ref, k_ref, v_ref, cc_ref, cr_ref, o_ref, lse_ref, m_s, l_s, acc_s = refs
        else:
            q_ref, k_ref, v_ref, o_ref, lse_ref, m_s, l_s, acc_s = refs
        i = pl.program_id(1)
        q = q_ref[...]
        m_s[...] = jnp.full_like(m_s, NEG)
        l_s[...] = jnp.zeros_like(l_s)
        acc_s[...] = jnp.zeros_like(acc_s)

        def sweep(jw, masked):
            off = pl.multiple_of(jw * keys, keys)
            z = _dot_nt(q, k_ref[pl.ds(off, keys), :]) * scale
            if has_bias:
                z = z + cc_ref[...] - cr_ref[jw]
            if masked:
                z = jnp.where(_sm_mask(i, jw, keys, chunk_shift), z, NEG)
            m_old = m_s[...]
            m_new = jnp.maximum(m_old, jnp.max(z, axis=1, keepdims=True))
            alpha = jnp.exp(m_old - m_new)
            p = jnp.exp(z - m_new)
            l_s[...] = alpha * l_s[...] + jnp.sum(p, axis=1, keepdims=True)
            acc_s[...] = alpha * acc_s[...] + _dot_nn(p.astype(BF16), v_ref[pl.ds(off, keys), :])
            m_s[...] = m_new

        def full(jw, carry):
            sweep(jw, False)
            return carry

        lax.fori_loop(0, i // per, full, 0)
        sweep(i // per, True)
        o_ref[...] = acc_s[...] / l_s[...]
        lse_ref[...] = m_s[...] + jnp.log(l_s[...])

    in_specs = [pl.BlockSpec((b, dqk), lambda h, i: (i, qo + h)),
                pl.BlockSpec((s, dqk), lambda h, i: (0, ko + h)),
                pl.BlockSpec((s, HEAD_DIM), lambda h, i: (0, vo + h))]
    args = [qa, ka, va]
    if has_bias:
        in_specs += [pl.BlockSpec((None, b, 1), lambda h, i: (h, i, 0)),
                     pl.BlockSpec((None, s // keys, 1, keys), lambda h, i: (h, 0, 0, 0))]
        args += [ccol, crow]
    return pl.pallas_call(
        body, name=name, grid=(N_HEADS, nq),
        in_specs=in_specs,
        out_specs=[pl.BlockSpec((b, HEAD_DIM), lambda h, i: (i, h)),
                   pl.BlockSpec((None, b, 1), lambda h, i: (h, i, 0))],
        out_shape=[jax.ShapeDtypeStruct((s, D_INNER), F32), jax.ShapeDtypeStruct((N_HEADS, s, 1), F32)],
        scratch_shapes=[pltpu.VMEM((b, 1), F32), pltpu.VMEM((b, 1), F32), pltpu.VMEM((b, HEAD_DIM), F32)],
        compiler_params=_cparams(("parallel", "arbitrary")),
    )(*args)


def _sm_bwd(qa, ka, va, do, lse, delta, ccol, crow, dqk, qo, ko, vo, chunk_shift, scale, grad_dtype, name):
    s = qa.shape[0]
    b = SM_BLK
    keys = min(SM_BWD_KEYS, s)
    per = keys // b
    nq = s // b
    nk = s // keys
    has_bias = ccol is not None

    def body(*refs):
        if has_bias:
            (q_ref, k_ref, v_ref, do_ref, lse_ref, dl_ref, cc_ref, cr_ref,
             dq_ref, dk_ref, dv_ref, dc_ref, dr_ref, dq_s, dkt_s, dvt_s, dc_s, dr_s) = refs
        else:
            (q_ref, k_ref, v_ref, do_ref, lse_ref, dl_ref,
             dq_ref, dk_ref, dv_ref, dq_s, dkt_s, dvt_s) = refs
        i = pl.program_id(1)

        @pl.when(i == 0)
        def _():
            dkt_s[...] = jnp.zeros_like(dkt_s)
            dvt_s[...] = jnp.zeros_like(dvt_s)
            if has_bias:
                dc_s[...] = jnp.zeros_like(dc_s)

        q = q_ref[...]
        dob = do_ref[...]
        q_t = _transpose_bf16(q)
        do_t = _transpose_bf16(dob)
        lse = lse_ref[...]
        delta = dl_ref[...]
        dq_s[...] = jnp.zeros_like(dq_s)
        if has_bias:
            dr_s[...] = jnp.zeros_like(dr_s)

        def sweep(jw, masked):
            off = pl.multiple_of(jw * keys, keys)
            kb = k_ref[pl.ds(off, keys), :]
            z = _dot_nt(q, kb) * scale
            if has_bias:
                z = z + cc_ref[...] - cr_ref[jw]
            p = jnp.exp(z - lse)
            if masked:
                p = jnp.where(_sm_mask(i, jw, keys, chunk_shift), p, 0.0)
            dvt_s[jw] += _dot_nn(do_t, p.astype(BF16))
            dz = p * (_dot_nt(dob, v_ref[pl.ds(off, keys), :]) - delta)
            if has_bias:
                dc_s[jw] += jnp.sum(dz, axis=0, keepdims=True)
                dr_s[...] += jnp.sum(dz, axis=1, keepdims=True)
            dzs = (dz * scale).astype(BF16)
            dkt_s[jw] += _dot_nn(q_t, dzs)
            dq_s[...] += _dot_nn(dzs, kb)

        def full(jw, carry):
            sweep(jw, False)
            return carry

        lax.fori_loop(0, i // per, full, 0)
        sweep(i // per, True)
        dq_ref[...] = dq_s[...].astype(grad_dtype)
        if has_bias:
            dr_ref[...] = dr_s[...]

        @pl.when(i == nq - 1)
        def _():
            for jw in range(nk):
                dk_ref[jw * keys:(jw + 1) * keys, :] = dkt_s[jw].T.astype(grad_dtype)
                dv_ref[jw * keys:(jw + 1) * keys, :] = dvt_s[jw].T.astype(grad_dtype)
            if has_bias:
                dc_ref[...] = dc_s[...]

    vec = pl.BlockSpec((None, b, 1), lambda h, i: (h, i, 0))
    in_specs = [pl.BlockSpec((b, dqk), lambda h, i: (i, qo + h)),
                pl.BlockSpec((s, dqk), lambda h, i: (0, ko + h)),
                pl.BlockSpec((s, HEAD_DIM), lambda h, i: (0, vo + h)),
                pl.BlockSpec((b, HEAD_DIM), lambda h, i: (i, h)),
                vec, vec]
    args = [qa, ka, va, do, lse, delta]
    out_specs = [pl.BlockSpec((b, dqk), lambda h, i: (i, h)),
                 pl.BlockSpec((s, dqk), lambda h, i: (0, h)),
                 pl.BlockSpec((s, HEAD_DIM), lambda h, i: (0, h))]
    out_shape = [jax.ShapeDtypeStruct((s, N_HEADS * dqk), grad_dtype),
                 jax.ShapeDtypeStruct((s, N_HEADS * dqk), grad_dtype),
                 jax.ShapeDtypeStruct((s, D_INNER), grad_dtype)]
    scratch = [pltpu.VMEM((b, dqk), F32), pltpu.VMEM((nk, dqk, keys), F32), pltpu.VMEM((nk, HEAD_DIM, keys), F32)]
    if has_bias:
        key_vec = pl.BlockSpec((None, nk, 1, keys), lambda h, i: (h, 0, 0, 0))
        in_specs += [vec, key_vec]
        args += [ccol, crow]
        out_specs += [key_vec, vec]
        out_shape += [jax.ShapeDtypeStruct((N_HEADS, nk, 1, keys), F32), jax.ShapeDtypeStruct((N_HEADS, s, 1), F32)]
        scratch += [pltpu.VMEM((nk, 1, keys), F32), pltpu.VMEM((b, 1), F32)]
    return pl.pallas_call(
        body, name=name, grid=(N_HEADS, nq),
        in_specs=in_specs, out_specs=out_specs, out_shape=out_shape, scratch_shapes=scratch,
        compiler_params=_cparams(("arbitrary", "arbitrary")),
    )(*args)


def fox_core(qkv, gate, c, name):
    s = qkv.shape[0]
    scale = HEAD_DIM ** -0.5
    cfg = dict(dqk=HEAD_DIM, qo=0, ko=N_HEADS, vo=2 * N_HEADS, chunk_shift=0, scale=scale)

    def layouts(c, keys):
        ct = c.T
        keys = min(keys, s)
        return ct.reshape(N_HEADS, s, 1), ct.reshape(N_HEADS, s // keys, 1, keys)

    def run(qkv, gate, c):
        ccol, crow = layouts(c, SM_FWD_KEYS)
        o, lse = _sm_fwd(qkv, qkv, qkv, ccol, crow, name=name + "_fwd", **cfg)
        return _gate_fwd(o, gate, name + "_gate"), o, lse

    @jax.custom_vjp
    def f(qkv, gate, c):
        return run(qkv, gate, c)[0]

    def fwd(qkv, gate, c):
        y, o, lse = run(qkv, gate, c)
        return y, (qkv, gate, c, o, lse)

    def bwd(res, dy):
        qkv, gate, c, o, lse = res
        ccol, crow = layouts(c, SM_BWD_KEYS)
        do, dgate, delta = _gate_bwd(dy, o, gate, name + "_gate_bwd")
        dq, dk, dv, colsum, rowsum = _sm_bwd(qkv, qkv, qkv, do, lse, delta, ccol, crow,
                                             grad_dtype=BF16, name=name + "_bwd", **cfg)
        dc = (rowsum.reshape(N_HEADS, s) - colsum.reshape(N_HEADS, s)).T
        return jnp.concatenate([dq, dk, dv], axis=1), dgate, dc

    f.defvjp(fwd, bwd)
    return f(qkv, gate, c)


def mla_core(qc, kc, v, gate, name):
    scale = (MLA_NOPE + MLA_ROPE) ** -0.5
    cfg = dict(dqk=MLA_QK_PAD, qo=0, ko=0, vo=0, chunk_shift=MLA_CHUNK.bit_length() - 1, scale=scale)

    def run(qc, kc, v, gate):
        o, lse = _sm_fwd(qc, kc, v, None, None, name=name + "_fwd", **cfg)
        return _gate_fwd(o, gate, name + "_gate"), o, lse

    @jax.custom_vjp
    def f(qc, kc, v, gate):
        return run(qc.astype(BF16), kc.astype(BF16), v.astype(BF16), gate)[0]

    def fwd(qc, kc, v, gate):
        qc, kc, v = qc.astype(BF16), kc.astype(BF16), v.astype(BF16)
        y, o, lse = run(qc, kc, v, gate)
        return y, (qc, kc, v, gate, o, lse)

    def bwd(res, dy):
        qc, kc, v, gate, o, lse = res
        do, dgate, delta = _gate_bwd(dy, o, gate, name + "_gate_bwd")
        dq, dk, dv = _sm_bwd(qc, kc, v, do, lse, delta, None, None, grad_dtype=F32, name=name + "_bwd", **cfg)
        return dq, dk, dv, dgate

    f.defvjp(fwd, bwd)
    return f(qc, kc, v, gate)


def _sq_loss_call(y, t, name):
    s, d = y.shape
    tm = _tile(s, 512, 8)

    def body(y_ref, t_ref, l_ref, e_ref):
        @pl.when(pl.program_id(0) == 0)
        def _():
            l_ref[...] = jnp.zeros_like(l_ref)

        e = y_ref[...] - t_ref[...]
        e_ref[...] = e * (1.0 / d)
        part = jnp.sum(jnp.sum(e * e, axis=1, keepdims=True), axis=0, keepdims=True)
        l_ref[...] += jnp.broadcast_to(part * (0.5 / d), l_ref.shape)

    row = pl.BlockSpec((tm, d), lambda i: (i, 0))
    return pl.pallas_call(
        body, name=name, grid=(s // tm,),
        in_specs=[row, row],
        out_specs=[pl.BlockSpec((8, 128), lambda i: (0, 0)), row],
        out_shape=[jax.ShapeDtypeStruct((8, 128), F32), jax.ShapeDtypeStruct((s, d), F32)],
        compiler_params=_cparams(("arbitrary",)),
    )(y, t)


@jax.custom_vjp
def sq_loss(y, t):
    return _sq_loss_call(y, t, "loss_fwd")[0][0, 0]


def _sq_loss_fwd(y, t):
    l, e = _sq_loss_call(y, t, "loss_fwd")
    return l[0, 0], e


def _sq_loss_bwd(e, g):
    return g * e, jnp.zeros_like(e)


sq_loss.defvjp(_sq_loss_fwd, _sq_loss_bwd)


def _cast_bf16(x, name):
    r, c = x.shape
    tb = _tile(r, 512, 16)

    def body(x_ref, o_ref):
        o_ref[...] = x_ref[...].astype(BF16)

    return pl.pallas_call(
        body, name=name, grid=(r // tb,),
        in_specs=[pl.BlockSpec((tb, c), lambda i: (i, 0))],
        out_specs=pl.BlockSpec((tb, c), lambda i: (i, 0)),
        out_shape=jax.ShapeDtypeStruct((r, c), BF16),
        compiler_params=_cparams(("parallel",)),
    )(x)


def _pair_sum(core_idx, g, recv, name):
    _, _, r, c = g.shape
    tb = _tile(r, 512, 16)

    def body(c_ref, g_ref, r_ref, o_ref):
        o_ref[...] = (g_ref[...] + r_ref[...]).astype(BF16)

    return pl.pallas_call(
        body, name=name,
        grid_spec=pltpu.PrefetchScalarGridSpec(
            num_scalar_prefetch=1, grid=(4, r // tb),
            in_specs=[pl.BlockSpec((None, None, tb, c), lambda q, i, c_ref: (c_ref[0], q, i, 0)),
                      pl.BlockSpec((None, tb, c), lambda q, i, c_ref: (q, i, 0))],
            out_specs=pl.BlockSpec((None, tb, c), lambda q, i, c_ref: (q, i, 0))),
        out_shape=jax.ShapeDtypeStruct((4, r, c), BF16),
        compiler_params=_cparams(("parallel", "parallel")),
    )(core_idx, g, recv)


def _adamw(w, parts, m, v, name):
    n, r, c = parts.shape
    tb = _tile(r, 256, 8)
    b1c = 1.0 - ADAM_B1 ** ADAM_STEP
    b2c = 1.0 - ADAM_B2 ** ADAM_STEP

    def body(w_ref, p_ref, m_ref, v_ref, g_ref, d_ref, nm_ref, nv_ref):
        g = p_ref[0].astype(F32)
        for k in range(1, n):
            g = g + p_ref[k].astype(F32)
        m_new = ADAM_B1 * m_ref[...] + (1.0 - ADAM_B1) * g
        v_new = ADAM_B2 * v_ref[...] + (1.0 - ADAM_B2) * (g * g)
        m_hat = m_new / b1c
        v_hat = v_new / b2c
        g_ref[...] = g
        d_ref[...] = -ADAM_LR * (m_hat / (jnp.sqrt(v_hat) + ADAM_EPS) + ADAM_WD * w_ref[...])
        nm_ref[...] = m_new
        nv_ref[...] = v_new

    row = pl.BlockSpec((tb, c), lambda i: (i, 0))
    return pl.pallas_call(
        body, name=name, grid=(r // tb,),
        in_specs=[row, pl.BlockSpec((n, tb, c), lambda i: (0, i, 0)), row, row],
        out_specs=[row] * 4,
        out_shape=[jax.ShapeDtypeStruct((r, c), F32)] * 4,
        compiler_params=_cparams(("parallel",)),
    )(w, parts, m, v)


ANY = pl.BlockSpec(memory_space=pl.ANY)


def _place():
    return lax.axis_index("x"), lax.axis_index("y"), lax.axis_index("c")


def _all_gather(shards, kinds, name):
    nm = len(shards)

    def out_shape(sh, kind):
        a, b = sh.shape
        return {"row": (N_DEV * a, b), "col": (a, N_DEV * b), "stack": (N_DEV, a, b)}[kind]

    def body(*refs):
        x_refs, out_refs = refs[:nm], refs[nm:2 * nm]
        send_sems, recv_sems, local_sems = refs[2 * nm:]
        x, y, cc = _place()
        me, sibling = (x, y, cc), (x, y, 1 - cc)
        chips = [(1 - x, y), (x, 1 - y), (1 - x, 1 - y)]

        def slot(mi, px, py, pc):
            d = 4 * px + 2 * py + pc
            a, b = x_refs[mi].shape
            if kinds[mi] == "row":
                return out_refs[mi].at[pl.ds(pl.multiple_of(d * a, a), a), :]
            if kinds[mi] == "col":
                return out_refs[mi].at[:, pl.ds(pl.multiple_of(d * b, 128), b)]
            return out_refs[mi].at[d]

        def copy(mi, k, block, to, src=None):
            return pltpu.make_async_remote_copy(
                src_ref=slot(mi, *block) if src is None else src, dst_ref=slot(mi, *block),
                send_sem=send_sems.at[7 * mi + k], recv_sem=recv_sems.at[7 * mi + k],
                device_id=to, device_id_type=MESH)

        mine = [pltpu.make_async_copy(x_refs[mi], slot(mi, *me), local_sems.at[mi]) for mi in range(nm)]
        for cp in mine:
            cp.start()
        first = []
        for mi in range(nm):
            first.append(copy(mi, 0, me, sibling, src=x_refs[mi]))
            first += [copy(mi, 1 + j, me, (*chip, cc), src=x_refs[mi]) for j, chip in enumerate(chips)]
        for cp in first:
            cp.start()
        passed = []
        for j, chip in enumerate(chips):
            for mi in range(nm):
                copy(mi, 1 + j, (*chip, cc), me).wait_recv()
                passed.append(copy(mi, 4 + j, (*chip, cc), sibling))
                passed[-1].start()
        for mi in range(nm):
            copy(mi, 0, sibling, me).wait_recv()
            for j, chip in enumerate(chips):
                copy(mi, 4 + j, (*chip, 1 - cc), me).wait_recv()
        for cp in first + passed:
            cp.wait_send()
        for cp in mine:
            cp.wait()

    return pl.pallas_call(
        body, name=name,
        out_shape=[jax.ShapeDtypeStruct(out_shape(sh, kd), sh.dtype) for sh, kd in zip(shards, kinds)],
        in_specs=[ANY] * nm, out_specs=[ANY] * nm,
        scratch_shapes=[pltpu.SemaphoreType.DMA((7 * nm,)), pltpu.SemaphoreType.DMA((7 * nm,)),
                        pltpu.SemaphoreType.DMA((nm,))],
    )(*shards)


def _pair_exchange(gs, name):
    nm = len(gs)

    def body(*refs):
        g_refs, recv_refs = refs[:nm], refs[nm:2 * nm]
        send_sems, recv_sems = refs[2 * nm:]
        x, y, cc = _place()
        copies = [pltpu.make_async_remote_copy(
            src_ref=g_refs[mi].at[1 - cc], dst_ref=recv_refs[mi],
            send_sem=send_sems.at[mi], recv_sem=recv_sems.at[mi], device_id=(x, y, 1 - cc), device_id_type=MESH)
            for mi in range(nm)]
        for cp in copies:
            cp.start()
        for cp in copies:
            cp.wait_recv()
        for cp in copies:
            cp.wait_send()

    return pl.pallas_call(
        body, name=name,
        out_shape=[jax.ShapeDtypeStruct(g.shape[1:], g.dtype) for g in gs],
        in_specs=[ANY] * nm, out_specs=[ANY] * nm,
        scratch_shapes=[pltpu.SemaphoreType.DMA((nm,)), pltpu.SemaphoreType.DMA((nm,))],
    )(*gs)


def _chip_exchange(parts, name):
    nm = len(parts)

    def body(*refs):
        p_refs, out_refs = refs[:nm], refs[nm:2 * nm]
        send_sems, recv_sems, local_sems = refs[2 * nm:]
        x, y, cc = _place()
        mine = 2 * x + y
        others = [(1 - x, y), (x, 1 - y), (1 - x, 1 - y)]
        keeps = [pltpu.make_async_copy(p_refs[mi].at[mine], out_refs[mi].at[mine], local_sems.at[mi]) for mi in range(nm)]
        for cp in keeps:
            cp.start()
        sends = []
        for px, py in others:
            q = 2 * px + py
            for mi in range(nm):
                sends.append(pltpu.make_async_remote_copy(
                    src_ref=p_refs[mi].at[q], dst_ref=out_refs[mi].at[mine],
                    send_sem=send_sems.at[4 * mi + q], recv_sem=recv_sems.at[4 * mi + mine],
                    device_id=(px, py, cc), device_id_type=MESH))
        for cp in sends:
            cp.start()
        for px, py in others:
            q = 2 * px + py
            for mi in range(nm):
                pltpu.make_async_remote_copy(
                    src_ref=p_refs[mi].at[q], dst_ref=out_refs[mi].at[q],
                    send_sem=send_sems.at[4 * mi + q], recv_sem=recv_sems.at[4 * mi + q],
                    device_id=(px, py, cc), device_id_type=MESH).wait_recv()
        for cp in sends:
            cp.wait_send()
        for cp in keeps:
            cp.wait()

    return pl.pallas_call(
        body, name=name,
        out_shape=[jax.ShapeDtypeStruct(p.shape, p.dtype) for p in parts],
        in_specs=[ANY] * nm, out_specs=[ANY] * nm,
        scratch_shapes=[pltpu.SemaphoreType.DMA((4 * nm,)), pltpu.SemaphoreType.DMA((4 * nm,)),
                        pltpu.SemaphoreType.DMA((nm,))],
    )(*parts)


def _all_reduce_small(v, name):
    shape = v.shape

    def body(v_ref, out_ref, buf, send_sems, recv_sems):
        x, y, cc = _place()
        me = 4 * x + 2 * y + cc
        buf[me] = v_ref[...]
        flips = [(a, b, d) for a in (0, 1) for b in (0, 1) for d in (0, 1)][1:]
        copies = []
        for k, (a, b, d) in enumerate(flips):
            peer = (x ^ a, y ^ b, cc ^ d)
            copies.append(pltpu.make_async_remote_copy(
                src_ref=v_ref, dst_ref=buf.at[me],
                send_sem=send_sems.at[k], recv_sem=recv_sems.at[k], device_id=peer, device_id_type=MESH))
        for cp in copies:
            cp.start()
        for k, (a, b, d) in enumerate(flips):
            peer_id = 4 * (x ^ a) + 2 * (y ^ b) + (cc ^ d)
            pltpu.make_async_remote_copy(
                src_ref=v_ref, dst_ref=buf.at[peer_id],
                send_sem=send_sems.at[k], recv_sem=recv_sems.at[k], device_id=(x, y, cc), device_id_type=MESH
            ).wait_recv()
        for cp in copies:
            cp.wait_send()
        total = buf[0]
        for k in range(1, N_DEV):
            total = total + buf[k]
        out_ref[...] = total

    vm = pl.BlockSpec(memory_space=pltpu.VMEM)
    return pl.pallas_call(
        body, name=name,
        out_shape=jax.ShapeDtypeStruct(shape, F32),
        in_specs=[vm], out_specs=vm,
        scratch_shapes=[pltpu.VMEM((N_DEV,) + shape, F32), pltpu.SemaphoreType.DMA((7,)), pltpu.SemaphoreType.DMA((7,))],
    )(v)


def _gather_kind(name, shape):
    if name not in COL_SHARDED:
        return "row"
    return "col" if shape[1] % 128 == 0 else "stack"


def _slab_of(name, shape):
    if name not in COL_SHARDED:
        return ("row", shape[0])
    return ("col", shape[1]) if shape[1] % 128 == 0 else None


def _to_slabs(g, shape):
    kk, nn = shape
    return g.reshape(kk, 4, 2, nn).transpose(2, 1, 0, 3)


SMALL_ROWS = 8


def _pack_small(arrs):
    rows = [arrs[n] for n in SMALL[:5]]
    last = jnp.concatenate([arrs["q_norm1"], arrs["kv_norm1"], arrs["b_f2"]])
    rows.append(jnp.pad(last, (0, PACK_COLS - last.shape[0])))
    rows += [jnp.zeros((PACK_COLS,), F32)] * (SMALL_ROWS - len(rows))
    return jnp.stack(rows)


def _unpack_small(p):
    out = {n: p[k] for k, n in enumerate(SMALL[:5])}
    out["q_norm1"] = p[5, :MLA_Q_RANK]
    out["kv_norm1"] = p[5, MLA_Q_RANK:MLA_Q_RANK + MLA_KV_RANK]
    out["b_f2"] = p[5, MLA_Q_RANK + MLA_KV_RANK:MLA_Q_RANK + MLA_KV_RANK + N_HEADS]
    return out


N_QKV = 3 * D_INNER
N_MAIN = 4 * D_INNER


def _rope(x, pos):
    r = x.shape[-1]
    inv_freq = ROPE_BASE ** (-jnp.arange(0, r, 2, dtype=F32) / r)
    ang = pos.astype(F32)[:, None, None] * inv_freq
    cos, sin = jnp.cos(ang), jnp.sin(ang)
    x1, x2 = x[..., : r // 2], x[..., r // 2:]
    return jnp.concatenate([x1 * cos - x2 * sin, x1 * sin + x2 * cos], axis=-1)


def _forward_loss(carriers, small, x, wfull, slabs, pos, target):
    s = x.shape[0]

    def out_proj(y, w_out, tag):
        return mm(y, wfull[w_out], carriers[w_out], slabs[w_out], name=tag + "_out")

    def sb_layer(x, ln, w_in, w_out, tag):
        h = rmsnorm(x, small[ln], tag + "_ln")
        qkv, gate = in_proj(h, wfull[w_in], (carriers[w_in],), True, tag)
        return x + out_proj(sb_core(qkv, gate, tag), w_out, tag)

    x = sb_layer(x, "ln0", "w_in0", "w_out0", "l0")

    h = rmsnorm(x, small["ln1"], "l1_ln")
    proj = mm(h, wfull["w_in1"], carriers["w_in1"], slabs["w_in1"], name="l1_in")
    i1, i2, i3 = MLA_Q_RANK, MLA_Q_RANK + MLA_KV_RANK, MLA_Q_RANK + MLA_KV_RANK + MLA_ROPE
    q = mm(rmsnorm(proj[:, :i1], small["q_norm1"], "l1_qn"), wfull["w_qb1"], carriers["w_qb1"], slabs["w_qb1"],
           name="l1_qb")
    q = q.reshape(s, N_HEADS, MLA_NOPE + MLA_ROPE)
    kv = mm(rmsnorm(proj[:, i1:i2], small["kv_norm1"], "l1_kvn"), wfull["w_kvb1"], carriers["w_kvb1"],
            slabs["w_kvb1"], name="l1_kvb")
    kv = kv.reshape(s, N_HEADS, MLA_NOPE + HEAD_DIM)
    k_rope = _rope(proj[:, i2:i3][:, None, :], pos)
    pad = jnp.zeros((s, N_HEADS, MLA_QK_PAD - MLA_NOPE - MLA_ROPE), F32)
    qc = jnp.concatenate([q[..., :MLA_NOPE], _rope(q[..., MLA_NOPE:], pos), pad], axis=-1)
    kc = jnp.concatenate([kv[..., :MLA_NOPE], jnp.broadcast_to(k_rope, (s, N_HEADS, MLA_ROPE)), pad], axis=-1)
    y = mla_core(qc.reshape(s, -1), kc.reshape(s, -1), kv[..., MLA_NOPE:].reshape(s, -1), proj[:, i3:], "l1")
    x = x + out_proj(y, "w_out1", "l1")

    h = rmsnorm(x, small["ln2"], "l2_ln")
    qkv, gate = in_proj(h, wfull["w_in2"], (carriers["w_in2_qkv"], carriers["w_in2_gate"]), False, "l2")
    f_logit = mm(h, wfull["w_in2"][:, N_MAIN:], carriers["w_in2_f"], None, name="l2_f") + small["b_f2"]
    c = jnp.cumsum(jax.nn.log_sigmoid(f_logit), axis=0)
    x = x + out_proj(fox_core(qkv, gate, c, "l2"), "w_out2", "l2")

    x = sb_layer(x, "ln3", "w_in3", "w_out3", "l3")
    return sq_loss(rmsnorm(x, small["final_norm"], "final_ln"), target)


def kernel(x, positions, ln0, w_in0, w_out0, ln1, w_in1, q_norm1, w_qb1, kv_norm1, w_kvb1, w_out1, ln2, w_in2, b_f2, w_out2, ln3, w_in3, w_out3, final_norm, loss_target, m_ln0, m_w_in0, m_w_out0, m_ln1, m_w_in1, m_q_norm1, m_w_qb1, m_kv_norm1, m_w_kvb1, m_w_out1, m_ln2, m_w_in2, m_b_f2, m_w_out2, m_ln3, m_w_in3, m_w_out3, m_final_norm, v_ln0, v_w_in0, v_w_out0, v_ln1, v_w_in1, v_q_norm1, v_w_qb1, v_kv_norm1, v_w_kvb1, v_w_out1, v_ln2, v_w_in2, v_b_f2, v_w_out2, v_ln3, v_w_in3, v_w_out3, v_final_norm):
    args = dict(locals())
    w = {n: args[n] for n in ALL_W}
    m = {n: args["m_" + n] for n in ALL_W}
    v = {n: args["v_" + n] for n in ALL_W}
    shapes = {n: w[n].shape for n in BIG}
    kinds = [_gather_kind(n, shapes[n]) for n in BIG]
    slabs = {n: _slab_of(n, shapes[n]) for n in BIG}

    gathered = _all_gather([w[n].astype(BF16) for n in BIG], kinds, "gather_w")
    wfull = {}
    for n, kind, full in zip(BIG, kinds, gathered):
        wfull[n] = full.transpose(1, 0, 2).reshape(shapes[n][0], -1) if kind == "stack" else full

    d_model = wfull["w_in0"].shape[0]
    carriers = {}
    for n in BIG:
        if n == "w_in2":
            continue
        if slabs[n] is None:
            carriers[n] = jnp.zeros(wfull[n].shape, F32)
        else:
            carriers[n] = jnp.zeros((2, 4) + shapes[n], F32)
    carriers["w_in2_qkv"] = jnp.zeros((d_model, N_QKV), F32)
    carriers["w_in2_gate"] = jnp.zeros((d_model, D_INNER), F32)
    carriers["w_in2_f"] = jnp.zeros((d_model, wfull["w_in2"].shape[1] - N_MAIN), F32)
    small = {n: w[n] for n in SMALL}

    def local_loss(carriers, small, x_seq):
        return _forward_loss(carriers, small, x_seq, wfull, slabs, positions[0], loss_target[0])

    loss_local, (g_car, g_small, g_x) = jax.value_and_grad(local_loss, argnums=(0, 1, 2))(carriers, small, x[0])
    loss = lax.psum(loss_local, ("x", "y", "c"))

    g_slab = {}
    for n in BIG:
        if n == "w_in2":
            g = jnp.concatenate([g_car["w_in2_qkv"], g_car["w_in2_gate"], g_car["w_in2_f"]], axis=1)
            g_slab[n] = _to_slabs(g, shapes[n])
        elif slabs[n] is None:
            g_slab[n] = _to_slabs(g_car[n], shapes[n])
        else:
            g_slab[n] = g_car[n]
    core_idx = lax.axis_index("c").astype(jnp.int32).reshape(1)
    from_sibling = _pair_exchange([g_slab[n] for n in BIG], "pair_exchange")
    chip_part = [_pair_sum(core_idx, g_slab[n], r, "pair_sum_" + n) for n, r in zip(BIG, from_sibling)]
    by_chip = _chip_exchange(chip_part, "chip_exchange")
    big = [{}, {}, {}, {}]
    for n, parts in zip(BIG, by_chip):
        for k, t in enumerate(_adamw(w[n], parts, m[n], v[n], "adamw_" + n)):
            big[k][n] = t

    g_small_sum = _all_reduce_small(_pack_small(g_small), "reduce_small")
    sm = _adamw(_pack_small(w), g_small_sum[None], _pack_small(m), _pack_small(v), "adamw_small")
    small_out = [_unpack_small(t) for t in sm]

    outs = [loss, g_x[None]]
    for k in range(4):
        outs += [small_out[k][n] if n in small_out[k] else big[k][n] for n in ALL_W]
    return tuple(outs)
```

```python
import jax
import jax.numpy as jnp
from jax import lax
from jax.experimental import pallas as pl
from jax.experimental.pallas import tpu as pltpu

F32 = jnp.float32
BF16 = jnp.bfloat16
MESH = pl.DeviceIdType.MESH

N_DEV = 8
N_HEADS = 16
HEAD_DIM = 128
D_INNER = N_HEADS * HEAD_DIM
MLA_Q_RANK = 256
MLA_KV_RANK = 128
MLA_NOPE = 128
MLA_ROPE = 64
MLA_QK_PAD = 256
MLA_CHUNK = 64
ROPE_BASE = 10000.0
EPS = 1e-6
NEG = -1e30
SB_CUT = 104.0

ADAM_LR = 0.001
ADAM_B1 = 0.9
ADAM_B2 = 0.999
ADAM_EPS = 1e-08
ADAM_WD = 0.01
ADAM_STEP = 10

PACK_COLS = 1024
VMEM_LIMIT = 56 * 1024 * 1024

BIG = ["w_in0", "w_out0", "w_in1", "w_qb1", "w_kvb1", "w_out1", "w_in2", "w_out2", "w_in3", "w_out3"]
COL_SHARDED = {"w_in0", "w_in1", "w_qb1", "w_kvb1", "w_in2", "w_in3"}
SMALL = ["ln0", "ln1", "ln2", "ln3", "final_norm", "q_norm1", "kv_norm1", "b_f2"]
ALL_W = ["ln0", "w_in0", "w_out0", "ln1", "w_in1", "q_norm1", "w_qb1", "kv_norm1", "w_kvb1", "w_out1",
         "ln2", "w_in2", "b_f2", "w_out2", "ln3", "w_in3", "w_out3", "final_norm"]


def _cparams(sem=None):
    return pltpu.CompilerParams(dimension_semantics=sem, vmem_limit_bytes=VMEM_LIMIT)


def _tile(dim, cap, align):
    if dim <= cap:
        return dim
    t = (cap // align) * align
    while t >= align:
        if dim % t == 0:
            return t
        t -= align
    return dim


def _dot(a, b, dims):
    return lax.dot_general(a, b, (dims, ((), ())), preferred_element_type=F32)


def _dot_nn(a, b):
    return _dot(a, b, ((1,), (0,)))


def _dot_nt(a, b):
    return _dot(a, b, ((1,), (1,)))


def _dot_tn(a, b):
    return _dot(a, b, ((0,), (0,)))


def _transpose_bf16(x):
    return x.astype(F32).T.astype(BF16)


def _matmul(a, b, mode, col0=0, n_cols=None, out_dtype=F32, name="mm"):
    if mode == "nn":
        m, r = a.shape
        n = n_cols or b.shape[1]
        tn, tr = _tile(n, 1024, 128), _tile(r, 1024, 128)
        tm = _tile(m, 1024 if tn <= 1024 else 512, 8)
        c0 = col0 // tn
        a_spec = pl.BlockSpec((tm, tr), lambda i, j, k: (i, k))
        b_spec = pl.BlockSpec((tr, tn), lambda i, j, k: (k, j + c0))
        dims = ((1,), (0,))
        assert col0 % tn == 0
    elif mode == "nt":
        m, r = a.shape
        n = b.shape[0]
        tn, tr = _tile(n, 1024, 128), _tile(r, 1024, 128)
        tm = _tile(m, 1024 if tr <= 1024 else 512, 8)
        c0 = col0 // tr
        a_spec = pl.BlockSpec((tm, tr), lambda i, j, k: (i, k))
        b_spec = pl.BlockSpec((tn, tr), lambda i, j, k: (j, k + c0))
        dims = ((1,), (1,))
        assert col0 % tr == 0
    else:
        r, m = a.shape
        n = b.shape[1]
        tm, tn, tr = _tile(m, 1024, 128), _tile(n, 1024, 128), _tile(r, 512, 16)
        a_spec = pl.BlockSpec((tr, tm), lambda i, j, k: (k, i))
        b_spec = pl.BlockSpec((tr, tn), lambda i, j, k: (k, j))
        dims = ((0,), (0,))
    nr = r // tr

    def body(a_ref, b_ref, o_ref, acc_ref):
        k = pl.program_id(2)

        @pl.when(k == 0)
        def _():
            acc_ref[...] = jnp.zeros_like(acc_ref)

        acc_ref[...] += _dot(a_ref[...].astype(BF16), b_ref[...].astype(BF16), dims)

        @pl.when(k == nr - 1)
        def _():
            o_ref[...] = acc_ref[...].astype(out_dtype)

    return pl.pallas_call(
        body,
        name=name,
        grid=(m // tm, n // tn, nr),
        in_specs=[a_spec, b_spec],
        out_specs=pl.BlockSpec((tm, tn), lambda i, j, k: (i, j)),
        out_shape=jax.ShapeDtypeStruct((m, n), out_dtype),
        scratch_shapes=[pltpu.VMEM((tm, tn), F32)],
        compiler_params=_cparams(("parallel", "parallel", "arbitrary")),
    )(a, b)


def _transpose_cast(a, name):
    r, m = a.shape
    tr = _tile(r, 512, 128)

    def body(a_ref, o_ref):
        o_ref[...] = a_ref[...].astype(F32).T.astype(BF16)

    return pl.pallas_call(
        body, name=name, grid=(r // tr,),
        in_specs=[pl.BlockSpec((tr, m), lambda i: (i, 0))],
        out_specs=pl.BlockSpec((m, tr), lambda i: (0, i)),
        out_shape=jax.ShapeDtypeStruct((m, r), BF16),
        compiler_params=_cparams(("parallel",)),
    )(a)


def _matmul_dw(a, b1, b2, slab, name):
    m, r = a.shape
    n1 = b1.shape[1]
    n = n1 + (b2.shape[1] if b2 is not None else 0)
    tr = _tile(r, 1024, 128)
    if slab is None:
        tm, tn = _tile(m, 1024, 128), _tile(n1, 1024, 128)
        out_spec = pl.BlockSpec((tm, tn), lambda i, j, k: (i, j))
        out_shape = (m, n)
    elif slab[0] == "col":
        tm, tn = _tile(m, 1024, 128), slab[1]
        out_spec = pl.BlockSpec((None, None, tm, tn), lambda i, j, k: (j % 2, j // 2, i, 0))
        out_shape = (2, 4, m, tn)
        assert n == N_DEV * tn
    else:
        tm, tn = slab[1], _tile(n, 1024, 128)
        out_spec = pl.BlockSpec((None, None, tm, tn), lambda i, j, k: (i % 2, i // 2, 0, j))
        out_shape = (2, 4, tm, n)
        assert m == N_DEV * tm
    assert n1 % tn == 0 and n % tn == 0
    if tn > 1024:
        tr = _tile(r, 256, 128)
    n1b = n1 // tn
    nr = r // tr

    def body(*refs):
        a_ref, b_refs, o_ref, acc_ref = refs[0], refs[1:-2], refs[-2], refs[-1]
        j = pl.program_id(1)
        k = pl.program_id(2)

        @pl.when(k == 0)
        def _():
            acc_ref[...] = jnp.zeros_like(acc_ref)

        at = a_ref[...]
        if b2 is None:
            acc_ref[...] += _dot_nn(at, b_refs[0][...].astype(BF16))
        else:
            @pl.when(j < n1b)
            def _():
                acc_ref[...] += _dot_nn(at, b_refs[0][...].astype(BF16))

            @pl.when(j >= n1b)
            def _():
                acc_ref[...] += _dot_nn(at, b_refs[1][...].astype(BF16))

        @pl.when(k == nr - 1)
        def _():
            o_ref[...] = acc_ref[...]

    in_specs = [pl.BlockSpec((tm, tr), lambda i, j, k: (i, k))]
    args = [a, b1]
    if b2 is None:
        in_specs.append(pl.BlockSpec((tr, tn), lambda i, j, k: (k, j)))
    else:
        in_specs.append(pl.BlockSpec((tr, tn), lambda i, j, k: (jnp.where(j < n1b, k, nr - 1), jnp.minimum(j, n1b - 1))))
        in_specs.append(pl.BlockSpec((tr, tn), lambda i, j, k: (jnp.where(j < n1b, 0, k), jnp.maximum(j - n1b, 0))))
        args.append(b2)
    return pl.pallas_call(
        body, name=name, grid=(m // tm, n // tn, nr),
        in_specs=in_specs, out_specs=out_spec,
        out_shape=jax.ShapeDtypeStruct(out_shape, F32),
        scratch_shapes=[pltpu.VMEM((tm, tn), F32)],
        compiler_params=_cparams(("parallel", "parallel", "arbitrary")),
    )(*args)


def mm(a, w, carrier, slab=None, name="mm"):
    @jax.custom_vjp
    def f(a, w, carrier):
        return _matmul(a, w, "nn", 0, None, F32, name + "_fwd")

    def fwd(a, w, carrier):
        return _matmul(a, w, "nn", 0, None, F32, name + "_fwd"), (a, w)

    def bwd(res, g):
        a, w = res
        da = _matmul(g, w, "nt", 0, None, F32, name + "_dx")
        return da, jnp.zeros_like(w), _matmul_dw(_transpose_cast(a, name + "_t"), g, None, slab, name + "_dw")

    f.defvjp(fwd, bwd)
    return f(a, w, carrier)


def in_proj(h, w, carriers, slab, name):
    def run(h, w):
        return (_matmul(h, w, "nn", 0, N_QKV, BF16, name + "_qkv"),
                _matmul(h, w, "nn", N_QKV, D_INNER, F32, name + "_g"))

    @jax.custom_vjp
    def f(h, w, *cars):
        return run(h, w)

    def fwd(h, w, *cars):
        return run(h, w), (h, w)

    def bwd(res, g):
        h, w = res
        g_qkv, g_gate = g
        dh = (_matmul(g_qkv, w, "nt", 0, None, F32, name + "_qkv_dx")
              + _matmul(g_gate, w, "nt", N_QKV, None, F32, name + "_g_dx"))
        h_t = _transpose_cast(h, name + "_t")
        if slab:
            dws = (_matmul_dw(h_t, g_qkv, g_gate, ("col", PACK_COLS), name + "_dw"),)
        else:
            dws = (_matmul_dw(h_t, g_qkv, None, None, name + "_qkv_dw"),
                   _matmul_dw(h_t, g_gate, None, None, name + "_g_dw"))
        return (dh, jnp.zeros_like(w)) + dws

    f.defvjp(fwd, bwd)
    return f(h, w, *carriers)


def _rms_fwd(x, g, name):
    s, d = x.shape
    tm = _tile(s, 512, 8)

    def body(x_ref, g_ref, y_ref):
        x = x_ref[...]
        r = lax.rsqrt(jnp.mean(x * x, axis=-1, keepdims=True) + EPS)
        y_ref[...] = x * r * g_ref[...]

    return pl.pallas_call(
        body, name=name, grid=(s // tm,),
        in_specs=[pl.BlockSpec((tm, d), lambda i: (i, 0)), pl.BlockSpec((1, d), lambda i: (0, 0))],
        out_specs=pl.BlockSpec((tm, d), lambda i: (i, 0)),
        out_shape=jax.ShapeDtypeStruct((s, d), F32),
        compiler_params=_cparams(("parallel",)),
    )(x, g)


def _rms_bwd(x, g, dy, name):
    s, d = x.shape
    tm = _tile(s, 512, 8)

    def body(x_ref, g_ref, dy_ref, dx_ref, dg_ref):
        @pl.when(pl.program_id(0) == 0)
        def _():
            dg_ref[...] = jnp.zeros_like(dg_ref)

        x = x_ref[...]
        dy = dy_ref[...]
        r = lax.rsqrt(jnp.mean(x * x, axis=-1, keepdims=True) + EPS)
        xh = x * r
        dg_ref[...] += jnp.sum(dy * xh, axis=0, keepdims=True)
        dxh = dy * g_ref[...]
        dx_ref[...] = r * (dxh - xh * jnp.mean(dxh * xh, axis=-1, keepdims=True))

    return pl.pallas_call(
        body, name=name, grid=(s // tm,),
        in_specs=[pl.BlockSpec((tm, d), lambda i: (i, 0)), pl.BlockSpec((1, d), lambda i: (0, 0)),
                  pl.BlockSpec((tm, d), lambda i: (i, 0))],
        out_specs=[pl.BlockSpec((tm, d), lambda i: (i, 0)), pl.BlockSpec((1, d), lambda i: (0, 0))],
        out_shape=[jax.ShapeDtypeStruct((s, d), F32), jax.ShapeDtypeStruct((1, d), F32)],
        compiler_params=_cparams(("arbitrary",)),
    )(x, g, dy)


def rmsnorm(x, g, name="rms"):
    @jax.custom_vjp
    def f(x, g):
        return _rms_fwd(x, g.reshape(1, -1), name + "_fwd")

    def fwd(x, g):
        return _rms_fwd(x, g.reshape(1, -1), name + "_fwd"), (x, g)

    def bwd(res, dy):
        x, g = res
        dx, dg = _rms_bwd(x, g.reshape(1, -1), dy, name + "_bwd")
        return dx, dg.reshape(-1)

    f.defvjp(fwd, bwd)
    return f(x, g)


def _gate_fwd(o, gate, name):
    s = o.shape[0]
    tm = 256

    def body(o_ref, g_ref, y_ref):
        g = g_ref[...]
        y_ref[...] = o_ref[...] * (g / (1.0 + jnp.exp(-g)))

    row = pl.BlockSpec((tm, D_INNER), lambda i: (i, 0))
    return pl.pallas_call(
        body, name=name, grid=(s // tm,),
        in_specs=[row, row], out_specs=row,
        out_shape=jax.ShapeDtypeStruct((s, D_INNER), F32),
        compiler_params=_cparams(("parallel",)),
    )(o, gate)


def _gate_bwd(dy, o, gate, name):
    s = o.shape[0]
    tm = 256

    def body(dy_ref, o_ref, g_ref, do_ref, dg_ref, dl_ref):
        g = g_ref[...]
        o = o_ref[...]
        dy = dy_ref[...]
        sg = 1.0 / (1.0 + jnp.exp(-g))
        do = dy * (g * sg)
        do_ref[...] = do.astype(BF16)
        dg_ref[...] = dy * o * (sg * (1.0 + g * (1.0 - sg)))
        prod = do * o
        for h in range(N_HEADS):
            dl_ref[h] = jnp.sum(prod[:, h * HEAD_DIM:(h + 1) * HEAD_DIM], axis=1, keepdims=True)

    row = pl.BlockSpec((tm, D_INNER), lambda i: (i, 0))
    return pl.pallas_call(
        body, name=name, grid=(s // tm,),
        in_specs=[row, row, row],
        out_specs=[row, row, pl.BlockSpec((N_HEADS, tm, 1), lambda i: (0, i, 0))],
        out_shape=[jax.ShapeDtypeStruct((s, D_INNER), BF16), jax.ShapeDtypeStruct((s, D_INNER), F32),
                   jax.ShapeDtypeStruct((N_HEADS, s, 1), F32)],
        compiler_params=_cparams(("parallel",)),
    )(dy, o, gate)


SB_BLK = 256


def _softplus(z):
    return jnp.maximum(z, 0.0) + jnp.log(1.0 + jnp.exp(-jnp.abs(z)))


def _tri_sum(x, tri):
    hi = x.astype(BF16)
    lo = (x - hi.astype(F32)).astype(BF16)
    return _dot_nn(hi, tri) + _dot_nn(lo, tri)


def _sb_block(q, kb, i, j, a_run, row, col, tri_suffix, scale):
    z = _dot_nt(q, kb) * scale
    mask = (col + j * SB_BLK) < (row + i * SB_BLK)
    sp = _softplus(z)
    ls = jnp.where(mask, -sp, 0.0)
    suffix = _tri_sum(ls, tri_suffix)
    w = jnp.where(mask, jnp.exp(z + suffix + a_run), 0.0)
    return z, mask, sp, ls, w


def _sb_fwd(qkv, name):
    s = qkv.shape[0]
    b = SB_BLK
    nq = s // b
    scale = HEAD_DIM ** -0.5

    def body(q_ref, k_ref, v_ref, o_ref):
        i = pl.program_id(1)
        q = q_ref[...]
        row = lax.broadcasted_iota(jnp.int32, (b, b), 0)
        col = lax.broadcasted_iota(jnp.int32, (b, b), 1)
        tri_suffix = (row >= col).astype(BF16)

        def cond(c):
            j, a_run, _ = c
            return jnp.logical_and(j >= 0, jnp.max(a_run) > -SB_CUT)

        def step(c):
            j, a_run, acc = c
            off = pl.multiple_of(j * b, b)
            kb = k_ref[pl.ds(off, b), :]
            vb = v_ref[pl.ds(off, b), :]
            _, _, _, ls, w = _sb_block(q, kb, i, j, a_run, row, col, tri_suffix, scale)
            acc = acc + _dot_nn(w.astype(BF16), vb)
            return j - 1, a_run + jnp.sum(ls, axis=1, keepdims=True), acc

        _, _, acc = lax.while_loop(cond, step, (i, jnp.zeros((b, 1), F32), jnp.zeros((b, HEAD_DIM), F32)))
        o_ref[...] = acc

    return pl.pallas_call(
        body, name=name, grid=(N_HEADS, nq),
        in_specs=[pl.BlockSpec((b, HEAD_DIM), lambda h, i: (i, h)),
                  pl.BlockSpec((s, HEAD_DIM), lambda h, i: (0, N_HEADS + h)),
                  pl.BlockSpec((s, HEAD_DIM), lambda h, i: (0, 2 * N_HEADS + h))],
        out_specs=pl.BlockSpec((b, HEAD_DIM), lambda h, i: (i, h)),
        out_shape=jax.ShapeDtypeStruct((s, D_INNER), F32),
        compiler_params=_cparams(("parallel", "arbitrary")),
    )(qkv, qkv, qkv)


def _sb_bwd(qkv, do, name):
    s = qkv.shape[0]
    b = SB_BLK
    nq = s // b
    scale = HEAD_DIM ** -0.5

    def body(q_ref, k_ref, v_ref, do_ref, dq_ref, dk_ref, dv_ref, dkt_s, dvt_s, g_buf, sig_buf):
        i = pl.program_id(1)

        @pl.when(i == 0)
        def _():
            dkt_s[...] = jnp.zeros_like(dkt_s)
            dvt_s[...] = jnp.zeros_like(dvt_s)

        q = q_ref[...]
        dob = do_ref[...]
        q_t = _transpose_bf16(q)
        do_t = _transpose_bf16(dob)
        row = lax.broadcasted_iota(jnp.int32, (b, b), 0)
        col = lax.broadcasted_iota(jnp.int32, (b, b), 1)
        tri_suffix = (row >= col).astype(BF16)
        tri_prefix = (row <= col).astype(BF16)

        def cond(c):
            j, a_run = c
            return jnp.logical_and(j >= 0, jnp.max(a_run) > -SB_CUT)

        def sweep(c):
            j, a_run = c
            off = pl.multiple_of(j * b, b)
            kb = k_ref[pl.ds(off, b), :]
            vb = v_ref[pl.ds(off, b), :]
            z, _, sp, ls, w = _sb_block(q, kb, i, j, a_run, row, col, tri_suffix, scale)
            g_buf[j] = w * _dot_nt(dob, vb)
            sig_buf[j] = jnp.exp(z - sp)
            dvt_s[j] += _dot_nn(do_t, w.astype(BF16))
            return j - 1, a_run + jnp.sum(ls, axis=1, keepdims=True)

        j_end, _ = lax.while_loop(cond, sweep, (i, jnp.zeros((b, 1), F32)))

        def back(j, c):
            g_run, dq = c
            off = pl.multiple_of(j * b, b)
            kb = k_ref[pl.ds(off, b), :]
            g = g_buf[j]
            g_incl = g_run + _tri_sum(g, tri_prefix)
            mask = (col + j * b) < (row + i * b)
            dz = jnp.where(mask, (g - sig_buf[j] * g_incl) * scale, 0.0).astype(BF16)
            dkt_s[j] += _dot_nn(q_t, dz)
            return g_run + jnp.sum(g, axis=1, keepdims=True), dq + _dot_nn(dz, kb)

        _, dq = lax.fori_loop(j_end + 1, i + 1, back, (jnp.zeros((b, 1), F32), jnp.zeros((b, HEAD_DIM), F32)))
        dq_ref[...] = dq.astype(BF16)

        @pl.when(i == nq - 1)
        def _():
            for jb in range(nq):
                dk_ref[jb * b:(jb + 1) * b, :] = dkt_s[jb].T.astype(BF16)
                dv_ref[jb * b:(jb + 1) * b, :] = dvt_s[jb].T.astype(BF16)

    blk = pl.BlockSpec((b, HEAD_DIM), lambda h, i: (i, h))
    head = pl.BlockSpec((s, HEAD_DIM), lambda h, i: (0, h))
    return pl.pallas_call(
        body, name=name, grid=(N_HEADS, nq),
        in_specs=[blk,
                  pl.BlockSpec((s, HEAD_DIM), lambda h, i: (0, N_HEADS + h)),
                  pl.BlockSpec((s, HEAD_DIM), lambda h, i: (0, 2 * N_HEADS + h)),
                  blk],
        out_specs=[blk, head, head],
        out_shape=[jax.ShapeDtypeStruct((s, D_INNER), BF16)] * 3,
        scratch_shapes=[pltpu.VMEM((nq, HEAD_DIM, b), F32), pltpu.VMEM((nq, HEAD_DIM, b), F32),
                        pltpu.VMEM((nq, b, b), F32), pltpu.VMEM((nq, b, b), F32)],
        compiler_params=_cparams(("arbitrary", "arbitrary")),
    )(qkv, qkv, qkv, do)


def sb_core(qkv, gate, name):
    def run(qkv, gate):
        o = _sb_fwd(qkv, name + "_fwd")
        return _gate_fwd(o, gate, name + "_gate"), o

    @jax.custom_vjp
    def f(qkv, gate):
        return run(qkv, gate)[0]

    def fwd(qkv, gate):
        y, o = run(qkv, gate)
        return y, (qkv, gate, o)

    def bwd(res, dy):
        qkv, gate, o = res
        do, dgate, _ = _gate_bwd(dy, o, gate, name + "_gate_bwd")
        dq, dk, dv = _sb_bwd(qkv, do, name + "_bwd")
        return jnp.concatenate([dq, dk, dv], axis=1), dgate

    f.defvjp(fwd, bwd)
    return f(qkv, gate)


SM_FWD_BLK = 256
SM_BLK = 512


SM_FWD_KEYS = 1024
SM_BWD_KEYS = 512


def _sm_mask(i, jw, rows, keys, chunk_shift):
    row = lax.broadcasted_iota(jnp.int32, (rows, keys), 0) + i * rows
    col = lax.broadcasted_iota(jnp.int32, (rows, keys), 1) + jw * keys
    return (col >> chunk_shift) <= (row >> chunk_shift)


def _sm_fwd(qa, ka, va, ccol, crow, dqk, qo, ko, vo, chunk_shift, scale, name):
    s = qa.shape[0]
    b = min(SM_FWD_BLK, s)
    keys = min(SM_FWD_KEYS, s)
    per = keys // b
    nq = s // b
    has_bias = ccol is not None

    def body(*refs):
        if has_bias:
            q_ref, k_ref, v_ref, cc_ref, cr_ref, o_ref, lse_ref, m_s, l_s, acc_s = refs
        else:
            q_ref, k_ref, v_ref, o_ref, lse_ref, m_s, l_s, acc_s = refs
        i = pl.program_id(1)
        q = q_ref[...]
        m_s[...] = jnp.full_like(m_s, NEG)
        l_s[...] = jnp.zeros_like(l_s)
        acc_s[...] = jnp.zeros_like(acc_s)

        def sweep(jw, masked):
            off = pl.multiple_of(jw * keys, keys)
            z = _dot_nt(q, k_ref[pl.ds(off, keys), :]) * scale
            if has_bias:
                z = z + cc_ref[...] - cr_ref[jw]
            if masked:
                z = jnp.where(_sm_mask(i, jw, b, keys, chunk_shift), z, NEG)
            m_old = m_s[...]
            m_new = jnp.maximum(m_old, jnp.max(z, axis=1, keepdims=True))
            alpha = jnp.exp(m_old - m_new)
            p = jnp.exp(z - m_new)
            l_s[...] = alpha * l_s[...] + jnp.sum(p, axis=1, keepdims=True)
            acc_s[...] = alpha * acc_s[...] + _dot_nn(p.astype(BF16), v_ref[pl.ds(off, keys), :])
            m_s[...] = m_new

        def full(jw, carry):
            sweep(jw, False)
            return carry

        lax.fori_loop(0, i // per, full, 0)
        sweep(i // per, True)
        o_ref[...] = acc_s[...] / l_s[...]
        lse_ref[...] = m_s[...] + jnp.log(l_s[...])

    in_specs = [pl.BlockSpec((b, dqk), lambda h, i: (i, qo + h)),
                pl.BlockSpec((s, dqk), lambda h, i: (0, ko + h)),
                pl.BlockSpec((s, HEAD_DIM), lambda h, i: (0, vo + h))]
    args = [qa, ka, va]
    if has_bias:
        in_specs += [pl.BlockSpec((None, b, 1), lambda h, i: (h, i, 0)),
                     pl.BlockSpec((None, s // keys, 1, keys), lambda h, i: (h, 0, 0, 0))]
        args += [ccol, crow]
    return pl.pallas_call(
        body, name=name, grid=(N_HEADS, nq),
        in_specs=in_specs,
        out_specs=[pl.BlockSpec((b, HEAD_DIM), lambda h, i: (i, h)),
                   pl.BlockSpec((None, b, 1), lambda h, i: (h, i, 0))],
        out_shape=[jax.ShapeDtypeStruct((s, D_INNER), F32), jax.ShapeDtypeStruct((N_HEADS, s, 1), F32)],
        scratch_shapes=[pltpu.VMEM((b, 1), F32), pltpu.VMEM((b, 1), F32), pltpu.VMEM((b, HEAD_DIM), F32)],
        compiler_params=_cparams(("parallel", "arbitrary")),
    )(*args)


def _sm_bwd(qa, ka, va, do, lse, delta, ccol, crow, dqk, qo, ko, vo, chunk_shift, scale, grad_dtype, name):
    s = qa.shape[0]
    b = SM_BLK
    keys = min(SM_BWD_KEYS, s)
    per = keys // b
    nq = s // b
    nk = s // keys
    has_bias = ccol is not None

    def body(*refs):
        if has_bias:
            (q_ref, k_ref, v_ref, do_ref, lse_ref, dl_ref, cc_ref, cr_ref,
             dq_ref, dk_ref, dv_ref, dc_ref, dr_ref, dq_s, dkt_s, dvt_s, dc_s, dr_s) = refs
        else:
            (q_ref, k_ref, v_ref, do_ref, lse_ref, dl_ref,
             dq_ref, dk_ref, dv_ref, dq_s, dkt_s, dvt_s) = refs
        i = pl.program_id(1)

        @pl.when(i == 0)
        def _():
            dkt_s[...] = jnp.zeros_like(dkt_s)
            dvt_s[...] = jnp.zeros_like(dvt_s)
            if has_bias:
                dc_s[...] = jnp.zeros_like(dc_s)

        q = q_ref[...]
        dob = do_ref[...]
        q_t = _transpose_bf16(q)
        do_t = _transpose_bf16(dob)
        lse = lse_ref[...]
        delta = dl_ref[...]
        dq_s[...] = jnp.zeros_like(dq_s)
        if has_bias:
            dr_s[...] = jnp.zeros_like(dr_s)

        def sweep(jw, masked):
            off = pl.multiple_of(jw * keys, keys)
            kb = k_ref[pl.ds(off, keys), :]
            z = _dot_nt(q, kb) * scale
            if has_bias:
                z = z + cc_ref[...] - cr_ref[jw]
            p = jnp.exp(z - lse)
            if masked:
                p = jnp.where(_sm_mask(i, jw, b, keys, chunk_shift), p, 0.0)
            dvt_s[jw] += _dot_nn(do_t, p.astype(BF16))
            dz = p * (_dot_nt(dob, v_ref[pl.ds(off, keys), :]) - delta)
            if has_bias:
                dc_s[jw] += jnp.sum(dz, axis=0, keepdims=True)
                dr_s[...] += jnp.sum(dz, axis=1, keepdims=True)
            dzs = (dz * scale).astype(BF16)
            dkt_s[jw] += _dot_nn(q_t, dzs)
            dq_s[...] += _dot_nn(dzs, kb)

        def full(jw, carry):
            sweep(jw, False)
            return carry

        lax.fori_loop(0, i // per, full, 0)
        sweep(i // per, True)
        dq_ref[...] = dq_s[...].astype(grad_dtype)
        if has_bias:
            dr_ref[...] = dr_s[...]

        @pl.when(i == nq - 1)
        def _():
            for jw in range(nk):
                dk_ref[jw * keys:(jw + 1) * keys, :] = dkt_s[jw].T.astype(grad_dtype)
                dv_ref[jw * keys:(jw + 1) * keys, :] = dvt_s[jw].T.astype(grad_dtype)
            if has_bias:
                dc_ref[...] = dc_s[...]

    vec = pl.BlockSpec((None, b, 1), lambda h, i: (h, i, 0))
    in_specs = [pl.BlockSpec((b, dqk), lambda h, i: (i, qo + h)),
                pl.BlockSpec((s, dqk), lambda h, i: (0, ko + h)),
                pl.BlockSpec((s, HEAD_DIM), lambda h, i: (0, vo + h)),
                pl.BlockSpec((b, HEAD_DIM), lambda h, i: (i, h)),
                vec, vec]
    args = [qa, ka, va, do, lse, delta]
    out_specs = [pl.BlockSpec((b, dqk), lambda h, i: (i, h)),
                 pl.BlockSpec((s, dqk), lambda h, i: (0, h)),
                 pl.BlockSpec((s, HEAD_DIM), lambda h, i: (0, h))]
    out_shape = [jax.ShapeDtypeStruct((s, N_HEADS * dqk), grad_dtype),
                 jax.ShapeDtypeStruct((s, N_HEADS * dqk), grad_dtype),
                 jax.ShapeDtypeStruct((s, D_INNER), grad_dtype)]
    scratch = [pltpu.VMEM((b, dqk), F32), pltpu.VMEM((nk, dqk, keys), F32), pltpu.VMEM((nk, HEAD_DIM, keys), F32)]
    if has_bias:
        key_vec = pl.BlockSpec((None, nk, 1, keys), lambda h, i: (h, 0, 0, 0))
        in_specs += [vec, key_vec]
        args += [ccol, crow]
        out_specs += [key_vec, vec]
        out_shape += [jax.ShapeDtypeStruct((N_HEADS, nk, 1, keys), F32), jax.ShapeDtypeStruct((N_HEADS, s, 1), F32)]
        scratch += [pltpu.VMEM((nk, 1, keys), F32), pltpu.VMEM((b, 1), F32)]
    return pl.pallas_call(
        body, name=name, grid=(N_HEADS, nq),
        in_specs=in_specs, out_specs=out_specs, out_shape=out_shape, scratch_shapes=scratch,
        compiler_params=_cparams(("arbitrary", "arbitrary")),
    )(*args)


def fox_core(qkv, gate, c, name):
    s = qkv.shape[0]
    scale = HEAD_DIM ** -0.5
    cfg = dict(dqk=HEAD_DIM, qo=0, ko=N_HEADS, vo=2 * N_HEADS, chunk_shift=0, scale=scale)

    def layouts(c, keys):
        ct = c.T
        keys = min(keys, s)
        return ct.reshape(N_HEADS, s, 1), ct.reshape(N_HEADS, s // keys, 1, keys)

    def run(qkv, gate, c):
        ccol, crow = layouts(c, SM_FWD_KEYS)
        o, lse = _sm_fwd(qkv, qkv, qkv, ccol, crow, name=name + "_fwd", **cfg)
        return _gate_fwd(o, gate, name + "_gate"), o, lse

    @jax.custom_vjp
    def f(qkv, gate, c):
        return run(qkv, gate, c)[0]

    def fwd(qkv, gate, c):
        y, o, lse = run(qkv, gate, c)
        return y, (qkv, gate, c, o, lse)

    def bwd(res, dy):
        qkv, gate, c, o, lse = res
        ccol, crow = layouts(c, SM_BWD_KEYS)
        do, dgate, delta = _gate_bwd(dy, o, gate, name + "_gate_bwd")
        dq, dk, dv, colsum, rowsum = _sm_bwd(qkv, qkv, qkv, do, lse, delta, ccol, crow,
                                             grad_dtype=BF16, name=name + "_bwd", **cfg)
        dc = (rowsum.reshape(N_HEADS, s) - colsum.reshape(N_HEADS, s)).T
        return jnp.concatenate([dq, dk, dv], axis=1), dgate, dc

    f.defvjp(fwd, bwd)
    return f(qkv, gate, c)


def mla_core(qc, kc, v, gate, name):
    scale = (MLA_NOPE + MLA_ROPE) ** -0.5
    cfg = dict(dqk=MLA_QK_PAD, qo=0, ko=0, vo=0, chunk_shift=MLA_CHUNK.bit_length() - 1, scale=scale)

    def run(qc, kc, v, gate):
        o, lse = _sm_fwd(qc, kc, v, None, None, name=name + "_fwd", **cfg)
        return _gate_fwd(o, gate, name + "_gate"), o, lse

    @jax.custom_vjp
    def f(qc, kc, v, gate):
        return run(qc.astype(BF16), kc.astype(BF16), v.astype(BF16), gate)[0]

    def fwd(qc, kc, v, gate):
        qc, kc, v = qc.astype(BF16), kc.astype(BF16), v.astype(BF16)
        y, o, lse = run(qc, kc, v, gate)
        return y, (qc, kc, v, gate, o, lse)

    def bwd(res, dy):
        qc, kc, v, gate, o, lse = res
        do, dgate, delta = _gate_bwd(dy, o, gate, name + "_gate_bwd")
        dq, dk, dv = _sm_bwd(qc, kc, v, do, lse, delta, None, None, grad_dtype=F32, name=name + "_bwd", **cfg)
        return dq, dk, dv, dgate

    f.defvjp(fwd, bwd)
    return f(qc, kc, v, gate)


def _sq_loss_call(y, t, name):
    s, d = y.shape
    tm = _tile(s, 512, 8)

    def body(y_ref, t_ref, l_ref, e_ref):
        @pl.when(pl.program_id(0) == 0)
        def _():
            l_ref[...] = jnp.zeros_like(l_ref)

        e = y_ref[...] - t_ref[...]
        e_ref[...] = e * (1.0 / d)
        part = jnp.sum(jnp.sum(e * e, axis=1, keepdims=True), axis=0, keepdims=True)
        l_ref[...] += jnp.broadcast_to(part * (0.5 / d), l_ref.shape)

    row = pl.BlockSpec((tm, d), lambda i: (i, 0))
    return pl.pallas_call(
        body, name=name, grid=(s // tm,),
        in_specs=[row, row],
        out_specs=[pl.BlockSpec((8, 128), lambda i: (0, 0)), row],
        out_shape=[jax.ShapeDtypeStruct((8, 128), F32), jax.ShapeDtypeStruct((s, d), F32)],
        compiler_params=_cparams(("arbitrary",)),
    )(y, t)


@jax.custom_vjp
def sq_loss(y, t):
    return _sq_loss_call(y, t, "loss_fwd")[0][0, 0]


def _sq_loss_fwd(y, t):
    l, e = _sq_loss_call(y, t, "loss_fwd")
    return l[0, 0], e


def _sq_loss_bwd(e, g):
    return g * e, jnp.zeros_like(e)


sq_loss.defvjp(_sq_loss_fwd, _sq_loss_bwd)


def _cast_bf16(x, name):
    r, c = x.shape
    tb = _tile(r, 512, 16)

    def body(x_ref, o_ref):
        o_ref[...] = x_ref[...].astype(BF16)

    return pl.pallas_call(
        body, name=name, grid=(r // tb,),
        in_specs=[pl.BlockSpec((tb, c), lambda i: (i, 0))],
        out_specs=pl.BlockSpec((tb, c), lambda i: (i, 0)),
        out_shape=jax.ShapeDtypeStruct((r, c), BF16),
        compiler_params=_cparams(("parallel",)),
    )(x)


def _pair_sum(core_idx, g, recv, name):
    _, _, r, c = g.shape
    tb = _tile(r, 512, 16)

    def body(c_ref, g_ref, r_ref, o_ref):
        o_ref[...] = (g_ref[...] + r_ref[...]).astype(BF16)

    return pl.pallas_call(
        body, name=name,
        grid_spec=pltpu.PrefetchScalarGridSpec(
            num_scalar_prefetch=1, grid=(4, r // tb),
            in_specs=[pl.BlockSpec((None, None, tb, c), lambda q, i, c_ref: (c_ref[0], q, i, 0)),
                      pl.BlockSpec((None, tb, c), lambda q, i, c_ref: (q, i, 0))],
            out_specs=pl.BlockSpec((None, tb, c), lambda q, i, c_ref: (q, i, 0))),
        out_shape=jax.ShapeDtypeStruct((4, r, c), BF16),
        compiler_params=_cparams(("parallel", "parallel")),
    )(core_idx, g, recv)


def _adamw(w, parts, m, v, name):
    n, r, c = parts.shape
    tb = _tile(r, 256, 8)
    b1c = 1.0 - ADAM_B1 ** ADAM_STEP
    b2c = 1.0 - ADAM_B2 ** ADAM_STEP

    def body(w_ref, p_ref, m_ref, v_ref, g_ref, d_ref, nm_ref, nv_ref):
        g = p_ref[0].astype(F32)
        for k in range(1, n):
            g = g + p_ref[k].astype(F32)
        m_new = ADAM_B1 * m_ref[...] + (1.0 - ADAM_B1) * g
        v_new = ADAM_B2 * v_ref[...] + (1.0 - ADAM_B2) * (g * g)
        m_hat = m_new / b1c
        v_hat = v_new / b2c
        g_ref[...] = g
        d_ref[...] = -ADAM_LR * (m_hat / (jnp.sqrt(v_hat) + ADAM_EPS) + ADAM_WD * w_ref[...])
        nm_ref[...] = m_new
        nv_ref[...] = v_new

    row = pl.BlockSpec((tb, c), lambda i: (i, 0))
    return pl.pallas_call(
        body, name=name, grid=(r // tb,),
        in_specs=[row, pl.BlockSpec((n, tb, c), lambda i: (0, i, 0)), row, row],
        out_specs=[row] * 4,
        out_shape=[jax.ShapeDtypeStruct((r, c), F32)] * 4,
        compiler_params=_cparams(("parallel",)),
    )(w, parts, m, v)


ANY = pl.BlockSpec(memory_space=pl.ANY)


def _place():
    return lax.axis_index("x"), lax.axis_index("y"), lax.axis_index("c")


def _all_gather(shards, kinds, name):
    nm = len(shards)

    def out_shape(sh, kind):
        a, b = sh.shape
        return {"row": (N_DEV * a, b), "col": (a, N_DEV * b), "stack": (N_DEV, a, b)}[kind]

    def body(*refs):
        x_refs, out_refs = refs[:nm], refs[nm:2 * nm]
        send_sems, recv_sems, local_sems = refs[2 * nm:]
        x, y, cc = _place()
        me, sibling = (x, y, cc), (x, y, 1 - cc)
        chips = [(1 - x, y), (x, 1 - y), (1 - x, 1 - y)]

        def slot(mi, px, py, pc):
            d = 4 * px + 2 * py + pc
            a, b = x_refs[mi].shape
            if kinds[mi] == "row":
                return out_refs[mi].at[pl.ds(pl.multiple_of(d * a, a), a), :]
            if kinds[mi] == "col":
                return out_refs[mi].at[:, pl.ds(pl.multiple_of(d * b, 128), b)]
            return out_refs[mi].at[d]

        def copy(mi, k, block, to, src=None):
            return pltpu.make_async_remote_copy(
                src_ref=slot(mi, *block) if src is None else src, dst_ref=slot(mi, *block),
                send_sem=send_sems.at[7 * mi + k], recv_sem=recv_sems.at[7 * mi + k],
                device_id=to, device_id_type=MESH)

        mine = [pltpu.make_async_copy(x_refs[mi], slot(mi, *me), local_sems.at[mi]) for mi in range(nm)]
        for cp in mine:
            cp.start()
        first = []
        for mi in range(nm):
            first.append(copy(mi, 0, me, sibling, src=x_refs[mi]))
            first += [copy(mi, 1 + j, me, (*chip, cc), src=x_refs[mi]) for j, chip in enumerate(chips)]
        for cp in first:
            cp.start()
        passed = []
        for j, chip in enumerate(chips):
            for mi in range(nm):
                copy(mi, 1 + j, (*chip, cc), me).wait_recv()
                passed.append(copy(mi, 4 + j, (*chip, cc), sibling))
                passed[-1].start()
        for mi in range(nm):
            copy(mi, 0, sibling, me).wait_recv()
            for j, chip in enumerate(chips):
                copy(mi, 4 + j, (*chip, 1 - cc), me).wait_recv()
        for cp in first + passed:
            cp.wait_send()
        for cp in mine:
            cp.wait()

    return pl.pallas_call(
        body, name=name,
        out_shape=[jax.ShapeDtypeStruct(out_shape(sh, kd), sh.dtype) for sh, kd in zip(shards, kinds)],
        in_specs=[ANY] * nm, out_specs=[ANY] * nm,
        scratch_shapes=[pltpu.SemaphoreType.DMA((7 * nm,)), pltpu.SemaphoreType.DMA((7 * nm,)),
                        pltpu.SemaphoreType.DMA((nm,))],
    )(*shards)


def _pair_exchange(gs, name):
    nm = len(gs)

    def body(*refs):
        g_refs, recv_refs = refs[:nm], refs[nm:2 * nm]
        send_sems, recv_sems = refs[2 * nm:]
        x, y, cc = _place()
        copies = [pltpu.make_async_remote_copy(
            src_ref=g_refs[mi].at[1 - cc], dst_ref=recv_refs[mi],
            send_sem=send_sems.at[mi], recv_sem=recv_sems.at[mi], device_id=(x, y, 1 - cc), device_id_type=MESH)
            for mi in range(nm)]
        for cp in copies:
            cp.start()
        for cp in copies:
            cp.wait_recv()
        for cp in copies:
            cp.wait_send()

    return pl.pallas_call(
        body, name=name,
        out_shape=[jax.ShapeDtypeStruct(g.shape[1:], g.dtype) for g in gs],
        in_specs=[ANY] * nm, out_specs=[ANY] * nm,
        scratch_shapes=[pltpu.SemaphoreType.DMA((nm,)), pltpu.SemaphoreType.DMA((nm,))],
    )(*gs)


def _chip_exchange(parts, name):
    nm = len(parts)

    def body(*refs):
        p_refs, out_refs = refs[:nm], refs[nm:2 * nm]
        send_sems, recv_sems, local_sems = refs[2 * nm:]
        x, y, cc = _place()
        mine = 2 * x + y
        others = [(1 - x, y), (x, 1 - y), (1 - x, 1 - y)]
        keeps = [pltpu.make_async_copy(p_refs[mi].at[mine], out_refs[mi].at[mine], local_sems.at[mi]) for mi in range(nm)]
        for cp in keeps:
            cp.start()
        sends = []
        for px, py in others:
            q = 2 * px + py
            for mi in range(nm):
                sends.append(pltpu.make_async_remote_copy(
                    src_ref=p_refs[mi].at[q], dst_ref=out_refs[mi].at[mine],
                    send_sem=send_sems.at[4 * mi + q], recv_sem=recv_sems.at[4 * mi + mine],
                    device_id=(px, py, cc), device_id_type=MESH))
        for cp in sends:
            cp.start()
        for px, py in others:
            q = 2 * px + py
            for mi in range(nm):
                pltpu.make_async_remote_copy(
                    src_ref=p_refs[mi].at[q], dst_ref=out_refs[mi].at[q],
                    send_sem=send_sems.at[4 * mi + q], recv_sem=recv_sems.at[4 * mi + q],
                    device_id=(px, py, cc), device_id_type=MESH).wait_recv()
        for cp in sends:
            cp.wait_send()
        for cp in keeps:
            cp.wait()

    return pl.pallas_call(
        body, name=name,
        out_shape=[jax.ShapeDtypeStruct(p.shape, p.dtype) for p in parts],
        in_specs=[ANY] * nm, out_specs=[ANY] * nm,
        scratch_shapes=[pltpu.SemaphoreType.DMA((4 * nm,)), pltpu.SemaphoreType.DMA((4 * nm,)),
                        pltpu.SemaphoreType.DMA((nm,))],
    )(*parts)


def _all_reduce_small(v, name):
    shape = v.shape

    def body(v_ref, out_ref, buf, send_sems, recv_sems):
        x, y, cc = _place()
        me = 4 * x + 2 * y + cc
        buf[me] = v_ref[...]
        flips = [(a, b, d) for a in (0, 1) for b in (0, 1) for d in (0, 1)][1:]
        copies = []
        for k, (a, b, d) in enumerate(flips):
            peer = (x ^ a, y ^ b, cc ^ d)
            copies.append(pltpu.make_async_remote_copy(
                src_ref=v_ref, dst_ref=buf.at[me],
                send_sem=send_sems.at[k], recv_sem=recv_sems.at[k], device_id=peer, device_id_type=MESH))
        for cp in copies:
            cp.start()
        for k, (a, b, d) in enumerate(flips):
            peer_id = 4 * (x ^ a) + 2 * (y ^ b) + (cc ^ d)
            pltpu.make_async_remote_copy(
                src_ref=v_ref, dst_ref=buf.at[peer_id],
                send_sem=send_sems.at[k], recv_sem=recv_sems.at[k], device_id=(x, y, cc), device_id_type=MESH
            ).wait_recv()
        for cp in copies:
            cp.wait_send()
        total = buf[0]
        for k in range(1, N_DEV):
            total = total + buf[k]
        out_ref[...] = total

    vm = pl.BlockSpec(memory_space=pltpu.VMEM)
    return pl.pallas_call(
        body, name=name,
        out_shape=jax.ShapeDtypeStruct(shape, F32),
        in_specs=[vm], out_specs=vm,
        scratch_shapes=[pltpu.VMEM((N_DEV,) + shape, F32), pltpu.SemaphoreType.DMA((7,)), pltpu.SemaphoreType.DMA((7,))],
    )(v)


def _gather_kind(name, shape):
    if name not in COL_SHARDED:
        return "row"
    return "col" if shape[1] % 128 == 0 else "stack"


def _slab_of(name, shape):
    if name not in COL_SHARDED:
        return ("row", shape[0])
    return ("col", shape[1]) if shape[1] % 128 == 0 else None


def _to_slabs(g, shape):
    kk, nn = shape
    return g.reshape(kk, 4, 2, nn).transpose(2, 1, 0, 3)


SMALL_ROWS = 8


def _pack_small(arrs):
    rows = [arrs[n] for n in SMALL[:5]]
    last = jnp.concatenate([arrs["q_norm1"], arrs["kv_norm1"], arrs["b_f2"]])
    rows.append(jnp.pad(last, (0, PACK_COLS - last.shape[0])))
    rows += [jnp.zeros((PACK_COLS,), F32)] * (SMALL_ROWS - len(rows))
    return jnp.stack(rows)


def _unpack_small(p):
    out = {n: p[k] for k, n in enumerate(SMALL[:5])}
    out["q_norm1"] = p[5, :MLA_Q_RANK]
    out["kv_norm1"] = p[5, MLA_Q_RANK:MLA_Q_RANK + MLA_KV_RANK]
    out["b_f2"] = p[5, MLA_Q_RANK + MLA_KV_RANK:MLA_Q_RANK + MLA_KV_RANK + N_HEADS]
    return out


N_QKV = 3 * D_INNER
N_MAIN = 4 * D_INNER


def _rope(x, pos):
    r = x.shape[-1]
    inv_freq = ROPE_BASE ** (-jnp.arange(0, r, 2, dtype=F32) / r)
    ang = pos.astype(F32)[:, None, None] * inv_freq
    cos, sin = jnp.cos(ang), jnp.sin(ang)
    x1, x2 = x[..., : r // 2], x[..., r // 2:]
    return jnp.concatenate([x1 * cos - x2 * sin, x1 * sin + x2 * cos], axis=-1)


def _forward_loss(carriers, small, x, wfull, slabs, pos, target):
    s = x.shape[0]

    def out_proj(y, w_out, tag):
        return mm(y, wfull[w_out], carriers[w_out], slabs[w_out], name=tag + "_out")

    def sb_layer(x, ln, w_in, w_out, tag):
        h = rmsnorm(x, small[ln], tag + "_ln")
        qkv, gate = in_proj(h, wfull[w_in], (carriers[w_in],), True, tag)
        return x + out_proj(sb_core(qkv, gate, tag), w_out, tag)

    x = sb_layer(x, "ln0", "w_in0", "w_out0", "l0")

    h = rmsnorm(x, small["ln1"], "l1_ln")
    proj = mm(h, wfull["w_in1"], carriers["w_in1"], slabs["w_in1"], name="l1_in")
    i1, i2, i3 = MLA_Q_RANK, MLA_Q_RANK + MLA_KV_RANK, MLA_Q_RANK + MLA_KV_RANK + MLA_ROPE
    q = mm(rmsnorm(proj[:, :i1], small["q_norm1"], "l1_qn"), wfull["w_qb1"], carriers["w_qb1"], slabs["w_qb1"],
           name="l1_qb")
    q = q.reshape(s, N_HEADS, MLA_NOPE + MLA_ROPE)
    kv = mm(rmsnorm(proj[:, i1:i2], small["kv_norm1"], "l1_kvn"), wfull["w_kvb1"], carriers["w_kvb1"],
            slabs["w_kvb1"], name="l1_kvb")
    kv = kv.reshape(s, N_HEADS, MLA_NOPE + HEAD_DIM)
    k_rope = _rope(proj[:, i2:i3][:, None, :], pos)
    pad = jnp.zeros((s, N_HEADS, MLA_QK_PAD - MLA_NOPE - MLA_ROPE), F32)
    qc = jnp.concatenate([q[..., :MLA_NOPE], _rope(q[..., MLA_NOPE:], pos), pad], axis=-1)
    kc = jnp.concatenate([kv[..., :MLA_NOPE], jnp.broadcast_to(k_rope, (s, N_HEADS, MLA_ROPE)), pad], axis=-1)
    y = mla_core(qc.reshape(s, -1), kc.reshape(s, -1), kv[..., MLA_NOPE:].reshape(s, -1), proj[:, i3:], "l1")
    x = x + out_proj(y, "w_out1", "l1")

    h = rmsnorm(x, small["ln2"], "l2_ln")
    qkv, gate = in_proj(h, wfull["w_in2"], (carriers["w_in2_qkv"], carriers["w_in2_gate"]), False, "l2")
    f_logit = mm(h, wfull["w_in2"][:, N_MAIN:], carriers["w_in2_f"], None, name="l2_f") + small["b_f2"]
    c = jnp.cumsum(jax.nn.log_sigmoid(f_logit), axis=0)
    x = x + out_proj(fox_core(qkv, gate, c, "l2"), "w_out2", "l2")

    x = sb_layer(x, "ln3", "w_in3", "w_out3", "l3")
    return sq_loss(rmsnorm(x, small["final_norm"], "final_ln"), target)


def kernel(x, positions, ln0, w_in0, w_out0, ln1, w_in1, q_norm1, w_qb1, kv_norm1, w_kvb1, w_out1, ln2, w_in2, b_f2, w_out2, ln3, w_in3, w_out3, final_norm, loss_target, m_ln0, m_w_in0, m_w_out0, m_ln1, m_w_in1, m_q_norm1, m_w_qb1, m_kv_norm1, m_w_kvb1, m_w_out1, m_ln2, m_w_in2, m_b_f2, m_w_out2, m_ln3, m_w_in3, m_w_out3, m_final_norm, v_ln0, v_w_in0, v_w_out0, v_ln1, v_w_in1, v_q_norm1, v_w_qb1, v_kv_norm1, v_w_kvb1, v_w_out1, v_ln2, v_w_in2, v_b_f2, v_w_out2, v_ln3, v_w_in3, v_w_out3, v_final_norm):
    args = dict(locals())
    w = {n: args[n] for n in ALL_W}
    m = {n: args["m_" + n] for n in ALL_W}
    v = {n: args["v_" + n] for n in ALL_W}
    shapes = {n: w[n].shape for n in BIG}
    kinds = [_gather_kind(n, shapes[n]) for n in BIG]
    slabs = {n: _slab_of(n, shapes[n]) for n in BIG}

    gathered = _all_gather([w[n].astype(BF16) for n in BIG], kinds, "gather_w")
    wfull = {}
    for n, kind, full in zip(BIG, kinds, gathered):
        wfull[n] = full.transpose(1, 0, 2).reshape(shapes[n][0], -1) if kind == "stack" else full

    d_model = wfull["w_in0"].shape[0]
    carriers = {}
    for n in BIG:
        if n == "w_in2":
            continue
        if slabs[n] is None:
            carriers[n] = jnp.zeros(wfull[n].shape, F32)
        else:
            carriers[n] = jnp.zeros((2, 4) + shapes[n], F32)
    carriers["w_in2_qkv"] = jnp.zeros((d_model, N_QKV), F32)
    carriers["w_in2_gate"] = jnp.zeros((d_model, D_INNER), F32)
    carriers["w_in2_f"] = jnp.zeros((d_model, wfull["w_in2"].shape[1] - N_MAIN), F32)
    small = {n: w[n] for n in SMALL}

    def local_loss(carriers, small, x_seq):
        return _forward_loss(carriers, small, x_seq, wfull, slabs, positions[0], loss_target[0])

    loss_local, (g_car, g_small, g_x) = jax.value_and_grad(local_loss, argnums=(0, 1, 2))(carriers, small, x[0])
    loss = lax.psum(loss_local, ("x", "y", "c"))

    g_slab = {}
    for n in BIG:
        if n == "w_in2":
            g = jnp.concatenate([g_car["w_in2_qkv"], g_car["w_in2_gate"], g_car["w_in2_f"]], axis=1)
            g_slab[n] = _to_slabs(g, shapes[n])
        elif slabs[n] is None:
            g_slab[n] = _to_slabs(g_car[n], shapes[n])
        else:
            g_slab[n] = g_car[n]
    core_idx = lax.axis_index("c").astype(jnp.int32).reshape(1)
    from_sibling = _pair_exchange([g_slab[n] for n in BIG], "pair_exchange")
    chip_part = [_pair_sum(core_idx, g_slab[n], r, "pair_sum_" + n) for n, r in zip(BIG, from_sibling)]
    by_chip = _chip_exchange(chip_part, "chip_exchange")
    big = [{}, {}, {}, {}]
    for n, parts in zip(BIG, by_chip):
        for k, t in enumerate(_adamw(w[n], parts, m[n], v[n], "adamw_" + n)):
            big[k][n] = t

    g_small_sum = _all_reduce_small(_pack_small(g_small), "reduce_small")
    sm = _adamw(_pack_small(w), g_small_sum[None], _pack_small(m), _pack_small(v), "adamw_small")
    small_out = [_unpack_small(t) for t in sm]

    outs = [loss, g_x[None]]
    for k in range(4):
        outs += [small_out[k][n] if n in small_out[k] else big[k][n] for n in ALL_W]
    return tuple(outs)
```

```python
import jax
import jax.numpy as jnp
from jax import lax
from jax.experimental import pallas as pl
from jax.experimental.pallas import tpu as pltpu

F32 = jnp.float32
BF16 = jnp.bfloat16
MESH = pl.DeviceIdType.MESH

N_DEV = 8
N_HEADS = 16
HEAD_DIM = 128
D_INNER = N_HEADS * HEAD_DIM
MLA_Q_RANK = 256
MLA_KV_RANK = 128
MLA_NOPE = 128
MLA_ROPE = 64
MLA_QK_PAD = 256
MLA_CHUNK = 64
ROPE_BASE = 10000.0
EPS = 1e-6
NEG = -1e30
SB_CUT = 104.0

ADAM_LR = 0.001
ADAM_B1 = 0.9
ADAM_B2 = 0.999
ADAM_EPS = 1e-08
ADAM_WD = 0.01
ADAM_STEP = 10

PACK_COLS = 1024
VMEM_LIMIT = 56 * 1024 * 1024

BIG = ["w_in0", "w_out0", "w_in1", "w_qb1", "w_kvb1", "w_out1", "w_in2", "w_out2", "w_in3", "w_out3"]
COL_SHARDED = {"w_in0", "w_in1", "w_qb1", "w_kvb1", "w_in2", "w_in3"}
SMALL = ["ln0", "ln1", "ln2", "ln3", "final_norm", "q_norm1", "kv_norm1", "b_f2"]
ALL_W = ["ln0", "w_in0", "w_out0", "ln1", "w_in1", "q_norm1", "w_qb1", "kv_norm1", "w_kvb1", "w_out1",
         "ln2", "w_in2", "b_f2", "w_out2", "ln3", "w_in3", "w_out3", "final_norm"]


def _cparams(sem=None):
    return pltpu.CompilerParams(dimension_semantics=sem, vmem_limit_bytes=VMEM_LIMIT)


def _tile(dim, cap, align):
    if dim <= cap:
        return dim
    t = (cap // align) * align
    while t >= align:
        if dim % t == 0:
            return t
        t -= align
    return dim


def _dot(a, b, dims):
    return lax.dot_general(a, b, (dims, ((), ())), preferred_element_type=F32)


def _dot_nn(a, b):
    return _dot(a, b, ((1,), (0,)))


def _dot_nt(a, b):
    return _dot(a, b, ((1,), (1,)))


def _dot_tn(a, b):
    return _dot(a, b, ((0,), (0,)))


def _transpose_bf16(x):
    return x.astype(F32).T.astype(BF16)


def _matmul(a, b, mode, col0=0, n_cols=None, out_dtype=F32, name="mm"):
    if mode == "nn":
        m, r = a.shape
        n = n_cols or b.shape[1]
        tn, tr = _tile(n, 1024, 128), _tile(r, 1024, 128)
        tm = _tile(m, 1024 if tn <= 1024 else 512, 8)
        c0 = col0 // tn
        a_spec = pl.BlockSpec((tm, tr), lambda i, j, k: (i, k))
        b_spec = pl.BlockSpec((tr, tn), lambda i, j, k: (k, j + c0))
        dims = ((1,), (0,))
        assert col0 % tn == 0
    elif mode == "nt":
        m, r = a.shape
        n = b.shape[0]
        tn, tr = _tile(n, 1024, 128), _tile(r, 1024, 128)
        tm = _tile(m, 1024 if tr <= 1024 else 512, 8)
        c0 = col0 // tr
        a_spec = pl.BlockSpec((tm, tr), lambda i, j, k: (i, k))
        b_spec = pl.BlockSpec((tn, tr), lambda i, j, k: (j, k + c0))
        dims = ((1,), (1,))
        assert col0 % tr == 0
    else:
        r, m = a.shape
        n = b.shape[1]
        tm, tn, tr = _tile(m, 1024, 128), _tile(n, 1024, 128), _tile(r, 512, 16)
        a_spec = pl.BlockSpec((tr, tm), lambda i, j, k: (k, i))
        b_spec = pl.BlockSpec((tr, tn), lambda i, j, k: (k, j))
        dims = ((0,), (0,))
    nr = r // tr

    def body(a_ref, b_ref, o_ref, acc_ref):
        k = pl.program_id(2)

        @pl.when(k == 0)
        def _():
            acc_ref[...] = jnp.zeros_like(acc_ref)

        acc_ref[...] += _dot(a_ref[...].astype(BF16), b_ref[...].astype(BF16), dims)

        @pl.when(k == nr - 1)
        def _():
            o_ref[...] = acc_ref[...].astype(out_dtype)

    return pl.pallas_call(
        body,
        name=name,
        grid=(m // tm, n // tn, nr),
        in_specs=[a_spec, b_spec],
        out_specs=pl.BlockSpec((tm, tn), lambda i, j, k: (i, j)),
        out_shape=jax.ShapeDtypeStruct((m, n), out_dtype),
        scratch_shapes=[pltpu.VMEM((tm, tn), F32)],
        compiler_params=_cparams(("parallel", "parallel", "arbitrary")),
    )(a, b)


def _transpose_cast(a, name):
    r, m = a.shape
    tr = _tile(r, 512, 128)

    def body(a_ref, o_ref):
        o_ref[...] = a_ref[...].astype(F32).T.astype(BF16)

    return pl.pallas_call(
        body, name=name, grid=(r // tr,),
        in_specs=[pl.BlockSpec((tr, m), lambda i: (i, 0))],
        out_specs=pl.BlockSpec((m, tr), lambda i: (0, i)),
        out_shape=jax.ShapeDtypeStruct((m, r), BF16),
        compiler_params=_cparams(("parallel",)),
    )(a)


def _matmul_dw(a, b1, b2, slab, name):
    m, r = a.shape
    n1 = b1.shape[1]
    n = n1 + (b2.shape[1] if b2 is not None else 0)
    tr = _tile(r, 1024, 128)
    if slab is None:
        tm, tn = _tile(m, 1024, 128), _tile(n1, 1024, 128)
        out_spec = pl.BlockSpec((tm, tn), lambda i, j, k: (i, j))
        out_shape = (m, n)
    elif slab[0] == "col":
        tm, tn = _tile(m, 1024, 128), slab[1]
        out_spec = pl.BlockSpec((None, None, tm, tn), lambda i, j, k: (j % 2, j // 2, i, 0))
        out_shape = (2, 4, m, tn)
        assert n == N_DEV * tn
    else:
        tm, tn = slab[1], _tile(n, 1024, 128)
        out_spec = pl.BlockSpec((None, None, tm, tn), lambda i, j, k: (i % 2, i // 2, 0, j))
        out_shape = (2, 4, tm, n)
        assert m == N_DEV * tm
    assert n1 % tn == 0 and n % tn == 0
    if tn > 1024:
        tr = _tile(r, 256, 128)
    n1b = n1 // tn
    nr = r // tr

    def body(*refs):
        a_ref, b_refs, o_ref, acc_ref = refs[0], refs[1:-2], refs[-2], refs[-1]
        j = pl.program_id(1)
        k = pl.program_id(2)

        @pl.when(k == 0)
        def _():
            acc_ref[...] = jnp.zeros_like(acc_ref)

        at = a_ref[...]
        if b2 is None:
            acc_ref[...] += _dot_nn(at, b_refs[0][...].astype(BF16))
        else:
            @pl.when(j < n1b)
            def _():
                acc_ref[...] += _dot_nn(at, b_refs[0][...].astype(BF16))

            @pl.when(j >= n1b)
            def _():
                acc_ref[...] += _dot_nn(at, b_refs[1][...].astype(BF16))

        @pl.when(k == nr - 1)
        def _():
            o_ref[...] = acc_ref[...]

    in_specs = [pl.BlockSpec((tm, tr), lambda i, j, k: (i, k))]
    args = [a, b1]
    if b2 is None:
        in_specs.append(pl.BlockSpec((tr, tn), lambda i, j, k: (k, j)))
    else:
        in_specs.append(pl.BlockSpec((tr, tn), lambda i, j, k: (jnp.where(j < n1b, k, nr - 1), jnp.minimum(j, n1b - 1))))
        in_specs.append(pl.BlockSpec((tr, tn), lambda i, j, k: (jnp.where(j < n1b, 0, k), jnp.maximum(j - n1b, 0))))
        args.append(b2)
    return pl.pallas_call(
        body, name=name, grid=(m // tm, n // tn, nr),
        in_specs=in_specs, out_specs=out_spec,
        out_shape=jax.ShapeDtypeStruct(out_shape, F32),
        scratch_shapes=[pltpu.VMEM((tm, tn), F32)],
        compiler_params=_cparams(("parallel", "parallel", "arbitrary")),
    )(*args)


def mm(a, w, carrier, slab=None, name="mm"):
    @jax.custom_vjp
    def f(a, w, carrier):
        return _matmul(a, w, "nn", 0, None, F32, name + "_fwd")

    def fwd(a, w, carrier):
        return _matmul(a, w, "nn", 0, None, F32, name + "_fwd"), (a, w)

    def bwd(res, g):
        a, w = res
        da = _matmul(g, w, "nt", 0, None, F32, name + "_dx")
        return da, jnp.zeros_like(w), _matmul_dw(_transpose_cast(a, name + "_t"), g, None, slab, name + "_dw")

    f.defvjp(fwd, bwd)
    return f(a, w, carrier)


def in_proj(h, w, carriers, slab, name):
    def run(h, w):
        return (_matmul(h, w, "nn", 0, N_QKV, BF16, name + "_qkv"),
                _matmul(h, w, "nn", N_QKV, D_INNER, F32, name + "_g"))

    @jax.custom_vjp
    def f(h, w, *cars):
        return run(h, w)

    def fwd(h, w, *cars):
        return run(h, w), (h, w)

    def bwd(res, g):
        h, w = res
        g_qkv, g_gate = g
        dh = (_matmul(g_qkv, w, "nt", 0, None, F32, name + "_qkv_dx")
              + _matmul(g_gate, w, "nt", N_QKV, None, F32, name + "_g_dx"))
        h_t = _transpose_cast(h, name + "_t")
        if slab:
            dws = (_matmul_dw(h_t, g_qkv, g_gate, ("col", PACK_COLS), name + "_dw"),)
        else:
            dws = (_matmul_dw(h_t, g_qkv, None, None, name + "_qkv_dw"),
                   _matmul_dw(h_t, g_gate, None, None, name + "_g_dw"))
        return (dh, jnp.zeros_like(w)) + dws

    f.defvjp(fwd, bwd)
    return f(h, w, *carriers)


def _rms_fwd(x, g, name):
    s, d = x.shape
    tm = _tile(s, 512, 8)

    def body(x_ref, g_ref, y_ref):
        x = x_ref[...]
        r = lax.rsqrt(jnp.mean(x * x, axis=-1, keepdims=True) + EPS)
        y_ref[...] = x * r * g_ref[...]

    return pl.pallas_call(
        body, name=name, grid=(s // tm,),
        in_specs=[pl.BlockSpec((tm, d), lambda i: (i, 0)), pl.BlockSpec((1, d), lambda i: (0, 0))],
        out_specs=pl.BlockSpec((tm, d), lambda i: (i, 0)),
        out_shape=jax.ShapeDtypeStruct((s, d), F32),
        compiler_params=_cparams(("parallel",)),
    )(x, g)


def _rms_bwd(x, g, dy, name):
    s, d = x.shape
    tm = _tile(s, 512, 8)

    def body(x_ref, g_ref, dy_ref, dx_ref, dg_ref):
        @pl.when(pl.program_id(0) == 0)
        def _():
            dg_ref[...] = jnp.zeros_like(dg_ref)

        x = x_ref[...]
        dy = dy_ref[...]
        r = lax.rsqrt(jnp.mean(x * x, axis=-1, keepdims=True) + EPS)
        xh = x * r
        dg_ref[...] += jnp.sum(dy * xh, axis=0, keepdims=True)
        dxh = dy * g_ref[...]
        dx_ref[...] = r * (dxh - xh * jnp.mean(dxh * xh, axis=-1, keepdims=True))

    return pl.pallas_call(
        body, name=name, grid=(s // tm,),
        in_specs=[pl.BlockSpec((tm, d), lambda i: (i, 0)), pl.BlockSpec((1, d), lambda i: (0, 0)),
                  pl.BlockSpec((tm, d), lambda i: (i, 0))],
        out_specs=[pl.BlockSpec((tm, d), lambda i: (i, 0)), pl.BlockSpec((1, d), lambda i: (0, 0))],
        out_shape=[jax.ShapeDtypeStruct((s, d), F32), jax.ShapeDtypeStruct((1, d), F32)],
        compiler_params=_cparams(("arbitrary",)),
    )(x, g, dy)


def rmsnorm(x, g, name="rms"):
    @jax.custom_vjp
    def f(x, g):
        return _rms_fwd(x, g.reshape(1, -1), name + "_fwd")

    def fwd(x, g):
        return _rms_fwd(x, g.reshape(1, -1), name + "_fwd"), (x, g)

    def bwd(res, dy):
        x, g = res
        dx, dg = _rms_bwd(x, g.reshape(1, -1), dy, name + "_bwd")
        return dx, dg.reshape(-1)

    f.defvjp(fwd, bwd)
    return f(x, g)


def _gate_fwd(o, gate, name):
    s = o.shape[0]
    tm = 256

    def body(o_ref, g_ref, y_ref):
        g = g_ref[...]
        y_ref[...] = o_ref[...] * (g / (1.0 + jnp.exp(-g)))

    row = pl.BlockSpec((tm, D_INNER), lambda i: (i, 0))
    return pl.pallas_call(
        body, name=name, grid=(s // tm,),
        in_specs=[row, row], out_specs=row,
        out_shape=jax.ShapeDtypeStruct((s, D_INNER), F32),
        compiler_params=_cparams(("parallel",)),
    )(o, gate)


def _gate_bwd(dy, o, gate, name):
    s = o.shape[0]
    tm = 256

    def body(dy_ref, o_ref, g_ref, do_ref, dg_ref, dl_ref):
        g = g_ref[...]
        o = o_ref[...]
        dy = dy_ref[...]
        sg = 1.0 / (1.0 + jnp.exp(-g))
        do = dy * (g * sg)
        do_ref[...] = do.astype(BF16)
        dg_ref[...] = dy * o * (sg * (1.0 + g * (1.0 - sg)))
        prod = do * o
        for h in range(N_HEADS):
            dl_ref[h] = jnp.sum(prod[:, h * HEAD_DIM:(h + 1) * HEAD_DIM], axis=1, keepdims=True)

    row = pl.BlockSpec((tm, D_INNER), lambda i: (i, 0))
    return pl.pallas_call(
        body, name=name, grid=(s // tm,),
        in_specs=[row, row, row],
        out_specs=[row, row, pl.BlockSpec((N_HEADS, tm, 1), lambda i: (0, i, 0))],
        out_shape=[jax.ShapeDtypeStruct((s, D_INNER), BF16), jax.ShapeDtypeStruct((s, D_INNER), F32),
                   jax.ShapeDtypeStruct((N_HEADS, s, 1), F32)],
        compiler_params=_cparams(("parallel",)),
    )(dy, o, gate)


SB_BLK = 256


def _softplus(z):
    return jnp.maximum(z, 0.0) + jnp.log(1.0 + jnp.exp(-jnp.abs(z)))


def _tri_sum(x, tri):
    hi = x.astype(BF16)
    lo = (x - hi.astype(F32)).astype(BF16)
    return _dot_nn(hi, tri) + _dot_nn(lo, tri)


SB_WIN = 2 * SB_BLK


def _sb_tri(kind):
    row = lax.broadcasted_iota(jnp.int32, (SB_BLK, SB_BLK), 0)
    col = lax.broadcasted_iota(jnp.int32, (SB_BLK, SB_BLK), 1)
    return ((row >= col) if kind == "suffix" else (row <= col)).astype(BF16)


def _sb_bounds(i, t):
    hi = (i + 1) * SB_BLK - t * SB_WIN
    return hi, pl.multiple_of(jnp.maximum(hi - SB_WIN, 0), SB_BLK)


def _sb_mask(i, hi, start):
    row = lax.broadcasted_iota(jnp.int32, (SB_BLK, SB_WIN), 0) + i * SB_BLK
    col = lax.broadcasted_iota(jnp.int32, (SB_BLK, SB_WIN), 1) + start
    return jnp.logical_and(col < row, col < hi)


def _sb_window(q, kwin, mask, a_run, tri_suffix, scale):
    b = SB_BLK
    z = _dot_nt(q, kwin) * scale
    sp = _softplus(z)
    ls = jnp.where(mask, -sp, 0.0)
    ls_l, ls_r = ls[:, :b], ls[:, b:]
    suffix = jnp.concatenate([_tri_sum(ls_l, tri_suffix) + jnp.sum(ls_r, axis=1, keepdims=True),
                              _tri_sum(ls_r, tri_suffix)], axis=1)
    w = jnp.where(mask, jnp.exp(z + suffix + a_run), 0.0)
    return z, sp, ls, w


def _sb_fwd(qkv, name):
    s = qkv.shape[0]
    b = SB_BLK
    nq = s // b
    scale = HEAD_DIM ** -0.5
    assert s >= SB_WIN

    def body(q_ref, k_ref, v_ref, o_ref):
        i = pl.program_id(1)
        q = q_ref[...]
        tri_suffix = _sb_tri("suffix")

        def cond(c):
            t, a_run, _ = c
            return jnp.logical_and((i + 1) * b - t * SB_WIN > 0, jnp.max(a_run) > -SB_CUT)

        def step(c):
            t, a_run, acc = c
            hi, start = _sb_bounds(i, t)
            _, _, ls, w = _sb_window(q, k_ref[pl.ds(start, SB_WIN), :], _sb_mask(i, hi, start), a_run,
                                     tri_suffix, scale)
            acc = acc + _dot_nn(w.astype(BF16), v_ref[pl.ds(start, SB_WIN), :])
            return t + 1, a_run + jnp.sum(ls, axis=1, keepdims=True), acc

        _, _, acc = lax.while_loop(cond, step, (0, jnp.zeros((b, 1), F32), jnp.zeros((b, HEAD_DIM), F32)))
        o_ref[...] = acc

    return pl.pallas_call(
        body, name=name, grid=(N_HEADS, nq),
        in_specs=[pl.BlockSpec((b, HEAD_DIM), lambda h, i: (i, h)),
                  pl.BlockSpec((s, HEAD_DIM), lambda h, i: (0, N_HEADS + h)),
                  pl.BlockSpec((s, HEAD_DIM), lambda h, i: (0, 2 * N_HEADS + h))],
        out_specs=pl.BlockSpec((b, HEAD_DIM), lambda h, i: (i, h)),
        out_shape=jax.ShapeDtypeStruct((s, D_INNER), F32),
        compiler_params=_cparams(("parallel", "arbitrary")),
    )(qkv, qkv, qkv)


def _sb_bwd(qkv, do, name):
    s = qkv.shape[0]
    b = SB_BLK
    nq = s // b
    n_win = -(-s // SB_WIN) + 1
    scale = HEAD_DIM ** -0.5
    assert s >= SB_WIN

    def body(q_ref, k_ref, v_ref, do_ref, dq_ref, dk_ref, dv_ref, dkt_s, dvt_s, g_buf, sig_buf):
        i = pl.program_id(1)

        @pl.when(i == 0)
        def _():
            dkt_s[...] = jnp.zeros_like(dkt_s)
            dvt_s[...] = jnp.zeros_like(dvt_s)

        q = q_ref[...]
        dob = do_ref[...]
        q_t = _transpose_bf16(q)
        do_t = _transpose_bf16(dob)
        tri_suffix = _sb_tri("suffix")
        tri_prefix = _sb_tri("prefix")

        def add_halves(acc_ref, start, upd):
            blk = start // b
            acc_ref[blk] += upd[:, :b]
            acc_ref[blk + 1] += upd[:, b:]

        def cond(c):
            t, a_run = c
            return jnp.logical_and((i + 1) * b - t * SB_WIN > 0, jnp.max(a_run) > -SB_CUT)

        def sweep(c):
            t, a_run = c
            hi, start = _sb_bounds(i, t)
            z, sp, ls, w = _sb_window(q, k_ref[pl.ds(start, SB_WIN), :], _sb_mask(i, hi, start), a_run,
                                      tri_suffix, scale)
            g_buf[t] = w * _dot_nt(dob, v_ref[pl.ds(start, SB_WIN), :])
            sig_buf[t] = jnp.exp(z - sp)
            add_halves(dvt_s, start, _dot_nn(do_t, w.astype(BF16)))
            return t + 1, a_run + jnp.sum(ls, axis=1, keepdims=True)

        n_steps, _ = lax.while_loop(cond, sweep, (0, jnp.zeros((b, 1), F32)))

        def back(u, c):
            g_run, dq = c
            t = n_steps - 1 - u
            hi, start = _sb_bounds(i, t)
            g = g_buf[t]
            g_l, g_r = g[:, :b], g[:, b:]
            g_incl = g_run + jnp.concatenate(
                [_tri_sum(g_l, tri_prefix),
                 _tri_sum(g_r, tri_prefix) + jnp.sum(g_l, axis=1, keepdims=True)], axis=1)
            dz = jnp.where(_sb_mask(i, hi, start), (g - sig_buf[t] * g_incl) * scale, 0.0).astype(BF16)
            add_halves(dkt_s, start, _dot_nn(q_t, dz))
            return g_run + jnp.sum(g, axis=1, keepdims=True), dq + _dot_nn(dz, k_ref[pl.ds(start, SB_WIN), :])

        _, dq = lax.fori_loop(0, n_steps, back, (jnp.zeros((b, 1), F32), jnp.zeros((b, HEAD_DIM), F32)))
        dq_ref[...] = dq.astype(BF16)

        @pl.when(i == nq - 1)
        def _():
            for jb in range(nq):
                dk_ref[jb * b:(jb + 1) * b, :] = dkt_s[jb].T.astype(BF16)
                dv_ref[jb * b:(jb + 1) * b, :] = dvt_s[jb].T.astype(BF16)

    blk = pl.BlockSpec((b, HEAD_DIM), lambda h, i: (i, h))
    head = pl.BlockSpec((s, HEAD_DIM), lambda h, i: (0, h))
    return pl.pallas_call(
        body, name=name, grid=(N_HEADS, nq),
        in_specs=[blk,
                  pl.BlockSpec((s, HEAD_DIM), lambda h, i: (0, N_HEADS + h)),
                  pl.BlockSpec((s, HEAD_DIM), lambda h, i: (0, 2 * N_HEADS + h)),
                  blk],
        out_specs=[blk, head, head],
        out_shape=[jax.ShapeDtypeStruct((s, D_INNER), BF16)] * 3,
        scratch_shapes=[pltpu.VMEM((nq, HEAD_DIM, b), F32), pltpu.VMEM((nq, HEAD_DIM, b), F32),
                        pltpu.VMEM((n_win, b, SB_WIN), F32), pltpu.VMEM((n_win, b, SB_WIN), F32)],
        compiler_params=_cparams(("arbitrary", "arbitrary")),
    )(qkv, qkv, qkv, do)


def sb_core(qkv, gate, name):
    def run(qkv, gate):
        o = _sb_fwd(qkv, name + "_fwd")
        return _gate_fwd(o, gate, name + "_gate"), o

    @jax.custom_vjp
    def f(qkv, gate):
        return run(qkv, gate)[0]

    def fwd(qkv, gate):
        y, o = run(qkv, gate)
        return y, (qkv, gate, o)

    def bwd(res, dy):
        qkv, gate, o = res
        do, dgate, _ = _gate_bwd(dy, o, gate, name + "_gate_bwd")
        dq, dk, dv = _sb_bwd(qkv, do, name + "_bwd")
        return jnp.concatenate([dq, dk, dv], axis=1), dgate

    f.defvjp(fwd, bwd)
    return f(qkv, gate)


SM_FWD_BLK = 256
SM_BLK = 512


SM_FWD_KEYS = 1024
SM_BWD_KEYS = 512


def _sm_mask(i, jw, rows, keys, chunk_shift):
    row = lax.broadcasted_iota(jnp.int32, (rows, keys), 0) + i * rows
    col = lax.broadcasted_iota(jnp.int32, (rows, keys), 1) + jw * keys
    return (col >> chunk_shift) <= (row >> chunk_shift)


def _sm_fwd(qa, ka, va, ccol, crow, dqk, qo, ko, vo, chunk_shift, scale, name):
    s = qa.shape[0]
    b = min(SM_FWD_BLK, s)
    keys = min(SM_FWD_KEYS, s)
    per = keys // b
    nq = s // b
    has_bias = ccol is not None

    def body(*refs):
        if has_bias:
            q_ref, k_ref, v_ref, cc_ref, cr_ref, o_ref, lse_ref, m_s, l_s, acc_s = refs
        else:
            q_ref, k_ref, v_ref, o_ref, lse_ref, m_s, l_s, acc_s = refs
        i = pl.program_id(1)
        q = q_ref[...]
        m_s[...] = jnp.full_like(m_s, NEG)
        l_s[...] = jnp.zeros_like(l_s)
        acc_s[...] = jnp.zeros_like(acc_s)

        def sweep(jw, masked):
            off = pl.multiple_of(jw * keys, keys)
            z = _dot_nt(q, k_ref[pl.ds(off, keys), :]) * scale
            if has_bias:
                z = z + cc_ref[...] - cr_ref[jw]
            if masked:
                z = jnp.where(_sm_mask(i, jw, b, keys, chunk_shift), z, NEG)
            m_old = m_s[...]
            m_new = jnp.maximum(m_old, jnp.max(z, axis=1, keepdims=True))
            alpha = jnp.exp(m_old - m_new)
            p = jnp.exp(z - m_new)
            l_s[...] = alpha * l_s[...] + jnp.sum(p, axis=1, keepdims=True)
            acc_s[...] = alpha * acc_s[...] + _dot_nn(p.astype(BF16), v_ref[pl.ds(off, keys), :])
            m_s[...] = m_new

        def full(jw, carry):
            sweep(jw, False)
            return carry

        lax.fori_loop(0, i // per, full, 0)
        sweep(i // per, True)
        o_ref[...] = acc_s[...] / l_s[...]
        lse_ref[...] = m_s[...] + jnp.log(l_s[...])

    in_specs = [pl.BlockSpec((b, dqk), lambda h, i: (i, qo + h)),
                pl.BlockSpec((s, dqk), lambda h, i: (0, ko + h)),
                pl.BlockSpec((s, HEAD_DIM), lambda h, i: (0, vo + h))]
    args = [qa, ka, va]
    if has_bias:
        in_specs += [pl.BlockSpec((None, b, 1), lambda h, i: (h, i, 0)),
                     pl.BlockSpec((None, s // keys, 1, keys), lambda h, i: (h, 0, 0, 0))]
        args += [ccol, crow]
    return pl.pallas_call(
        body, name=name, grid=(N_HEADS, nq),
        in_specs=in_specs,
        out_specs=[pl.BlockSpec((b, HEAD_DIM), lambda h, i: (i, h)),
                   pl.BlockSpec((None, b, 1), lambda h, i: (h, i, 0))],
        out_shape=[jax.ShapeDtypeStruct((s, D_INNER), F32), jax.ShapeDtypeStruct((N_HEADS, s, 1), F32)],
        scratch_shapes=[pltpu.VMEM((b, 1), F32), pltpu.VMEM((b, 1), F32), pltpu.VMEM((b, HEAD_DIM), F32)],
        compiler_params=_cparams(("parallel", "arbitrary")),
    )(*args)


def _sm_bwd(qa, ka, va, do, lse, delta, ccol, crow, dqk, qo, ko, vo, chunk_shift, scale, grad_dtype, name):
    s = qa.shape[0]
    b = SM_BLK
    keys = min(SM_BWD_KEYS, s)
    per = keys // b
    nq = s // b
    nk = s // keys
    has_bias = ccol is not None

    def body(*refs):
        if has_bias:
            (q_ref, k_ref, v_ref, do_ref, lse_ref, dl_ref, cc_ref, cr_ref,
             dq_ref, dk_ref, dv_ref, dc_ref, dr_ref, dq_s, dkt_s, dvt_s, dc_s, dr_s) = refs
        else:
            (q_ref, k_ref, v_ref, do_ref, lse_ref, dl_ref,
             dq_ref, dk_ref, dv_ref, dq_s, dkt_s, dvt_s) = refs
        i = pl.program_id(1)

        @pl.when(i == 0)
        def _():
            dkt_s[...] = jnp.zeros_like(dkt_s)
            dvt_s[...] = jnp.zeros_like(dvt_s)
            if has_bias:
                dc_s[...] = jnp.zeros_like(dc_s)

        q = q_ref[...]
        dob = do_ref[...]
        q_t = _transpose_bf16(q)
        do_t = _transpose_bf16(dob)
        lse = lse_ref[...]
        delta = dl_ref[...]
        dq_s[...] = jnp.zeros_like(dq_s)
        if has_bias:
            dr_s[...] = jnp.zeros_like(dr_s)

        def sweep(jw, masked):
            off = pl.multiple_of(jw * keys, keys)
            kb = k_ref[pl.ds(off, keys), :]
            z = _dot_nt(q, kb) * scale
            if has_bias:
                z = z + cc_ref[...] - cr_ref[jw]
            p = jnp.exp(z - lse)
            if masked:
                p = jnp.where(_sm_mask(i, jw, b, keys, chunk_shift), p, 0.0)
            dvt_s[jw] += _dot_nn(do_t, p.astype(BF16))
            dz = p * (_dot_nt(dob, v_ref[pl.ds(off, keys), :]) - delta)
            if has_bias:
                dc_s[jw] += jnp.sum(dz, axis=0, keepdims=True)
                dr_s[...] += jnp.sum(dz, axis=1, keepdims=True)
            dzs = (dz * scale).astype(BF16)
            dkt_s[jw] += _dot_nn(q_t, dzs)
            dq_s[...] += _dot_nn(dzs, kb)

        def full(jw, carry):
            sweep(jw, False)
            return carry

        lax.fori_loop(0, i // per, full, 0)
        sweep(i // per, True)
        dq_ref[...] = dq_s[...].astype(grad_dtype)
        if has_bias:
            dr_ref[...] = dr_s[...]

        @pl.when(i == nq - 1)
        def _():
            for jw in range(nk):
                dk_ref[jw * keys:(jw + 1) * keys, :] = dkt_s[jw].T.astype(grad_dtype)
                dv_ref[jw * keys:(jw + 1) * keys, :] = dvt_s[jw].T.astype(grad_dtype)
            if has_bias:
                dc_ref[...] = dc_s[...]

    vec = pl.BlockSpec((None, b, 1), lambda h, i: (h, i, 0))
    in_specs = [pl.BlockSpec((b, dqk), lambda h, i: (i, qo + h)),
                pl.BlockSpec((s, dqk), lambda h, i: (0, ko + h)),
                pl.BlockSpec((s, HEAD_DIM), lambda h, i: (0, vo + h)),
                pl.BlockSpec((b, HEAD_DIM), lambda h, i: (i, h)),
                vec, vec]
    args = [qa, ka, va, do, lse, delta]
    out_specs = [pl.BlockSpec((b, dqk), lambda h, i: (i, h)),
                 pl.BlockSpec((s, dqk), lambda h, i: (0, h)),
                 pl.BlockSpec((s, HEAD_DIM), lambda h, i: (0, h))]
    out_shape = [jax.ShapeDtypeStruct((s, N_HEADS * dqk), grad_dtype),
                 jax.ShapeDtypeStruct((s, N_HEADS * dqk), grad_dtype),
                 jax.ShapeDtypeStruct((s, D_INNER), grad_dtype)]
    scratch = [pltpu.VMEM((b, dqk), F32), pltpu.VMEM((nk, dqk, keys), F32), pltpu.VMEM((nk, HEAD_DIM, keys), F32)]
    if has_bias:
        key_vec = pl.BlockSpec((None, nk, 1, keys), lambda h, i: (h, 0, 0, 0))
        in_specs += [vec, key_vec]
        args += [ccol, crow]
        out_specs += [key_vec, vec]
        out_shape += [jax.ShapeDtypeStruct((N_HEADS, nk, 1, keys), F32), jax.ShapeDtypeStruct((N_HEADS, s, 1), F32)]
        scratch += [pltpu.VMEM((nk, 1, keys), F32), pltpu.VMEM((b, 1), F32)]
    return pl.pallas_call(
        body, name=name, grid=(N_HEADS, nq),
        in_specs=in_specs, out_specs=out_specs, out_shape=out_shape, scratch_shapes=scratch,
        compiler_params=_cparams(("arbitrary", "arbitrary")),
    )(*args)


def fox_core(qkv, gate, c, name):
    s = qkv.shape[0]
    scale = HEAD_DIM ** -0.5
    cfg = dict(dqk=HEAD_DIM, qo=0, ko=N_HEADS, vo=2 * N_HEADS, chunk_shift=0, scale=scale)

    def layouts(c, keys):
        ct = c.T
        keys = min(keys, s)
        return ct.reshape(N_HEADS, s, 1), ct.reshape(N_HEADS, s // keys, 1, keys)

    def run(qkv, gate, c):
        ccol, crow = layouts(c, SM_FWD_KEYS)
        o, lse = _sm_fwd(qkv, qkv, qkv, ccol, crow, name=name + "_fwd", **cfg)
        return _gate_fwd(o, gate, name + "_gate"), o, lse

    @jax.custom_vjp
    def f(qkv, gate, c):
        return run(qkv, gate, c)[0]

    def fwd(qkv, gate, c):
        y, o, lse = run(qkv, gate, c)
        return y, (qkv, gate, c, o, lse)

    def bwd(res, dy):
        qkv, gate, c, o, lse = res
        ccol, crow = layouts(c, SM_BWD_KEYS)
        do, dgate, delta = _gate_bwd(dy, o, gate, name + "_gate_bwd")
        dq, dk, dv, colsum, rowsum = _sm_bwd(qkv, qkv, qkv, do, lse, delta, ccol, crow,
                                             grad_dtype=BF16, name=name + "_bwd", **cfg)
        dc = (rowsum.reshape(N_HEADS, s) - colsum.reshape(N_HEADS, s)).T
        return jnp.concatenate([dq, dk, dv], axis=1), dgate, dc

    f.defvjp(fwd, bwd)
    return f(qkv, gate, c)


def mla_core(qc, kc, v, gate, name):
    scale = (MLA_NOPE + MLA_ROPE) ** -0.5
    cfg = dict(dqk=MLA_QK_PAD, qo=0, ko=0, vo=0, chunk_shift=MLA_CHUNK.bit_length() - 1, scale=scale)

    def run(qc, kc, v, gate):
        o, lse = _sm_fwd(qc, kc, v, None, None, name=name + "_fwd", **cfg)
        return _gate_fwd(o, gate, name + "_gate"), o, lse

    @jax.custom_vjp
    def f(qc, kc, v, gate):
        return run(qc.astype(BF16), kc.astype(BF16), v.astype(BF16), gate)[0]

    def fwd(qc, kc, v, gate):
        qc, kc, v = qc.astype(BF16), kc.astype(BF16), v.astype(BF16)
        y, o, lse = run(qc, kc, v, gate)
        return y, (qc, kc, v, gate, o, lse)

    def bwd(res, dy):
        qc, kc, v, gate, o, lse = res
        do, dgate, delta = _gate_bwd(dy, o, gate, name + "_gate_bwd")
        dq, dk, dv = _sm_bwd(qc, kc, v, do, lse, delta, None, None, grad_dtype=F32, name=name + "_bwd", **cfg)
        return dq, dk, dv, dgate

    f.defvjp(fwd, bwd)
    return f(qc, kc, v, gate)


def _sq_loss_call(y, t, name):
    s, d = y.shape
    tm = _tile(s, 512, 8)

    def body(y_ref, t_ref, l_ref, e_ref):
        @pl.when(pl.program_id(0) == 0)
        def _():
            l_ref[...] = jnp.zeros_like(l_ref)

        e = y_ref[...] - t_ref[...]
        e_ref[...] = e * (1.0 / d)
        part = jnp.sum(jnp.sum(e * e, axis=1, keepdims=True), axis=0, keepdims=True)
        l_ref[...] += jnp.broadcast_to(part * (0.5 / d), l_ref.shape)

    row = pl.BlockSpec((tm, d), lambda i: (i, 0))
    return pl.pallas_call(
        body, name=name, grid=(s // tm,),
        in_specs=[row, row],
        out_specs=[pl.BlockSpec((8, 128), lambda i: (0, 0)), row],
        out_shape=[jax.ShapeDtypeStruct((8, 128), F32), jax.ShapeDtypeStruct((s, d), F32)],
        compiler_params=_cparams(("arbitrary",)),
    )(y, t)


@jax.custom_vjp
def sq_loss(y, t):
    return _sq_loss_call(y, t, "loss_fwd")[0][0, 0]


def _sq_loss_fwd(y, t):
    l, e = _sq_loss_call(y, t, "loss_fwd")
    return l[0, 0], e


def _sq_loss_bwd(e, g):
    return g * e, jnp.zeros_like(e)


sq_loss.defvjp(_sq_loss_fwd, _sq_loss_bwd)


def _cast_bf16(x, name):
    r, c = x.shape
    tb = _tile(r, 512, 16)

    def body(x_ref, o_ref):
        o_ref[...] = x_ref[...].astype(BF16)

    return pl.pallas_call(
        body, name=name, grid=(r // tb,),
        in_specs=[pl.BlockSpec((tb, c), lambda i: (i, 0))],
        out_specs=pl.BlockSpec((tb, c), lambda i: (i, 0)),
        out_shape=jax.ShapeDtypeStruct((r, c), BF16),
        compiler_params=_cparams(("parallel",)),
    )(x)


def _pair_sum(core_idx, g, recv, name):
    _, _, r, c = g.shape
    tb = _tile(r, 512, 16)

    def body(c_ref, g_ref, r_ref, o_ref):
        o_ref[...] = (g_ref[...] + r_ref[...]).astype(BF16)

    return pl.pallas_call(
        body, name=name,
        grid_spec=pltpu.PrefetchScalarGridSpec(
            num_scalar_prefetch=1, grid=(4, r // tb),
            in_specs=[pl.BlockSpec((None, None, tb, c), lambda q, i, c_ref: (c_ref[0], q, i, 0)),
                      pl.BlockSpec((None, tb, c), lambda q, i, c_ref: (q, i, 0))],
            out_specs=pl.BlockSpec((None, tb, c), lambda q, i, c_ref: (q, i, 0))),
        out_shape=jax.ShapeDtypeStruct((4, r, c), BF16),
        compiler_params=_cparams(("parallel", "parallel")),
    )(core_idx, g, recv)


def _adamw(w, parts, m, v, name):
    n, r, c = parts.shape
    tb = _tile(r, 256, 8)
    b1c = 1.0 - ADAM_B1 ** ADAM_STEP
    b2c = 1.0 - ADAM_B2 ** ADAM_STEP

    def body(w_ref, p_ref, m_ref, v_ref, g_ref, d_ref, nm_ref, nv_ref):
        g = p_ref[0].astype(F32)
        for k in range(1, n):
            g = g + p_ref[k].astype(F32)
        m_new = ADAM_B1 * m_ref[...] + (1.0 - ADAM_B1) * g
        v_new = ADAM_B2 * v_ref[...] + (1.0 - ADAM_B2) * (g * g)
        m_hat = m_new / b1c
        v_hat = v_new / b2c
        g_ref[...] = g
        d_ref[...] = -ADAM_LR * (m_hat / (jnp.sqrt(v_hat) + ADAM_EPS) + ADAM_WD * w_ref[...])
        nm_ref[...] = m_new
        nv_ref[...] = v_new

    row = pl.BlockSpec((tb, c), lambda i: (i, 0))
    return pl.pallas_call(
        body, name=name, grid=(r // tb,),
        in_specs=[row, pl.BlockSpec((n, tb, c), lambda i: (0, i, 0)), row, row],
        out_specs=[row] * 4,
        out_shape=[jax.ShapeDtypeStruct((r, c), F32)] * 4,
        compiler_params=_cparams(("parallel",)),
    )(w, parts, m, v)


ANY = pl.BlockSpec(memory_space=pl.ANY)


def _place():
    return lax.axis_index("x"), lax.axis_index("y"), lax.axis_index("c")


def _all_gather(shards, kinds, name):
    nm = len(shards)

    def out_shape(sh, kind):
        a, b = sh.shape
        return {"row": (N_DEV * a, b), "col": (a, N_DEV * b), "stack": (N_DEV, a, b)}[kind]

    def body(*refs):
        x_refs, out_refs = refs[:nm], refs[nm:2 * nm]
        send_sems, recv_sems, local_sems = refs[2 * nm:]
        x, y, cc = _place()
        me, sibling = (x, y, cc), (x, y, 1 - cc)
        chips = [(1 - x, y), (x, 1 - y), (1 - x, 1 - y)]

        def slot(mi, px, py, pc):
            d = 4 * px + 2 * py + pc
            a, b = x_refs[mi].shape
            if kinds[mi] == "row":
                return out_refs[mi].at[pl.ds(pl.multiple_of(d * a, a), a), :]
            if kinds[mi] == "col":
                return out_refs[mi].at[:, pl.ds(pl.multiple_of(d * b, 128), b)]
            return out_refs[mi].at[d]

        def copy(mi, k, block, to, src=None):
            return pltpu.make_async_remote_copy(
                src_ref=slot(mi, *block) if src is None else src, dst_ref=slot(mi, *block),
                send_sem=send_sems.at[7 * mi + k], recv_sem=recv_sems.at[7 * mi + k],
                device_id=to, device_id_type=MESH)

        mine = [pltpu.make_async_copy(x_refs[mi], slot(mi, *me), local_sems.at[mi]) for mi in range(nm)]
        for cp in mine:
            cp.start()
        first = []
        for mi in range(nm):
            first.append(copy(mi, 0, me, sibling, src=x_refs[mi]))
            first += [copy(mi, 1 + j, me, (*chip, cc), src=x_refs[mi]) for j, chip in enumerate(chips)]
        for cp in first:
            cp.start()
        passed = []
        for j, chip in enumerate(chips):
            for mi in range(nm):
                copy(mi, 1 + j, (*chip, cc), me).wait_recv()
                passed.append(copy(mi, 4 + j, (*chip, cc), sibling))
                passed[-1].start()
        for mi in range(nm):
            copy(mi, 0, sibling, me).wait_recv()
            for j, chip in enumerate(chips):
                copy(mi, 4 + j, (*chip, 1 - cc), me).wait_recv()
        for cp in first + passed:
            cp.wait_send()
        for cp in mine:
            cp.wait()

    return pl.pallas_call(
        body, name=name,
        out_shape=[jax.ShapeDtypeStruct(out_shape(sh, kd), sh.dtype) for sh, kd in zip(shards, kinds)],
        in_specs=[ANY] * nm, out_specs=[ANY] * nm,
        scratch_shapes=[pltpu.SemaphoreType.DMA((7 * nm,)), pltpu.SemaphoreType.DMA((7 * nm,)),
                        pltpu.SemaphoreType.DMA((nm,))],
    )(*shards)


def _pair_exchange(gs, name):
    nm = len(gs)

    def body(*refs):
        g_refs, recv_refs = refs[:nm], refs[nm:2 * nm]
        send_sems, recv_sems = refs[2 * nm:]
        x, y, cc = _place()
        copies = [pltpu.make_async_remote_copy(
            src_ref=g_refs[mi].at[1 - cc], dst_ref=recv_refs[mi],
            send_sem=send_sems.at[mi], recv_sem=recv_sems.at[mi], device_id=(x, y, 1 - cc), device_id_type=MESH)
            for mi in range(nm)]
        for cp in copies:
            cp.start()
        for cp in copies:
            cp.wait_recv()
        for cp in copies:
            cp.wait_send()

    return pl.pallas_call(
        body, name=name,
        out_shape=[jax.ShapeDtypeStruct(g.shape[1:], g.dtype) for g in gs],
        in_specs=[ANY] * nm, out_specs=[ANY] * nm,
        scratch_shapes=[pltpu.SemaphoreType.DMA((nm,)), pltpu.SemaphoreType.DMA((nm,))],
    )(*gs)


def _chip_exchange(parts, name):
    nm = len(parts)

    def body(*refs):
        p_refs, out_refs = refs[:nm], refs[nm:2 * nm]
        send_sems, recv_sems, local_sems = refs[2 * nm:]
        x, y, cc = _place()
        mine = 2 * x + y
        others = [(1 - x, y), (x, 1 - y), (1 - x, 1 - y)]
        keeps = [pltpu.make_async_copy(p_refs[mi].at[mine], out_refs[mi].at[mine], local_sems.at[mi]) for mi in range(nm)]
        for cp in keeps:
            cp.start()
        sends = []
        for px, py in others:
            q = 2 * px + py
            for mi in range(nm):
                sends.append(pltpu.make_async_remote_copy(
                    src_ref=p_refs[mi].at[q], dst_ref=out_refs[mi].at[mine],
                    send_sem=send_sems.at[4 * mi + q], recv_sem=recv_sems.at[4 * mi + mine],
                    device_id=(px, py, cc), device_id_type=MESH))
        for cp in sends:
            cp.start()
        for px, py in others:
            q = 2 * px + py
            for mi in range(nm):
                pltpu.make_async_remote_copy(
                    src_ref=p_refs[mi].at[q], dst_ref=out_refs[mi].at[q],
                    send_sem=send_sems.at[4 * mi + q], recv_sem=recv_sems.at[4 * mi + q],
                    device_id=(px, py, cc), device_id_type=MESH).wait_recv()
        for cp in sends:
            cp.wait_send()
        for cp in keeps:
            cp.wait()

    return pl.pallas_call(
        body, name=name,
        out_shape=[jax.ShapeDtypeStruct(p.shape, p.dtype) for p in parts],
        in_specs=[ANY] * nm, out_specs=[ANY] * nm,
        scratch_shapes=[pltpu.SemaphoreType.DMA((4 * nm,)), pltpu.SemaphoreType.DMA((4 * nm,)),
                        pltpu.SemaphoreType.DMA((nm,))],
    )(*parts)


def _all_reduce_small(v, name):
    shape = v.shape

    def body(v_ref, out_ref, buf, send_sems, recv_sems):
        x, y, cc = _place()
        me = 4 * x + 2 * y + cc
        buf[me] = v_ref[...]
        flips = [(a, b, d) for a in (0, 1) for b in (0, 1) for d in (0, 1)][1:]
        copies = []
        for k, (a, b, d) in enumerate(flips):
            peer = (x ^ a, y ^ b, cc ^ d)
            copies.append(pltpu.make_async_remote_copy(
                src_ref=v_ref, dst_ref=buf.at[me],
                send_sem=send_sems.at[k], recv_sem=recv_sems.at[k], device_id=peer, device_id_type=MESH))
        for cp in copies:
            cp.start()
        for k, (a, b, d) in enumerate(flips):
            peer_id = 4 * (x ^ a) + 2 * (y ^ b) + (cc ^ d)
            pltpu.make_async_remote_copy(
                src_ref=v_ref, dst_ref=buf.at[peer_id],
                send_sem=send_sems.at[k], recv_sem=recv_sems.at[k], device_id=(x, y, cc), device_id_type=MESH
            ).wait_recv()
        for cp in copies:
            cp.wait_send()
        total = buf[0]
        for k in range(1, N_DEV):
            total = total + buf[k]
        out_ref[...] = total

    vm = pl.BlockSpec(memory_space=pltpu.VMEM)
    return pl.pallas_call(
        body, name=name,
        out_shape=jax.ShapeDtypeStruct(shape, F32),
        in_specs=[vm], out_specs=vm,
        scratch_shapes=[pltpu.VMEM((N_DEV,) + shape, F32), pltpu.SemaphoreType.DMA((7,)), pltpu.SemaphoreType.DMA((7,))],
    )(v)


def _gather_kind(name, shape):
    if name not in COL_SHARDED:
        return "row"
    return "col" if shape[1] % 128 == 0 else "stack"


def _slab_of(name, shape):
    if name not in COL_SHARDED:
        return ("row", shape[0])
    return ("col", shape[1]) if shape[1] % 128 == 0 else None


def _to_slabs(g, shape):
    kk, nn = shape
    return g.reshape(kk, 4, 2, nn).transpose(2, 1, 0, 3)


SMALL_ROWS = 8


def _pack_small(arrs):
    rows = [arrs[n] for n in SMALL[:5]]
    last = jnp.concatenate([arrs["q_norm1"], arrs["kv_norm1"], arrs["b_f2"]])
    rows.append(jnp.pad(last, (0, PACK_COLS - last.shape[0])))
    rows += [jnp.zeros((PACK_COLS,), F32)] * (SMALL_ROWS - len(rows))
    return jnp.stack(rows)


def _unpack_small(p):
    out = {n: p[k] for k, n in enumerate(SMALL[:5])}
    out["q_norm1"] = p[5, :MLA_Q_RANK]
    out["kv_norm1"] = p[5, MLA_Q_RANK:MLA_Q_RANK + MLA_KV_RANK]
    out["b_f2"] = p[5, MLA_Q_RANK + MLA_KV_RANK:MLA_Q_RANK + MLA_KV_RANK + N_HEADS]
    return out


N_QKV = 3 * D_INNER
N_MAIN = 4 * D_INNER


def _rope(x, pos):
    r = x.shape[-1]
    inv_freq = ROPE_BASE ** (-jnp.arange(0, r, 2, dtype=F32) / r)
    ang = pos.astype(F32)[:, None, None] * inv_freq
    cos, sin = jnp.cos(ang), jnp.sin(ang)
    x1, x2 = x[..., : r // 2], x[..., r // 2:]
    return jnp.concatenate([x1 * cos - x2 * sin, x1 * sin + x2 * cos], axis=-1)


def _forward_loss(carriers, small, x, wfull, slabs, pos, target):
    s = x.shape[0]

    def out_proj(y, w_out, tag):
        return mm(y, wfull[w_out], carriers[w_out], slabs[w_out], name=tag + "_out")

    def sb_layer(x, ln, w_in, w_out, tag):
        h = rmsnorm(x, small[ln], tag + "_ln")
        qkv, gate = in_proj(h, wfull[w_in], (carriers[w_in],), True, tag)
        return x + out_proj(sb_core(qkv, gate, tag), w_out, tag)

    x = sb_layer(x, "ln0", "w_in0", "w_out0", "l0")

    h = rmsnorm(x, small["ln1"], "l1_ln")
    proj = mm(h, wfull["w_in1"], carriers["w_in1"], slabs["w_in1"], name="l1_in")
    i1, i2, i3 = MLA_Q_RANK, MLA_Q_RANK + MLA_KV_RANK, MLA_Q_RANK + MLA_KV_RANK + MLA_ROPE
    q = mm(rmsnorm(proj[:, :i1], small["q_norm1"], "l1_qn"), wfull["w_qb1"], carriers["w_qb1"], slabs["w_qb1"],
           name="l1_qb")
    q = q.reshape(s, N_HEADS, MLA_NOPE + MLA_ROPE)
    kv = mm(rmsnorm(proj[:, i1:i2], small["kv_norm1"], "l1_kvn"), wfull["w_kvb1"], carriers["w_kvb1"],
            slabs["w_kvb1"], name="l1_kvb")
    kv = kv.reshape(s, N_HEADS, MLA_NOPE + HEAD_DIM)
    k_rope = _rope(proj[:, i2:i3][:, None, :], pos)
    pad = jnp.zeros((s, N_HEADS, MLA_QK_PAD - MLA_NOPE - MLA_ROPE), F32)
    qc = jnp.concatenate([q[..., :MLA_NOPE], _rope(q[..., MLA_NOPE:], pos), pad], axis=-1)
    kc = jnp.concatenate([kv[..., :MLA_NOPE], jnp.broadcast_to(k_rope, (s, N_HEADS, MLA_ROPE)), pad], axis=-1)
    y = mla_core(qc.reshape(s, -1), kc.reshape(s, -1), kv[..., MLA_NOPE:].reshape(s, -1), proj[:, i3:], "l1")
    x = x + out_proj(y, "w_out1", "l1")

    h = rmsnorm(x, small["ln2"], "l2_ln")
    qkv, gate = in_proj(h, wfull["w_in2"], (carriers["w_in2_qkv"], carriers["w_in2_gate"]), False, "l2")
    f_logit = mm(h, wfull["w_in2"][:, N_MAIN:], carriers["w_in2_f"], None, name="l2_f") + small["b_f2"]
    c = jnp.cumsum(jax.nn.log_sigmoid(f_logit), axis=0)
    x = x + out_proj(fox_core(qkv, gate, c, "l2"), "w_out2", "l2")

    x = sb_layer(x, "ln3", "w_in3", "w_out3", "l3")
    return sq_loss(rmsnorm(x, small["final_norm"], "final_ln"), target)


def kernel(x, positions, ln0, w_in0, w_out0, ln1, w_in1, q_norm1, w_qb1, kv_norm1, w_kvb1, w_out1, ln2, w_in2, b_f2, w_out2, ln3, w_in3, w_out3, final_norm, loss_target, m_ln0, m_w_in0, m_w_out0, m_ln1, m_w_in1, m_q_norm1, m_w_qb1, m_kv_norm1, m_w_kvb1, m_w_out1, m_ln2, m_w_in2, m_b_f2, m_w_out2, m_ln3, m_w_in3, m_w_out3, m_final_norm, v_ln0, v_w_in0, v_w_out0, v_ln1, v_w_in1, v_q_norm1, v_w_qb1, v_kv_norm1, v_w_kvb1, v_w_out1, v_ln2, v_w_in2, v_b_f2, v_w_out2, v_ln3, v_w_in3, v_w_out3, v_final_norm):
    args = dict(locals())
    w = {n: args[n] for n in ALL_W}
    m = {n: args["m_" + n] for n in ALL_W}
    v = {n: args["v_" + n] for n in ALL_W}
    shapes = {n: w[n].shape for n in BIG}
    kinds = [_gather_kind(n, shapes[n]) for n in BIG]
    slabs = {n: _slab_of(n, shapes[n]) for n in BIG}

    gathered = _all_gather([w[n].astype(BF16) for n in BIG], kinds, "gather_w")
    wfull = {}
    for n, kind, full in zip(BIG, kinds, gathered):
        wfull[n] = full.transpose(1, 0, 2).reshape(shapes[n][0], -1) if kind == "stack" else full

    d_model = wfull["w_in0"].shape[0]
    carriers = {}
    for n in BIG:
        if n == "w_in2":
            continue
        if slabs[n] is None:
            carriers[n] = jnp.zeros(wfull[n].shape, F32)
        else:
            carriers[n] = jnp.zeros((2, 4) + shapes[n], F32)
    carriers["w_in2_qkv"] = jnp.zeros((d_model, N_QKV), F32)
    carriers["w_in2_gate"] = jnp.zeros((d_model, D_INNER), F32)
    carriers["w_in2_f"] = jnp.zeros((d_model, wfull["w_in2"].shape[1] - N_MAIN), F32)
    small = {n: w[n] for n in SMALL}

    def local_loss(carriers, small, x_seq):
        return _forward_loss(carriers, small, x_seq, wfull, slabs, positions[0], loss_target[0])

    loss_local, (g_car, g_small, g_x) = jax.value_and_grad(local_loss, argnums=(0, 1, 2))(carriers, small, x[0])
    loss = lax.psum(loss_local, ("x", "y", "c"))

    g_slab = {}
    for n in BIG:
        if n == "w_in2":
            g = jnp.concatenate([g_car["w_in2_qkv"], g_car["w_in2_gate"], g_car["w_in2_f"]], axis=1)
            g_slab[n] = _to_slabs(g, shapes[n])
        elif slabs[n] is None:
            g_slab[n] = _to_slabs(g_car[n], shapes[n])
        else:
            g_slab[n] = g_car[n]
    core_idx = lax.axis_index("c").astype(jnp.int32).reshape(1)
    from_sibling = _pair_exchange([g_slab[n] for n in BIG], "pair_exchange")
    chip_part = [_pair_sum(core_idx, g_slab[n], r, "pair_sum_" + n) for n, r in zip(BIG, from_sibling)]
    by_chip = _chip_exchange(chip_part, "chip_exchange")
    big = [{}, {}, {}, {}]
    for n, parts in zip(BIG, by_chip):
        for k, t in enumerate(_adamw(w[n], parts, m[n], v[n], "adamw_" + n)):
            big[k][n] = t

    g_small_sum = _all_reduce_small(_pack_small(g_small), "reduce_small")
    sm = _adamw(_pack_small(w), g_small_sum[None], _pack_small(m), _pack_small(v), "adamw_small")
    small_out = [_unpack_small(t) for t in sm]

    outs = [loss, g_x[None]]
    for k in range(4):
        outs += [small_out[k][n] if n in small_out[k] else big[k][n] for n in ALL_W]
    return tuple(outs)
```

```python
import jax
import jax.numpy as jnp
from jax import lax
from jax.experimental import pallas as pl
from jax.experimental.pallas import tpu as pltpu

F32 = jnp.float32
BF16 = jnp.bfloat16
MESH = pl.DeviceIdType.MESH

N_DEV = 8
N_HEADS = 16
HEAD_DIM = 128
D_INNER = N_HEADS * HEAD_DIM
MLA_Q_RANK = 256
MLA_KV_RANK = 128
MLA_NOPE = 128
MLA_ROPE = 64
MLA_QK_PAD = 256
MLA_CHUNK = 64
ROPE_BASE = 10000.0
EPS = 1e-6
NEG = -1e30
SB_CUT = 104.0

ADAM_LR = 0.001
ADAM_B1 = 0.9
ADAM_B2 = 0.999
ADAM_EPS = 1e-08
ADAM_WD = 0.01
ADAM_STEP = 10

PACK_COLS = 1024
VMEM_LIMIT = 56 * 1024 * 1024

BIG = ["w_in0", "w_out0", "w_in1", "w_qb1", "w_kvb1", "w_out1", "w_in2", "w_out2", "w_in3", "w_out3"]
COL_SHARDED = {"w_in0", "w_in1", "w_qb1", "w_kvb1", "w_in2", "w_in3"}
SMALL = ["ln0", "ln1", "ln2", "ln3", "final_norm", "q_norm1", "kv_norm1", "b_f2"]
ALL_W = ["ln0", "w_in0", "w_out0", "ln1", "w_in1", "q_norm1", "w_qb1", "kv_norm1", "w_kvb1", "w_out1",
         "ln2", "w_in2", "b_f2", "w_out2", "ln3", "w_in3", "w_out3", "final_norm"]


def _cparams(sem=None):
    return pltpu.CompilerParams(dimension_semantics=sem, vmem_limit_bytes=VMEM_LIMIT)


def _tile(dim, cap, align):
    if dim <= cap:
        return dim
    t = (cap // align) * align
    while t >= align:
        if dim % t == 0:
            return t
        t -= align
    return dim


def _dot(a, b, dims):
    return lax.dot_general(a, b, (dims, ((), ())), preferred_element_type=F32)


def _dot_nn(a, b):
    return _dot(a, b, ((1,), (0,)))


def _dot_nt(a, b):
    return _dot(a, b, ((1,), (1,)))


def _dot_tn(a, b):
    return _dot(a, b, ((0,), (0,)))


def _transpose_bf16(x):
    return x.astype(F32).T.astype(BF16)


def _matmul(a, b, mode, col0=0, n_cols=None, out_dtype=F32, name="mm"):
    if mode == "nn":
        m, r = a.shape
        n = n_cols or b.shape[1]
        tn, tr = _tile(n, 1024, 128), _tile(r, 1024, 128)
        tm = _tile(m, 1024 if tn <= 1024 else 512, 8)
        c0 = col0 // tn
        a_spec = pl.BlockSpec((tm, tr), lambda i, j, k: (i, k))
        b_spec = pl.BlockSpec((tr, tn), lambda i, j, k: (k, j + c0))
        dims = ((1,), (0,))
        assert col0 % tn == 0
    elif mode == "nt":
        m, r = a.shape
        n = b.shape[0]
        tn, tr = _tile(n, 1024, 128), _tile(r, 1024, 128)
        tm = _tile(m, 1024 if tr <= 1024 else 512, 8)
        c0 = col0 // tr
        a_spec = pl.BlockSpec((tm, tr), lambda i, j, k: (i, k))
        b_spec = pl.BlockSpec((tn, tr), lambda i, j, k: (j, k + c0))
        dims = ((1,), (1,))
        assert col0 % tr == 0
    else:
        r, m = a.shape
        n = b.shape[1]
        tm, tn, tr = _tile(m, 1024, 128), _tile(n, 1024, 128), _tile(r, 512, 16)
        a_spec = pl.BlockSpec((tr, tm), lambda i, j, k: (k, i))
        b_spec = pl.BlockSpec((tr, tn), lambda i, j, k: (k, j))
        dims = ((0,), (0,))
    nr = r // tr

    def body(a_ref, b_ref, o_ref, acc_ref):
        k = pl.program_id(2)

        @pl.when(k == 0)
        def _():
            acc_ref[...] = jnp.zeros_like(acc_ref)

        acc_ref[...] += _dot(a_ref[...].astype(BF16), b_ref[...].astype(BF16), dims)

        @pl.when(k == nr - 1)
        def _():
            o_ref[...] = acc_ref[...].astype(out_dtype)

    return pl.pallas_call(
        body,
        name=name,
        grid=(m // tm, n // tn, nr),
        in_specs=[a_spec, b_spec],
        out_specs=pl.BlockSpec((tm, tn), lambda i, j, k: (i, j)),
        out_shape=jax.ShapeDtypeStruct((m, n), out_dtype),
        scratch_shapes=[pltpu.VMEM((tm, tn), F32)],
        compiler_params=_cparams(("parallel", "parallel", "arbitrary")),
    )(a, b)


def _transpose_cast(a, name):
    r, m = a.shape
    tr = _tile(r, 512, 128)

    def body(a_ref, o_ref):
        o_ref[...] = a_ref[...].astype(F32).T.astype(BF16)

    return pl.pallas_call(
        body, name=name, grid=(r // tr,),
        in_specs=[pl.BlockSpec((tr, m), lambda i: (i, 0))],
        out_specs=pl.BlockSpec((m, tr), lambda i: (0, i)),
        out_shape=jax.ShapeDtypeStruct((m, r), BF16),
        compiler_params=_cparams(("parallel",)),
    )(a)


def _matmul_dw(a, b1, b2, slab, name):
    m, r = a.shape
    n1 = b1.shape[1]
    n = n1 + (b2.shape[1] if b2 is not None else 0)
    tr = _tile(r, 1024, 128)
    if slab is None:
        tm, tn = _tile(m, 1024, 128), _tile(n1, 1024, 128)
        out_spec = pl.BlockSpec((tm, tn), lambda i, j, k: (i, j))
        out_shape = (m, n)
    elif slab[0] == "col":
        tm, tn = _tile(m, 1024, 128), slab[1]
        out_spec = pl.BlockSpec((None, None, tm, tn), lambda i, j, k: (j % 2, j // 2, i, 0))
        out_shape = (2, 4, m, tn)
        assert n == N_DEV * tn
    else:
        tm, tn = slab[1], _tile(n, 1024, 128)
        out_spec = pl.BlockSpec((None, None, tm, tn), lambda i, j, k: (i % 2, i // 2, 0, j))
        out_shape = (2, 4, tm, n)
        assert m == N_DEV * tm
    assert n1 % tn == 0 and n % tn == 0
    if tn > 1024:
        tr = _tile(r, 256, 128)
    n1b = n1 // tn
    nr = r // tr

    def body(*refs):
        a_ref, b_refs, o_ref, acc_ref = refs[0], refs[1:-2], refs[-2], refs[-1]
        j = pl.program_id(1)
        k = pl.program_id(2)

        @pl.when(k == 0)
        def _():
            acc_ref[...] = jnp.zeros_like(acc_ref)

        at = a_ref[...]
        if b2 is None:
            acc_ref[...] += _dot_nn(at, b_refs[0][...].astype(BF16))
        else:
            @pl.when(j < n1b)
            def _():
                acc_ref[...] += _dot_nn(at, b_refs[0][...].astype(BF16))

            @pl.when(j >= n1b)
            def _():
                acc_ref[...] += _dot_nn(at, b_refs[1][...].astype(BF16))

        @pl.when(k == nr - 1)
        def _():
            o_ref[...] = acc_ref[...]

    in_specs = [pl.BlockSpec((tm, tr), lambda i, j, k: (i, k))]
    args = [a, b1]
    if b2 is None:
        in_specs.append(pl.BlockSpec((tr, tn), lambda i, j, k: (k, j)))
    else:
        in_specs.append(pl.BlockSpec((tr, tn), lambda i, j, k: (jnp.where(j < n1b, k, nr - 1), jnp.minimum(j, n1b - 1))))
        in_specs.append(pl.BlockSpec((tr, tn), lambda i, j, k: (jnp.where(j < n1b, 0, k), jnp.maximum(j - n1b, 0))))
        args.append(b2)
    return pl.pallas_call(
        body, name=name, grid=(m // tm, n // tn, nr),
        in_specs=in_specs, out_specs=out_spec,
        out_shape=jax.ShapeDtypeStruct(out_shape, F32),
        scratch_shapes=[pltpu.VMEM((tm, tn), F32)],
        compiler_params=_cparams(("parallel", "parallel", "arbitrary")),
    )(*args)


def mm(a, w, carrier, slab=None, name="mm"):
    @jax.custom_vjp
    def f(a, w, carrier):
        return _matmul(a, w, "nn", 0, None, F32, name + "_fwd")

    def fwd(a, w, carrier):
        return _matmul(a, w, "nn", 0, None, F32, name + "_fwd"), (a, w)

    def bwd(res, g):
        a, w = res
        da = _matmul(g, w, "nt", 0, None, F32, name + "_dx")
        return da, jnp.zeros_like(w), _matmul_dw(_transpose_cast(a, name + "_t"), g, None, slab, name + "_dw")

    f.defvjp(fwd, bwd)
    return f(a, w, carrier)


def in_proj(h, w, carriers, slab, name):
    def run(h, w):
        return (_matmul(h, w, "nn", 0, N_QKV, BF16, name + "_qkv"),
                _matmul(h, w, "nn", N_QKV, D_INNER, F32, name + "_g"))

    @jax.custom_vjp
    def f(h, w, *cars):
        return run(h, w)

    def fwd(h, w, *cars):
        return run(h, w), (h, w)

    def bwd(res, g):
        h, w = res
        g_qkv, g_gate = g
        dh = (_matmul(g_qkv, w, "nt", 0, None, F32, name + "_qkv_dx")
              + _matmul(g_gate, w, "nt", N_QKV, None, F32, name + "_g_dx"))
        h_t = _transpose_cast(h, name + "_t")
        if slab:
            dws = (_matmul_dw(h_t, g_qkv, g_gate, ("col", PACK_COLS), name + "_dw"),)
        else:
            dws = (_matmul_dw(h_t, g_qkv, None, None, name + "_qkv_dw"),
                   _matmul_dw(h_t, g_gate, None, None, name + "_g_dw"))
        return (dh, jnp.zeros_like(w)) + dws

    f.defvjp(fwd, bwd)
    return f(h, w, *carriers)


def _rms_fwd(x, g, name):
    s, d = x.shape
    tm = _tile(s, 512, 8)

    def body(x_ref, g_ref, y_ref):
        x = x_ref[...]
        r = lax.rsqrt(jnp.mean(x * x, axis=-1, keepdims=True) + EPS)
        y_ref[...] = x * r * g_ref[...]

    return pl.pallas_call(
        body, name=name, grid=(s // tm,),
        in_specs=[pl.BlockSpec((tm, d), lambda i: (i, 0)), pl.BlockSpec((1, d), lambda i: (0, 0))],
        out_specs=pl.BlockSpec((tm, d), lambda i: (i, 0)),
        out_shape=jax.ShapeDtypeStruct((s, d), F32),
        compiler_params=_cparams(("parallel",)),
    )(x, g)


def _rms_bwd(x, g, dy, name):
    s, d = x.shape
    tm = _tile(s, 512, 8)

    def body(x_ref, g_ref, dy_ref, dx_ref, dg_ref):
        @pl.when(pl.program_id(0) == 0)
        def _():
            dg_ref[...] = jnp.zeros_like(dg_ref)

        x = x_ref[...]
        dy = dy_ref[...]
        r = lax.rsqrt(jnp.mean(x * x, axis=-1, keepdims=True) + EPS)
        xh = x * r
        dg_ref[...] += jnp.sum(dy * xh, axis=0, keepdims=True)
        dxh = dy * g_ref[...]
        dx_ref[...] = r * (dxh - xh * jnp.mean(dxh * xh, axis=-1, keepdims=True))

    return pl.pallas_call(
        body, name=name, grid=(s // tm,),
        in_specs=[pl.BlockSpec((tm, d), lambda i: (i, 0)), pl.BlockSpec((1, d), lambda i: (0, 0)),
                  pl.BlockSpec((tm, d), lambda i: (i, 0))],
        out_specs=[pl.BlockSpec((tm, d), lambda i: (i, 0)), pl.BlockSpec((1, d), lambda i: (0, 0))],
        out_shape=[jax.ShapeDtypeStruct((s, d), F32), jax.ShapeDtypeStruct((1, d), F32)],
        compiler_params=_cparams(("arbitrary",)),
    )(x, g, dy)


def rmsnorm(x, g, name="rms"):
    @jax.custom_vjp
    def f(x, g):
        return _rms_fwd(x, g.reshape(1, -1), name + "_fwd")

    def fwd(x, g):
        return _rms_fwd(x, g.reshape(1, -1), name + "_fwd"), (x, g)

    def bwd(res, dy):
        x, g = res
        dx, dg = _rms_bwd(x, g.reshape(1, -1), dy, name + "_bwd")
        return dx, dg.reshape(-1)

    f.defvjp(fwd, bwd)
    return f(x, g)


def _gate_fwd(o, gate, name):
    s = o.shape[0]
    tm = 256

    def body(o_ref, g_ref, y_ref):
        g = g_ref[...]
        y_ref[...] = o_ref[...] * (g / (1.0 + jnp.exp(-g)))

    row = pl.BlockSpec((tm, D_INNER), lambda i: (i, 0))
    return pl.pallas_call(
        body, name=name, grid=(s // tm,),
        in_specs=[row, row], out_specs=row,
        out_shape=jax.ShapeDtypeStruct((s, D_INNER), F32),
        compiler_params=_cparams(("parallel",)),
    )(o, gate)


def _gate_bwd(dy, o, gate, name):
    s = o.shape[0]
    tm = 256

    def body(dy_ref, o_ref, g_ref, do_ref, dg_ref, dl_ref):
        g = g_ref[...]
        o = o_ref[...]
        dy = dy_ref[...]
        sg = 1.0 / (1.0 + jnp.exp(-g))
        do = dy * (g * sg)
        do_ref[...] = do.astype(BF16)
        dg_ref[...] = dy * o * (sg * (1.0 + g * (1.0 - sg)))
        prod = do * o
        for h in range(N_HEADS):
            dl_ref[h] = jnp.sum(prod[:, h * HEAD_DIM:(h + 1) * HEAD_DIM], axis=1, keepdims=True)

    row = pl.BlockSpec((tm, D_INNER), lambda i: (i, 0))
    return pl.pallas_call(
        body, name=name, grid=(s // tm,),
        in_specs=[row, row, row],
        out_specs=[row, row, pl.BlockSpec((N_HEADS, tm, 1), lambda i: (0, i, 0))],
        out_shape=[jax.ShapeDtypeStruct((s, D_INNER), BF16), jax.ShapeDtypeStruct((s, D_INNER), F32),
                   jax.ShapeDtypeStruct((N_HEADS, s, 1), F32)],
        compiler_params=_cparams(("parallel",)),
    )(dy, o, gate)


SB_BLK = 256


def _softplus(z):
    return jnp.maximum(z, 0.0) + jnp.log(1.0 + jnp.exp(-jnp.abs(z)))


def _tri_sum(x, tri):
    hi = x.astype(BF16)
    lo = (x - hi.astype(F32)).astype(BF16)
    return _dot_nn(hi, tri) + _dot_nn(lo, tri)


SB_WIN = 2 * SB_BLK


def _sb_tri(kind):
    row = lax.broadcasted_iota(jnp.int32, (SB_BLK, SB_BLK), 0)
    col = lax.broadcasted_iota(jnp.int32, (SB_BLK, SB_BLK), 1)
    return ((row >= col) if kind == "suffix" else (row <= col)).astype(BF16)


def _sb_bounds(i, t):
    hi = (i + 1) * SB_BLK - t * SB_WIN
    return hi, pl.multiple_of(jnp.maximum(hi - SB_WIN, 0), SB_BLK)


def _sb_mask(i, hi, start):
    row = lax.broadcasted_iota(jnp.int32, (SB_BLK, SB_WIN), 0) + i * SB_BLK
    col = lax.broadcasted_iota(jnp.int32, (SB_BLK, SB_WIN), 1) + start
    return jnp.logical_and(col < row, col < hi)


def _sb_window(q, kwin, mask, a_run, tri_suffix, scale):
    b = SB_BLK
    z = _dot_nt(q, kwin) * scale
    sp = _softplus(z)
    ls = jnp.where(mask, -sp, 0.0)
    ls_l, ls_r = ls[:, :b], ls[:, b:]
    suffix = jnp.concatenate([_tri_sum(ls_l, tri_suffix) + jnp.sum(ls_r, axis=1, keepdims=True),
                              _tri_sum(ls_r, tri_suffix)], axis=1)
    w = jnp.where(mask, jnp.exp(z + suffix + a_run), 0.0)
    return z, sp, ls, w


def _sb_fwd(qkv, name, gather=None):
    s = qkv.shape[0]
    b = SB_BLK
    nq = s // b
    scale = HEAD_DIM ** -0.5
    assert s >= SB_WIN
    shards, kinds = gather if gather else ((), ())
    nm = len(shards)

    def body(*refs):
        q_ref, k_ref, v_ref = refs[:3]
        o_ref = refs[3 + nm]
        i = pl.program_id(1)
        if nm:
            start, finish = _gather_steps(refs[3:3 + nm], refs[4 + nm:4 + 2 * nm], kinds, *refs[4 + 2 * nm:])
            first_step = jnp.logical_and(pl.program_id(0) == 0, i == 0)
            last_step = jnp.logical_and(pl.program_id(0) == N_HEADS - 1, i == nq - 1)
            pl.when(first_step)(start)
        q = q_ref[...]
        tri_suffix = _sb_tri("suffix")

        def cond(c):
            t, a_run, _ = c
            return jnp.logical_and((i + 1) * b - t * SB_WIN > 0, jnp.max(a_run) > -SB_CUT)

        def step(c):
            t, a_run, acc = c
            hi, start = _sb_bounds(i, t)
            _, _, ls, w = _sb_window(q, k_ref[pl.ds(start, SB_WIN), :], _sb_mask(i, hi, start), a_run,
                                     tri_suffix, scale)
            acc = acc + _dot_nn(w.astype(BF16), v_ref[pl.ds(start, SB_WIN), :])
            return t + 1, a_run + jnp.sum(ls, axis=1, keepdims=True), acc

        _, _, acc = lax.while_loop(cond, step, (0, jnp.zeros((b, 1), F32), jnp.zeros((b, HEAD_DIM), F32)))
        o_ref[...] = acc
        if nm:
            pl.when(last_step)(finish)

    out = pl.pallas_call(
        body, name=name, grid=(N_HEADS, nq),
        in_specs=[pl.BlockSpec((b, HEAD_DIM), lambda h, i: (i, h)),
                  pl.BlockSpec((s, HEAD_DIM), lambda h, i: (0, N_HEADS + h)),
                  pl.BlockSpec((s, HEAD_DIM), lambda h, i: (0, 2 * N_HEADS + h))] + [ANY] * nm,
        out_specs=[pl.BlockSpec((b, HEAD_DIM), lambda h, i: (i, h))] + [ANY] * nm,
        out_shape=[jax.ShapeDtypeStruct((s, D_INNER), F32)] + _gather_out_shapes(shards, kinds),
        scratch_shapes=_gather_scratch(nm) if nm else [],
        compiler_params=_cparams(("arbitrary", "arbitrary")),
    )(qkv, qkv, qkv, *shards)
    return out[0], list(out[1:])


def _sb_bwd(qkv, do, name):
    s = qkv.shape[0]
    b = SB_BLK
    nq = s // b
    n_win = -(-s // SB_WIN) + 1
    scale = HEAD_DIM ** -0.5
    assert s >= SB_WIN

    def body(q_ref, k_ref, v_ref, do_ref, dq_ref, dk_ref, dv_ref, dkt_s, dvt_s, g_buf, sig_buf):
        i = pl.program_id(1)

        @pl.when(i == 0)
        def _():
            dkt_s[...] = jnp.zeros_like(dkt_s)
            dvt_s[...] = jnp.zeros_like(dvt_s)

        q = q_ref[...]
        dob = do_ref[...]
        q_t = _transpose_bf16(q)
        do_t = _transpose_bf16(dob)
        tri_suffix = _sb_tri("suffix")
        tri_prefix = _sb_tri("prefix")

        def add_halves(acc_ref, start, upd):
            blk = start // b
            acc_ref[blk] += upd[:, :b]
            acc_ref[blk + 1] += upd[:, b:]

        def cond(c):
            t, a_run = c
            return jnp.logical_and((i + 1) * b - t * SB_WIN > 0, jnp.max(a_run) > -SB_CUT)

        def sweep(c):
            t, a_run = c
            hi, start = _sb_bounds(i, t)
            z, sp, ls, w = _sb_window(q, k_ref[pl.ds(start, SB_WIN), :], _sb_mask(i, hi, start), a_run,
                                      tri_suffix, scale)
            g_buf[t] = w * _dot_nt(dob, v_ref[pl.ds(start, SB_WIN), :])
            sig_buf[t] = jnp.exp(z - sp)
            add_halves(dvt_s, start, _dot_nn(do_t, w.astype(BF16)))
            return t + 1, a_run + jnp.sum(ls, axis=1, keepdims=True)

        n_steps, _ = lax.while_loop(cond, sweep, (0, jnp.zeros((b, 1), F32)))

        def back(u, c):
            g_run, dq = c
            t = n_steps - 1 - u
            hi, start = _sb_bounds(i, t)
            g = g_buf[t]
            g_l, g_r = g[:, :b], g[:, b:]
            g_incl = g_run + jnp.concatenate(
                [_tri_sum(g_l, tri_prefix),
                 _tri_sum(g_r, tri_prefix) + jnp.sum(g_l, axis=1, keepdims=True)], axis=1)
            dz = jnp.where(_sb_mask(i, hi, start), (g - sig_buf[t] * g_incl) * scale, 0.0).astype(BF16)
            add_halves(dkt_s, start, _dot_nn(q_t, dz))
            return g_run + jnp.sum(g, axis=1, keepdims=True), dq + _dot_nn(dz, k_ref[pl.ds(start, SB_WIN), :])

        _, dq = lax.fori_loop(0, n_steps, back, (jnp.zeros((b, 1), F32), jnp.zeros((b, HEAD_DIM), F32)))
        dq_ref[...] = dq.astype(BF16)

        @pl.when(i == nq - 1)
        def _():
            for jb in range(nq):
                dk_ref[jb * b:(jb + 1) * b, :] = dkt_s[jb].T.astype(BF16)
                dv_ref[jb * b:(jb + 1) * b, :] = dvt_s[jb].T.astype(BF16)

    blk = pl.BlockSpec((b, HEAD_DIM), lambda h, i: (i, h))
    head = pl.BlockSpec((s, HEAD_DIM), lambda h, i: (0, h))
    return pl.pallas_call(
        body, name=name, grid=(N_HEADS, nq),
        in_specs=[blk,
                  pl.BlockSpec((s, HEAD_DIM), lambda h, i: (0, N_HEADS + h)),
                  pl.BlockSpec((s, HEAD_DIM), lambda h, i: (0, 2 * N_HEADS + h)),
                  blk],
        out_specs=[blk, head, head],
        out_shape=[jax.ShapeDtypeStruct((s, D_INNER), BF16)] * 3,
        scratch_shapes=[pltpu.VMEM((nq, HEAD_DIM, b), F32), pltpu.VMEM((nq, HEAD_DIM, b), F32),
                        pltpu.VMEM((n_win, b, SB_WIN), F32), pltpu.VMEM((n_win, b, SB_WIN), F32)],
        compiler_params=_cparams(("arbitrary", "arbitrary")),
    )(qkv, qkv, qkv, do)


def sb_core(qkv, gate, name, gather=None):
    shards, kinds = gather if gather else ((), ())

    def run(qkv, gate, *shards):
        o, gathered = _sb_fwd(qkv, name + "_fwd", (shards, kinds) if shards else None)
        return (_gate_fwd(o, gate, name + "_gate"), *gathered), o

    @jax.custom_vjp
    def f(qkv, gate, *shards):
        return run(qkv, gate, *shards)[0]

    def fwd(qkv, gate, *shards):
        outs, o = run(qkv, gate, *shards)
        return outs, (qkv, gate, o)

    def bwd(res, cts):
        qkv, gate, o = res
        do, dgate, _ = _gate_bwd(cts[0], o, gate, name + "_gate_bwd")
        dq, dk, dv = _sb_bwd(qkv, do, name + "_bwd")
        return (jnp.concatenate([dq, dk, dv], axis=1), dgate) + tuple(jnp.zeros(sh.shape, sh.dtype) for sh in shards)

    f.defvjp(fwd, bwd)
    outs = f(qkv, gate, *shards)
    return outs[0], list(outs[1:])


SM_FWD_BLK = 256
SM_BLK = 512


SM_FWD_KEYS = 1024
SM_BWD_KEYS = 512


def _sm_mask(i, jw, rows, keys, chunk_shift):
    row = lax.broadcasted_iota(jnp.int32, (rows, keys), 0) + i * rows
    col = lax.broadcasted_iota(jnp.int32, (rows, keys), 1) + jw * keys
    return (col >> chunk_shift) <= (row >> chunk_shift)


def _sm_fwd(qa, ka, va, ccol, crow, dqk, qo, ko, vo, chunk_shift, scale, name):
    s = qa.shape[0]
    b = min(SM_FWD_BLK, s)
    keys = min(SM_FWD_KEYS, s)
    per = keys // b
    nq = s // b
    has_bias = ccol is not None

    def body(*refs):
        if has_bias:
            q_ref, k_ref, v_ref, cc_ref, cr_ref, o_ref, lse_ref, m_s, l_s, acc_s = refs
        else:
            q_ref, k_ref, v_ref, o_ref, lse_ref, m_s, l_s, acc_s = refs
        i = pl.program_id(1)
        q = q_ref[...]
        m_s[...] = jnp.full_like(m_s, NEG)
        l_s[...] = jnp.zeros_like(l_s)
        acc_s[...] = jnp.zeros_like(acc_s)

        def sweep(jw, masked):
            off = pl.multiple_of(jw * keys, keys)
            z = _dot_nt(q, k_ref[pl.ds(off, keys), :]) * scale
            if has_bias:
                z = z + cc_ref[...] - cr_ref[jw]
            if masked:
                z = jnp.where(_sm_mask(i, jw, b, keys, chunk_shift), z, NEG)
            m_old = m_s[...]
            m_new = jnp.maximum(m_old, jnp.max(z, axis=1, keepdims=True))
            alpha = jnp.exp(m_old - m_new)
            p = jnp.exp(z - m_new)
            l_s[...] = alpha * l_s[...] + jnp.sum(p, axis=1, keepdims=True)
            acc_s[...] = alpha * acc_s[...] + _dot_nn(p.astype(BF16), v_ref[pl.ds(off, keys), :])
            m_s[...] = m_new

        def full(jw, carry):
            sweep(jw, False)
            return carry

        lax.fori_loop(0, i // per, full, 0)
        sweep(i // per, True)
        o_ref[...] = acc_s[...] / l_s[...]
        lse_ref[...] = m_s[...] + jnp.log(l_s[...])

    in_specs = [pl.BlockSpec((b, dqk), lambda h, i: (i, qo + h)),
                pl.BlockSpec((s, dqk), lambda h, i: (0, ko + h)),
                pl.BlockSpec((s, HEAD_DIM), lambda h, i: (0, vo + h))]
    args = [qa, ka, va]
    if has_bias:
        in_specs += [pl.BlockSpec((None, b, 1), lambda h, i: (h, i, 0)),
                     pl.BlockSpec((None, s // keys, 1, keys), lambda h, i: (h, 0, 0, 0))]
        args += [ccol, crow]
    return pl.pallas_call(
        body, name=name, grid=(N_HEADS, nq),
        in_specs=in_specs,
        out_specs=[pl.BlockSpec((b, HEAD_DIM), lambda h, i: (i, h)),
                   pl.BlockSpec((None, b, 1), lambda h, i: (h, i, 0))],
        out_shape=[jax.ShapeDtypeStruct((s, D_INNER), F32), jax.ShapeDtypeStruct((N_HEADS, s, 1), F32)],
        scratch_shapes=[pltpu.VMEM((b, 1), F32), pltpu.VMEM((b, 1), F32), pltpu.VMEM((b, HEAD_DIM), F32)],
        compiler_params=_cparams(("parallel", "arbitrary")),
    )(*args)


def _sm_bwd(qa, ka, va, do, lse, delta, ccol, crow, dqk, qo, ko, vo, chunk_shift, scale, grad_dtype, name):
    s = qa.shape[0]
    b = SM_BLK
    keys = min(SM_BWD_KEYS, s)
    per = keys // b
    nq = s // b
    nk = s // keys
    has_bias = ccol is not None

    def body(*refs):
        if has_bias:
            (q_ref, k_ref, v_ref, do_ref, lse_ref, dl_ref, cc_ref, cr_ref,
             dq_ref, dk_ref, dv_ref, dc_ref, dr_ref, dq_s, dkt_s, dvt_s, dc_s, dr_s) = refs
        else:
            (q_ref, k_ref, v_ref, do_ref, lse_ref, dl_ref,
             dq_ref, dk_ref, dv_ref, dq_s, dkt_s, dvt_s) = refs
        i = pl.program_id(1)

        @pl.when(i == 0)
        def _():
            dkt_s[...] = jnp.zeros_like(dkt_s)
            dvt_s[...] = jnp.zeros_like(dvt_s)
            if has_bias:
                dc_s[...] = jnp.zeros_like(dc_s)

        q = q_ref[...]
        dob = do_ref[...]
        q_t = _transpose_bf16(q)
        do_t = _transpose_bf16(dob)
        lse = lse_ref[...]
        delta = dl_ref[...]
        dq_s[...] = jnp.zeros_like(dq_s)
        if has_bias:
            dr_s[...] = jnp.zeros_like(dr_s)

        def sweep(jw, masked):
            off = pl.multiple_of(jw * keys, keys)
            kb = k_ref[pl.ds(off, keys), :]
            z = _dot_nt(q, kb) * scale
            if has_bias:
                z = z + cc_ref[...] - cr_ref[jw]
            p = jnp.exp(z - lse)
            if masked:
                p = jnp.where(_sm_mask(i, jw, b, keys, chunk_shift), p, 0.0)
            dvt_s[jw] += _dot_nn(do_t, p.astype(BF16))
            dz = p * (_dot_nt(dob, v_ref[pl.ds(off, keys), :]) - delta)
            if has_bias:
                dc_s[jw] += jnp.sum(dz, axis=0, keepdims=True)
                dr_s[...] += jnp.sum(dz, axis=1, keepdims=True)
            dzs = (dz * scale).astype(BF16)
            dkt_s[jw] += _dot_nn(q_t, dzs)
            dq_s[...] += _dot_nn(dzs, kb)

        def full(jw, carry):
            sweep(jw, False)
            return carry

        lax.fori_loop(0, i // per, full, 0)
        sweep(i // per, True)
        dq_ref[...] = dq_s[...].astype(grad_dtype)
        if has_bias:
            dr_ref[...] = dr_s[...]

        @pl.when(i == nq - 1)
        def _():
            for jw in range(nk):
                dk_ref[jw * keys:(jw + 1) * keys, :] = dkt_s[jw].T.astype(grad_dtype)
                dv_ref[jw * keys:(jw + 1) * keys, :] = dvt_s[jw].T.astype(grad_dtype)
            if has_bias:
                dc_ref[...] = dc_s[...]

    vec = pl.BlockSpec((None, b, 1), lambda h, i: (h, i, 0))
    in_specs = [pl.BlockSpec((b, dqk), lambda h, i: (i, qo + h)),
                pl.BlockSpec((s, dqk), lambda h, i: (0, ko + h)),
                pl.BlockSpec((s, HEAD_DIM), lambda h, i: (0, vo + h)),
                pl.BlockSpec((b, HEAD_DIM), lambda h, i: (i, h)),
                vec, vec]
    args = [qa, ka, va, do, lse, delta]
    out_specs = [pl.BlockSpec((b, dqk), lambda h, i: (i, h)),
                 pl.BlockSpec((s, dqk), lambda h, i: (0, h)),
                 pl.BlockSpec((s, HEAD_DIM), lambda h, i: (0, h))]
    out_shape = [jax.ShapeDtypeStruct((s, N_HEADS * dqk), grad_dtype),
                 jax.ShapeDtypeStruct((s, N_HEADS * dqk), grad_dtype),
                 jax.ShapeDtypeStruct((s, D_INNER), grad_dtype)]
    scratch = [pltpu.VMEM((b, dqk), F32), pltpu.VMEM((nk, dqk, keys), F32), pltpu.VMEM((nk, HEAD_DIM, keys), F32)]
    if has_bias:
        key_vec = pl.BlockSpec((None, nk, 1, keys), lambda h, i: (h, 0, 0, 0))
        in_specs += [vec, key_vec]
        args += [ccol, crow]
        out_specs += [key_vec, vec]
        out_shape += [jax.ShapeDtypeStruct((N_HEADS, nk, 1, keys), F32), jax.ShapeDtypeStruct((N_HEADS, s, 1), F32)]
        scratch += [pltpu.VMEM((nk, 1, keys), F32), pltpu.VMEM((b, 1), F32)]
    return pl.pallas_call(
        body, name=name, grid=(N_HEADS, nq),
        in_specs=in_specs, out_specs=out_specs, out_shape=out_shape, scratch_shapes=scratch,
        compiler_params=_cparams(("arbitrary", "arbitrary")),
    )(*args)


def fox_core(qkv, gate, c, name):
    s = qkv.shape[0]
    scale = HEAD_DIM ** -0.5
    cfg = dict(dqk=HEAD_DIM, qo=0, ko=N_HEADS, vo=2 * N_HEADS, chunk_shift=0, scale=scale)

    def layouts(c, keys):
        ct = c.T
        keys = min(keys, s)
        return ct.reshape(N_HEADS, s, 1), ct.reshape(N_HEADS, s // keys, 1, keys)

    def run(qkv, gate, c):
        ccol, crow = layouts(c, SM_FWD_KEYS)
        o, lse = _sm_fwd(qkv, qkv, qkv, ccol, crow, name=name + "_fwd", **cfg)
        return _gate_fwd(o, gate, name + "_gate"), o, lse

    @jax.custom_vjp
    def f(qkv, gate, c):
        return run(qkv, gate, c)[0]

    def fwd(qkv, gate, c):
        y, o, lse = run(qkv, gate, c)
        return y, (qkv, gate, c, o, lse)

    def bwd(res, dy):
        qkv, gate, c, o, lse = res
        ccol, crow = layouts(c, SM_BWD_KEYS)
        do, dgate, delta = _gate_bwd(dy, o, gate, name + "_gate_bwd")
        dq, dk, dv, colsum, rowsum = _sm_bwd(qkv, qkv, qkv, do, lse, delta, ccol, crow,
                                             grad_dtype=BF16, name=name + "_bwd", **cfg)
        dc = (rowsum.reshape(N_HEADS, s) - colsum.reshape(N_HEADS, s)).T
        return jnp.concatenate([dq, dk, dv], axis=1), dgate, dc

    f.defvjp(fwd, bwd)
    return f(qkv, gate, c)


def mla_core(qc, kc, v, gate, name):
    scale = (MLA_NOPE + MLA_ROPE) ** -0.5
    cfg = dict(dqk=MLA_QK_PAD, qo=0, ko=0, vo=0, chunk_shift=MLA_CHUNK.bit_length() - 1, scale=scale)

    def run(qc, kc, v, gate):
        o, lse = _sm_fwd(qc, kc, v, None, None, name=name + "_fwd", **cfg)
        return _gate_fwd(o, gate, name + "_gate"), o, lse

    @jax.custom_vjp
    def f(qc, kc, v, gate):
        return run(qc.astype(BF16), kc.astype(BF16), v.astype(BF16), gate)[0]

    def fwd(qc, kc, v, gate):
        qc, kc, v = qc.astype(BF16), kc.astype(BF16), v.astype(BF16)
        y, o, lse = run(qc, kc, v, gate)
        return y, (qc, kc, v, gate, o, lse)

    def bwd(res, dy):
        qc, kc, v, gate, o, lse = res
        do, dgate, delta = _gate_bwd(dy, o, gate, name + "_gate_bwd")
        dq, dk, dv = _sm_bwd(qc, kc, v, do, lse, delta, None, None, grad_dtype=F32, name=name + "_bwd", **cfg)
        return dq, dk, dv, dgate

    f.defvjp(fwd, bwd)
    return f(qc, kc, v, gate)


def _sq_loss_call(y, t, name):
    s, d = y.shape
    tm = _tile(s, 512, 8)

    def body(y_ref, t_ref, l_ref, e_ref):
        @pl.when(pl.program_id(0) == 0)
        def _():
            l_ref[...] = jnp.zeros_like(l_ref)

        e = y_ref[...] - t_ref[...]
        e_ref[...] = e * (1.0 / d)
        part = jnp.sum(jnp.sum(e * e, axis=1, keepdims=True), axis=0, keepdims=True)
        l_ref[...] += jnp.broadcast_to(part * (0.5 / d), l_ref.shape)

    row = pl.BlockSpec((tm, d), lambda i: (i, 0))
    return pl.pallas_call(
        body, name=name, grid=(s // tm,),
        in_specs=[row, row],
        out_specs=[pl.BlockSpec((8, 128), lambda i: (0, 0)), row],
        out_shape=[jax.ShapeDtypeStruct((8, 128), F32), jax.ShapeDtypeStruct((s, d), F32)],
        compiler_params=_cparams(("arbitrary",)),
    )(y, t)


@jax.custom_vjp
def sq_loss(y, t):
    return _sq_loss_call(y, t, "loss_fwd")[0][0, 0]


def _sq_loss_fwd(y, t):
    l, e = _sq_loss_call(y, t, "loss_fwd")
    return l[0, 0], e


def _sq_loss_bwd(e, g):
    return g * e, jnp.zeros_like(e)


sq_loss.defvjp(_sq_loss_fwd, _sq_loss_bwd)


def _cast_bf16(x, name):
    r, c = x.shape
    tb = _tile(r, 512, 16)

    def body(x_ref, o_ref):
        o_ref[...] = x_ref[...].astype(BF16)

    return pl.pallas_call(
        body, name=name, grid=(r // tb,),
        in_specs=[pl.BlockSpec((tb, c), lambda i: (i, 0))],
        out_specs=pl.BlockSpec((tb, c), lambda i: (i, 0)),
        out_shape=jax.ShapeDtypeStruct((r, c), BF16),
        compiler_params=_cparams(("parallel",)),
    )(x)


def _pair_sum(core_idx, g, recv, name):
    _, _, r, c = g.shape
    tb = _tile(r, 512, 16)

    def body(c_ref, g_ref, r_ref, o_ref):
        o_ref[...] = (g_ref[...] + r_ref[...]).astype(BF16)

    return pl.pallas_call(
        body, name=name,
        grid_spec=pltpu.PrefetchScalarGridSpec(
            num_scalar_prefetch=1, grid=(4, r // tb),
            in_specs=[pl.BlockSpec((None, None, tb, c), lambda q, i, c_ref: (c_ref[0], q, i, 0)),
                      pl.BlockSpec((None, tb, c), lambda q, i, c_ref: (q, i, 0))],
            out_specs=pl.BlockSpec((None, tb, c), lambda q, i, c_ref: (q, i, 0))),
        out_shape=jax.ShapeDtypeStruct((4, r, c), BF16),
        compiler_params=_cparams(("parallel", "parallel")),
    )(core_idx, g, recv)


def _adamw(w, parts, m, v, name):
    n, r, c = parts.shape
    tb = _tile(r, 256, 8)
    b1c = 1.0 - ADAM_B1 ** ADAM_STEP
    b2c = 1.0 - ADAM_B2 ** ADAM_STEP

    def body(w_ref, p_ref, m_ref, v_ref, g_ref, d_ref, nm_ref, nv_ref):
        g = p_ref[0].astype(F32)
        for k in range(1, n):
            g = g + p_ref[k].astype(F32)
        m_new = ADAM_B1 * m_ref[...] + (1.0 - ADAM_B1) * g
        v_new = ADAM_B2 * v_ref[...] + (1.0 - ADAM_B2) * (g * g)
        m_hat = m_new / b1c
        v_hat = v_new / b2c
        g_ref[...] = g
        d_ref[...] = -ADAM_LR * (m_hat / (jnp.sqrt(v_hat) + ADAM_EPS) + ADAM_WD * w_ref[...])
        nm_ref[...] = m_new
        nv_ref[...] = v_new

    row = pl.BlockSpec((tb, c), lambda i: (i, 0))
    return pl.pallas_call(
        body, name=name, grid=(r // tb,),
        in_specs=[row, pl.BlockSpec((n, tb, c), lambda i: (0, i, 0)), row, row],
        out_specs=[row] * 4,
        out_shape=[jax.ShapeDtypeStruct((r, c), F32)] * 4,
        compiler_params=_cparams(("parallel",)),
    )(w, parts, m, v)


ANY = pl.BlockSpec(memory_space=pl.ANY)


def _place():
    return lax.axis_index("x"), lax.axis_index("y"), lax.axis_index("c")


def _all_gather(shards, kinds, name):
    nm = len(shards)

    def body(*refs):
        start, finish = _gather_steps(refs[:nm], refs[nm:2 * nm], kinds, *refs[2 * nm:])
        start()
        finish()

    return pl.pallas_call(
        body, name=name,
        out_shape=_gather_out_shapes(shards, kinds),
        in_specs=[ANY] * nm, out_specs=[ANY] * nm,
        scratch_shapes=_gather_scratch(nm),
    )(*shards)


def _gather_out_shapes(shards, kinds):
    def full(sh, kind):
        a, b = sh.shape
        return {"row": (N_DEV * a, b), "col": (a, N_DEV * b), "stack": (N_DEV, a, b)}[kind]

    return [jax.ShapeDtypeStruct(full(sh, kd), sh.dtype) for sh, kd in zip(shards, kinds)]


def _gather_scratch(nm):
    return [pltpu.SemaphoreType.DMA((7 * nm,)), pltpu.SemaphoreType.DMA((7 * nm,)), pltpu.SemaphoreType.DMA((nm,))]


def _gather_steps(x_refs, out_refs, kinds, send_sems, recv_sems, local_sems):
    nm = len(x_refs)
    x, y, cc = _place()
    me, sibling = (x, y, cc), (x, y, 1 - cc)
    chips = [(1 - x, y), (x, 1 - y), (1 - x, 1 - y)]

    def slot(mi, px, py, pc):
        d = 4 * px + 2 * py + pc
        a, b = x_refs[mi].shape
        if kinds[mi] == "row":
            return out_refs[mi].at[pl.ds(pl.multiple_of(d * a, a), a), :]
        if kinds[mi] == "col":
            return out_refs[mi].at[:, pl.ds(pl.multiple_of(d * b, 128), b)]
        return out_refs[mi].at[d]

    def copy(mi, k, block, to, src=None):
        return pltpu.make_async_remote_copy(
            src_ref=slot(mi, *block) if src is None else src, dst_ref=slot(mi, *block),
            send_sem=send_sems.at[7 * mi + k], recv_sem=recv_sems.at[7 * mi + k],
            device_id=to, device_id_type=MESH)

    def own_copies():
        mine = [pltpu.make_async_copy(x_refs[mi], slot(mi, *me), local_sems.at[mi]) for mi in range(nm)]
        first = []
        for mi in range(nm):
            first.append(copy(mi, 0, me, sibling, src=x_refs[mi]))
            first += [copy(mi, 1 + j, me, (*chip, cc), src=x_refs[mi]) for j, chip in enumerate(chips)]
        return mine, first

    def start():
        mine, first = own_copies()
        for cp in mine + first:
            cp.start()

    def finish():
        mine, first = own_copies()
        passed = []
        for j, chip in enumerate(chips):
            for mi in range(nm):
                copy(mi, 1 + j, (*chip, cc), me).wait_recv()
                passed.append(copy(mi, 4 + j, (*chip, cc), sibling))
                passed[-1].start()
        for mi in range(nm):
            copy(mi, 0, sibling, me).wait_recv()
            for j, chip in enumerate(chips):
                copy(mi, 4 + j, (*chip, 1 - cc), me).wait_recv()
        for cp in first + passed:
            cp.wait_send()
        for cp in mine:
            cp.wait()

    return start, finish


def _pair_exchange(gs, name):
    nm = len(gs)

    def body(*refs):
        g_refs, recv_refs = refs[:nm], refs[nm:2 * nm]
        send_sems, recv_sems = refs[2 * nm:]
        x, y, cc = _place()
        copies = [pltpu.make_async_remote_copy(
            src_ref=g_refs[mi].at[1 - cc], dst_ref=recv_refs[mi],
            send_sem=send_sems.at[mi], recv_sem=recv_sems.at[mi], device_id=(x, y, 1 - cc), device_id_type=MESH)
            for mi in range(nm)]
        for cp in copies:
            cp.start()
        for cp in copies:
            cp.wait_recv()
        for cp in copies:
            cp.wait_send()

    return pl.pallas_call(
        body, name=name,
        out_shape=[jax.ShapeDtypeStruct(g.shape[1:], g.dtype) for g in gs],
        in_specs=[ANY] * nm, out_specs=[ANY] * nm,
        scratch_shapes=[pltpu.SemaphoreType.DMA((nm,)), pltpu.SemaphoreType.DMA((nm,))],
    )(*gs)


def _chip_exchange(parts, name):
    nm = len(parts)

    def body(*refs):
        p_refs, out_refs = refs[:nm], refs[nm:2 * nm]
        send_sems, recv_sems, local_sems = refs[2 * nm:]
        x, y, cc = _place()
        mine = 2 * x + y
        others = [(1 - x, y), (x, 1 - y), (1 - x, 1 - y)]
        keeps = [pltpu.make_async_copy(p_refs[mi].at[mine], out_refs[mi].at[mine], local_sems.at[mi]) for mi in range(nm)]
        for cp in keeps:
            cp.start()
        sends = []
        for px, py in others:
            q = 2 * px + py
            for mi in range(nm):
                sends.append(pltpu.make_async_remote_copy(
                    src_ref=p_refs[mi].at[q], dst_ref=out_refs[mi].at[mine],
                    send_sem=send_sems.at[4 * mi + q], recv_sem=recv_sems.at[4 * mi + mine],
                    device_id=(px, py, cc), device_id_type=MESH))
        for cp in sends:
            cp.start()
        for px, py in others:
            q = 2 * px + py
            for mi in range(nm):
                pltpu.make_async_remote_copy(
                    src_ref=p_refs[mi].at[q], dst_ref=out_refs[mi].at[q],
                    send_sem=send_sems.at[4 * mi + q], recv_sem=recv_sems.at[4 * mi + q],
                    device_id=(px, py, cc), device_id_type=MESH).wait_recv()
        for cp in sends:
            cp.wait_send()
        for cp in keeps:
            cp.wait()

    return pl.pallas_call(
        body, name=name,
        out_shape=[jax.ShapeDtypeStruct(p.shape, p.dtype) for p in parts],
        in_specs=[ANY] * nm, out_specs=[ANY] * nm,
        scratch_shapes=[pltpu.SemaphoreType.DMA((4 * nm,)), pltpu.SemaphoreType.DMA((4 * nm,)),
                        pltpu.SemaphoreType.DMA((nm,))],
    )(*parts)


def _all_reduce_small(v, name):
    shape = v.shape

    def body(v_ref, out_ref, buf, send_sems, recv_sems):
        x, y, cc = _place()
        me = 4 * x + 2 * y + cc
        buf[me] = v_ref[...]
        flips = [(a, b, d) for a in (0, 1) for b in (0, 1) for d in (0, 1)][1:]
        copies = []
        for k, (a, b, d) in enumerate(flips):
            peer = (x ^ a, y ^ b, cc ^ d)
            copies.append(pltpu.make_async_remote_copy(
                src_ref=v_ref, dst_ref=buf.at[me],
                send_sem=send_sems.at[k], recv_sem=recv_sems.at[k], device_id=peer, device_id_type=MESH))
        for cp in copies:
            cp.start()
        for k, (a, b, d) in enumerate(flips):
            peer_id = 4 * (x ^ a) + 2 * (y ^ b) + (cc ^ d)
            pltpu.make_async_remote_copy(
                src_ref=v_ref, dst_ref=buf.at[peer_id],
                send_sem=send_sems.at[k], recv_sem=recv_sems.at[k], device_id=(x, y, cc), device_id_type=MESH
            ).wait_recv()
        for cp in copies:
            cp.wait_send()
        total = buf[0]
        for k in range(1, N_DEV):
            total = total + buf[k]
        out_ref[...] = total

    vm = pl.BlockSpec(memory_space=pltpu.VMEM)
    return pl.pallas_call(
        body, name=name,
        out_shape=jax.ShapeDtypeStruct(shape, F32),
        in_specs=[vm], out_specs=vm,
        scratch_shapes=[pltpu.VMEM((N_DEV,) + shape, F32), pltpu.SemaphoreType.DMA((7,)), pltpu.SemaphoreType.DMA((7,))],
    )(v)


def _gather_kind(name, shape):
    if name not in COL_SHARDED:
        return "row"
    return "col" if shape[1] % 128 == 0 else "stack"


def _slab_of(name, shape):
    if name not in COL_SHARDED:
        return ("row", shape[0])
    return ("col", shape[1]) if shape[1] % 128 == 0 else None


def _to_slabs(g, shape):
    kk, nn = shape
    return g.reshape(kk, 4, 2, nn).transpose(2, 1, 0, 3)


SMALL_ROWS = 8


def _pack_small(arrs):
    rows = [arrs[n] for n in SMALL[:5]]
    last = jnp.concatenate([arrs["q_norm1"], arrs["kv_norm1"], arrs["b_f2"]])
    rows.append(jnp.pad(last, (0, PACK_COLS - last.shape[0])))
    rows += [jnp.zeros((PACK_COLS,), F32)] * (SMALL_ROWS - len(rows))
    return jnp.stack(rows)


def _unpack_small(p):
    out = {n: p[k] for k, n in enumerate(SMALL[:5])}
    out["q_norm1"] = p[5, :MLA_Q_RANK]
    out["kv_norm1"] = p[5, MLA_Q_RANK:MLA_Q_RANK + MLA_KV_RANK]
    out["b_f2"] = p[5, MLA_Q_RANK + MLA_KV_RANK:MLA_Q_RANK + MLA_KV_RANK + N_HEADS]
    return out


N_QKV = 3 * D_INNER
N_MAIN = 4 * D_INNER


def _rope(x, pos):
    r = x.shape[-1]
    inv_freq = ROPE_BASE ** (-jnp.arange(0, r, 2, dtype=F32) / r)
    ang = pos.astype(F32)[:, None, None] * inv_freq
    cos, sin = jnp.cos(ang), jnp.sin(ang)
    x1, x2 = x[..., : r // 2], x[..., r // 2:]
    return jnp.concatenate([x1 * cos - x2 * sin, x1 * sin + x2 * cos], axis=-1)


def _forward_loss(carriers, small, x, wfull, late, slabs, pos, target):
    s = x.shape[0]
    wfull = dict(wfull)

    def out_proj(y, w_out, tag):
        return mm(y, wfull[w_out], carriers[w_out], slabs[w_out], name=tag + "_out")

    def sb_layer(x, ln, w_in, w_out, tag, gather=None):
        h = rmsnorm(x, small[ln], tag + "_ln")
        qkv, gate = in_proj(h, wfull[w_in], (carriers[w_in],), True, tag)
        y, gathered = sb_core(qkv, gate, tag, gather)
        return x + out_proj(y, w_out, tag), gathered

    names, shards, kinds, shapes = late
    x, gathered = sb_layer(x, "ln0", "w_in0", "w_out0", "l0", (shards, kinds))
    for n, kind, full in zip(names, kinds, gathered):
        wfull[n] = full.transpose(1, 0, 2).reshape(shapes[n][0], -1) if kind == "stack" else full

    h = rmsnorm(x, small["ln1"], "l1_ln")
    proj = mm(h, wfull["w_in1"], carriers["w_in1"], slabs["w_in1"], name="l1_in")
    i1, i2, i3 = MLA_Q_RANK, MLA_Q_RANK + MLA_KV_RANK, MLA_Q_RANK + MLA_KV_RANK + MLA_ROPE
    q = mm(rmsnorm(proj[:, :i1], small["q_norm1"], "l1_qn"), wfull["w_qb1"], carriers["w_qb1"], slabs["w_qb1"],
           name="l1_qb")
    q = q.reshape(s, N_HEADS, MLA_NOPE + MLA_ROPE)
    kv = mm(rmsnorm(proj[:, i1:i2], small["kv_norm1"], "l1_kvn"), wfull["w_kvb1"], carriers["w_kvb1"],
            slabs["w_kvb1"], name="l1_kvb")
    kv = kv.reshape(s, N_HEADS, MLA_NOPE + HEAD_DIM)
    k_rope = _rope(proj[:, i2:i3][:, None, :], pos)
    pad = jnp.zeros((s, N_HEADS, MLA_QK_PAD - MLA_NOPE - MLA_ROPE), F32)
    qc = jnp.concatenate([q[..., :MLA_NOPE], _rope(q[..., MLA_NOPE:], pos), pad], axis=-1)
    kc = jnp.concatenate([kv[..., :MLA_NOPE], jnp.broadcast_to(k_rope, (s, N_HEADS, MLA_ROPE)), pad], axis=-1)
    y = mla_core(qc.reshape(s, -1), kc.reshape(s, -1), kv[..., MLA_NOPE:].reshape(s, -1), proj[:, i3:], "l1")
    x = x + out_proj(y, "w_out1", "l1")

    h = rmsnorm(x, small["ln2"], "l2_ln")
    qkv, gate = in_proj(h, wfull["w_in2"], (carriers["w_in2_qkv"], carriers["w_in2_gate"]), False, "l2")
    f_logit = mm(h, wfull["w_in2"][:, N_MAIN:], carriers["w_in2_f"], None, name="l2_f") + small["b_f2"]
    c = jnp.cumsum(jax.nn.log_sigmoid(f_logit), axis=0)
    x = x + out_proj(fox_core(qkv, gate, c, "l2"), "w_out2", "l2")

    x, _ = sb_layer(x, "ln3", "w_in3", "w_out3", "l3")
    return sq_loss(rmsnorm(x, small["final_norm"], "final_ln"), target)


def kernel(x, positions, ln0, w_in0, w_out0, ln1, w_in1, q_norm1, w_qb1, kv_norm1, w_kvb1, w_out1, ln2, w_in2, b_f2, w_out2, ln3, w_in3, w_out3, final_norm, loss_target, m_ln0, m_w_in0, m_w_out0, m_ln1, m_w_in1, m_q_norm1, m_w_qb1, m_kv_norm1, m_w_kvb1, m_w_out1, m_ln2, m_w_in2, m_b_f2, m_w_out2, m_ln3, m_w_in3, m_w_out3, m_final_norm, v_ln0, v_w_in0, v_w_out0, v_ln1, v_w_in1, v_q_norm1, v_w_qb1, v_kv_norm1, v_w_kvb1, v_w_out1, v_ln2, v_w_in2, v_b_f2, v_w_out2, v_ln3, v_w_in3, v_w_out3, v_final_norm):
    args = dict(locals())
    w = {n: args[n] for n in ALL_W}
    m = {n: args["m_" + n] for n in ALL_W}
    v = {n: args["v_" + n] for n in ALL_W}
    shapes = {n: w[n].shape for n in BIG}
    kinds = [_gather_kind(n, shapes[n]) for n in BIG]
    slabs = {n: _slab_of(n, shapes[n]) for n in BIG}

    first = BIG[:2]
    late_names = BIG[2:]
    gathered = _all_gather([w[n].astype(BF16) for n in first], kinds[:2], "gather_w")
    wfull = dict(zip(first, gathered))
    late = (late_names, [w[n].astype(BF16) for n in late_names], kinds[2:], shapes)

    d_model = shapes["w_in0"][0]
    carriers = {}
    for n in BIG:
        if n == "w_in2":
            continue
        if slabs[n] is None:
            carriers[n] = jnp.zeros((shapes[n][0], N_DEV * shapes[n][1]), F32)
        else:
            carriers[n] = jnp.zeros((2, 4) + shapes[n], F32)
    carriers["w_in2_qkv"] = jnp.zeros((d_model, N_QKV), F32)
    carriers["w_in2_gate"] = jnp.zeros((d_model, D_INNER), F32)
    carriers["w_in2_f"] = jnp.zeros((d_model, N_DEV * shapes["w_in2"][1] - N_MAIN), F32)
    small = {n: w[n] for n in SMALL}

    def local_loss(carriers, small, x_seq):
        return _forward_loss(carriers, small, x_seq, wfull, late, slabs, positions[0], loss_target[0])

    loss_local, (g_car, g_small, g_x) = jax.value_and_grad(local_loss, argnums=(0, 1, 2))(carriers, small, x[0])
    loss = lax.psum(loss_local, ("x", "y", "c"))

    g_slab = {}
    for n in BIG:
        if n == "w_in2":
            g = jnp.concatenate([g_car["w_in2_qkv"], g_car["w_in2_gate"], g_car["w_in2_f"]], axis=1)
            g_slab[n] = _to_slabs(g, shapes[n])
        elif slabs[n] is None:
            g_slab[n] = _to_slabs(g_car[n], shapes[n])
        else:
            g_slab[n] = g_car[n]
    core_idx = lax.axis_index("c").astype(jnp.int32).reshape(1)
    from_sibling = _pair_exchange([g_slab[n] for n in BIG], "pair_exchange")
    chip_part = [_pair_sum(core_idx, g_slab[n], r, "pair_sum_" + n) for n, r in zip(BIG, from_sibling)]
    by_chip = _chip_exchange(chip_part, "chip_exchange")
    big = [{}, {}, {}, {}]
    for n, parts in zip(BIG, by_chip):
        for k, t in enumerate(_adamw(w[n], parts, m[n], v[n], "adamw_" + n)):
            big[k][n] = t

    g_small_sum = _all_reduce_small(_pack_small(g_small), "reduce_small")
    sm = _adamw(_pack_small(w), g_small_sum[None], _pack_small(m), _pack_small(v), "adamw_small")
    small_out = [_unpack_small(t) for t in sm]

    outs = [loss, g_x[None]]
    for k in range(4):
        outs += [small_out[k][n] if n in small_out[k] else big[k][n] for n in ALL_W]
    return tuple(outs)
```

```python
import jax
import jax.numpy as jnp
from jax import lax
from jax.experimental import pallas as pl
from jax.experimental.pallas import tpu as pltpu

F32 = jnp.float32
BF16 = jnp.bfloat16
MESH = pl.DeviceIdType.MESH

N_DEV = 8
N_HEADS = 16
HEAD_DIM = 128
D_INNER = N_HEADS * HEAD_DIM
MLA_Q_RANK = 256
MLA_KV_RANK = 128
MLA_NOPE = 128
MLA_ROPE = 64
MLA_QK_PAD = 256
MLA_CHUNK = 64
ROPE_BASE = 10000.0
EPS = 1e-6
NEG = -1e30
SB_CUT = 104.0

ADAM_LR = 0.001
ADAM_B1 = 0.9
ADAM_B2 = 0.999
ADAM_EPS = 1e-08
ADAM_WD = 0.01
ADAM_STEP = 10

PACK_COLS = 1024
VMEM_LIMIT = 56 * 1024 * 1024

BIG = ["w_in0", "w_out0", "w_in1", "w_qb1", "w_kvb1", "w_out1", "w_in2", "w_out2", "w_in3", "w_out3"]
COL_SHARDED = {"w_in0", "w_in1", "w_qb1", "w_kvb1", "w_in2", "w_in3"}
SMALL = ["ln0", "ln1", "ln2", "ln3", "final_norm", "q_norm1", "kv_norm1", "b_f2"]
ALL_W = ["ln0", "w_in0", "w_out0", "ln1", "w_in1", "q_norm1", "w_qb1", "kv_norm1", "w_kvb1", "w_out1",
         "ln2", "w_in2", "b_f2", "w_out2", "ln3", "w_in3", "w_out3", "final_norm"]


def _cparams(sem=None):
    return pltpu.CompilerParams(dimension_semantics=sem, vmem_limit_bytes=VMEM_LIMIT)


def _tile(dim, cap, align):
    if dim <= cap:
        return dim
    t = (cap // align) * align
    while t >= align:
        if dim % t == 0:
            return t
        t -= align
    return dim


def _dot(a, b, dims):
    return lax.dot_general(a, b, (dims, ((), ())), preferred_element_type=F32)


def _dot_nn(a, b):
    return _dot(a, b, ((1,), (0,)))


def _dot_nt(a, b):
    return _dot(a, b, ((1,), (1,)))


def _dot_tn(a, b):
    return _dot(a, b, ((0,), (0,)))


def _transpose_bf16(x):
    return x.astype(F32).T.astype(BF16)


def _matmul(a, b, mode, col0=0, n_cols=None, out_dtype=F32, name="mm"):
    if mode == "nn":
        m, r = a.shape
        n = n_cols or b.shape[1]
        tn, tr = _tile(n, 1024, 128), _tile(r, 1024, 128)
        tm = _tile(m, 1024 if tn <= 1024 else 512, 8)
        c0 = col0 // tn
        a_spec = pl.BlockSpec((tm, tr), lambda i, j, k: (i, k))
        b_spec = pl.BlockSpec((tr, tn), lambda i, j, k: (k, j + c0))
        dims = ((1,), (0,))
        assert col0 % tn == 0
    elif mode == "nt":
        m, r = a.shape
        n = b.shape[0]
        tn, tr = _tile(n, 1024, 128), _tile(r, 1024, 128)
        tm = _tile(m, 1024 if tr <= 1024 else 512, 8)
        c0 = col0 // tr
        a_spec = pl.BlockSpec((tm, tr), lambda i, j, k: (i, k))
        b_spec = pl.BlockSpec((tn, tr), lambda i, j, k: (j, k + c0))
        dims = ((1,), (1,))
        assert col0 % tr == 0
    else:
        r, m = a.shape
        n = b.shape[1]
        tm, tn, tr = _tile(m, 1024, 128), _tile(n, 1024, 128), _tile(r, 512, 16)
        a_spec = pl.BlockSpec((tr, tm), lambda i, j, k: (k, i))
        b_spec = pl.BlockSpec((tr, tn), lambda i, j, k: (k, j))
        dims = ((0,), (0,))
    nr = r // tr

    def body(a_ref, b_ref, o_ref, acc_ref):
        k = pl.program_id(2)

        @pl.when(k == 0)
        def _():
            acc_ref[...] = jnp.zeros_like(acc_ref)

        acc_ref[...] += _dot(a_ref[...].astype(BF16), b_ref[...].astype(BF16), dims)

        @pl.when(k == nr - 1)
        def _():
            o_ref[...] = acc_ref[...].astype(out_dtype)

    return pl.pallas_call(
        body,
        name=name,
        grid=(m // tm, n // tn, nr),
        in_specs=[a_spec, b_spec],
        out_specs=pl.BlockSpec((tm, tn), lambda i, j, k: (i, j)),
        out_shape=jax.ShapeDtypeStruct((m, n), out_dtype),
        scratch_shapes=[pltpu.VMEM((tm, tn), F32)],
        compiler_params=_cparams(("parallel", "parallel", "arbitrary")),
    )(a, b)


def _transpose_cast(a, name):
    r, m = a.shape
    tr = _tile(r, 512, 128)

    def body(a_ref, o_ref):
        o_ref[...] = a_ref[...].astype(F32).T.astype(BF16)

    return pl.pallas_call(
        body, name=name, grid=(r // tr,),
        in_specs=[pl.BlockSpec((tr, m), lambda i: (i, 0))],
        out_specs=pl.BlockSpec((m, tr), lambda i: (0, i)),
        out_shape=jax.ShapeDtypeStruct((m, r), BF16),
        compiler_params=_cparams(("parallel",)),
    )(a)


def _matmul_dw(a, b1, b2, slab, name):
    m, r = a.shape
    n1 = b1.shape[1]
    n = n1 + (b2.shape[1] if b2 is not None else 0)
    tr = _tile(r, 1024, 128)
    if slab is None:
        tm, tn = _tile(m, 1024, 128), _tile(n1, 1024, 128)
        out_spec = pl.BlockSpec((tm, tn), lambda i, j, k: (i, j))
        out_shape = (m, n)
    elif slab[0] == "col":
        tm, tn = _tile(m, 1024, 128), slab[1]
        out_spec = pl.BlockSpec((None, None, tm, tn), lambda i, j, k: (j % 2, j // 2, i, 0))
        out_shape = (2, 4, m, tn)
        assert n == N_DEV * tn
    else:
        tm, tn = slab[1], _tile(n, 1024, 128)
        out_spec = pl.BlockSpec((None, None, tm, tn), lambda i, j, k: (i % 2, i // 2, 0, j))
        out_shape = (2, 4, tm, n)
        assert m == N_DEV * tm
    assert n1 % tn == 0 and n % tn == 0
    if tn > 1024:
        tr = _tile(r, 256, 128)
    n1b = n1 // tn
    nr = r // tr

    def body(*refs):
        a_ref, b_refs, o_ref, acc_ref = refs[0], refs[1:-2], refs[-2], refs[-1]
        j = pl.program_id(1)
        k = pl.program_id(2)

        @pl.when(k == 0)
        def _():
            acc_ref[...] = jnp.zeros_like(acc_ref)

        at = a_ref[...]
        if b2 is None:
            acc_ref[...] += _dot_nn(at, b_refs[0][...].astype(BF16))
        else:
            @pl.when(j < n1b)
            def _():
                acc_ref[...] += _dot_nn(at, b_refs[0][...].astype(BF16))

            @pl.when(j >= n1b)
            def _():
                acc_ref[...] += _dot_nn(at, b_refs[1][...].astype(BF16))

        @pl.when(k == nr - 1)
        def _():
            o_ref[...] = acc_ref[...]

    in_specs = [pl.BlockSpec((tm, tr), lambda i, j, k: (i, k))]
    args = [a, b1]
    if b2 is None:
        in_specs.append(pl.BlockSpec((tr, tn), lambda i, j, k: (k, j)))
    else:
        in_specs.append(pl.BlockSpec((tr, tn), lambda i, j, k: (jnp.where(j < n1b, k, nr - 1), jnp.minimum(j, n1b - 1))))
        in_specs.append(pl.BlockSpec((tr, tn), lambda i, j, k: (jnp.where(j < n1b, 0, k), jnp.maximum(j - n1b, 0))))
        args.append(b2)
    return pl.pallas_call(
        body, name=name, grid=(m // tm, n // tn, nr),
        in_specs=in_specs, out_specs=out_spec,
        out_shape=jax.ShapeDtypeStruct(out_shape, F32),
        scratch_shapes=[pltpu.VMEM((tm, tn), F32)],
        compiler_params=_cparams(("parallel", "parallel", "arbitrary")),
    )(*args)


def mm(a, w, carrier, slab=None, name="mm"):
    @jax.custom_vjp
    def f(a, w, carrier):
        return _matmul(a, w, "nn", 0, None, F32, name + "_fwd")

    def fwd(a, w, carrier):
        return _matmul(a, w, "nn", 0, None, F32, name + "_fwd"), (a, w)

    def bwd(res, g):
        a, w = res
        da = _matmul(g, w, "nt", 0, None, F32, name + "_dx")
        return da, jnp.zeros_like(w), _matmul_dw(_transpose_cast(a, name + "_t"), g, None, slab, name + "_dw")

    f.defvjp(fwd, bwd)
    return f(a, w, carrier)


def in_proj(h, w, carriers, slab, name):
    def run(h, w):
        return (_matmul(h, w, "nn", 0, N_QKV, BF16, name + "_qkv"),
                _matmul(h, w, "nn", N_QKV, D_INNER, F32, name + "_g"))

    @jax.custom_vjp
    def f(h, w, *cars):
        return run(h, w)

    def fwd(h, w, *cars):
        return run(h, w), (h, w)

    def bwd(res, g):
        h, w = res
        g_qkv, g_gate = g
        dh = (_matmul(g_qkv, w, "nt", 0, None, F32, name + "_qkv_dx")
              + _matmul(g_gate, w, "nt", N_QKV, None, F32, name + "_g_dx"))
        h_t = _transpose_cast(h, name + "_t")
        if slab:
            dws = (_matmul_dw(h_t, g_qkv, g_gate, ("col", PACK_COLS), name + "_dw"),)
        else:
            dws = (_matmul_dw(h_t, g_qkv, None, None, name + "_qkv_dw"),
                   _matmul_dw(h_t, g_gate, None, None, name + "_g_dw"))
        return (dh, jnp.zeros_like(w)) + dws

    f.defvjp(fwd, bwd)
    return f(h, w, *carriers)


def _rms_fwd(x, g, name):
    s, d = x.shape
    tm = _tile(s, 512, 8)

    def body(x_ref, g_ref, y_ref):
        x = x_ref[...]
        r = lax.rsqrt(jnp.mean(x * x, axis=-1, keepdims=True) + EPS)
        y_ref[...] = x * r * g_ref[...]

    return pl.pallas_call(
        body, name=name, grid=(s // tm,),
        in_specs=[pl.BlockSpec((tm, d), lambda i: (i, 0)), pl.BlockSpec((1, d), lambda i: (0, 0))],
        out_specs=pl.BlockSpec((tm, d), lambda i: (i, 0)),
        out_shape=jax.ShapeDtypeStruct((s, d), F32),
        compiler_params=_cparams(("parallel",)),
    )(x, g)


def _rms_bwd(x, g, dy, name):
    s, d = x.shape
    tm = _tile(s, 512, 8)

    def body(x_ref, g_ref, dy_ref, dx_ref, dg_ref):
        @pl.when(pl.program_id(0) == 0)
        def _():
            dg_ref[...] = jnp.zeros_like(dg_ref)

        x = x_ref[...]
        dy = dy_ref[...]
        r = lax.rsqrt(jnp.mean(x * x, axis=-1, keepdims=True) + EPS)
        xh = x * r
        dg_ref[...] += jnp.sum(dy * xh, axis=0, keepdims=True)
        dxh = dy * g_ref[...]
        dx_ref[...] = r * (dxh - xh * jnp.mean(dxh * xh, axis=-1, keepdims=True))

    return pl.pallas_call(
        body, name=name, grid=(s // tm,),
        in_specs=[pl.BlockSpec((tm, d), lambda i: (i, 0)), pl.BlockSpec((1, d), lambda i: (0, 0)),
                  pl.BlockSpec((tm, d), lambda i: (i, 0))],
        out_specs=[pl.BlockSpec((tm, d), lambda i: (i, 0)), pl.BlockSpec((1, d), lambda i: (0, 0))],
        out_shape=[jax.ShapeDtypeStruct((s, d), F32), jax.ShapeDtypeStruct((1, d), F32)],
        compiler_params=_cparams(("arbitrary",)),
    )(x, g, dy)


def rmsnorm(x, g, name="rms"):
    @jax.custom_vjp
    def f(x, g):
        return _rms_fwd(x, g.reshape(1, -1), name + "_fwd")

    def fwd(x, g):
        return _rms_fwd(x, g.reshape(1, -1), name + "_fwd"), (x, g)

    def bwd(res, dy):
        x, g = res
        dx, dg = _rms_bwd(x, g.reshape(1, -1), dy, name + "_bwd")
        return dx, dg.reshape(-1)

    f.defvjp(fwd, bwd)
    return f(x, g)


def _gate_fwd(o, gate, name):
    s = o.shape[0]
    tm = 256

    def body(o_ref, g_ref, y_ref):
        g = g_ref[...]
        y_ref[...] = o_ref[...] * (g / (1.0 + jnp.exp(-g)))

    row = pl.BlockSpec((tm, D_INNER), lambda i: (i, 0))
    return pl.pallas_call(
        body, name=name, grid=(s // tm,),
        in_specs=[row, row], out_specs=row,
        out_shape=jax.ShapeDtypeStruct((s, D_INNER), F32),
        compiler_params=_cparams(("parallel",)),
    )(o, gate)


def _gate_bwd(dy, o, gate, name):
    s = o.shape[0]
    tm = 256

    def body(dy_ref, o_ref, g_ref, do_ref, dg_ref, dl_ref):
        g = g_ref[...]
        o = o_ref[...]
        dy = dy_ref[...]
        sg = 1.0 / (1.0 + jnp.exp(-g))
        do = dy * (g * sg)
        do_ref[...] = do.astype(BF16)
        dg_ref[...] = dy * o * (sg * (1.0 + g * (1.0 - sg)))
        prod = do * o
        for h in range(N_HEADS):
            dl_ref[h] = jnp.sum(prod[:, h * HEAD_DIM:(h + 1) * HEAD_DIM], axis=1, keepdims=True)

    row = pl.BlockSpec((tm, D_INNER), lambda i: (i, 0))
    return pl.pallas_call(
        body, name=name, grid=(s // tm,),
        in_specs=[row, row, row],
        out_specs=[row, row, pl.BlockSpec((N_HEADS, tm, 1), lambda i: (0, i, 0))],
        out_shape=[jax.ShapeDtypeStruct((s, D_INNER), BF16), jax.ShapeDtypeStruct((s, D_INNER), F32),
                   jax.ShapeDtypeStruct((N_HEADS, s, 1), F32)],
        compiler_params=_cparams(("parallel",)),
    )(dy, o, gate)


SB_BLK = 256


def _softplus(z):
    return jnp.maximum(z, 0.0) + jnp.log(1.0 + jnp.exp(-jnp.abs(z)))


def _tri_sum(x, tri):
    hi = x.astype(BF16)
    lo = (x - hi.astype(F32)).astype(BF16)
    return _dot_nn(hi, tri) + _dot_nn(lo, tri)


SB_WIN = 2 * SB_BLK


def _sb_tri(kind):
    row = lax.broadcasted_iota(jnp.int32, (SB_BLK, SB_BLK), 0)
    col = lax.broadcasted_iota(jnp.int32, (SB_BLK, SB_BLK), 1)
    return ((row >= col) if kind == "suffix" else (row <= col)).astype(BF16)


def _sb_bounds(i, t):
    hi = (i + 1) * SB_BLK - t * SB_WIN
    return hi, pl.multiple_of(jnp.maximum(hi - SB_WIN, 0), SB_BLK)


def _sb_mask(i, hi, start):
    row = lax.broadcasted_iota(jnp.int32, (SB_BLK, SB_WIN), 0) + i * SB_BLK
    col = lax.broadcasted_iota(jnp.int32, (SB_BLK, SB_WIN), 1) + start
    return jnp.logical_and(col < row, col < hi)


def _sb_window(q, kwin, mask, a_run, tri_suffix, scale):
    b = SB_BLK
    z = _dot_nt(q, kwin) * scale
    sp = _softplus(z)
    ls = jnp.where(mask, -sp, 0.0)
    ls_l, ls_r = ls[:, :b], ls[:, b:]
    suffix = jnp.concatenate([_tri_sum(ls_l, tri_suffix) + jnp.sum(ls_r, axis=1, keepdims=True),
                              _tri_sum(ls_r, tri_suffix)], axis=1)
    w = jnp.where(mask, jnp.exp(z + suffix + a_run), 0.0)
    return z, sp, ls, w


def _sb_fwd(qkv, name, gather=None):
    s = qkv.shape[0]
    b = SB_BLK
    nq = s // b
    scale = HEAD_DIM ** -0.5
    assert s >= SB_WIN
    shards, kinds = gather if gather else ((), ())
    nm = len(shards)

    def body(*refs):
        q_ref, k_ref, v_ref = refs[:3]
        o_ref = refs[3 + nm]
        i = pl.program_id(1)
        if nm:
            start, finish = _gather_steps(refs[3:3 + nm], refs[4 + nm:4 + 2 * nm], kinds, *refs[4 + 2 * nm:])
            first_step = jnp.logical_and(pl.program_id(0) == 0, i == 0)
            last_step = jnp.logical_and(pl.program_id(0) == N_HEADS - 1, i == nq - 1)
            pl.when(first_step)(start)
        q = q_ref[...]
        tri_suffix = _sb_tri("suffix")

        def cond(c):
            t, a_run, _ = c
            return jnp.logical_and((i + 1) * b - t * SB_WIN > 0, jnp.max(a_run) > -SB_CUT)

        def step(c):
            t, a_run, acc = c
            hi, start = _sb_bounds(i, t)
            _, _, ls, w = _sb_window(q, k_ref[pl.ds(start, SB_WIN), :], _sb_mask(i, hi, start), a_run,
                                     tri_suffix, scale)
            acc = acc + _dot_nn(w.astype(BF16), v_ref[pl.ds(start, SB_WIN), :])
            return t + 1, a_run + jnp.sum(ls, axis=1, keepdims=True), acc

        _, _, acc = lax.while_loop(cond, step, (0, jnp.zeros((b, 1), F32), jnp.zeros((b, HEAD_DIM), F32)))
        o_ref[...] = acc
        if nm:
            pl.when(last_step)(finish)

    out = pl.pallas_call(
        body, name=name, grid=(N_HEADS, nq),
        in_specs=[pl.BlockSpec((b, HEAD_DIM), lambda h, i: (i, h)),
                  pl.BlockSpec((s, HEAD_DIM), lambda h, i: (0, N_HEADS + h)),
                  pl.BlockSpec((s, HEAD_DIM), lambda h, i: (0, 2 * N_HEADS + h))] + [ANY] * nm,
        out_specs=[pl.BlockSpec((b, HEAD_DIM), lambda h, i: (i, h))] + [ANY] * nm,
        out_shape=[jax.ShapeDtypeStruct((s, D_INNER), F32)] + _gather_out_shapes(shards, kinds),
        scratch_shapes=_gather_scratch(nm) if nm else [],
        compiler_params=_cparams(("arbitrary", "arbitrary")),
    )(qkv, qkv, qkv, *shards)
    return out[0], list(out[1:])


def _sb_bwd(qkv, do, name, exchange=()):
    s = qkv.shape[0]
    b = SB_BLK
    nq = s // b
    n_win = -(-s // SB_WIN) + 1
    scale = HEAD_DIM ** -0.5
    assert s >= SB_WIN
    nm = len(exchange)

    def body(*refs):
        q_ref, k_ref, v_ref, do_ref = refs[:4]
        dq_ref, dk_ref, dv_ref = refs[4 + nm:7 + nm]
        dkt_s, dvt_s, g_buf, sig_buf = refs[7 + 2 * nm:11 + 2 * nm]
        i = pl.program_id(1)
        if nm:
            start_x, finish_x = _exchange_steps(refs[4:4 + nm], refs[7 + nm:7 + 2 * nm], *refs[11 + 2 * nm:])
            pl.when(jnp.logical_and(pl.program_id(0) == 0, i == 0))(start_x)

        @pl.when(i == 0)
        def _():
            dkt_s[...] = jnp.zeros_like(dkt_s)
            dvt_s[...] = jnp.zeros_like(dvt_s)

        q = q_ref[...]
        dob = do_ref[...]
        q_t = _transpose_bf16(q)
        do_t = _transpose_bf16(dob)
        tri_suffix = _sb_tri("suffix")
        tri_prefix = _sb_tri("prefix")

        def add_halves(acc_ref, start, upd):
            blk = start // b
            acc_ref[blk] += upd[:, :b]
            acc_ref[blk + 1] += upd[:, b:]

        def cond(c):
            t, a_run = c
            return jnp.logical_and((i + 1) * b - t * SB_WIN > 0, jnp.max(a_run) > -SB_CUT)

        def sweep(c):
            t, a_run = c
            hi, start = _sb_bounds(i, t)
            z, sp, ls, w = _sb_window(q, k_ref[pl.ds(start, SB_WIN), :], _sb_mask(i, hi, start), a_run,
                                      tri_suffix, scale)
            g_buf[t] = w * _dot_nt(dob, v_ref[pl.ds(start, SB_WIN), :])
            sig_buf[t] = jnp.exp(z - sp)
            add_halves(dvt_s, start, _dot_nn(do_t, w.astype(BF16)))
            return t + 1, a_run + jnp.sum(ls, axis=1, keepdims=True)

        n_steps, _ = lax.while_loop(cond, sweep, (0, jnp.zeros((b, 1), F32)))

        def back(u, c):
            g_run, dq = c
            t = n_steps - 1 - u
            hi, start = _sb_bounds(i, t)
            g = g_buf[t]
            g_l, g_r = g[:, :b], g[:, b:]
            g_incl = g_run + jnp.concatenate(
                [_tri_sum(g_l, tri_prefix),
                 _tri_sum(g_r, tri_prefix) + jnp.sum(g_l, axis=1, keepdims=True)], axis=1)
            dz = jnp.where(_sb_mask(i, hi, start), (g - sig_buf[t] * g_incl) * scale, 0.0).astype(BF16)
            add_halves(dkt_s, start, _dot_nn(q_t, dz))
            return g_run + jnp.sum(g, axis=1, keepdims=True), dq + _dot_nn(dz, k_ref[pl.ds(start, SB_WIN), :])

        _, dq = lax.fori_loop(0, n_steps, back, (jnp.zeros((b, 1), F32), jnp.zeros((b, HEAD_DIM), F32)))
        dq_ref[...] = dq.astype(BF16)

        @pl.when(i == nq - 1)
        def _():
            for jb in range(nq):
                dk_ref[jb * b:(jb + 1) * b, :] = dkt_s[jb].T.astype(BF16)
                dv_ref[jb * b:(jb + 1) * b, :] = dvt_s[jb].T.astype(BF16)

        if nm:
            pl.when(jnp.logical_and(pl.program_id(0) == N_HEADS - 1, i == nq - 1))(finish_x)

    blk = pl.BlockSpec((b, HEAD_DIM), lambda h, i: (i, h))
    head = pl.BlockSpec((s, HEAD_DIM), lambda h, i: (0, h))
    out = pl.pallas_call(
        body, name=name, grid=(N_HEADS, nq),
        in_specs=[blk,
                  pl.BlockSpec((s, HEAD_DIM), lambda h, i: (0, N_HEADS + h)),
                  pl.BlockSpec((s, HEAD_DIM), lambda h, i: (0, 2 * N_HEADS + h)),
                  blk] + [ANY] * nm,
        out_specs=[blk, head, head] + [ANY] * nm,
        out_shape=[jax.ShapeDtypeStruct((s, D_INNER), BF16)] * 3
        + [jax.ShapeDtypeStruct(p.shape, p.dtype) for p in exchange],
        scratch_shapes=[pltpu.VMEM((nq, HEAD_DIM, b), F32), pltpu.VMEM((nq, HEAD_DIM, b), F32),
                        pltpu.VMEM((n_win, b, SB_WIN), F32), pltpu.VMEM((n_win, b, SB_WIN), F32)]
        + (_exchange_scratch(nm) if nm else []),
        compiler_params=_cparams(("arbitrary", "arbitrary")),
    )(qkv, qkv, qkv, do, *exchange)
    return out[0], out[1], out[2], list(out[3:])


def sb_core(qkv, gate, name, gather=None, late=None):
    shards, kinds = gather if gather else ((), ())
    slots, carrier_shapes, to_slabs = late if late else ((), (), None)
    n_sh = len(shards)

    def run(qkv, gate, *shards):
        o, gathered = _sb_fwd(qkv, name + "_fwd", (shards, kinds) if shards else None)
        carriers = [jnp.zeros(sh, F32) for sh in carrier_shapes]
        return (_gate_fwd(o, gate, name + "_gate"), *gathered, *carriers), o

    @jax.custom_vjp
    def f(qkv, gate, *extra):
        return run(qkv, gate, *extra[:n_sh])[0]

    def fwd(qkv, gate, *extra):
        outs, o = run(qkv, gate, *extra[:n_sh])
        return outs, (qkv, gate, o)

    def bwd(res, cts):
        qkv, gate, o = res
        do, dgate, _ = _gate_bwd(cts[0], o, gate, name + "_gate_bwd")
        parts = []
        if slots:
            g_slabs = to_slabs(cts[1 + n_sh:])
            core_idx = lax.axis_index("c").astype(jnp.int32).reshape(1)
            from_sibling = _pair_exchange(g_slabs, name + "_pair_exchange")
            parts = [_pair_sum(core_idx, g, r, name + "_pair_sum%d" % k)
                     for k, (g, r) in enumerate(zip(g_slabs, from_sibling))]
        dq, dk, dv, by_chip = _sb_bwd(qkv, do, name + "_bwd", parts)
        zeros = tuple(jnp.zeros(sh.shape, sh.dtype) for sh in shards)
        return (jnp.concatenate([dq, dk, dv], axis=1), dgate) + zeros + tuple(by_chip)

    f.defvjp(fwd, bwd)
    outs = f(qkv, gate, *shards, *slots)
    return outs[0], list(outs[1:1 + n_sh]), list(outs[1 + n_sh:])


SM_FWD_BLK = 256
SM_BLK = 512


SM_FWD_KEYS = 1024
SM_BWD_KEYS = 512


def _sm_mask(i, jw, rows, keys, chunk_shift):
    row = lax.broadcasted_iota(jnp.int32, (rows, keys), 0) + i * rows
    col = lax.broadcasted_iota(jnp.int32, (rows, keys), 1) + jw * keys
    return (col >> chunk_shift) <= (row >> chunk_shift)


def _sm_fwd(qa, ka, va, ccol, crow, dqk, qo, ko, vo, chunk_shift, scale, name):
    s = qa.shape[0]
    b = min(SM_FWD_BLK, s)
    keys = min(SM_FWD_KEYS, s)
    per = keys // b
    nq = s // b
    has_bias = ccol is not None

    def body(*refs):
        if has_bias:
            q_ref, k_ref, v_ref, cc_ref, cr_ref, o_ref, lse_ref, m_s, l_s, acc_s = refs
        else:
            q_ref, k_ref, v_ref, o_ref, lse_ref, m_s, l_s, acc_s = refs
        i = pl.program_id(1)
        q = q_ref[...]
        m_s[...] = jnp.full_like(m_s, NEG)
        l_s[...] = jnp.zeros_like(l_s)
        acc_s[...] = jnp.zeros_like(acc_s)

        def sweep(jw, masked):
            off = pl.multiple_of(jw * keys, keys)
            z = _dot_nt(q, k_ref[pl.ds(off, keys), :]) * scale
            if has_bias:
                z = z + cc_ref[...] - cr_ref[jw]
            if masked:
                z = jnp.where(_sm_mask(i, jw, b, keys, chunk_shift), z, NEG)
            m_old = m_s[...]
            m_new = jnp.maximum(m_old, jnp.max(z, axis=1, keepdims=True))
            alpha = jnp.exp(m_old - m_new)
            p = jnp.exp(z - m_new)
            l_s[...] = alpha * l_s[...] + jnp.sum(p, axis=1, keepdims=True)
            acc_s[...] = alpha * acc_s[...] + _dot_nn(p.astype(BF16), v_ref[pl.ds(off, keys), :])
            m_s[...] = m_new

        def full(jw, carry):
            sweep(jw, False)
            return carry

        lax.fori_loop(0, i // per, full, 0)
        sweep(i // per, True)
        o_ref[...] = acc_s[...] / l_s[...]
        lse_ref[...] = m_s[...] + jnp.log(l_s[...])

    in_specs = [pl.BlockSpec((b, dqk), lambda h, i: (i, qo + h)),
                pl.BlockSpec((s, dqk), lambda h, i: (0, ko + h)),
                pl.BlockSpec((s, HEAD_DIM), lambda h, i: (0, vo + h))]
    args = [qa, ka, va]
    if has_bias:
        in_specs += [pl.BlockSpec((None, b, 1), lambda h, i: (h, i, 0)),
                     pl.BlockSpec((None, s // keys, 1, keys), lambda h, i: (h, 0, 0, 0))]
        args += [ccol, crow]
    return pl.pallas_call(
        body, name=name, grid=(N_HEADS, nq),
        in_specs=in_specs,
        out_specs=[pl.BlockSpec((b, HEAD_DIM), lambda h, i: (i, h)),
                   pl.BlockSpec((None, b, 1), lambda h, i: (h, i, 0))],
        out_shape=[jax.ShapeDtypeStruct((s, D_INNER), F32), jax.ShapeDtypeStruct((N_HEADS, s, 1), F32)],
        scratch_shapes=[pltpu.VMEM((b, 1), F32), pltpu.VMEM((b, 1), F32), pltpu.VMEM((b, HEAD_DIM), F32)],
        compiler_params=_cparams(("parallel", "arbitrary")),
    )(*args)


def _sm_bwd(qa, ka, va, do, lse, delta, ccol, crow, dqk, qo, ko, vo, chunk_shift, scale, grad_dtype, name):
    s = qa.shape[0]
    b = SM_BLK
    keys = min(SM_BWD_KEYS, s)
    per = keys // b
    nq = s // b
    nk = s // keys
    has_bias = ccol is not None

    def body(*refs):
        if has_bias:
            (q_ref, k_ref, v_ref, do_ref, lse_ref, dl_ref, cc_ref, cr_ref,
             dq_ref, dk_ref, dv_ref, dc_ref, dr_ref, dq_s, dkt_s, dvt_s, dc_s, dr_s) = refs
        else:
            (q_ref, k_ref, v_ref, do_ref, lse_ref, dl_ref,
             dq_ref, dk_ref, dv_ref, dq_s, dkt_s, dvt_s) = refs
        i = pl.program_id(1)

        @pl.when(i == 0)
        def _():
            dkt_s[...] = jnp.zeros_like(dkt_s)
            dvt_s[...] = jnp.zeros_like(dvt_s)
            if has_bias:
                dc_s[...] = jnp.zeros_like(dc_s)

        q = q_ref[...]
        dob = do_ref[...]
        q_t = _transpose_bf16(q)
        do_t = _transpose_bf16(dob)
        lse = lse_ref[...]
        delta = dl_ref[...]
        dq_s[...] = jnp.zeros_like(dq_s)
        if has_bias:
            dr_s[...] = jnp.zeros_like(dr_s)

        def sweep(jw, masked):
            off = pl.multiple_of(jw * keys, keys)
            kb = k_ref[pl.ds(off, keys), :]
            z = _dot_nt(q, kb) * scale
            if has_bias:
                z = z + cc_ref[...] - cr_ref[jw]
            p = jnp.exp(z - lse)
            if masked:
                p = jnp.where(_sm_mask(i, jw, b, keys, chunk_shift), p, 0.0)
            dvt_s[jw] += _dot_nn(do_t, p.astype(BF16))
            dz = p * (_dot_nt(dob, v_ref[pl.ds(off, keys), :]) - delta)
            if has_bias:
                dc_s[jw] += jnp.sum(dz, axis=0, keepdims=True)
                dr_s[...] += jnp.sum(dz, axis=1, keepdims=True)
            dzs = (dz * scale).astype(BF16)
            dkt_s[jw] += _dot_nn(q_t, dzs)
            dq_s[...] += _dot_nn(dzs, kb)

        def full(jw, carry):
            sweep(jw, False)
            return carry

        lax.fori_loop(0, i // per, full, 0)
        sweep(i // per, True)
        dq_ref[...] = dq_s[...].astype(grad_dtype)
        if has_bias:
            dr_ref[...] = dr_s[...]

        @pl.when(i == nq - 1)
        def _():
            for jw in range(nk):
                dk_ref[jw * keys:(jw + 1) * keys, :] = dkt_s[jw].T.astype(grad_dtype)
                dv_ref[jw * keys:(jw + 1) * keys, :] = dvt_s[jw].T.astype(grad_dtype)
            if has_bias:
                dc_ref[...] = dc_s[...]

    vec = pl.BlockSpec((None, b, 1), lambda h, i: (h, i, 0))
    in_specs = [pl.BlockSpec((b, dqk), lambda h, i: (i, qo + h)),
                pl.BlockSpec((s, dqk), lambda h, i: (0, ko + h)),
                pl.BlockSpec((s, HEAD_DIM), lambda h, i: (0, vo + h)),
                pl.BlockSpec((b, HEAD_DIM), lambda h, i: (i, h)),
                vec, vec]
    args = [qa, ka, va, do, lse, delta]
    out_specs = [pl.BlockSpec((b, dqk), lambda h, i: (i, h)),
                 pl.BlockSpec((s, dqk), lambda h, i: (0, h)),
                 pl.BlockSpec((s, HEAD_DIM), lambda h, i: (0, h))]
    out_shape = [jax.ShapeDtypeStruct((s, N_HEADS * dqk), grad_dtype),
                 jax.ShapeDtypeStruct((s, N_HEADS * dqk), grad_dtype),
                 jax.ShapeDtypeStruct((s, D_INNER), grad_dtype)]
    scratch = [pltpu.VMEM((b, dqk), F32), pltpu.VMEM((nk, dqk, keys), F32), pltpu.VMEM((nk, HEAD_DIM, keys), F32)]
    if has_bias:
        key_vec = pl.BlockSpec((None, nk, 1, keys), lambda h, i: (h, 0, 0, 0))
        in_specs += [vec, key_vec]
        args += [ccol, crow]
        out_specs += [key_vec, vec]
        out_shape += [jax.ShapeDtypeStruct((N_HEADS, nk, 1, keys), F32), jax.ShapeDtypeStruct((N_HEADS, s, 1), F32)]
        scratch += [pltpu.VMEM((nk, 1, keys), F32), pltpu.VMEM((b, 1), F32)]
    return pl.pallas_call(
        body, name=name, grid=(N_HEADS, nq),
        in_specs=in_specs, out_specs=out_specs, out_shape=out_shape, scratch_shapes=scratch,
        compiler_params=_cparams(("arbitrary", "arbitrary")),
    )(*args)


def fox_core(qkv, gate, c, name):
    s = qkv.shape[0]
    scale = HEAD_DIM ** -0.5
    cfg = dict(dqk=HEAD_DIM, qo=0, ko=N_HEADS, vo=2 * N_HEADS, chunk_shift=0, scale=scale)

    def layouts(c, keys):
        ct = c.T
        keys = min(keys, s)
        return ct.reshape(N_HEADS, s, 1), ct.reshape(N_HEADS, s // keys, 1, keys)

    def run(qkv, gate, c):
        ccol, crow = layouts(c, SM_FWD_KEYS)
        o, lse = _sm_fwd(qkv, qkv, qkv, ccol, crow, name=name + "_fwd", **cfg)
        return _gate_fwd(o, gate, name + "_gate"), o, lse

    @jax.custom_vjp
    def f(qkv, gate, c):
        return run(qkv, gate, c)[0]

    def fwd(qkv, gate, c):
        y, o, lse = run(qkv, gate, c)
        return y, (qkv, gate, c, o, lse)

    def bwd(res, dy):
        qkv, gate, c, o, lse = res
        ccol, crow = layouts(c, SM_BWD_KEYS)
        do, dgate, delta = _gate_bwd(dy, o, gate, name + "_gate_bwd")
        dq, dk, dv, colsum, rowsum = _sm_bwd(qkv, qkv, qkv, do, lse, delta, ccol, crow,
                                             grad_dtype=BF16, name=name + "_bwd", **cfg)
        dc = (rowsum.reshape(N_HEADS, s) - colsum.reshape(N_HEADS, s)).T
        return jnp.concatenate([dq, dk, dv], axis=1), dgate, dc

    f.defvjp(fwd, bwd)
    return f(qkv, gate, c)


def mla_core(qc, kc, v, gate, name):
    scale = (MLA_NOPE + MLA_ROPE) ** -0.5
    cfg = dict(dqk=MLA_QK_PAD, qo=0, ko=0, vo=0, chunk_shift=MLA_CHUNK.bit_length() - 1, scale=scale)

    def run(qc, kc, v, gate):
        o, lse = _sm_fwd(qc, kc, v, None, None, name=name + "_fwd", **cfg)
        return _gate_fwd(o, gate, name + "_gate"), o, lse

    @jax.custom_vjp
    def f(qc, kc, v, gate):
        return run(qc.astype(BF16), kc.astype(BF16), v.astype(BF16), gate)[0]

    def fwd(qc, kc, v, gate):
        qc, kc, v = qc.astype(BF16), kc.astype(BF16), v.astype(BF16)
        y, o, lse = run(qc, kc, v, gate)
        return y, (qc, kc, v, gate, o, lse)

    def bwd(res, dy):
        qc, kc, v, gate, o, lse = res
        do, dgate, delta = _gate_bwd(dy, o, gate, name + "_gate_bwd")
        dq, dk, dv = _sm_bwd(qc, kc, v, do, lse, delta, None, None, grad_dtype=F32, name=name + "_bwd", **cfg)
        return dq, dk, dv, dgate

    f.defvjp(fwd, bwd)
    return f(qc, kc, v, gate)


def _sq_loss_call(y, t, name):
    s, d = y.shape
    tm = _tile(s, 512, 8)

    def body(y_ref, t_ref, l_ref, e_ref):
        @pl.when(pl.program_id(0) == 0)
        def _():
            l_ref[...] = jnp.zeros_like(l_ref)

        e = y_ref[...] - t_ref[...]
        e_ref[...] = e * (1.0 / d)
        part = jnp.sum(jnp.sum(e * e, axis=1, keepdims=True), axis=0, keepdims=True)
        l_ref[...] += jnp.broadcast_to(part * (0.5 / d), l_ref.shape)

    row = pl.BlockSpec((tm, d), lambda i: (i, 0))
    return pl.pallas_call(
        body, name=name, grid=(s // tm,),
        in_specs=[row, row],
        out_specs=[pl.BlockSpec((8, 128), lambda i: (0, 0)), row],
        out_shape=[jax.ShapeDtypeStruct((8, 128), F32), jax.ShapeDtypeStruct((s, d), F32)],
        compiler_params=_cparams(("arbitrary",)),
    )(y, t)


@jax.custom_vjp
def sq_loss(y, t):
    return _sq_loss_call(y, t, "loss_fwd")[0][0, 0]


def _sq_loss_fwd(y, t):
    l, e = _sq_loss_call(y, t, "loss_fwd")
    return l[0, 0], e


def _sq_loss_bwd(e, g):
    return g * e, jnp.zeros_like(e)


sq_loss.defvjp(_sq_loss_fwd, _sq_loss_bwd)


def _cast_bf16(x, name):
    r, c = x.shape
    tb = _tile(r, 512, 16)

    def body(x_ref, o_ref):
        o_ref[...] = x_ref[...].astype(BF16)

    return pl.pallas_call(
        body, name=name, grid=(r // tb,),
        in_specs=[pl.BlockSpec((tb, c), lambda i: (i, 0))],
        out_specs=pl.BlockSpec((tb, c), lambda i: (i, 0)),
        out_shape=jax.ShapeDtypeStruct((r, c), BF16),
        compiler_params=_cparams(("parallel",)),
    )(x)


def _pair_sum(core_idx, g, recv, name):
    _, _, r, c = g.shape
    tb = _tile(r, 512, 16)

    def body(c_ref, g_ref, r_ref, o_ref):
        o_ref[...] = (g_ref[...] + r_ref[...]).astype(BF16)

    return pl.pallas_call(
        body, name=name,
        grid_spec=pltpu.PrefetchScalarGridSpec(
            num_scalar_prefetch=1, grid=(4, r // tb),
            in_specs=[pl.BlockSpec((None, None, tb, c), lambda q, i, c_ref: (c_ref[0], q, i, 0)),
                      pl.BlockSpec((None, tb, c), lambda q, i, c_ref: (q, i, 0))],
            out_specs=pl.BlockSpec((None, tb, c), lambda q, i, c_ref: (q, i, 0))),
        out_shape=jax.ShapeDtypeStruct((4, r, c), BF16),
        compiler_params=_cparams(("parallel", "parallel")),
    )(core_idx, g, recv)


def _adamw(w, parts, m, v, name):
    n, r, c = parts.shape
    tb = _tile(r, 256, 8)
    b1c = 1.0 - ADAM_B1 ** ADAM_STEP
    b2c = 1.0 - ADAM_B2 ** ADAM_STEP

    def body(w_ref, p_ref, m_ref, v_ref, g_ref, d_ref, nm_ref, nv_ref):
        g = p_ref[0].astype(F32)
        for k in range(1, n):
            g = g + p_ref[k].astype(F32)
        m_new = ADAM_B1 * m_ref[...] + (1.0 - ADAM_B1) * g
        v_new = ADAM_B2 * v_ref[...] + (1.0 - ADAM_B2) * (g * g)
        m_hat = m_new / b1c
        v_hat = v_new / b2c
        g_ref[...] = g
        d_ref[...] = -ADAM_LR * (m_hat / (jnp.sqrt(v_hat) + ADAM_EPS) + ADAM_WD * w_ref[...])
        nm_ref[...] = m_new
        nv_ref[...] = v_new

    row = pl.BlockSpec((tb, c), lambda i: (i, 0))
    return pl.pallas_call(
        body, name=name, grid=(r // tb,),
        in_specs=[row, pl.BlockSpec((n, tb, c), lambda i: (0, i, 0)), row, row],
        out_specs=[row] * 4,
        out_shape=[jax.ShapeDtypeStruct((r, c), F32)] * 4,
        compiler_params=_cparams(("parallel",)),
    )(w, parts, m, v)


ANY = pl.BlockSpec(memory_space=pl.ANY)


def _place():
    return lax.axis_index("x"), lax.axis_index("y"), lax.axis_index("c")


def _all_gather(shards, kinds, name):
    nm = len(shards)

    def body(*refs):
        start, finish = _gather_steps(refs[:nm], refs[nm:2 * nm], kinds, *refs[2 * nm:])
        start()
        finish()

    return pl.pallas_call(
        body, name=name,
        out_shape=_gather_out_shapes(shards, kinds),
        in_specs=[ANY] * nm, out_specs=[ANY] * nm,
        scratch_shapes=_gather_scratch(nm),
    )(*shards)


def _gather_out_shapes(shards, kinds):
    def full(sh, kind):
        a, b = sh.shape
        return {"row": (N_DEV * a, b), "col": (a, N_DEV * b), "stack": (N_DEV, a, b)}[kind]

    return [jax.ShapeDtypeStruct(full(sh, kd), sh.dtype) for sh, kd in zip(shards, kinds)]


def _gather_scratch(nm):
    return [pltpu.SemaphoreType.DMA((7 * nm,)), pltpu.SemaphoreType.DMA((7 * nm,)), pltpu.SemaphoreType.DMA((nm,))]


def _gather_steps(x_refs, out_refs, kinds, send_sems, recv_sems, local_sems):
    nm = len(x_refs)
    x, y, cc = _place()
    me, sibling = (x, y, cc), (x, y, 1 - cc)
    chips = [(1 - x, y), (x, 1 - y), (1 - x, 1 - y)]

    def slot(mi, px, py, pc):
        d = 4 * px + 2 * py + pc
        a, b = x_refs[mi].shape
        if kinds[mi] == "row":
            return out_refs[mi].at[pl.ds(pl.multiple_of(d * a, a), a), :]
        if kinds[mi] == "col":
            return out_refs[mi].at[:, pl.ds(pl.multiple_of(d * b, 128), b)]
        return out_refs[mi].at[d]

    def copy(mi, k, block, to, src=None):
        return pltpu.make_async_remote_copy(
            src_ref=slot(mi, *block) if src is None else src, dst_ref=slot(mi, *block),
            send_sem=send_sems.at[7 * mi + k], recv_sem=recv_sems.at[7 * mi + k],
            device_id=to, device_id_type=MESH)

    def own_copies():
        mine = [pltpu.make_async_copy(x_refs[mi], slot(mi, *me), local_sems.at[mi]) for mi in range(nm)]
        first = []
        for mi in range(nm):
            first.append(copy(mi, 0, me, sibling, src=x_refs[mi]))
            first += [copy(mi, 1 + j, me, (*chip, cc), src=x_refs[mi]) for j, chip in enumerate(chips)]
        return mine, first

    def start():
        mine, first = own_copies()
        for cp in mine + first:
            cp.start()

    def finish():
        mine, first = own_copies()
        passed = []
        for j, chip in enumerate(chips):
            for mi in range(nm):
                copy(mi, 1 + j, (*chip, cc), me).wait_recv()
                passed.append(copy(mi, 4 + j, (*chip, cc), sibling))
                passed[-1].start()
        for mi in range(nm):
            copy(mi, 0, sibling, me).wait_recv()
            for j, chip in enumerate(chips):
                copy(mi, 4 + j, (*chip, 1 - cc), me).wait_recv()
        for cp in first + passed:
            cp.wait_send()
        for cp in mine:
            cp.wait()

    return start, finish


def _pair_exchange(gs, name):
    nm = len(gs)

    def body(*refs):
        g_refs, recv_refs = refs[:nm], refs[nm:2 * nm]
        send_sems, recv_sems = refs[2 * nm:]
        x, y, cc = _place()
        copies = [pltpu.make_async_remote_copy(
            src_ref=g_refs[mi].at[1 - cc], dst_ref=recv_refs[mi],
            send_sem=send_sems.at[mi], recv_sem=recv_sems.at[mi], device_id=(x, y, 1 - cc), device_id_type=MESH)
            for mi in range(nm)]
        for cp in copies:
            cp.start()
        for cp in copies:
            cp.wait_recv()
        for cp in copies:
            cp.wait_send()

    return pl.pallas_call(
        body, name=name,
        out_shape=[jax.ShapeDtypeStruct(g.shape[1:], g.dtype) for g in gs],
        in_specs=[ANY] * nm, out_specs=[ANY] * nm,
        scratch_shapes=[pltpu.SemaphoreType.DMA((nm,)), pltpu.SemaphoreType.DMA((nm,))],
    )(*gs)


def _chip_exchange(parts, name):
    nm = len(parts)

    def body(*refs):
        start, finish = _exchange_steps(refs[:nm], refs[nm:2 * nm], *refs[2 * nm:])
        start()
        finish()

    return pl.pallas_call(
        body, name=name,
        out_shape=[jax.ShapeDtypeStruct(p.shape, p.dtype) for p in parts],
        in_specs=[ANY] * nm, out_specs=[ANY] * nm,
        scratch_shapes=_exchange_scratch(nm),
    )(*parts)


def _exchange_scratch(nm):
    return [pltpu.SemaphoreType.DMA((4 * nm,)), pltpu.SemaphoreType.DMA((4 * nm,)), pltpu.SemaphoreType.DMA((nm,))]


def _exchange_steps(p_refs, out_refs, send_sems, recv_sems, local_sems):
    nm = len(p_refs)
    x, y, cc = _place()
    mine = 2 * x + y
    others = [(1 - x, y), (x, 1 - y), (1 - x, 1 - y)]

    def own_copies():
        keeps = [pltpu.make_async_copy(p_refs[mi].at[mine], out_refs[mi].at[mine], local_sems.at[mi])
                 for mi in range(nm)]
        sends = []
        for px, py in others:
            q = 2 * px + py
            for mi in range(nm):
                sends.append(pltpu.make_async_remote_copy(
                    src_ref=p_refs[mi].at[q], dst_ref=out_refs[mi].at[mine],
                    send_sem=send_sems.at[4 * mi + q], recv_sem=recv_sems.at[4 * mi + mine],
                    device_id=(px, py, cc), device_id_type=MESH))
        return keeps, sends

    def start():
        keeps, sends = own_copies()
        for cp in keeps + sends:
            cp.start()

    def finish():
        keeps, sends = own_copies()
        for px, py in others:
            q = 2 * px + py
            for mi in range(nm):
                pltpu.make_async_remote_copy(
                    src_ref=p_refs[mi].at[q], dst_ref=out_refs[mi].at[q],
                    send_sem=send_sems.at[4 * mi + q], recv_sem=recv_sems.at[4 * mi + q],
                    device_id=(px, py, cc), device_id_type=MESH).wait_recv()
        for cp in sends:
            cp.wait_send()
        for cp in keeps:
            cp.wait()

    return start, finish


def _all_reduce_small(v, name):
    shape = v.shape

    def body(v_ref, out_ref, buf, send_sems, recv_sems):
        x, y, cc = _place()
        me = 4 * x + 2 * y + cc
        buf[me] = v_ref[...]
        flips = [(a, b, d) for a in (0, 1) for b in (0, 1) for d in (0, 1)][1:]
        copies = []
        for k, (a, b, d) in enumerate(flips):
            peer = (x ^ a, y ^ b, cc ^ d)
            copies.append(pltpu.make_async_remote_copy(
                src_ref=v_ref, dst_ref=buf.at[me],
                send_sem=send_sems.at[k], recv_sem=recv_sems.at[k], device_id=peer, device_id_type=MESH))
        for cp in copies:
            cp.start()
        for k, (a, b, d) in enumerate(flips):
            peer_id = 4 * (x ^ a) + 2 * (y ^ b) + (cc ^ d)
            pltpu.make_async_remote_copy(
                src_ref=v_ref, dst_ref=buf.at[peer_id],
                send_sem=send_sems.at[k], recv_sem=recv_sems.at[k], device_id=(x, y, cc), device_id_type=MESH
            ).wait_recv()
        for cp in copies:
            cp.wait_send()
        total = buf[0]
        for k in range(1, N_DEV):
            total = total + buf[k]
        out_ref[...] = total

    vm = pl.BlockSpec(memory_space=pltpu.VMEM)
    return pl.pallas_call(
        body, name=name,
        out_shape=jax.ShapeDtypeStruct(shape, F32),
        in_specs=[vm], out_specs=vm,
        scratch_shapes=[pltpu.VMEM((N_DEV,) + shape, F32), pltpu.SemaphoreType.DMA((7,)), pltpu.SemaphoreType.DMA((7,))],
    )(v)


def _gather_kind(name, shape):
    if name not in COL_SHARDED:
        return "row"
    return "col" if shape[1] % 128 == 0 else "stack"


def _slab_of(name, shape):
    if name not in COL_SHARDED:
        return ("row", shape[0])
    return ("col", shape[1]) if shape[1] % 128 == 0 else None


def _to_slabs(g, shape):
    kk, nn = shape
    return g.reshape(kk, 4, 2, nn).transpose(2, 1, 0, 3)


SMALL_ROWS = 8


def _pack_small(arrs):
    rows = [arrs[n] for n in SMALL[:5]]
    last = jnp.concatenate([arrs["q_norm1"], arrs["kv_norm1"], arrs["b_f2"]])
    rows.append(jnp.pad(last, (0, PACK_COLS - last.shape[0])))
    rows += [jnp.zeros((PACK_COLS,), F32)] * (SMALL_ROWS - len(rows))
    return jnp.stack(rows)


def _unpack_small(p):
    out = {n: p[k] for k, n in enumerate(SMALL[:5])}
    out["q_norm1"] = p[5, :MLA_Q_RANK]
    out["kv_norm1"] = p[5, MLA_Q_RANK:MLA_Q_RANK + MLA_KV_RANK]
    out["b_f2"] = p[5, MLA_Q_RANK + MLA_KV_RANK:MLA_Q_RANK + MLA_KV_RANK + N_HEADS]
    return out


N_QKV = 3 * D_INNER
N_MAIN = 4 * D_INNER


def _rope(x, pos):
    r = x.shape[-1]
    inv_freq = ROPE_BASE ** (-jnp.arange(0, r, 2, dtype=F32) / r)
    ang = pos.astype(F32)[:, None, None] * inv_freq
    cos, sin = jnp.cos(ang), jnp.sin(ang)
    x1, x2 = x[..., : r // 2], x[..., r // 2:]
    return jnp.concatenate([x1 * cos - x2 * sin, x1 * sin + x2 * cos], axis=-1)


def _forward_loss(carriers, small, x, wfull, late, late_grads, slabs, pos, target):
    s = x.shape[0]
    wfull = dict(wfull)
    carriers = dict(carriers)

    def out_proj(y, w_out, tag):
        return mm(y, wfull[w_out], carriers[w_out], slabs[w_out], name=tag + "_out")

    def sb_layer(x, ln, w_in, w_out, tag, gather=None, late=None):
        h = rmsnorm(x, small[ln], tag + "_ln")
        qkv, gate = in_proj(h, wfull[w_in], (carriers[w_in],), True, tag)
        y, gathered, late_carriers = sb_core(qkv, gate, tag, gather, late)
        return x + out_proj(y, w_out, tag), gathered, late_carriers

    names, shards, kinds, shapes = late
    car_keys, car_shapes, to_slabs = late_grads
    x, gathered, late_carriers = sb_layer(x, "ln0", "w_in0", "w_out0", "l0", (shards, kinds),
                                          (carriers["slots"], car_shapes, to_slabs))
    carriers.update(zip(car_keys, late_carriers))
    for n, kind, full in zip(names, kinds, gathered):
        wfull[n] = full.transpose(1, 0, 2).reshape(shapes[n][0], -1) if kind == "stack" else full

    h = rmsnorm(x, small["ln1"], "l1_ln")
    proj = mm(h, wfull["w_in1"], carriers["w_in1"], slabs["w_in1"], name="l1_in")
    i1, i2, i3 = MLA_Q_RANK, MLA_Q_RANK + MLA_KV_RANK, MLA_Q_RANK + MLA_KV_RANK + MLA_ROPE
    q = mm(rmsnorm(proj[:, :i1], small["q_norm1"], "l1_qn"), wfull["w_qb1"], carriers["w_qb1"], slabs["w_qb1"],
           name="l1_qb")
    q = q.reshape(s, N_HEADS, MLA_NOPE + MLA_ROPE)
    kv = mm(rmsnorm(proj[:, i1:i2], small["kv_norm1"], "l1_kvn"), wfull["w_kvb1"], carriers["w_kvb1"],
            slabs["w_kvb1"], name="l1_kvb")
    kv = kv.reshape(s, N_HEADS, MLA_NOPE + HEAD_DIM)
    k_rope = _rope(proj[:, i2:i3][:, None, :], pos)
    pad = jnp.zeros((s, N_HEADS, MLA_QK_PAD - MLA_NOPE - MLA_ROPE), F32)
    qc = jnp.concatenate([q[..., :MLA_NOPE], _rope(q[..., MLA_NOPE:], pos), pad], axis=-1)
    kc = jnp.concatenate([kv[..., :MLA_NOPE], jnp.broadcast_to(k_rope, (s, N_HEADS, MLA_ROPE)), pad], axis=-1)
    y = mla_core(qc.reshape(s, -1), kc.reshape(s, -1), kv[..., MLA_NOPE:].reshape(s, -1), proj[:, i3:], "l1")
    x = x + out_proj(y, "w_out1", "l1")

    h = rmsnorm(x, small["ln2"], "l2_ln")
    qkv, gate = in_proj(h, wfull["w_in2"], (carriers["w_in2_qkv"], carriers["w_in2_gate"]), False, "l2")
    f_logit = mm(h, wfull["w_in2"][:, N_MAIN:], carriers["w_in2_f"], None, name="l2_f") + small["b_f2"]
    c = jnp.cumsum(jax.nn.log_sigmoid(f_logit), axis=0)
    x = x + out_proj(fox_core(qkv, gate, c, "l2"), "w_out2", "l2")

    x, _, _ = sb_layer(x, "ln3", "w_in3", "w_out3", "l3")
    return sq_loss(rmsnorm(x, small["final_norm"], "final_ln"), target)


def kernel(x, positions, ln0, w_in0, w_out0, ln1, w_in1, q_norm1, w_qb1, kv_norm1, w_kvb1, w_out1, ln2, w_in2, b_f2, w_out2, ln3, w_in3, w_out3, final_norm, loss_target, m_ln0, m_w_in0, m_w_out0, m_ln1, m_w_in1, m_q_norm1, m_w_qb1, m_kv_norm1, m_w_kvb1, m_w_out1, m_ln2, m_w_in2, m_b_f2, m_w_out2, m_ln3, m_w_in3, m_w_out3, m_final_norm, v_ln0, v_w_in0, v_w_out0, v_ln1, v_w_in1, v_q_norm1, v_w_qb1, v_kv_norm1, v_w_kvb1, v_w_out1, v_ln2, v_w_in2, v_b_f2, v_w_out2, v_ln3, v_w_in3, v_w_out3, v_final_norm):
    args = dict(locals())
    w = {n: args[n] for n in ALL_W}
    m = {n: args["m_" + n] for n in ALL_W}
    v = {n: args["v_" + n] for n in ALL_W}
    shapes = {n: w[n].shape for n in BIG}
    kinds = [_gather_kind(n, shapes[n]) for n in BIG]
    slabs = {n: _slab_of(n, shapes[n]) for n in BIG}

    first = BIG[:2]
    late_names = BIG[2:]
    gathered = _all_gather([w[n].astype(BF16) for n in first], kinds[:2], "gather_w")
    wfull = dict(zip(first, gathered))
    late = (late_names, [w[n].astype(BF16) for n in late_names], kinds[2:], shapes)

    d_model = shapes["w_in0"][0]
    car_shapes = {}
    for n in late_names:
        if n == "w_in2":
            car_shapes["w_in2_qkv"] = (d_model, N_QKV)
            car_shapes["w_in2_gate"] = (d_model, D_INNER)
            car_shapes["w_in2_f"] = (d_model, N_DEV * shapes[n][1] - N_MAIN)
        elif slabs[n] is None:
            car_shapes[n] = (shapes[n][0], N_DEV * shapes[n][1])
        else:
            car_shapes[n] = (2, 4) + shapes[n]
    car_keys = list(car_shapes)

    def to_slabs(cts):
        g_car = dict(zip(car_keys, cts))
        out = []
        for n in late_names:
            if n == "w_in2":
                g = jnp.concatenate([g_car["w_in2_qkv"], g_car["w_in2_gate"], g_car["w_in2_f"]], axis=1)
                out.append(_to_slabs(g, shapes[n]))
            elif slabs[n] is None:
                out.append(_to_slabs(g_car[n], shapes[n]))
            else:
                out.append(g_car[n])
        return out

    late_grads = (car_keys, [car_shapes[k] for k in car_keys], to_slabs)
    carriers = {n: jnp.zeros((2, 4) + shapes[n], F32) for n in first}
    carriers["slots"] = [jnp.zeros((4,) + shapes[n], BF16) for n in late_names]
    small = {n: w[n] for n in SMALL}

    def local_loss(carriers, small, x_seq):
        return _forward_loss(carriers, small, x_seq, wfull, late, late_grads, slabs, positions[0], loss_target[0])

    loss_local, (g_car, g_small, g_x) = jax.value_and_grad(local_loss, argnums=(0, 1, 2))(carriers, small, x[0])
    loss = lax.psum(loss_local, ("x", "y", "c"))

    core_idx = lax.axis_index("c").astype(jnp.int32).reshape(1)
    from_sibling = _pair_exchange([g_car[n] for n in first], "pair_exchange")
    chip_part = [_pair_sum(core_idx, g_car[n], r, "pair_sum_" + n) for n, r in zip(first, from_sibling)]
    by_chip = dict(zip(first, _chip_exchange(chip_part, "chip_exchange")))
    by_chip.update(zip(late_names, g_car["slots"]))
    big = [{}, {}, {}, {}]
    for n in BIG:
        for k, t in enumerate(_adamw(w[n], by_chip[n], m[n], v[n], "adamw_" + n)):
            big[k][n] = t

    g_small_sum = _all_reduce_small(_pack_small(g_small), "reduce_small")
    sm = _adamw(_pack_small(w), g_small_sum[None], _pack_small(m), _pack_small(v), "adamw_small")
    small_out = [_unpack_small(t) for t in sm]

    outs = [loss, g_x[None]]
    for k in range(4):
        outs += [small_out[k][n] if n in small_out[k] else big[k][n] for n in ALL_W]
    return tuple(outs)
```

```python
import jax
import jax.numpy as jnp
from jax import lax
from jax.experimental import pallas as pl
from jax.experimental.pallas import tpu as pltpu

F32 = jnp.float32
BF16 = jnp.bfloat16
MESH = pl.DeviceIdType.MESH

N_DEV = 8
N_HEADS = 16
HEAD_DIM = 128
D_INNER = N_HEADS * HEAD_DIM
MLA_Q_RANK = 256
MLA_KV_RANK = 128
MLA_NOPE = 128
MLA_ROPE = 64
MLA_QK_PAD = 256
MLA_CHUNK = 64
ROPE_BASE = 10000.0
EPS = 1e-6
NEG = -1e30
SB_CUT = 104.0

ADAM_LR = 0.001
ADAM_B1 = 0.9
ADAM_B2 = 0.999
ADAM_EPS = 1e-08
ADAM_WD = 0.01
ADAM_STEP = 10

PACK_COLS = 1024
VMEM_LIMIT = 56 * 1024 * 1024

BIG = ["w_in0", "w_out0", "w_in1", "w_qb1", "w_kvb1", "w_out1", "w_in2", "w_out2", "w_in3", "w_out3"]
COL_SHARDED = {"w_in0", "w_in1", "w_qb1", "w_kvb1", "w_in2", "w_in3"}
SMALL = ["ln0", "ln1", "ln2", "ln3", "final_norm", "q_norm1", "kv_norm1", "b_f2"]
ALL_W = ["ln0", "w_in0", "w_out0", "ln1", "w_in1", "q_norm1", "w_qb1", "kv_norm1", "w_kvb1", "w_out1",
         "ln2", "w_in2", "b_f2", "w_out2", "ln3", "w_in3", "w_out3", "final_norm"]


def _cparams(sem=None):
    return pltpu.CompilerParams(dimension_semantics=sem, vmem_limit_bytes=VMEM_LIMIT)


def _tile(dim, cap, align):
    if dim <= cap:
        return dim
    t = (cap // align) * align
    while t >= align:
        if dim % t == 0:
            return t
        t -= align
    return dim


def _dot(a, b, dims):
    return lax.dot_general(a, b, (dims, ((), ())), preferred_element_type=F32)


def _dot_nn(a, b):
    return _dot(a, b, ((1,), (0,)))


def _dot_nt(a, b):
    return _dot(a, b, ((1,), (1,)))


def _dot_tn(a, b):
    return _dot(a, b, ((0,), (0,)))


def _transpose_bf16(x):
    return x.astype(F32).T.astype(BF16)


def _matmul(a, b, mode, col0=0, n_cols=None, out_dtype=F32, name="mm"):
    if mode == "nn":
        m, r = a.shape
        n = n_cols or b.shape[1]
        tn, tr = _tile(n, 1024, 128), _tile(r, 1024, 128)
        tm = _tile(m, 1024 if tn <= 1024 else 512, 8)
        c0 = col0 // tn
        a_spec = pl.BlockSpec((tm, tr), lambda i, j, k: (i, k))
        b_spec = pl.BlockSpec((tr, tn), lambda i, j, k: (k, j + c0))
        dims = ((1,), (0,))
        assert col0 % tn == 0
    elif mode == "nt":
        m, r = a.shape
        n = b.shape[0]
        tn, tr = _tile(n, 1024, 128), _tile(r, 1024, 128)
        tm = _tile(m, 1024 if tr <= 1024 else 512, 8)
        c0 = col0 // tr
        a_spec = pl.BlockSpec((tm, tr), lambda i, j, k: (i, k))
        b_spec = pl.BlockSpec((tn, tr), lambda i, j, k: (j, k + c0))
        dims = ((1,), (1,))
        assert col0 % tr == 0
    else:
        r, m = a.shape
        n = b.shape[1]
        tm, tn, tr = _tile(m, 1024, 128), _tile(n, 1024, 128), _tile(r, 512, 16)
        a_spec = pl.BlockSpec((tr, tm), lambda i, j, k: (k, i))
        b_spec = pl.BlockSpec((tr, tn), lambda i, j, k: (k, j))
        dims = ((0,), (0,))
    nr = r // tr

    def body(a_ref, b_ref, o_ref, acc_ref):
        k = pl.program_id(2)

        @pl.when(k == 0)
        def _():
            acc_ref[...] = jnp.zeros_like(acc_ref)

        acc_ref[...] += _dot(a_ref[...].astype(BF16), b_ref[...].astype(BF16), dims)

        @pl.when(k == nr - 1)
        def _():
            o_ref[...] = acc_ref[...].astype(out_dtype)

    return pl.pallas_call(
        body,
        name=name,
        grid=(m // tm, n // tn, nr),
        in_specs=[a_spec, b_spec],
        out_specs=pl.BlockSpec((tm, tn), lambda i, j, k: (i, j)),
        out_shape=jax.ShapeDtypeStruct((m, n), out_dtype),
        scratch_shapes=[pltpu.VMEM((tm, tn), F32)],
        compiler_params=_cparams(("parallel", "parallel", "arbitrary")),
    )(a, b)


def _transpose_cast(a, name):
    r, m = a.shape
    tr = _tile(r, 512, 128)

    def body(a_ref, o_ref):
        o_ref[...] = a_ref[...].astype(F32).T.astype(BF16)

    return pl.pallas_call(
        body, name=name, grid=(r // tr,),
        in_specs=[pl.BlockSpec((tr, m), lambda i: (i, 0))],
        out_specs=pl.BlockSpec((m, tr), lambda i: (0, i)),
        out_shape=jax.ShapeDtypeStruct((m, r), BF16),
        compiler_params=_cparams(("parallel",)),
    )(a)


def _matmul_dw(a, b1, b2, slab, name):
    m, r = a.shape
    n1 = b1.shape[1]
    n = n1 + (b2.shape[1] if b2 is not None else 0)
    tr = _tile(r, 1024, 128)
    if slab is None:
        tm, tn = _tile(m, 1024, 128), _tile(n1, 1024, 128)
        out_spec = pl.BlockSpec((tm, tn), lambda i, j, k: (i, j))
        out_shape = (m, n)
    elif slab[0] == "col":
        tm, tn = _tile(m, 1024, 128), slab[1]
        out_spec = pl.BlockSpec((None, None, tm, tn), lambda i, j, k: (j % 2, j // 2, i, 0))
        out_shape = (2, 4, m, tn)
        assert n == N_DEV * tn
    else:
        tm, tn = slab[1], _tile(n, 1024, 128)
        out_spec = pl.BlockSpec((None, None, tm, tn), lambda i, j, k: (i % 2, i // 2, 0, j))
        out_shape = (2, 4, tm, n)
        assert m == N_DEV * tm
    assert n1 % tn == 0 and n % tn == 0
    if tn > 1024:
        tr = _tile(r, 256, 128)
    n1b = n1 // tn
    nr = r // tr

    def body(*refs):
        a_ref, b_refs, o_ref, acc_ref = refs[0], refs[1:-2], refs[-2], refs[-1]
        j = pl.program_id(1)
        k = pl.program_id(2)

        @pl.when(k == 0)
        def _():
            acc_ref[...] = jnp.zeros_like(acc_ref)

        at = a_ref[...]
        if b2 is None:
            acc_ref[...] += _dot_nn(at, b_refs[0][...].astype(BF16))
        else:
            @pl.when(j < n1b)
            def _():
                acc_ref[...] += _dot_nn(at, b_refs[0][...].astype(BF16))

            @pl.when(j >= n1b)
            def _():
                acc_ref[...] += _dot_nn(at, b_refs[1][...].astype(BF16))

        @pl.when(k == nr - 1)
        def _():
            o_ref[...] = acc_ref[...]

    in_specs = [pl.BlockSpec((tm, tr), lambda i, j, k: (i, k))]
    args = [a, b1]
    if b2 is None:
        in_specs.append(pl.BlockSpec((tr, tn), lambda i, j, k: (k, j)))
    else:
        in_specs.append(pl.BlockSpec((tr, tn), lambda i, j, k: (jnp.where(j < n1b, k, nr - 1), jnp.minimum(j, n1b - 1))))
        in_specs.append(pl.BlockSpec((tr, tn), lambda i, j, k: (jnp.where(j < n1b, 0, k), jnp.maximum(j - n1b, 0))))
        args.append(b2)
    return pl.pallas_call(
        body, name=name, grid=(m // tm, n // tn, nr),
        in_specs=in_specs, out_specs=out_spec,
        out_shape=jax.ShapeDtypeStruct(out_shape, F32),
        scratch_shapes=[pltpu.VMEM((tm, tn), F32)],
        compiler_params=_cparams(("parallel", "parallel", "arbitrary")),
    )(*args)


def mm(a, w, carrier, slab=None, out_dtype=F32, name="mm"):
    @jax.custom_vjp
    def f(a, w, carrier):
        return _matmul(a, w, "nn", 0, None, out_dtype, name + "_fwd")

    def fwd(a, w, carrier):
        return _matmul(a, w, "nn", 0, None, out_dtype, name + "_fwd"), (a, w)

    def bwd(res, g):
        a, w = res
        da = _matmul(g, w, "nt", 0, None, F32, name + "_dx")
        return da, jnp.zeros_like(w), _matmul_dw(_transpose_cast(a, name + "_t"), g, None, slab, name + "_dw")

    f.defvjp(fwd, bwd)
    return f(a, w, carrier)


def in_proj(h, w, carriers, slab, name):
    def run(h, w):
        return (_matmul(h, w, "nn", 0, N_QKV, BF16, name + "_qkv"),
                _matmul(h, w, "nn", N_QKV, D_INNER, F32, name + "_g"))

    @jax.custom_vjp
    def f(h, w, *cars):
        return run(h, w)

    def fwd(h, w, *cars):
        return run(h, w), (h, w)

    def bwd(res, g):
        h, w = res
        g_qkv, g_gate = g
        dh = (_matmul(g_qkv, w, "nt", 0, None, F32, name + "_qkv_dx")
              + _matmul(g_gate, w, "nt", N_QKV, None, F32, name + "_g_dx"))
        h_t = _transpose_cast(h, name + "_t")
        if slab:
            dws = (_matmul_dw(h_t, g_qkv, g_gate, ("col", PACK_COLS), name + "_dw"),)
        else:
            dws = (_matmul_dw(h_t, g_qkv, None, None, name + "_qkv_dw"),
                   _matmul_dw(h_t, g_gate, None, None, name + "_g_dw"))
        return (dh, jnp.zeros_like(w)) + dws

    f.defvjp(fwd, bwd)
    return f(h, w, *carriers)


def _rms_fwd(x, g, name):
    s, d = x.shape
    tm = _tile(s, 512, 8)

    def body(x_ref, g_ref, y_ref):
        x = x_ref[...]
        r = lax.rsqrt(jnp.mean(x * x, axis=-1, keepdims=True) + EPS)
        y_ref[...] = x * r * g_ref[...]

    return pl.pallas_call(
        body, name=name, grid=(s // tm,),
        in_specs=[pl.BlockSpec((tm, d), lambda i: (i, 0)), pl.BlockSpec((1, d), lambda i: (0, 0))],
        out_specs=pl.BlockSpec((tm, d), lambda i: (i, 0)),
        out_shape=jax.ShapeDtypeStruct((s, d), F32),
        compiler_params=_cparams(("parallel",)),
    )(x, g)


def _rms_bwd(x, g, dy, name):
    s, d = x.shape
    tm = _tile(s, 512, 8)

    def body(x_ref, g_ref, dy_ref, dx_ref, dg_ref):
        @pl.when(pl.program_id(0) == 0)
        def _():
            dg_ref[...] = jnp.zeros_like(dg_ref)

        x = x_ref[...]
        dy = dy_ref[...]
        r = lax.rsqrt(jnp.mean(x * x, axis=-1, keepdims=True) + EPS)
        xh = x * r
        dg_ref[...] += jnp.sum(dy * xh, axis=0, keepdims=True)
        dxh = dy * g_ref[...]
        dx_ref[...] = r * (dxh - xh * jnp.mean(dxh * xh, axis=-1, keepdims=True))

    return pl.pallas_call(
        body, name=name, grid=(s // tm,),
        in_specs=[pl.BlockSpec((tm, d), lambda i: (i, 0)), pl.BlockSpec((1, d), lambda i: (0, 0)),
                  pl.BlockSpec((tm, d), lambda i: (i, 0))],
        out_specs=[pl.BlockSpec((tm, d), lambda i: (i, 0)), pl.BlockSpec((1, d), lambda i: (0, 0))],
        out_shape=[jax.ShapeDtypeStruct((s, d), F32), jax.ShapeDtypeStruct((1, d), F32)],
        compiler_params=_cparams(("arbitrary",)),
    )(x, g, dy)


def rmsnorm(x, g, name="rms"):
    @jax.custom_vjp
    def f(x, g):
        return _rms_fwd(x, g.reshape(1, -1), name + "_fwd")

    def fwd(x, g):
        return _rms_fwd(x, g.reshape(1, -1), name + "_fwd"), (x, g)

    def bwd(res, dy):
        x, g = res
        dx, dg = _rms_bwd(x, g.reshape(1, -1), dy, name + "_bwd")
        return dx, dg.reshape(-1)

    f.defvjp(fwd, bwd)
    return f(x, g)


def _gate_fwd(o, gate, name):
    s = o.shape[0]
    tm = 256

    def body(o_ref, g_ref, y_ref):
        g = g_ref[...]
        y_ref[...] = o_ref[...] * (g / (1.0 + jnp.exp(-g)))

    row = pl.BlockSpec((tm, D_INNER), lambda i: (i, 0))
    return pl.pallas_call(
        body, name=name, grid=(s // tm,),
        in_specs=[row, row], out_specs=row,
        out_shape=jax.ShapeDtypeStruct((s, D_INNER), F32),
        compiler_params=_cparams(("parallel",)),
    )(o, gate)


def _gate_bwd(dy, o, gate, name):
    s = o.shape[0]
    tm = 256

    def body(dy_ref, o_ref, g_ref, do_ref, dg_ref, dl_ref):
        g = g_ref[...]
        o = o_ref[...]
        dy = dy_ref[...]
        sg = 1.0 / (1.0 + jnp.exp(-g))
        do = dy * (g * sg)
        do_ref[...] = do.astype(BF16)
        dg_ref[...] = dy * o * (sg * (1.0 + g * (1.0 - sg)))
        prod = do * o
        for h in range(N_HEADS):
            dl_ref[h] = jnp.sum(prod[:, h * HEAD_DIM:(h + 1) * HEAD_DIM], axis=1, keepdims=True)

    row = pl.BlockSpec((tm, D_INNER), lambda i: (i, 0))
    return pl.pallas_call(
        body, name=name, grid=(s // tm,),
        in_specs=[row, row, row],
        out_specs=[row, row, pl.BlockSpec((N_HEADS, tm, 1), lambda i: (0, i, 0))],
        out_shape=[jax.ShapeDtypeStruct((s, D_INNER), BF16), jax.ShapeDtypeStruct((s, D_INNER), F32),
                   jax.ShapeDtypeStruct((N_HEADS, s, 1), F32)],
        compiler_params=_cparams(("parallel",)),
    )(dy, o, gate)


SB_BLK = 256


def _softplus(z):
    return jnp.maximum(z, 0.0) + jnp.log(1.0 + jnp.exp(-jnp.abs(z)))


def _tri_sum(x, tri):
    hi = x.astype(BF16)
    lo = (x - hi.astype(F32)).astype(BF16)
    return _dot_nn(hi, tri) + _dot_nn(lo, tri)


SB_WIN = 2 * SB_BLK


def _sb_tri(kind):
    row = lax.broadcasted_iota(jnp.int32, (SB_BLK, SB_BLK), 0)
    col = lax.broadcasted_iota(jnp.int32, (SB_BLK, SB_BLK), 1)
    return ((row >= col) if kind == "suffix" else (row <= col)).astype(BF16)


def _sb_bounds(i, t):
    hi = (i + 1) * SB_BLK - t * SB_WIN
    return hi, pl.multiple_of(jnp.maximum(hi - SB_WIN, 0), SB_BLK)


def _sb_mask(i, hi, start):
    row = lax.broadcasted_iota(jnp.int32, (SB_BLK, SB_WIN), 0) + i * SB_BLK
    col = lax.broadcasted_iota(jnp.int32, (SB_BLK, SB_WIN), 1) + start
    return jnp.logical_and(col < row, col < hi)


def _sb_window(q, kwin, mask, a_run, tri_suffix, scale):
    b = SB_BLK
    z = _dot_nt(q, kwin) * scale
    sp = _softplus(z)
    ls = jnp.where(mask, -sp, 0.0)
    ls_l, ls_r = ls[:, :b], ls[:, b:]
    suffix = jnp.concatenate([_tri_sum(ls_l, tri_suffix) + jnp.sum(ls_r, axis=1, keepdims=True),
                              _tri_sum(ls_r, tri_suffix)], axis=1)
    w = jnp.where(mask, jnp.exp(z + suffix + a_run), 0.0)
    return z, sp, ls, w


def _sb_fwd(qkv, name, gather=None):
    s = qkv.shape[0]
    b = SB_BLK
    nq = s // b
    scale = HEAD_DIM ** -0.5
    assert s >= SB_WIN
    shards, kinds = gather if gather else ((), ())
    nm = len(shards)

    def body(*refs):
        q_ref, k_ref, v_ref = refs[:3]
        o_ref = refs[3 + nm]
        i = pl.program_id(1)
        if nm:
            start, finish = _gather_steps(refs[3:3 + nm], refs[4 + nm:4 + 2 * nm], kinds, *refs[4 + 2 * nm:])
            first_step = jnp.logical_and(pl.program_id(0) == 0, i == 0)
            last_step = jnp.logical_and(pl.program_id(0) == N_HEADS - 1, i == nq - 1)
            pl.when(first_step)(start)
        q = q_ref[...]
        tri_suffix = _sb_tri("suffix")

        def cond(c):
            t, a_run, _ = c
            return jnp.logical_and((i + 1) * b - t * SB_WIN > 0, jnp.max(a_run) > -SB_CUT)

        def step(c):
            t, a_run, acc = c
            hi, start = _sb_bounds(i, t)
            _, _, ls, w = _sb_window(q, k_ref[pl.ds(start, SB_WIN), :], _sb_mask(i, hi, start), a_run,
                                     tri_suffix, scale)
            acc = acc + _dot_nn(w.astype(BF16), v_ref[pl.ds(start, SB_WIN), :])
            return t + 1, a_run + jnp.sum(ls, axis=1, keepdims=True), acc

        _, _, acc = lax.while_loop(cond, step, (0, jnp.zeros((b, 1), F32), jnp.zeros((b, HEAD_DIM), F32)))
        o_ref[...] = acc
        if nm:
            pl.when(last_step)(finish)

    out = pl.pallas_call(
        body, name=name, grid=(N_HEADS, nq),
        in_specs=[pl.BlockSpec((b, HEAD_DIM), lambda h, i: (i, h)),
                  pl.BlockSpec((s, HEAD_DIM), lambda h, i: (0, N_HEADS + h)),
                  pl.BlockSpec((s, HEAD_DIM), lambda h, i: (0, 2 * N_HEADS + h))] + [ANY] * nm,
        out_specs=[pl.BlockSpec((b, HEAD_DIM), lambda h, i: (i, h))] + [ANY] * nm,
        out_shape=[jax.ShapeDtypeStruct((s, D_INNER), F32)] + _gather_out_shapes(shards, kinds),
        scratch_shapes=_gather_scratch(nm) if nm else [],
        compiler_params=_cparams(("arbitrary", "arbitrary")),
    )(qkv, qkv, qkv, *shards)
    return out[0], list(out[1:])


def _sb_bwd(qkv, do, name, exchange=()):
    s = qkv.shape[0]
    b = SB_BLK
    nq = s // b
    n_win = -(-s // SB_WIN) + 1
    scale = HEAD_DIM ** -0.5
    assert s >= SB_WIN
    nm = len(exchange)

    def body(*refs):
        q_ref, k_ref, v_ref, do_ref = refs[:4]
        dq_ref, dk_ref, dv_ref = refs[4 + nm:7 + nm]
        dkt_s, dvt_s, g_buf, sig_buf = refs[7 + 2 * nm:11 + 2 * nm]
        i = pl.program_id(1)
        if nm:
            start_x, finish_x = _exchange_steps(refs[4:4 + nm], refs[7 + nm:7 + 2 * nm], *refs[11 + 2 * nm:])
            pl.when(jnp.logical_and(pl.program_id(0) == 0, i == 0))(start_x)

        @pl.when(i == 0)
        def _():
            dkt_s[...] = jnp.zeros_like(dkt_s)
            dvt_s[...] = jnp.zeros_like(dvt_s)

        q = q_ref[...]
        dob = do_ref[...]
        q_t = _transpose_bf16(q)
        do_t = _transpose_bf16(dob)
        tri_suffix = _sb_tri("suffix")
        tri_prefix = _sb_tri("prefix")

        def add_halves(acc_ref, start, upd):
            blk = start // b
            acc_ref[blk] += upd[:, :b]
            acc_ref[blk + 1] += upd[:, b:]

        def cond(c):
            t, a_run = c
            return jnp.logical_and((i + 1) * b - t * SB_WIN > 0, jnp.max(a_run) > -SB_CUT)

        def sweep(c):
            t, a_run = c
            hi, start = _sb_bounds(i, t)
            z, sp, ls, w = _sb_window(q, k_ref[pl.ds(start, SB_WIN), :], _sb_mask(i, hi, start), a_run,
                                      tri_suffix, scale)
            g_buf[t] = w * _dot_nt(dob, v_ref[pl.ds(start, SB_WIN), :])
            sig_buf[t] = jnp.exp(z - sp)
            add_halves(dvt_s, start, _dot_nn(do_t, w.astype(BF16)))
            return t + 1, a_run + jnp.sum(ls, axis=1, keepdims=True)

        n_steps, _ = lax.while_loop(cond, sweep, (0, jnp.zeros((b, 1), F32)))

        def back(u, c):
            g_run, dq = c
            t = n_steps - 1 - u
            hi, start = _sb_bounds(i, t)
            g = g_buf[t]
            g_l, g_r = g[:, :b], g[:, b:]
            g_incl = g_run + jnp.concatenate(
                [_tri_sum(g_l, tri_prefix),
                 _tri_sum(g_r, tri_prefix) + jnp.sum(g_l, axis=1, keepdims=True)], axis=1)
            dz = jnp.where(_sb_mask(i, hi, start), (g - sig_buf[t] * g_incl) * scale, 0.0).astype(BF16)
            add_halves(dkt_s, start, _dot_nn(q_t, dz))
            return g_run + jnp.sum(g, axis=1, keepdims=True), dq + _dot_nn(dz, k_ref[pl.ds(start, SB_WIN), :])

        _, dq = lax.fori_loop(0, n_steps, back, (jnp.zeros((b, 1), F32), jnp.zeros((b, HEAD_DIM), F32)))
        dq_ref[...] = dq.astype(BF16)

        @pl.when(i == nq - 1)
        def _():
            for jb in range(nq):
                dk_ref[jb * b:(jb + 1) * b, :] = dkt_s[jb].T.astype(BF16)
                dv_ref[jb * b:(jb + 1) * b, :] = dvt_s[jb].T.astype(BF16)

        if nm:
            pl.when(jnp.logical_and(pl.program_id(0) == N_HEADS - 1, i == nq - 1))(finish_x)

    blk = pl.BlockSpec((b, HEAD_DIM), lambda h, i: (i, h))
    head = pl.BlockSpec((s, HEAD_DIM), lambda h, i: (0, h))
    out = pl.pallas_call(
        body, name=name, grid=(N_HEADS, nq),
        in_specs=[blk,
                  pl.BlockSpec((s, HEAD_DIM), lambda h, i: (0, N_HEADS + h)),
                  pl.BlockSpec((s, HEAD_DIM), lambda h, i: (0, 2 * N_HEADS + h)),
                  blk] + [ANY] * nm,
        out_specs=[blk, head, head] + [ANY] * nm,
        out_shape=[jax.ShapeDtypeStruct((s, D_INNER), BF16)] * 3
        + [jax.ShapeDtypeStruct(p.shape, p.dtype) for p in exchange],
        scratch_shapes=[pltpu.VMEM((nq, HEAD_DIM, b), F32), pltpu.VMEM((nq, HEAD_DIM, b), F32),
                        pltpu.VMEM((n_win, b, SB_WIN), F32), pltpu.VMEM((n_win, b, SB_WIN), F32)]
        + (_exchange_scratch(nm) if nm else []),
        compiler_params=_cparams(("arbitrary", "arbitrary")),
    )(qkv, qkv, qkv, do, *exchange)
    return out[0], out[1], out[2], list(out[3:])


def sb_core(qkv, gate, name, gather=None, late=None):
    shards, kinds = gather if gather else ((), ())
    slots, carrier_shapes, to_slabs = late if late else ((), (), None)
    n_sh = len(shards)

    def run(qkv, gate, *shards):
        o, gathered = _sb_fwd(qkv, name + "_fwd", (shards, kinds) if shards else None)
        carriers = [jnp.zeros(sh, F32) for sh in carrier_shapes]
        return (_gate_fwd(o, gate, name + "_gate"), *gathered, *carriers), o

    @jax.custom_vjp
    def f(qkv, gate, *extra):
        return run(qkv, gate, *extra[:n_sh])[0]

    def fwd(qkv, gate, *extra):
        outs, o = run(qkv, gate, *extra[:n_sh])
        return outs, (qkv, gate, o)

    def bwd(res, cts):
        qkv, gate, o = res
        do, dgate, _ = _gate_bwd(cts[0], o, gate, name + "_gate_bwd")
        parts = []
        if slots:
            g_slabs = to_slabs(cts[1 + n_sh:])
            core_idx = lax.axis_index("c").astype(jnp.int32).reshape(1)
            from_sibling = _pair_exchange(g_slabs, name + "_pair_exchange")
            parts = [_pair_sum(core_idx, g, r, name + "_pair_sum%d" % k)
                     for k, (g, r) in enumerate(zip(g_slabs, from_sibling))]
        dq, dk, dv, by_chip = _sb_bwd(qkv, do, name + "_bwd", parts)
        zeros = tuple(jnp.zeros(sh.shape, sh.dtype) for sh in shards)
        return (jnp.concatenate([dq, dk, dv], axis=1), dgate) + zeros + tuple(by_chip)

    f.defvjp(fwd, bwd)
    outs = f(qkv, gate, *shards, *slots)
    return outs[0], list(outs[1:1 + n_sh]), list(outs[1 + n_sh:])


SM_FWD_BLK = 256
SM_BLK = 512


SM_FWD_KEYS = 1024
SM_BWD_KEYS = 512


def _sm_mask(i, jw, rows, keys, chunk_shift):
    row = lax.broadcasted_iota(jnp.int32, (rows, keys), 0) + i * rows
    col = lax.broadcasted_iota(jnp.int32, (rows, keys), 1) + jw * keys
    return (col >> chunk_shift) <= (row >> chunk_shift)


def _sm_fwd(qa, ka, va, ccol, crow, dqk, qo, ko, vo, chunk_shift, scale, name):
    s = qa.shape[0]
    b = min(SM_FWD_BLK, s)
    keys = min(SM_FWD_KEYS, s)
    per = keys // b
    nq = s // b
    has_bias = ccol is not None

    def body(*refs):
        if has_bias:
            q_ref, k_ref, v_ref, cc_ref, cr_ref, o_ref, lse_ref, m_s, l_s, acc_s = refs
        else:
            q_ref, k_ref, v_ref, o_ref, lse_ref, m_s, l_s, acc_s = refs
        i = pl.program_id(1)
        q = q_ref[...]
        m_s[...] = jnp.full_like(m_s, NEG)
        l_s[...] = jnp.zeros_like(l_s)
        acc_s[...] = jnp.zeros_like(acc_s)

        def sweep(jw, masked):
            off = pl.multiple_of(jw * keys, keys)
            z = _dot_nt(q, k_ref[pl.ds(off, keys), :]) * scale
            if has_bias:
                z = z + cc_ref[...] - cr_ref[jw]
            if masked:
                z = jnp.where(_sm_mask(i, jw, b, keys, chunk_shift), z, NEG)
            m_old = m_s[...]
            m_new = jnp.maximum(m_old, jnp.max(z, axis=1, keepdims=True))
            alpha = jnp.exp(m_old - m_new)
            p = jnp.exp(z - m_new)
            l_s[...] = alpha * l_s[...] + jnp.sum(p, axis=1, keepdims=True)
            acc_s[...] = alpha * acc_s[...] + _dot_nn(p.astype(BF16), v_ref[pl.ds(off, keys), :])
            m_s[...] = m_new

        def full(jw, carry):
            sweep(jw, False)
            return carry

        lax.fori_loop(0, i // per, full, 0)
        sweep(i // per, True)
        o_ref[...] = acc_s[...] / l_s[...]
        lse_ref[...] = m_s[...] + jnp.log(l_s[...])

    in_specs = [pl.BlockSpec((b, dqk), lambda h, i: (i, qo + h)),
                pl.BlockSpec((s, dqk), lambda h, i: (0, ko + h)),
                pl.BlockSpec((s, HEAD_DIM), lambda h, i: (0, vo + h))]
    args = [qa, ka, va]
    if has_bias:
        in_specs += [pl.BlockSpec((None, b, 1), lambda h, i: (h, i, 0)),
                     pl.BlockSpec((None, s // keys, 1, keys), lambda h, i: (h, 0, 0, 0))]
        args += [ccol, crow]
    return pl.pallas_call(
        body, name=name, grid=(N_HEADS, nq),
        in_specs=in_specs,
        out_specs=[pl.BlockSpec((b, HEAD_DIM), lambda h, i: (i, h)),
                   pl.BlockSpec((None, b, 1), lambda h, i: (h, i, 0))],
        out_shape=[jax.ShapeDtypeStruct((s, D_INNER), F32), jax.ShapeDtypeStruct((N_HEADS, s, 1), F32)],
        scratch_shapes=[pltpu.VMEM((b, 1), F32), pltpu.VMEM((b, 1), F32), pltpu.VMEM((b, HEAD_DIM), F32)],
        compiler_params=_cparams(("parallel", "arbitrary")),
    )(*args)


def _sm_bwd(qa, ka, va, do, lse, delta, ccol, crow, dqk, qo, ko, vo, chunk_shift, scale, grad_dtype, name):
    s = qa.shape[0]
    b = SM_BLK
    keys = min(SM_BWD_KEYS, s)
    per = keys // b
    nq = s // b
    nk = s // keys
    has_bias = ccol is not None

    def body(*refs):
        if has_bias:
            (q_ref, k_ref, v_ref, do_ref, lse_ref, dl_ref, cc_ref, cr_ref,
             dq_ref, dk_ref, dv_ref, dc_ref, dr_ref, dq_s, dkt_s, dvt_s, dc_s, dr_s) = refs
        else:
            (q_ref, k_ref, v_ref, do_ref, lse_ref, dl_ref,
             dq_ref, dk_ref, dv_ref, dq_s, dkt_s, dvt_s) = refs
        i = pl.program_id(1)

        @pl.when(i == 0)
        def _():
            dkt_s[...] = jnp.zeros_like(dkt_s)
            dvt_s[...] = jnp.zeros_like(dvt_s)
            if has_bias:
                dc_s[...] = jnp.zeros_like(dc_s)

        q = q_ref[...]
        dob = do_ref[...]
        q_t = _transpose_bf16(q)
        do_t = _transpose_bf16(dob)
        lse = lse_ref[...]
        delta = dl_ref[...]
        dq_s[...] = jnp.zeros_like(dq_s)
        if has_bias:
            dr_s[...] = jnp.zeros_like(dr_s)

        def sweep(jw, masked):
            off = pl.multiple_of(jw * keys, keys)
            kb = k_ref[pl.ds(off, keys), :]
            z = _dot_nt(q, kb) * scale
            if has_bias:
                z = z + cc_ref[...] - cr_ref[jw]
            p = jnp.exp(z - lse)
            if masked:
                p = jnp.where(_sm_mask(i, jw, b, keys, chunk_shift), p, 0.0)
            dvt_s[jw] += _dot_nn(do_t, p.astype(BF16))
            dz = p * (_dot_nt(dob, v_ref[pl.ds(off, keys), :]) - delta)
            if has_bias:
                dc_s[jw] += jnp.sum(dz, axis=0, keepdims=True)
                dr_s[...] += jnp.sum(dz, axis=1, keepdims=True)
            dzs = (dz * scale).astype(BF16)
            dkt_s[jw] += _dot_nn(q_t, dzs)
            dq_s[...] += _dot_nn(dzs, kb)

        def full(jw, carry):
            sweep(jw, False)
            return carry

        lax.fori_loop(0, i // per, full, 0)
        sweep(i // per, True)
        dq_ref[...] = dq_s[...].astype(grad_dtype)
        if has_bias:
            dr_ref[...] = dr_s[...]

        @pl.when(i == nq - 1)
        def _():
            for jw in range(nk):
                dk_ref[jw * keys:(jw + 1) * keys, :] = dkt_s[jw].T.astype(grad_dtype)
                dv_ref[jw * keys:(jw + 1) * keys, :] = dvt_s[jw].T.astype(grad_dtype)
            if has_bias:
                dc_ref[...] = dc_s[...]

    vec = pl.BlockSpec((None, b, 1), lambda h, i: (h, i, 0))
    in_specs = [pl.BlockSpec((b, dqk), lambda h, i: (i, qo + h)),
                pl.BlockSpec((s, dqk), lambda h, i: (0, ko + h)),
                pl.BlockSpec((s, HEAD_DIM), lambda h, i: (0, vo + h)),
                pl.BlockSpec((b, HEAD_DIM), lambda h, i: (i, h)),
                vec, vec]
    args = [qa, ka, va, do, lse, delta]
    out_specs = [pl.BlockSpec((b, dqk), lambda h, i: (i, h)),
                 pl.BlockSpec((s, dqk), lambda h, i: (0, h)),
                 pl.BlockSpec((s, HEAD_DIM), lambda h, i: (0, h))]
    out_shape = [jax.ShapeDtypeStruct((s, N_HEADS * dqk), grad_dtype),
                 jax.ShapeDtypeStruct((s, N_HEADS * dqk), grad_dtype),
                 jax.ShapeDtypeStruct((s, D_INNER), grad_dtype)]
    scratch = [pltpu.VMEM((b, dqk), F32), pltpu.VMEM((nk, dqk, keys), F32), pltpu.VMEM((nk, HEAD_DIM, keys), F32)]
    if has_bias:
        key_vec = pl.BlockSpec((None, nk, 1, keys), lambda h, i: (h, 0, 0, 0))
        in_specs += [vec, key_vec]
        args += [ccol, crow]
        out_specs += [key_vec, vec]
        out_shape += [jax.ShapeDtypeStruct((N_HEADS, nk, 1, keys), F32), jax.ShapeDtypeStruct((N_HEADS, s, 1), F32)]
        scratch += [pltpu.VMEM((nk, 1, keys), F32), pltpu.VMEM((b, 1), F32)]
    return pl.pallas_call(
        body, name=name, grid=(N_HEADS, nq),
        in_specs=in_specs, out_specs=out_specs, out_shape=out_shape, scratch_shapes=scratch,
        compiler_params=_cparams(("arbitrary", "arbitrary")),
    )(*args)


def fox_core(qkv, gate, c, name):
    s = qkv.shape[0]
    scale = HEAD_DIM ** -0.5
    cfg = dict(dqk=HEAD_DIM, qo=0, ko=N_HEADS, vo=2 * N_HEADS, chunk_shift=0, scale=scale)

    def layouts(c, keys):
        ct = c.T
        keys = min(keys, s)
        return ct.reshape(N_HEADS, s, 1), ct.reshape(N_HEADS, s // keys, 1, keys)

    def run(qkv, gate, c):
        ccol, crow = layouts(c, SM_FWD_KEYS)
        o, lse = _sm_fwd(qkv, qkv, qkv, ccol, crow, name=name + "_fwd", **cfg)
        return _gate_fwd(o, gate, name + "_gate"), o, lse

    @jax.custom_vjp
    def f(qkv, gate, c):
        return run(qkv, gate, c)[0]

    def fwd(qkv, gate, c):
        y, o, lse = run(qkv, gate, c)
        return y, (qkv, gate, c, o, lse)

    def bwd(res, dy):
        qkv, gate, c, o, lse = res
        ccol, crow = layouts(c, SM_BWD_KEYS)
        do, dgate, delta = _gate_bwd(dy, o, gate, name + "_gate_bwd")
        dq, dk, dv, colsum, rowsum = _sm_bwd(qkv, qkv, qkv, do, lse, delta, ccol, crow,
                                             grad_dtype=BF16, name=name + "_bwd", **cfg)
        dc = (rowsum.reshape(N_HEADS, s) - colsum.reshape(N_HEADS, s)).T
        return jnp.concatenate([dq, dk, dv], axis=1), dgate, dc

    f.defvjp(fwd, bwd)
    return f(qkv, gate, c)


def _mla_rope(x, cosv, sinv, out_dtype, name):
    s, width = x.shape
    tm = 256
    half = MLA_ROPE // 2

    def body(x_ref, c_ref, s_ref, o_ref):
        c = c_ref[...]
        sn = s_ref[...]
        lane = lax.broadcasted_iota(jnp.int32, (tm, HEAD_DIM), 1)
        for h in range(N_HEADS):
            lo = h * MLA_QK_PAD
            o_ref[:, lo:lo + HEAD_DIM] = x_ref[:, lo:lo + HEAD_DIM].astype(out_dtype)
            g = x_ref[:, lo + HEAD_DIM:lo + MLA_QK_PAD].astype(F32)
            swapped = jnp.where(lane < half, pltpu.roll(g, HEAD_DIM - half, 1), pltpu.roll(g, half, 1))
            o_ref[:, lo + HEAD_DIM:lo + MLA_QK_PAD] = (g * c + swapped * sn).astype(out_dtype)

    row = pl.BlockSpec((tm, width), lambda i: (i, 0))
    tab = pl.BlockSpec((tm, HEAD_DIM), lambda i: (i, 0))
    return pl.pallas_call(
        body, name=name, grid=(s // tm,),
        in_specs=[row, tab, tab], out_specs=row,
        out_shape=jax.ShapeDtypeStruct((s, width), out_dtype),
        compiler_params=_cparams(("parallel",)),
    )(x, cosv, sinv)


def _mla_fwd(qc, kv, kr, chunk_shift, scale, name):
    s = qc.shape[0]
    b = min(SM_FWD_BLK, s)
    keys = min(SM_FWD_KEYS, s)
    per = keys // b
    nq = s // b

    def body(qn_ref, qr_ref, kn_ref, v_ref, kr_ref, o_ref, lse_ref, m_s, l_s, acc_s):
        i = pl.program_id(1)
        qn = qn_ref[...]
        qr = qr_ref[...]
        m_s[...] = jnp.full_like(m_s, NEG)
        l_s[...] = jnp.zeros_like(l_s)
        acc_s[...] = jnp.zeros_like(acc_s)

        def sweep(jw, masked):
            off = pl.multiple_of(jw * keys, keys)
            z = (_dot_nt(qn, kn_ref[pl.ds(off, keys), :]) + _dot_nt(qr, kr_ref[pl.ds(off, keys), :])) * scale
            if masked:
                z = jnp.where(_sm_mask(i, jw, b, keys, chunk_shift), z, NEG)
            m_old = m_s[...]
            m_new = jnp.maximum(m_old, jnp.max(z, axis=1, keepdims=True))
            alpha = jnp.exp(m_old - m_new)
            p = jnp.exp(z - m_new)
            l_s[...] = alpha * l_s[...] + jnp.sum(p, axis=1, keepdims=True)
            acc_s[...] = alpha * acc_s[...] + _dot_nn(p.astype(BF16), v_ref[pl.ds(off, keys), :])
            m_s[...] = m_new

        def full(jw, carry):
            sweep(jw, False)
            return carry

        lax.fori_loop(0, i // per, full, 0)
        sweep(i // per, True)
        o_ref[...] = acc_s[...] / l_s[...]
        lse_ref[...] = m_s[...] + jnp.log(l_s[...])

    return pl.pallas_call(
        body, name=name, grid=(N_HEADS, nq),
        in_specs=[pl.BlockSpec((b, HEAD_DIM), lambda h, i: (i, 2 * h)),
                  pl.BlockSpec((b, HEAD_DIM), lambda h, i: (i, 2 * h + 1)),
                  pl.BlockSpec((s, HEAD_DIM), lambda h, i: (0, 2 * h)),
                  pl.BlockSpec((s, HEAD_DIM), lambda h, i: (0, 2 * h + 1)),
                  pl.BlockSpec((s, HEAD_DIM), lambda h, i: (0, 0))],
        out_specs=[pl.BlockSpec((b, HEAD_DIM), lambda h, i: (i, h)),
                   pl.BlockSpec((None, b, 1), lambda h, i: (h, i, 0))],
        out_shape=[jax.ShapeDtypeStruct((s, D_INNER), F32), jax.ShapeDtypeStruct((N_HEADS, s, 1), F32)],
        scratch_shapes=[pltpu.VMEM((b, 1), F32), pltpu.VMEM((b, 1), F32), pltpu.VMEM((b, HEAD_DIM), F32)],
        compiler_params=_cparams(("parallel", "arbitrary")),
    )(qc, qc, kv, kv, kr)


def _mla_bwd(qc, kv, kr, do, lse, delta, chunk_shift, scale, name):
    s = qc.shape[0]
    b = min(SM_BLK, s)
    keys = min(SM_BWD_KEYS, s)
    per = keys // b
    nq = s // b
    nk = s // keys

    def body(qn_ref, qr_ref, kn_ref, v_ref, kr_ref, do_ref, lse_ref, dl_ref, dq_ref, dkv_ref, dkr_ref,
             dqn_s, dqr_s, dkt_s, dvt_s, dkrt_s):
        h = pl.program_id(0)
        i = pl.program_id(1)

        @pl.when(jnp.logical_and(h == 0, i == 0))
        def _():
            dkrt_s[...] = jnp.zeros_like(dkrt_s)

        @pl.when(i == 0)
        def _():
            dkt_s[...] = jnp.zeros_like(dkt_s)
            dvt_s[...] = jnp.zeros_like(dvt_s)

        qn = qn_ref[...]
        qr = qr_ref[...]
        dob = do_ref[...]
        qn_t = _transpose_bf16(qn)
        qr_t = _transpose_bf16(qr)
        do_t = _transpose_bf16(dob)
        lse_i = lse_ref[...]
        delta_i = dl_ref[...]
        dqn_s[...] = jnp.zeros_like(dqn_s)
        dqr_s[...] = jnp.zeros_like(dqr_s)

        def sweep(jw, masked):
            off = pl.multiple_of(jw * keys, keys)
            kn = kn_ref[pl.ds(off, keys), :]
            krw = kr_ref[pl.ds(off, keys), :]
            p = jnp.exp((_dot_nt(qn, kn) + _dot_nt(qr, krw)) * scale - lse_i)
            if masked:
                p = jnp.where(_sm_mask(i, jw, b, keys, chunk_shift), p, 0.0)
            dvt_s[jw] += _dot_nn(do_t, p.astype(BF16))
            dz = p * (_dot_nt(dob, v_ref[pl.ds(off, keys), :]) - delta_i)
            dzs = (dz * scale).astype(BF16)
            dkt_s[jw] += _dot_nn(qn_t, dzs)
            dkrt_s[jw] += _dot_nn(qr_t, dzs)
            dqn_s[...] += _dot_nn(dzs, kn)
            dqr_s[...] += _dot_nn(dzs, krw)

        def full(jw, carry):
            sweep(jw, False)
            return carry

        lax.fori_loop(0, i // per, full, 0)
        sweep(i // per, True)
        dq_ref[:, :HEAD_DIM] = dqn_s[...]
        dq_ref[:, HEAD_DIM:] = dqr_s[...]

        @pl.when(i == nq - 1)
        def _():
            for jw in range(nk):
                dkv_ref[jw * keys:(jw + 1) * keys, :HEAD_DIM] = dkt_s[jw].T.astype(BF16)
                dkv_ref[jw * keys:(jw + 1) * keys, HEAD_DIM:] = dvt_s[jw].T.astype(BF16)

        @pl.when(jnp.logical_and(h == N_HEADS - 1, i == nq - 1))
        def _():
            for jw in range(nk):
                dkr_ref[jw * keys:(jw + 1) * keys, :] = dkrt_s[jw].T

    vec = pl.BlockSpec((None, b, 1), lambda h, i: (h, i, 0))
    acc = pltpu.VMEM((nk, HEAD_DIM, keys), F32)
    return pl.pallas_call(
        body, name=name, grid=(N_HEADS, nq),
        in_specs=[pl.BlockSpec((b, HEAD_DIM), lambda h, i: (i, 2 * h)),
                  pl.BlockSpec((b, HEAD_DIM), lambda h, i: (i, 2 * h + 1)),
                  pl.BlockSpec((s, HEAD_DIM), lambda h, i: (0, 2 * h)),
                  pl.BlockSpec((s, HEAD_DIM), lambda h, i: (0, 2 * h + 1)),
                  pl.BlockSpec((s, HEAD_DIM), lambda h, i: (0, 0)),
                  pl.BlockSpec((b, HEAD_DIM), lambda h, i: (i, h)),
                  vec, vec],
        out_specs=[pl.BlockSpec((b, MLA_QK_PAD), lambda h, i: (i, h)),
                   pl.BlockSpec((s, MLA_QK_PAD), lambda h, i: (0, h)),
                   pl.BlockSpec((s, HEAD_DIM), lambda h, i: (0, 0))],
        out_shape=[jax.ShapeDtypeStruct((s, N_HEADS * MLA_QK_PAD), F32),
                   jax.ShapeDtypeStruct((s, N_HEADS * MLA_QK_PAD), BF16),
                   jax.ShapeDtypeStruct((s, HEAD_DIM), F32)],
        scratch_shapes=[pltpu.VMEM((b, HEAD_DIM), F32), pltpu.VMEM((b, HEAD_DIM), F32), acc, acc, acc],
        compiler_params=_cparams(("arbitrary", "arbitrary")),
    )(qc, qc, kv, kv, kr, do, lse, delta)


def mla_core(qp, kv, kr, gate, cosv, sinv, name):
    scale = (MLA_NOPE + MLA_ROPE) ** -0.5
    shift = MLA_CHUNK.bit_length() - 1

    def run(qp, kv, kr, gate, cosv, sinv):
        qc = _mla_rope(qp, cosv, sinv, BF16, name + "_rope")
        krb = kr.astype(BF16)
        o, lse = _mla_fwd(qc, kv, krb, shift, scale, name + "_fwd")
        return _gate_fwd(o, gate, name + "_gate"), (qc, kv, krb, gate, o, lse, cosv, sinv)

    @jax.custom_vjp
    def f(qp, kv, kr, gate, cosv, sinv):
        return run(qp, kv, kr, gate, cosv, sinv)[0]

    def bwd(res, dy):
        qc, kv, krb, gate, o, lse, cosv, sinv = res
        do, dgate, delta = _gate_bwd(dy, o, gate, name + "_gate_bwd")
        dqc, dkv, dkr = _mla_bwd(qc, kv, krb, do, lse, delta, shift, scale, name + "_bwd")
        dqp = _mla_rope(dqc, cosv, -sinv, F32, name + "_rope_bwd")
        return dqp, dkv, dkr, dgate, jnp.zeros_like(cosv), jnp.zeros_like(sinv)

    f.defvjp(run, bwd)
    return f(qp, kv, kr, gate, cosv, sinv)


def _sq_loss_call(y, t, name):
    s, d = y.shape
    tm = _tile(s, 512, 8)

    def body(y_ref, t_ref, l_ref, e_ref):
        @pl.when(pl.program_id(0) == 0)
        def _():
            l_ref[...] = jnp.zeros_like(l_ref)

        e = y_ref[...] - t_ref[...]
        e_ref[...] = e * (1.0 / d)
        part = jnp.sum(jnp.sum(e * e, axis=1, keepdims=True), axis=0, keepdims=True)
        l_ref[...] += jnp.broadcast_to(part * (0.5 / d), l_ref.shape)

    row = pl.BlockSpec((tm, d), lambda i: (i, 0))
    return pl.pallas_call(
        body, name=name, grid=(s // tm,),
        in_specs=[row, row],
        out_specs=[pl.BlockSpec((8, 128), lambda i: (0, 0)), row],
        out_shape=[jax.ShapeDtypeStruct((8, 128), F32), jax.ShapeDtypeStruct((s, d), F32)],
        compiler_params=_cparams(("arbitrary",)),
    )(y, t)


@jax.custom_vjp
def sq_loss(y, t):
    return _sq_loss_call(y, t, "loss_fwd")[0][0, 0]


def _sq_loss_fwd(y, t):
    l, e = _sq_loss_call(y, t, "loss_fwd")
    return l[0, 0], e


def _sq_loss_bwd(e, g):
    return g * e, jnp.zeros_like(e)


sq_loss.defvjp(_sq_loss_fwd, _sq_loss_bwd)


def _cast_bf16(x, name):
    r, c = x.shape
    tb = _tile(r, 512, 16)

    def body(x_ref, o_ref):
        o_ref[...] = x_ref[...].astype(BF16)

    return pl.pallas_call(
        body, name=name, grid=(r // tb,),
        in_specs=[pl.BlockSpec((tb, c), lambda i: (i, 0))],
        out_specs=pl.BlockSpec((tb, c), lambda i: (i, 0)),
        out_shape=jax.ShapeDtypeStruct((r, c), BF16),
        compiler_params=_cparams(("parallel",)),
    )(x)


def _pair_sum(core_idx, g, recv, name):
    _, _, r, c = g.shape
    tb = _tile(r, 512, 16)

    def body(c_ref, g_ref, r_ref, o_ref):
        o_ref[...] = (g_ref[...] + r_ref[...]).astype(BF16)

    return pl.pallas_call(
        body, name=name,
        grid_spec=pltpu.PrefetchScalarGridSpec(
            num_scalar_prefetch=1, grid=(4, r // tb),
            in_specs=[pl.BlockSpec((None, None, tb, c), lambda q, i, c_ref: (c_ref[0], q, i, 0)),
                      pl.BlockSpec((None, tb, c), lambda q, i, c_ref: (q, i, 0))],
            out_specs=pl.BlockSpec((None, tb, c), lambda q, i, c_ref: (q, i, 0))),
        out_shape=jax.ShapeDtypeStruct((4, r, c), BF16),
        compiler_params=_cparams(("parallel", "parallel")),
    )(core_idx, g, recv)


def _adamw(w, parts, m, v, name):
    n, r, c = parts.shape
    tb = _tile(r, 256, 8)
    b1c = 1.0 - ADAM_B1 ** ADAM_STEP
    b2c = 1.0 - ADAM_B2 ** ADAM_STEP

    def body(w_ref, p_ref, m_ref, v_ref, g_ref, d_ref, nm_ref, nv_ref):
        g = p_ref[0].astype(F32)
        for k in range(1, n):
            g = g + p_ref[k].astype(F32)
        m_new = ADAM_B1 * m_ref[...] + (1.0 - ADAM_B1) * g
        v_new = ADAM_B2 * v_ref[...] + (1.0 - ADAM_B2) * (g * g)
        m_hat = m_new / b1c
        v_hat = v_new / b2c
        g_ref[...] = g
        d_ref[...] = -ADAM_LR * (m_hat / (jnp.sqrt(v_hat) + ADAM_EPS) + ADAM_WD * w_ref[...])
        nm_ref[...] = m_new
        nv_ref[...] = v_new

    row = pl.BlockSpec((tb, c), lambda i: (i, 0))
    return pl.pallas_call(
        body, name=name, grid=(r // tb,),
        in_specs=[row, pl.BlockSpec((n, tb, c), lambda i: (0, i, 0)), row, row],
        out_specs=[row] * 4,
        out_shape=[jax.ShapeDtypeStruct((r, c), F32)] * 4,
        compiler_params=_cparams(("parallel",)),
    )(w, parts, m, v)


ANY = pl.BlockSpec(memory_space=pl.ANY)


def _place():
    return lax.axis_index("x"), lax.axis_index("y"), lax.axis_index("c")


def _all_gather(shards, kinds, name):
    nm = len(shards)

    def body(*refs):
        start, finish = _gather_steps(refs[:nm], refs[nm:2 * nm], kinds, *refs[2 * nm:])
        start()
        finish()

    return pl.pallas_call(
        body, name=name,
        out_shape=_gather_out_shapes(shards, kinds),
        in_specs=[ANY] * nm, out_specs=[ANY] * nm,
        scratch_shapes=_gather_scratch(nm),
    )(*shards)


def _gather_out_shapes(shards, kinds):
    def full(sh, kind):
        a, b = sh.shape
        return {"row": (N_DEV * a, b), "col": (a, N_DEV * b), "stack": (N_DEV, a, b)}[kind]

    return [jax.ShapeDtypeStruct(full(sh, kd), sh.dtype) for sh, kd in zip(shards, kinds)]


def _gather_scratch(nm):
    return [pltpu.SemaphoreType.DMA((7 * nm,)), pltpu.SemaphoreType.DMA((7 * nm,)), pltpu.SemaphoreType.DMA((nm,))]


def _gather_steps(x_refs, out_refs, kinds, send_sems, recv_sems, local_sems):
    nm = len(x_refs)
    x, y, cc = _place()
    me, sibling = (x, y, cc), (x, y, 1 - cc)
    chips = [(1 - x, y), (x, 1 - y), (1 - x, 1 - y)]

    def slot(mi, px, py, pc):
        d = 4 * px + 2 * py + pc
        a, b = x_refs[mi].shape
        if kinds[mi] == "row":
            return out_refs[mi].at[pl.ds(pl.multiple_of(d * a, a), a), :]
        if kinds[mi] == "col":
            return out_refs[mi].at[:, pl.ds(pl.multiple_of(d * b, 128), b)]
        return out_refs[mi].at[d]

    def copy(mi, k, block, to, src=None):
        return pltpu.make_async_remote_copy(
            src_ref=slot(mi, *block) if src is None else src, dst_ref=slot(mi, *block),
            send_sem=send_sems.at[7 * mi + k], recv_sem=recv_sems.at[7 * mi + k],
            device_id=to, device_id_type=MESH)

    def own_copies():
        mine = [pltpu.make_async_copy(x_refs[mi], slot(mi, *me), local_sems.at[mi]) for mi in range(nm)]
        first = []
        for mi in range(nm):
            first.append(copy(mi, 0, me, sibling, src=x_refs[mi]))
            first += [copy(mi, 1 + j, me, (*chip, cc), src=x_refs[mi]) for j, chip in enumerate(chips)]
        return mine, first

    def start():
        mine, first = own_copies()
        for cp in mine + first:
            cp.start()

    def finish():
        mine, first = own_copies()
        passed = []
        for j, chip in enumerate(chips):
            for mi in range(nm):
                copy(mi, 1 + j, (*chip, cc), me).wait_recv()
                passed.append(copy(mi, 4 + j, (*chip, cc), sibling))
                passed[-1].start()
        for mi in range(nm):
            copy(mi, 0, sibling, me).wait_recv()
            for j, chip in enumerate(chips):
                copy(mi, 4 + j, (*chip, 1 - cc), me).wait_recv()
        for cp in first + passed:
            cp.wait_send()
        for cp in mine:
            cp.wait()

    return start, finish


def _pair_exchange(gs, name):
    nm = len(gs)

    def body(*refs):
        g_refs, recv_refs = refs[:nm], refs[nm:2 * nm]
        send_sems, recv_sems = refs[2 * nm:]
        x, y, cc = _place()
        copies = [pltpu.make_async_remote_copy(
            src_ref=g_refs[mi].at[1 - cc], dst_ref=recv_refs[mi],
            send_sem=send_sems.at[mi], recv_sem=recv_sems.at[mi], device_id=(x, y, 1 - cc), device_id_type=MESH)
            for mi in range(nm)]
        for cp in copies:
            cp.start()
        for cp in copies:
            cp.wait_recv()
        for cp in copies:
            cp.wait_send()

    return pl.pallas_call(
        body, name=name,
        out_shape=[jax.ShapeDtypeStruct(g.shape[1:], g.dtype) for g in gs],
        in_specs=[ANY] * nm, out_specs=[ANY] * nm,
        scratch_shapes=[pltpu.SemaphoreType.DMA((nm,)), pltpu.SemaphoreType.DMA((nm,))],
    )(*gs)


def _chip_exchange(parts, name):
    nm = len(parts)

    def body(*refs):
        start, finish = _exchange_steps(refs[:nm], refs[nm:2 * nm], *refs[2 * nm:])
        start()
        finish()

    return pl.pallas_call(
        body, name=name,
        out_shape=[jax.ShapeDtypeStruct(p.shape, p.dtype) for p in parts],
        in_specs=[ANY] * nm, out_specs=[ANY] * nm,
        scratch_shapes=_exchange_scratch(nm),
    )(*parts)


def _exchange_scratch(nm):
    return [pltpu.SemaphoreType.DMA((4 * nm,)), pltpu.SemaphoreType.DMA((4 * nm,)), pltpu.SemaphoreType.DMA((nm,))]


def _exchange_steps(p_refs, out_refs, send_sems, recv_sems, local_sems):
    nm = len(p_refs)
    x, y, cc = _place()
    mine = 2 * x + y
    others = [(1 - x, y), (x, 1 - y), (1 - x, 1 - y)]

    def own_copies():
        keeps = [pltpu.make_async_copy(p_refs[mi].at[mine], out_refs[mi].at[mine], local_sems.at[mi])
                 for mi in range(nm)]
        sends = []
        for px, py in others:
            q = 2 * px + py
            for mi in range(nm):
                sends.append(pltpu.make_async_remote_copy(
                    src_ref=p_refs[mi].at[q], dst_ref=out_refs[mi].at[mine],
                    send_sem=send_sems.at[4 * mi + q], recv_sem=recv_sems.at[4 * mi + mine],
                    device_id=(px, py, cc), device_id_type=MESH))
        return keeps, sends

    def start():
        keeps, sends = own_copies()
        for cp in keeps + sends:
            cp.start()

    def finish():
        keeps, sends = own_copies()
        for px, py in others:
            q = 2 * px + py
            for mi in range(nm):
                pltpu.make_async_remote_copy(
                    src_ref=p_refs[mi].at[q], dst_ref=out_refs[mi].at[q],
                    send_sem=send_sems.at[4 * mi + q], recv_sem=recv_sems.at[4 * mi + q],
                    device_id=(px, py, cc), device_id_type=MESH).wait_recv()
        for cp in sends:
            cp.wait_send()
        for cp in keeps:
            cp.wait()

    return start, finish


def _all_reduce_small(v, name):
    shape = v.shape

    def body(v_ref, out_ref, buf, send_sems, recv_sems):
        x, y, cc = _place()
        me = 4 * x + 2 * y + cc
        buf[me] = v_ref[...]
        flips = [(a, b, d) for a in (0, 1) for b in (0, 1) for d in (0, 1)][1:]
        copies = []
        for k, (a, b, d) in enumerate(flips):
            peer = (x ^ a, y ^ b, cc ^ d)
            copies.append(pltpu.make_async_remote_copy(
                src_ref=v_ref, dst_ref=buf.at[me],
                send_sem=send_sems.at[k], recv_sem=recv_sems.at[k], device_id=peer, device_id_type=MESH))
        for cp in copies:
            cp.start()
        for k, (a, b, d) in enumerate(flips):
            peer_id = 4 * (x ^ a) + 2 * (y ^ b) + (cc ^ d)
            pltpu.make_async_remote_copy(
                src_ref=v_ref, dst_ref=buf.at[peer_id],
                send_sem=send_sems.at[k], recv_sem=recv_sems.at[k], device_id=(x, y, cc), device_id_type=MESH
            ).wait_recv()
        for cp in copies:
            cp.wait_send()
        total = buf[0]
        for k in range(1, N_DEV):
            total = total + buf[k]
        out_ref[...] = total

    vm = pl.BlockSpec(memory_space=pltpu.VMEM)
    return pl.pallas_call(
        body, name=name,
        out_shape=jax.ShapeDtypeStruct(shape, F32),
        in_specs=[vm], out_specs=vm,
        scratch_shapes=[pltpu.VMEM((N_DEV,) + shape, F32), pltpu.SemaphoreType.DMA((7,)), pltpu.SemaphoreType.DMA((7,))],
    )(v)


def _gather_kind(name, shape):
    if name not in COL_SHARDED:
        return "row"
    return "col" if shape[1] % 128 == 0 else "stack"


def _slab_of(name, shape):
    if name not in COL_SHARDED:
        return ("row", shape[0])
    return ("col", shape[1]) if shape[1] % 128 == 0 else None


def _to_slabs(g, shape):
    kk, nn = shape
    return g.reshape(kk, 4, 2, nn).transpose(2, 1, 0, 3)


SMALL_ROWS = 8


def _pack_small(arrs):
    rows = [arrs[n] for n in SMALL[:5]]
    last = jnp.concatenate([arrs["q_norm1"], arrs["kv_norm1"], arrs["b_f2"]])
    rows.append(jnp.pad(last, (0, PACK_COLS - last.shape[0])))
    rows += [jnp.zeros((PACK_COLS,), F32)] * (SMALL_ROWS - len(rows))
    return jnp.stack(rows)


def _unpack_small(p):
    out = {n: p[k] for k, n in enumerate(SMALL[:5])}
    out["q_norm1"] = p[5, :MLA_Q_RANK]
    out["kv_norm1"] = p[5, MLA_Q_RANK:MLA_Q_RANK + MLA_KV_RANK]
    out["b_f2"] = p[5, MLA_Q_RANK + MLA_KV_RANK:MLA_Q_RANK + MLA_KV_RANK + N_HEADS]
    return out


N_QKV = 3 * D_INNER
N_MAIN = 4 * D_INNER


def _rope(x, pos):
    r = x.shape[-1]
    inv_freq = ROPE_BASE ** (-jnp.arange(0, r, 2, dtype=F32) / r)
    ang = pos.astype(F32)[:, None, None] * inv_freq
    cos, sin = jnp.cos(ang), jnp.sin(ang)
    x1, x2 = x[..., : r // 2], x[..., r // 2:]
    return jnp.concatenate([x1 * cos - x2 * sin, x1 * sin + x2 * cos], axis=-1)


def _forward_loss(carriers, small, x, wfull, late, late_grads, slabs, pos, target):
    s = x.shape[0]
    wfull = dict(wfull)
    carriers = dict(carriers)

    def out_proj(y, w_out, tag):
        return mm(y, wfull[w_out], carriers[w_out], slabs[w_out], name=tag + "_out")

    def sb_layer(x, ln, w_in, w_out, tag, gather=None, late=None):
        h = rmsnorm(x, small[ln], tag + "_ln")
        qkv, gate = in_proj(h, wfull[w_in], (carriers[w_in],), True, tag)
        y, gathered, late_carriers = sb_core(qkv, gate, tag, gather, late)
        return x + out_proj(y, w_out, tag), gathered, late_carriers

    names, shards, kinds, shapes = late
    car_keys, car_shapes, to_slabs = late_grads
    x, gathered, late_carriers = sb_layer(x, "ln0", "w_in0", "w_out0", "l0", (shards, kinds),
                                          (carriers["slots"], car_shapes, to_slabs))
    carriers.update(zip(car_keys, late_carriers))
    for n, kind, full in zip(names, kinds, gathered):
        wfull[n] = full.transpose(1, 0, 2).reshape(shapes[n][0], -1) if kind == "stack" else full

    h = rmsnorm(x, small["ln1"], "l1_ln")
    proj = mm(h, wfull["w_in1"], carriers["w_in1"], slabs["w_in1"], name="l1_in")
    i1, i2, i3 = MLA_Q_RANK, MLA_Q_RANK + MLA_KV_RANK, MLA_Q_RANK + MLA_KV_RANK + MLA_ROPE
    w_qb = jnp.pad(wfull["w_qb1"].reshape(MLA_Q_RANK, N_HEADS, MLA_NOPE + MLA_ROPE),
                   ((0, 0), (0, 0), (0, MLA_QK_PAD - MLA_NOPE - MLA_ROPE))).reshape(MLA_Q_RANK, N_HEADS * MLA_QK_PAD)
    qp = mm(rmsnorm(proj[:, :i1], small["q_norm1"], "l1_qn"), w_qb, carriers["w_qb1"], None, name="l1_qb")
    kv = mm(rmsnorm(proj[:, i1:i2], small["kv_norm1"], "l1_kvn"), wfull["w_kvb1"], carriers["w_kvb1"],
            slabs["w_kvb1"], BF16, name="l1_kvb")
    kr = jnp.pad(_rope(proj[:, i2:i3][:, None, :], pos)[:, 0, :], ((0, 0), (0, HEAD_DIM - MLA_ROPE)))
    inv_freq = ROPE_BASE ** (-jnp.arange(0, MLA_ROPE, 2, dtype=F32) / MLA_ROPE)
    ang = pos.astype(F32)[:, None] * inv_freq
    cos, sin = jnp.cos(ang), jnp.sin(ang)
    rest = HEAD_DIM - MLA_ROPE
    cosv = jnp.concatenate([cos, cos, jnp.ones((s, rest), F32)], axis=1)
    sinv = jnp.concatenate([-sin, sin, jnp.zeros((s, rest), F32)], axis=1)
    y = mla_core(qp, kv, kr, proj[:, i3:], cosv, sinv, "l1")
    x = x + out_proj(y, "w_out1", "l1")

    h = rmsnorm(x, small["ln2"], "l2_ln")
    qkv, gate = in_proj(h, wfull["w_in2"], (carriers["w_in2_qkv"], carriers["w_in2_gate"]), False, "l2")
    f_logit = mm(h, wfull["w_in2"][:, N_MAIN:], carriers["w_in2_f"], None, name="l2_f") + small["b_f2"]
    c = jnp.cumsum(jax.nn.log_sigmoid(f_logit), axis=0)
    x = x + out_proj(fox_core(qkv, gate, c, "l2"), "w_out2", "l2")

    x, _, _ = sb_layer(x, "ln3", "w_in3", "w_out3", "l3")
    return sq_loss(rmsnorm(x, small["final_norm"], "final_ln"), target)


def kernel(x, positions, ln0, w_in0, w_out0, ln1, w_in1, q_norm1, w_qb1, kv_norm1, w_kvb1, w_out1, ln2, w_in2, b_f2, w_out2, ln3, w_in3, w_out3, final_norm, loss_target, m_ln0, m_w_in0, m_w_out0, m_ln1, m_w_in1, m_q_norm1, m_w_qb1, m_kv_norm1, m_w_kvb1, m_w_out1, m_ln2, m_w_in2, m_b_f2, m_w_out2, m_ln3, m_w_in3, m_w_out3, m_final_norm, v_ln0, v_w_in0, v_w_out0, v_ln1, v_w_in1, v_q_norm1, v_w_qb1, v_kv_norm1, v_w_kvb1, v_w_out1, v_ln2, v_w_in2, v_b_f2, v_w_out2, v_ln3, v_w_in3, v_w_out3, v_final_norm):
    args = dict(locals())
    w = {n: args[n] for n in ALL_W}
    m = {n: args["m_" + n] for n in ALL_W}
    v = {n: args["v_" + n] for n in ALL_W}
    shapes = {n: w[n].shape for n in BIG}
    kinds = [_gather_kind(n, shapes[n]) for n in BIG]
    slabs = {n: _slab_of(n, shapes[n]) for n in BIG}

    first = BIG[:2]
    late_names = BIG[2:]
    gathered = _all_gather([w[n].astype(BF16) for n in first], kinds[:2], "gather_w")
    wfull = dict(zip(first, gathered))
    late = (late_names, [w[n].astype(BF16) for n in late_names], kinds[2:], shapes)

    d_model = shapes["w_in0"][0]
    car_shapes = {}
    for n in late_names:
        if n == "w_in2":
            car_shapes["w_in2_qkv"] = (d_model, N_QKV)
            car_shapes["w_in2_gate"] = (d_model, D_INNER)
            car_shapes["w_in2_f"] = (d_model, N_DEV * shapes[n][1] - N_MAIN)
        elif n == "w_qb1":
            car_shapes[n] = (MLA_Q_RANK, N_HEADS * MLA_QK_PAD)
        elif slabs[n] is None:
            car_shapes[n] = (shapes[n][0], N_DEV * shapes[n][1])
        else:
            car_shapes[n] = (2, 4) + shapes[n]
    car_keys = list(car_shapes)

    def to_slabs(cts):
        g_car = dict(zip(car_keys, cts))
        out = []
        for n in late_names:
            if n == "w_in2":
                g = jnp.concatenate([g_car["w_in2_qkv"], g_car["w_in2_gate"], g_car["w_in2_f"]], axis=1)
                out.append(_to_slabs(g, shapes[n]))
            elif n == "w_qb1":
                g = g_car[n].reshape(MLA_Q_RANK, N_HEADS, MLA_QK_PAD)[:, :, :MLA_NOPE + MLA_ROPE]
                out.append(_to_slabs(g.reshape(MLA_Q_RANK, -1), shapes[n]))
            elif slabs[n] is None:
                out.append(_to_slabs(g_car[n], shapes[n]))
            else:
                out.append(g_car[n])
        return out

    late_grads = (car_keys, [car_shapes[k] for k in car_keys], to_slabs)
    carriers = {n: jnp.zeros((2, 4) + shapes[n], F32) for n in first}
    carriers["slots"] = [jnp.zeros((4,) + shapes[n], BF16) for n in late_names]
    small = {n: w[n] for n in SMALL}

    def local_loss(carriers, small, x_seq):
        return _forward_loss(carriers, small, x_seq, wfull, late, late_grads, slabs, positions[0], loss_target[0])

    loss_local, (g_car, g_small, g_x) = jax.value_and_grad(local_loss, argnums=(0, 1, 2))(carriers, small, x[0])
    loss = lax.psum(loss_local, ("x", "y", "c"))

    core_idx = lax.axis_index("c").astype(jnp.int32).reshape(1)
    from_sibling = _pair_exchange([g_car[n] for n in first], "pair_exchange")
    chip_part = [_pair_sum(core_idx, g_car[n], r, "pair_sum_" + n) for n, r in zip(first, from_sibling)]
    by_chip = dict(zip(first, _chip_exchange(chip_part, "chip_exchange")))
    by_chip.update(zip(late_names, g_car["slots"]))
    big = [{}, {}, {}, {}]
    for n in BIG:
        for k, t in enumerate(_adamw(w[n], by_chip[n], m[n], v[n], "adamw_" + n)):
            big[k][n] = t

    g_small_sum = _all_reduce_small(_pack_small(g_small), "reduce_small")
    sm = _adamw(_pack_small(w), g_small_sum[None], _pack_small(m), _pack_small(v), "adamw_small")
    small_out = [_unpack_small(t) for t in sm]

    outs = [loss, g_x[None]]
    for k in range(4):
        outs += [small_out[k][n] if n in small_out[k] else big[k][n] for n in ALL_W]
    return tuple(outs)
```

```python
import jax
import jax.numpy as jnp
from jax import lax
from jax.experimental import pallas as pl
from jax.experimental.pallas import tpu as pltpu

F32 = jnp.float32
BF16 = jnp.bfloat16
MESH = pl.DeviceIdType.MESH

N_DEV = 8
N_HEADS = 16
HEAD_DIM = 128
D_INNER = N_HEADS * HEAD_DIM
MLA_Q_RANK = 256
MLA_KV_RANK = 128
MLA_NOPE = 128
MLA_ROPE = 64
MLA_QK_PAD = 256
MLA_CHUNK = 64
ROPE_BASE = 10000.0
EPS = 1e-6
NEG = -1e30
SB_CUT = 104.0

ADAM_LR = 0.001
ADAM_B1 = 0.9
ADAM_B2 = 0.999
ADAM_EPS = 1e-08
ADAM_WD = 0.01
ADAM_STEP = 10

PACK_COLS = 1024
VMEM_LIMIT = 56 * 1024 * 1024

BIG = ["w_in0", "w_out0", "w_in1", "w_qb1", "w_kvb1", "w_out1", "w_in2", "w_out2", "w_in3", "w_out3"]
COL_SHARDED = {"w_in0", "w_in1", "w_qb1", "w_kvb1", "w_in2", "w_in3"}
SMALL = ["ln0", "ln1", "ln2", "ln3", "final_norm", "q_norm1", "kv_norm1", "b_f2"]
ALL_W = ["ln0", "w_in0", "w_out0", "ln1", "w_in1", "q_norm1", "w_qb1", "kv_norm1", "w_kvb1", "w_out1",
         "ln2", "w_in2", "b_f2", "w_out2", "ln3", "w_in3", "w_out3", "final_norm"]


def _cparams(sem=None):
    return pltpu.CompilerParams(dimension_semantics=sem, vmem_limit_bytes=VMEM_LIMIT)


def _tile(dim, cap, align):
    if dim <= cap:
        return dim
    t = (cap // align) * align
    while t >= align:
        if dim % t == 0:
            return t
        t -= align
    return dim


def _dot(a, b, dims):
    return lax.dot_general(a, b, (dims, ((), ())), preferred_element_type=F32)


def _dot_nn(a, b):
    return _dot(a, b, ((1,), (0,)))


def _dot_nt(a, b):
    return _dot(a, b, ((1,), (1,)))


def _dot_tn(a, b):
    return _dot(a, b, ((0,), (0,)))


def _transpose_bf16(x):
    return x.astype(F32).T.astype(BF16)


def _matmul(a, b, mode, col0=0, n_cols=None, out_dtype=F32, name="mm", res=None):
    if mode == "nn":
        m, r = a.shape
        n = n_cols or b.shape[1]
        tn, tr = _tile(n, 1024, 128), _tile(r, 1024, 128)
        tm = _tile(m, 1024 if tn <= 1024 else 512, 8)
        c0 = col0 // tn
        a_spec = pl.BlockSpec((tm, tr), lambda i, j, k: (i, k))
        b_spec = pl.BlockSpec((tr, tn), lambda i, j, k: (k, j + c0))
        dims = ((1,), (0,))
        assert col0 % tn == 0
    elif mode == "nt":
        m, r = a.shape
        n = b.shape[0]
        tn, tr = _tile(n, 1024, 128), _tile(r, 1024, 128)
        tm = _tile(m, 1024 if tr <= 1024 else 512, 8)
        c0 = col0 // tr
        a_spec = pl.BlockSpec((tm, tr), lambda i, j, k: (i, k))
        b_spec = pl.BlockSpec((tn, tr), lambda i, j, k: (j, k + c0))
        dims = ((1,), (1,))
        assert col0 % tr == 0
    else:
        r, m = a.shape
        n = b.shape[1]
        tm, tn, tr = _tile(m, 1024, 128), _tile(n, 1024, 128), _tile(r, 512, 16)
        a_spec = pl.BlockSpec((tr, tm), lambda i, j, k: (k, i))
        b_spec = pl.BlockSpec((tr, tn), lambda i, j, k: (k, j))
        dims = ((0,), (0,))
    nr = r // tr
    out_spec = pl.BlockSpec((tm, tn), lambda i, j, k: (i, j))

    def body(*refs):
        a_ref, b_ref, o_ref, acc_ref = refs[0], refs[1], refs[-2], refs[-1]
        k = pl.program_id(2)

        @pl.when(k == 0)
        def _():
            acc_ref[...] = jnp.zeros_like(acc_ref)

        acc_ref[...] += _dot(a_ref[...].astype(BF16), b_ref[...].astype(BF16), dims)

        @pl.when(k == nr - 1)
        def _():
            out = acc_ref[...] if res is None else acc_ref[...] + refs[2][...]
            o_ref[...] = out.astype(out_dtype)

    return pl.pallas_call(
        body,
        name=name,
        grid=(m // tm, n // tn, nr),
        in_specs=[a_spec, b_spec] + ([] if res is None else [out_spec]),
        out_specs=out_spec,
        out_shape=jax.ShapeDtypeStruct((m, n), out_dtype),
        scratch_shapes=[pltpu.VMEM((tm, tn), F32)],
        compiler_params=_cparams(("parallel", "parallel", "arbitrary")),
    )(*([a, b] if res is None else [a, b, res]))


def _transpose_cast(a, name):
    r, m = a.shape
    tr = _tile(r, 512, 128)

    def body(a_ref, o_ref):
        o_ref[...] = a_ref[...].astype(F32).T.astype(BF16)

    return pl.pallas_call(
        body, name=name, grid=(r // tr,),
        in_specs=[pl.BlockSpec((tr, m), lambda i: (i, 0))],
        out_specs=pl.BlockSpec((m, tr), lambda i: (0, i)),
        out_shape=jax.ShapeDtypeStruct((m, r), BF16),
        compiler_params=_cparams(("parallel",)),
    )(a)


def _matmul_dw(a, b1, b2, slab, name):
    m, r = a.shape
    n1 = b1.shape[1]
    n = n1 + (b2.shape[1] if b2 is not None else 0)
    tr = _tile(r, 1024, 128)
    if slab is None:
        tm, tn = _tile(m, 1024, 128), _tile(n1, 1024, 128)
        out_spec = pl.BlockSpec((tm, tn), lambda i, j, k: (i, j))
        out_shape = (m, n)
    elif slab[0] == "col":
        tm, tn = _tile(m, 1024, 128), slab[1]
        out_spec = pl.BlockSpec((None, None, tm, tn), lambda i, j, k: (j % 2, j // 2, i, 0))
        out_shape = (2, 4, m, tn)
        assert n == N_DEV * tn
    else:
        tm, tn = slab[1], _tile(n, 1024, 128)
        out_spec = pl.BlockSpec((None, None, tm, tn), lambda i, j, k: (i % 2, i // 2, 0, j))
        out_shape = (2, 4, tm, n)
        assert m == N_DEV * tm
    assert n1 % tn == 0 and n % tn == 0
    if tn > 1024:
        tr = _tile(r, 256, 128)
    n1b = n1 // tn
    nr = r // tr

    def body(*refs):
        a_ref, b_refs, o_ref, acc_ref = refs[0], refs[1:-2], refs[-2], refs[-1]
        j = pl.program_id(1)
        k = pl.program_id(2)

        @pl.when(k == 0)
        def _():
            acc_ref[...] = jnp.zeros_like(acc_ref)

        at = a_ref[...]
        if b2 is None:
            acc_ref[...] += _dot_nn(at, b_refs[0][...].astype(BF16))
        else:
            @pl.when(j < n1b)
            def _():
                acc_ref[...] += _dot_nn(at, b_refs[0][...].astype(BF16))

            @pl.when(j >= n1b)
            def _():
                acc_ref[...] += _dot_nn(at, b_refs[1][...].astype(BF16))

        @pl.when(k == nr - 1)
        def _():
            o_ref[...] = acc_ref[...]

    in_specs = [pl.BlockSpec((tm, tr), lambda i, j, k: (i, k))]
    args = [a, b1]
    if b2 is None:
        in_specs.append(pl.BlockSpec((tr, tn), lambda i, j, k: (k, j)))
    else:
        in_specs.append(pl.BlockSpec((tr, tn), lambda i, j, k: (jnp.where(j < n1b, k, nr - 1), jnp.minimum(j, n1b - 1))))
        in_specs.append(pl.BlockSpec((tr, tn), lambda i, j, k: (jnp.where(j < n1b, 0, k), jnp.maximum(j - n1b, 0))))
        args.append(b2)
    return pl.pallas_call(
        body, name=name, grid=(m // tm, n // tn, nr),
        in_specs=in_specs, out_specs=out_spec,
        out_shape=jax.ShapeDtypeStruct(out_shape, F32),
        scratch_shapes=[pltpu.VMEM((tm, tn), F32)],
        compiler_params=_cparams(("parallel", "parallel", "arbitrary")),
    )(*args)


def mm(a, w, carrier, slab=None, out_dtype=F32, name="mm", skip=None):
    @jax.custom_vjp
    def f(a, w, carrier, *skip):
        return _matmul(a, w, "nn", 0, None, out_dtype, name + "_fwd", *skip)

    def fwd(a, w, carrier, *skip):
        return _matmul(a, w, "nn", 0, None, out_dtype, name + "_fwd", *skip), (a, w)

    def bwd(res, g):
        a, w = res
        da = _matmul(g, w, "nt", 0, None, F32, name + "_dx")
        dw = _matmul_dw(_transpose_cast(a, name + "_t"), g, None, slab, name + "_dw")
        return (da, jnp.zeros_like(w), dw) + ((g,) if skip is not None else ())

    f.defvjp(fwd, bwd)
    return f(a, w, carrier, *(() if skip is None else (skip,)))


def in_proj(h, w, carriers, slab, name):
    def run(h, w):
        return (_matmul(h, w, "nn", 0, N_QKV, BF16, name + "_qkv"),
                _matmul(h, w, "nn", N_QKV, D_INNER, F32, name + "_g"))

    @jax.custom_vjp
    def f(h, w, *cars):
        return run(h, w)

    def fwd(h, w, *cars):
        return run(h, w), (h, w)

    def bwd(res, g):
        h, w = res
        g_qkv, g_gate = g
        dh = (_matmul(g_qkv, w, "nt", 0, None, F32, name + "_qkv_dx")
              + _matmul(g_gate, w, "nt", N_QKV, None, F32, name + "_g_dx"))
        h_t = _transpose_cast(h, name + "_t")
        if slab:
            dws = (_matmul_dw(h_t, g_qkv, g_gate, ("col", PACK_COLS), name + "_dw"),)
        else:
            dws = (_matmul_dw(h_t, g_qkv, None, None, name + "_qkv_dw"),
                   _matmul_dw(h_t, g_gate, None, None, name + "_g_dw"))
        return (dh, jnp.zeros_like(w)) + dws

    f.defvjp(fwd, bwd)
    return f(h, w, *carriers)


def _rms_fwd(x, g, name):
    s, d = x.shape
    tm = _tile(s, 512, 8)

    def body(x_ref, g_ref, y_ref):
        x = x_ref[...]
        r = lax.rsqrt(jnp.mean(x * x, axis=-1, keepdims=True) + EPS)
        y_ref[...] = x * r * g_ref[...]

    return pl.pallas_call(
        body, name=name, grid=(s // tm,),
        in_specs=[pl.BlockSpec((tm, d), lambda i: (i, 0)), pl.BlockSpec((1, d), lambda i: (0, 0))],
        out_specs=pl.BlockSpec((tm, d), lambda i: (i, 0)),
        out_shape=jax.ShapeDtypeStruct((s, d), F32),
        compiler_params=_cparams(("parallel",)),
    )(x, g)


def _rms_bwd(x, g, dy, name, dskip=None):
    s, d = x.shape
    tm = _tile(s, 512, 8)

    def body(*refs):
        x_ref, g_ref, dy_ref, dx_ref, dg_ref = refs[0], refs[1], refs[2], refs[-2], refs[-1]

        @pl.when(pl.program_id(0) == 0)
        def _():
            dg_ref[...] = jnp.zeros_like(dg_ref)

        x = x_ref[...]
        dy = dy_ref[...]
        r = lax.rsqrt(jnp.mean(x * x, axis=-1, keepdims=True) + EPS)
        xh = x * r
        dg_ref[...] += jnp.sum(dy * xh, axis=0, keepdims=True)
        dxh = dy * g_ref[...]
        dx = r * (dxh - xh * jnp.mean(dxh * xh, axis=-1, keepdims=True))
        dx_ref[...] = dx if dskip is None else dx + refs[3][...]

    row = pl.BlockSpec((tm, d), lambda i: (i, 0))
    vec = pl.BlockSpec((1, d), lambda i: (0, 0))
    return pl.pallas_call(
        body, name=name, grid=(s // tm,),
        in_specs=[row, vec, row] + ([] if dskip is None else [row]),
        out_specs=[row, vec],
        out_shape=[jax.ShapeDtypeStruct((s, d), F32), jax.ShapeDtypeStruct((1, d), F32)],
        compiler_params=_cparams(("arbitrary",)),
    )(*([x, g, dy] if dskip is None else [x, g, dy, dskip]))


def rmsnorm(x, g, name="rms", skip=False):
    def run(x, g):
        y = _rms_fwd(x, g.reshape(1, -1), name + "_fwd")
        return (y, x) if skip else y

    @jax.custom_vjp
    def f(x, g):
        return run(x, g)

    def fwd(x, g):
        return run(x, g), (x, g)

    def bwd(res, ct):
        x, g = res
        dy, dskip = ct if skip else (ct, None)
        dx, dg = _rms_bwd(x, g.reshape(1, -1), dy, name + "_bwd", dskip)
        return dx, dg.reshape(-1)

    f.defvjp(fwd, bwd)
    return f(x, g)


def _gate_fwd(o, gate, name):
    s = o.shape[0]
    tm = 256

    def body(o_ref, g_ref, y_ref):
        g = g_ref[...]
        y_ref[...] = o_ref[...] * (g / (1.0 + jnp.exp(-g)))

    row = pl.BlockSpec((tm, D_INNER), lambda i: (i, 0))
    return pl.pallas_call(
        body, name=name, grid=(s // tm,),
        in_specs=[row, row], out_specs=row,
        out_shape=jax.ShapeDtypeStruct((s, D_INNER), F32),
        compiler_params=_cparams(("parallel",)),
    )(o, gate)


def _gate_bwd(dy, o, gate, name):
    s = o.shape[0]
    tm = 256

    def body(dy_ref, o_ref, g_ref, do_ref, dg_ref, dl_ref):
        g = g_ref[...]
        o = o_ref[...]
        dy = dy_ref[...]
        sg = 1.0 / (1.0 + jnp.exp(-g))
        do = dy * (g * sg)
        do_ref[...] = do.astype(BF16)
        dg_ref[...] = dy * o * (sg * (1.0 + g * (1.0 - sg)))
        prod = do * o
        for h in range(N_HEADS):
            dl_ref[h] = jnp.sum(prod[:, h * HEAD_DIM:(h + 1) * HEAD_DIM], axis=1, keepdims=True)

    row = pl.BlockSpec((tm, D_INNER), lambda i: (i, 0))
    return pl.pallas_call(
        body, name=name, grid=(s // tm,),
        in_specs=[row, row, row],
        out_specs=[row, row, pl.BlockSpec((N_HEADS, tm, 1), lambda i: (0, i, 0))],
        out_shape=[jax.ShapeDtypeStruct((s, D_INNER), BF16), jax.ShapeDtypeStruct((s, D_INNER), F32),
                   jax.ShapeDtypeStruct((N_HEADS, s, 1), F32)],
        compiler_params=_cparams(("parallel",)),
    )(dy, o, gate)


SB_BLK = 256


def _softplus(z):
    return jnp.maximum(z, 0.0) + jnp.log(1.0 + jnp.exp(-jnp.abs(z)))


def _tri_sum(x, tri):
    hi = x.astype(BF16)
    lo = (x - hi.astype(F32)).astype(BF16)
    return _dot_nn(hi, tri) + _dot_nn(lo, tri)


SB_WIN = 2 * SB_BLK


def _sb_tri(kind):
    row = lax.broadcasted_iota(jnp.int32, (SB_BLK, SB_BLK), 0)
    col = lax.broadcasted_iota(jnp.int32, (SB_BLK, SB_BLK), 1)
    return ((row >= col) if kind == "suffix" else (row <= col)).astype(BF16)


def _sb_bounds(i, t):
    hi = (i + 1) * SB_BLK - t * SB_WIN
    return hi, pl.multiple_of(jnp.maximum(hi - SB_WIN, 0), SB_BLK)


def _sb_mask(i, hi, start):
    row = lax.broadcasted_iota(jnp.int32, (SB_BLK, SB_WIN), 0) + i * SB_BLK
    col = lax.broadcasted_iota(jnp.int32, (SB_BLK, SB_WIN), 1) + start
    return jnp.logical_and(col < row, col < hi)


def _sb_window(q, kwin, mask, a_run, tri_suffix, scale):
    b = SB_BLK
    z = _dot_nt(q, kwin) * scale
    sp = _softplus(z)
    ls = jnp.where(mask, -sp, 0.0)
    ls_l, ls_r = ls[:, :b], ls[:, b:]
    suffix = jnp.concatenate([_tri_sum(ls_l, tri_suffix) + jnp.sum(ls_r, axis=1, keepdims=True),
                              _tri_sum(ls_r, tri_suffix)], axis=1)
    w = jnp.where(mask, jnp.exp(z + suffix + a_run), 0.0)
    return z, sp, ls, w


def _sb_fwd(qkv, name, gather=None):
    s = qkv.shape[0]
    b = SB_BLK
    nq = s // b
    scale = HEAD_DIM ** -0.5
    assert s >= SB_WIN
    shards, kinds = gather if gather else ((), ())
    nm = len(shards)

    def body(*refs):
        q_ref, k_ref, v_ref = refs[:3]
        o_ref = refs[3 + nm]
        i = pl.program_id(1)
        if nm:
            start, finish = _gather_steps(refs[3:3 + nm], refs[4 + nm:4 + 2 * nm], kinds, *refs[4 + 2 * nm:])
            first_step = jnp.logical_and(pl.program_id(0) == 0, i == 0)
            last_step = jnp.logical_and(pl.program_id(0) == N_HEADS - 1, i == nq - 1)
            pl.when(first_step)(start)
        q = q_ref[...]
        tri_suffix = _sb_tri("suffix")

        def cond(c):
            t, a_run, _ = c
            return jnp.logical_and((i + 1) * b - t * SB_WIN > 0, jnp.max(a_run) > -SB_CUT)

        def step(c):
            t, a_run, acc = c
            hi, start = _sb_bounds(i, t)
            _, _, ls, w = _sb_window(q, k_ref[pl.ds(start, SB_WIN), :], _sb_mask(i, hi, start), a_run,
                                     tri_suffix, scale)
            acc = acc + _dot_nn(w.astype(BF16), v_ref[pl.ds(start, SB_WIN), :])
            return t + 1, a_run + jnp.sum(ls, axis=1, keepdims=True), acc

        _, _, acc = lax.while_loop(cond, step, (0, jnp.zeros((b, 1), F32), jnp.zeros((b, HEAD_DIM), F32)))
        o_ref[...] = acc
        if nm:
            pl.when(last_step)(finish)

    out = pl.pallas_call(
        body, name=name, grid=(N_HEADS, nq),
        in_specs=[pl.BlockSpec((b, HEAD_DIM), lambda h, i: (i, h)),
                  pl.BlockSpec((s, HEAD_DIM), lambda h, i: (0, N_HEADS + h)),
                  pl.BlockSpec((s, HEAD_DIM), lambda h, i: (0, 2 * N_HEADS + h))] + [ANY] * nm,
        out_specs=[pl.BlockSpec((b, HEAD_DIM), lambda h, i: (i, h))] + [ANY] * nm,
        out_shape=[jax.ShapeDtypeStruct((s, D_INNER), F32)] + _gather_out_shapes(shards, kinds),
        scratch_shapes=_gather_scratch(nm) if nm else [],
        compiler_params=_cparams(("arbitrary", "arbitrary")),
    )(qkv, qkv, qkv, *shards)
    return out[0], list(out[1:])


def _sb_bwd(qkv, do, name, exchange=()):
    s = qkv.shape[0]
    b = SB_BLK
    nq = s // b
    n_win = -(-s // SB_WIN) + 1
    scale = HEAD_DIM ** -0.5
    assert s >= SB_WIN
    nm = len(exchange)

    def body(*refs):
        q_ref, k_ref, v_ref, do_ref = refs[:4]
        dq_ref, dk_ref, dv_ref = refs[4 + nm:7 + nm]
        dkt_s, dvt_s, g_buf, sig_buf = refs[7 + 2 * nm:11 + 2 * nm]
        i = pl.program_id(1)
        if nm:
            start_x, finish_x = _exchange_steps(refs[4:4 + nm], refs[7 + nm:7 + 2 * nm], *refs[11 + 2 * nm:])
            pl.when(jnp.logical_and(pl.program_id(0) == 0, i == 0))(start_x)

        @pl.when(i == 0)
        def _():
            dkt_s[...] = jnp.zeros_like(dkt_s)
            dvt_s[...] = jnp.zeros_like(dvt_s)

        q = q_ref[...]
        dob = do_ref[...]
        q_t = _transpose_bf16(q)
        do_t = _transpose_bf16(dob)
        tri_suffix = _sb_tri("suffix")
        tri_prefix = _sb_tri("prefix")

        def add_halves(acc_ref, start, upd):
            blk = start // b
            acc_ref[blk] += upd[:, :b]
            acc_ref[blk + 1] += upd[:, b:]

        def cond(c):
            t, a_run = c
            return jnp.logical_and((i + 1) * b - t * SB_WIN > 0, jnp.max(a_run) > -SB_CUT)

        def sweep(c):
            t, a_run = c
            hi, start = _sb_bounds(i, t)
            z, sp, ls, w = _sb_window(q, k_ref[pl.ds(start, SB_WIN), :], _sb_mask(i, hi, start), a_run,
                                      tri_suffix, scale)
            g_buf[t] = w * _dot_nt(dob, v_ref[pl.ds(start, SB_WIN), :])
            sig_buf[t] = jnp.exp(z - sp)
            add_halves(dvt_s, start, _dot_nn(do_t, w.astype(BF16)))
            return t + 1, a_run + jnp.sum(ls, axis=1, keepdims=True)

        n_steps, _ = lax.while_loop(cond, sweep, (0, jnp.zeros((b, 1), F32)))

        def back(u, c):
            g_run, dq = c
            t = n_steps - 1 - u
            hi, start = _sb_bounds(i, t)
            g = g_buf[t]
            g_l, g_r = g[:, :b], g[:, b:]
            g_incl = g_run + jnp.concatenate(
                [_tri_sum(g_l, tri_prefix),
                 _tri_sum(g_r, tri_prefix) + jnp.sum(g_l, axis=1, keepdims=True)], axis=1)
            dz = jnp.where(_sb_mask(i, hi, start), (g - sig_buf[t] * g_incl) * scale, 0.0).astype(BF16)
            add_halves(dkt_s, start, _dot_nn(q_t, dz))
            return g_run + jnp.sum(g, axis=1, keepdims=True), dq + _dot_nn(dz, k_ref[pl.ds(start, SB_WIN), :])

        _, dq = lax.fori_loop(0, n_steps, back, (jnp.zeros((b, 1), F32), jnp.zeros((b, HEAD_DIM), F32)))
        dq_ref[...] = dq.astype(BF16)

        @pl.when(i == nq - 1)
        def _():
            for jb in range(nq):
                dk_ref[jb * b:(jb + 1) * b, :] = dkt_s[jb].T.astype(BF16)
                dv_ref[jb * b:(jb + 1) * b, :] = dvt_s[jb].T.astype(BF16)

        if nm:
            pl.when(jnp.logical_and(pl.program_id(0) == N_HEADS - 1, i == nq - 1))(finish_x)

    blk = pl.BlockSpec((b, HEAD_DIM), lambda h, i: (i, h))
    head = pl.BlockSpec((s, HEAD_DIM), lambda h, i: (0, h))
    out = pl.pallas_call(
        body, name=name, grid=(N_HEADS, nq),
        in_specs=[blk,
                  pl.BlockSpec((s, HEAD_DIM), lambda h, i: (0, N_HEADS + h)),
                  pl.BlockSpec((s, HEAD_DIM), lambda h, i: (0, 2 * N_HEADS + h)),
                  blk] + [ANY] * nm,
        out_specs=[blk, head, head] + [ANY] * nm,
        out_shape=[jax.ShapeDtypeStruct((s, D_INNER), BF16)] * 3
        + [jax.ShapeDtypeStruct(p.shape, p.dtype) for p in exchange],
        scratch_shapes=[pltpu.VMEM((nq, HEAD_DIM, b), F32), pltpu.VMEM((nq, HEAD_DIM, b), F32),
                        pltpu.VMEM((n_win, b, SB_WIN), F32), pltpu.VMEM((n_win, b, SB_WIN), F32)]
        + (_exchange_scratch(nm) if nm else []),
        compiler_params=_cparams(("arbitrary", "arbitrary")),
    )(qkv, qkv, qkv, do, *exchange)
    return out[0], out[1], out[2], list(out[3:])


def sb_core(qkv, gate, name, gather=None, late=None):
    shards, kinds = gather if gather else ((), ())
    slots, carrier_shapes, to_slabs = late if late else ((), (), None)
    n_sh = len(shards)

    def run(qkv, gate, *shards):
        o, gathered = _sb_fwd(qkv, name + "_fwd", (shards, kinds) if shards else None)
        carriers = [jnp.zeros(sh, F32) for sh in carrier_shapes]
        return (_gate_fwd(o, gate, name + "_gate"), *gathered, *carriers), o

    @jax.custom_vjp
    def f(qkv, gate, *extra):
        return run(qkv, gate, *extra[:n_sh])[0]

    def fwd(qkv, gate, *extra):
        outs, o = run(qkv, gate, *extra[:n_sh])
        return outs, (qkv, gate, o)

    def bwd(res, cts):
        qkv, gate, o = res
        do, dgate, _ = _gate_bwd(cts[0], o, gate, name + "_gate_bwd")
        parts = []
        if slots:
            g_slabs = to_slabs(cts[1 + n_sh:])
            core_idx = lax.axis_index("c").astype(jnp.int32).reshape(1)
            from_sibling = _pair_exchange(g_slabs, name + "_pair_exchange")
            parts = [_pair_sum(core_idx, g, r, name + "_pair_sum%d" % k)
                     for k, (g, r) in enumerate(zip(g_slabs, from_sibling))]
        dq, dk, dv, by_chip = _sb_bwd(qkv, do, name + "_bwd", parts)
        zeros = tuple(jnp.zeros(sh.shape, sh.dtype) for sh in shards)
        return (jnp.concatenate([dq, dk, dv], axis=1), dgate) + zeros + tuple(by_chip)

    f.defvjp(fwd, bwd)
    outs = f(qkv, gate, *shards, *slots)
    return outs[0], list(outs[1:1 + n_sh]), list(outs[1 + n_sh:])


SM_FWD_BLK = 256
SM_BLK = 512


SM_FWD_KEYS = 1024
SM_BWD_KEYS = 512


def _sm_mask(i, jw, rows, keys, chunk_shift):
    row = lax.broadcasted_iota(jnp.int32, (rows, keys), 0) + i * rows
    col = lax.broadcasted_iota(jnp.int32, (rows, keys), 1) + jw * keys
    return (col >> chunk_shift) <= (row >> chunk_shift)


def _sm_fwd(qa, ka, va, ccol, crow, dqk, qo, ko, vo, chunk_shift, scale, name):
    s = qa.shape[0]
    b = min(SM_FWD_BLK, s)
    keys = min(SM_FWD_KEYS, s)
    per = keys // b
    nq = s // b
    has_bias = ccol is not None

    def body(*refs):
        if has_bias:
            q_ref, k_ref, v_ref, cc_ref, cr_ref, o_ref, lse_ref, m_s, l_s, acc_s = refs
        else:
            q_ref, k_ref, v_ref, o_ref, lse_ref, m_s, l_s, acc_s = refs
        i = pl.program_id(1)
        q = q_ref[...]
        m_s[...] = jnp.full_like(m_s, NEG)
        l_s[...] = jnp.zeros_like(l_s)
        acc_s[...] = jnp.zeros_like(acc_s)

        def sweep(jw, masked):
            off = pl.multiple_of(jw * keys, keys)
            z = _dot_nt(q, k_ref[pl.ds(off, keys), :]) * scale
            if has_bias:
                z = z + cc_ref[...] - cr_ref[jw]
            if masked:
                z = jnp.where(_sm_mask(i, jw, b, keys, chunk_shift), z, NEG)
            m_old = m_s[...]
            m_new = jnp.maximum(m_old, jnp.max(z, axis=1, keepdims=True))
            alpha = jnp.exp(m_old - m_new)
            p = jnp.exp(z - m_new)
            l_s[...] = alpha * l_s[...] + jnp.sum(p, axis=1, keepdims=True)
            acc_s[...] = alpha * acc_s[...] + _dot_nn(p.astype(BF16), v_ref[pl.ds(off, keys), :])
            m_s[...] = m_new

        def full(jw, carry):
            sweep(jw, False)
            return carry

        lax.fori_loop(0, i // per, full, 0)
        sweep(i // per, True)
        o_ref[...] = acc_s[...] / l_s[...]
        lse_ref[...] = m_s[...] + jnp.log(l_s[...])

    in_specs = [pl.BlockSpec((b, dqk), lambda h, i: (i, qo + h)),
                pl.BlockSpec((s, dqk), lambda h, i: (0, ko + h)),
                pl.BlockSpec((s, HEAD_DIM), lambda h, i: (0, vo + h))]
    args = [qa, ka, va]
    if has_bias:
        in_specs += [pl.BlockSpec((None, b, 1), lambda h, i: (h, i, 0)),
                     pl.BlockSpec((None, s // keys, 1, keys), lambda h, i: (h, 0, 0, 0))]
        args += [ccol, crow]
    return pl.pallas_call(
        body, name=name, grid=(N_HEADS, nq),
        in_specs=in_specs,
        out_specs=[pl.BlockSpec((b, HEAD_DIM), lambda h, i: (i, h)),
                   pl.BlockSpec((None, b, 1), lambda h, i: (h, i, 0))],
        out_shape=[jax.ShapeDtypeStruct((s, D_INNER), F32), jax.ShapeDtypeStruct((N_HEADS, s, 1), F32)],
        scratch_shapes=[pltpu.VMEM((b, 1), F32), pltpu.VMEM((b, 1), F32), pltpu.VMEM((b, HEAD_DIM), F32)],
        compiler_params=_cparams(("parallel", "arbitrary")),
    )(*args)


def _sm_bwd(qa, ka, va, do, lse, delta, ccol, crow, dqk, qo, ko, vo, chunk_shift, scale, grad_dtype, name):
    s = qa.shape[0]
    b = SM_BLK
    keys = min(SM_BWD_KEYS, s)
    per = keys // b
    nq = s // b
    nk = s // keys
    has_bias = ccol is not None

    def body(*refs):
        if has_bias:
            (q_ref, k_ref, v_ref, do_ref, lse_ref, dl_ref, cc_ref, cr_ref,
             dq_ref, dk_ref, dv_ref, dc_ref, dr_ref, dq_s, dkt_s, dvt_s, dc_s, dr_s) = refs
        else:
            (q_ref, k_ref, v_ref, do_ref, lse_ref, dl_ref,
             dq_ref, dk_ref, dv_ref, dq_s, dkt_s, dvt_s) = refs
        i = pl.program_id(1)

        @pl.when(i == 0)
        def _():
            dkt_s[...] = jnp.zeros_like(dkt_s)
            dvt_s[...] = jnp.zeros_like(dvt_s)
            if has_bias:
                dc_s[...] = jnp.zeros_like(dc_s)

        q = q_ref[...]
        dob = do_ref[...]
        q_t = _transpose_bf16(q)
        do_t = _transpose_bf16(dob)
        lse = lse_ref[...]
        delta = dl_ref[...]
        dq_s[...] = jnp.zeros_like(dq_s)
        if has_bias:
            dr_s[...] = jnp.zeros_like(dr_s)

        def sweep(jw, masked):
            off = pl.multiple_of(jw * keys, keys)
            kb = k_ref[pl.ds(off, keys), :]
            z = _dot_nt(q, kb) * scale
            if has_bias:
                z = z + cc_ref[...] - cr_ref[jw]
            p = jnp.exp(z - lse)
            if masked:
                p = jnp.where(_sm_mask(i, jw, b, keys, chunk_shift), p, 0.0)
            dvt_s[jw] += _dot_nn(do_t, p.astype(BF16))
            dz = p * (_dot_nt(dob, v_ref[pl.ds(off, keys), :]) - delta)
            if has_bias:
                dc_s[jw] += jnp.sum(dz, axis=0, keepdims=True)
                dr_s[...] += jnp.sum(dz, axis=1, keepdims=True)
            dzs = (dz * scale).astype(BF16)
            dkt_s[jw] += _dot_nn(q_t, dzs)
            dq_s[...] += _dot_nn(dzs, kb)

        def full(jw, carry):
            sweep(jw, False)
            return carry

        lax.fori_loop(0, i // per, full, 0)
        sweep(i // per, True)
        dq_ref[...] = dq_s[...].astype(grad_dtype)
        if has_bias:
            dr_ref[...] = dr_s[...]

        @pl.when(i == nq - 1)
        def _():
            for jw in range(nk):
                dk_ref[jw * keys:(jw + 1) * keys, :] = dkt_s[jw].T.astype(grad_dtype)
                dv_ref[jw * keys:(jw + 1) * keys, :] = dvt_s[jw].T.astype(grad_dtype)
            if has_bias:
                dc_ref[...] = dc_s[...]

    vec = pl.BlockSpec((None, b, 1), lambda h, i: (h, i, 0))
    in_specs = [pl.BlockSpec((b, dqk), lambda h, i: (i, qo + h)),
                pl.BlockSpec((s, dqk), lambda h, i: (0, ko + h)),
                pl.BlockSpec((s, HEAD_DIM), lambda h, i: (0, vo + h)),
                pl.BlockSpec((b, HEAD_DIM), lambda h, i: (i, h)),
                vec, vec]
    args = [qa, ka, va, do, lse, delta]
    out_specs = [pl.BlockSpec((b, dqk), lambda h, i: (i, h)),
                 pl.BlockSpec((s, dqk), lambda h, i: (0, h)),
                 pl.BlockSpec((s, HEAD_DIM), lambda h, i: (0, h))]
    out_shape = [jax.ShapeDtypeStruct((s, N_HEADS * dqk), grad_dtype),
                 jax.ShapeDtypeStruct((s, N_HEADS * dqk), grad_dtype),
                 jax.ShapeDtypeStruct((s, D_INNER), grad_dtype)]
    scratch = [pltpu.VMEM((b, dqk), F32), pltpu.VMEM((nk, dqk, keys), F32), pltpu.VMEM((nk, HEAD_DIM, keys), F32)]
    if has_bias:
        key_vec = pl.BlockSpec((None, nk, 1, keys), lambda h, i: (h, 0, 0, 0))
        in_specs += [vec, key_vec]
        args += [ccol, crow]
        out_specs += [key_vec, vec]
        out_shape += [jax.ShapeDtypeStruct((N_HEADS, nk, 1, keys), F32), jax.ShapeDtypeStruct((N_HEADS, s, 1), F32)]
        scratch += [pltpu.VMEM((nk, 1, keys), F32), pltpu.VMEM((b, 1), F32)]
    return pl.pallas_call(
        body, name=name, grid=(N_HEADS, nq),
        in_specs=in_specs, out_specs=out_specs, out_shape=out_shape, scratch_shapes=scratch,
        compiler_params=_cparams(("arbitrary", "arbitrary")),
    )(*args)


def fox_core(qkv, gate, c, name):
    s = qkv.shape[0]
    scale = HEAD_DIM ** -0.5
    cfg = dict(dqk=HEAD_DIM, qo=0, ko=N_HEADS, vo=2 * N_HEADS, chunk_shift=0, scale=scale)

    def layouts(c, keys):
        ct = c.T
        keys = min(keys, s)
        return ct.reshape(N_HEADS, s, 1), ct.reshape(N_HEADS, s // keys, 1, keys)

    def run(qkv, gate, c):
        ccol, crow = layouts(c, SM_FWD_KEYS)
        o, lse = _sm_fwd(qkv, qkv, qkv, ccol, crow, name=name + "_fwd", **cfg)
        return _gate_fwd(o, gate, name + "_gate"), o, lse

    @jax.custom_vjp
    def f(qkv, gate, c):
        return run(qkv, gate, c)[0]

    def fwd(qkv, gate, c):
        y, o, lse = run(qkv, gate, c)
        return y, (qkv, gate, c, o, lse)

    def bwd(res, dy):
        qkv, gate, c, o, lse = res
        ccol, crow = layouts(c, SM_BWD_KEYS)
        do, dgate, delta = _gate_bwd(dy, o, gate, name + "_gate_bwd")
        dq, dk, dv, colsum, rowsum = _sm_bwd(qkv, qkv, qkv, do, lse, delta, ccol, crow,
                                             grad_dtype=BF16, name=name + "_bwd", **cfg)
        dc = (rowsum.reshape(N_HEADS, s) - colsum.reshape(N_HEADS, s)).T
        return jnp.concatenate([dq, dk, dv], axis=1), dgate, dc

    f.defvjp(fwd, bwd)
    return f(qkv, gate, c)


def _mla_rope(x, cosv, sinv, out_dtype, name):
    s, width = x.shape
    tm = 256
    half = MLA_ROPE // 2

    def body(x_ref, c_ref, s_ref, o_ref):
        c = c_ref[...]
        sn = s_ref[...]
        lane = lax.broadcasted_iota(jnp.int32, (tm, HEAD_DIM), 1)
        for h in range(N_HEADS):
            lo = h * MLA_QK_PAD
            o_ref[:, lo:lo + HEAD_DIM] = x_ref[:, lo:lo + HEAD_DIM].astype(out_dtype)
            g = x_ref[:, lo + HEAD_DIM:lo + MLA_QK_PAD].astype(F32)
            swapped = jnp.where(lane < half, pltpu.roll(g, HEAD_DIM - half, 1), pltpu.roll(g, half, 1))
            o_ref[:, lo + HEAD_DIM:lo + MLA_QK_PAD] = (g * c + swapped * sn).astype(out_dtype)

    row = pl.BlockSpec((tm, width), lambda i: (i, 0))
    tab = pl.BlockSpec((tm, HEAD_DIM), lambda i: (i, 0))
    return pl.pallas_call(
        body, name=name, grid=(s // tm,),
        in_specs=[row, tab, tab], out_specs=row,
        out_shape=jax.ShapeDtypeStruct((s, width), out_dtype),
        compiler_params=_cparams(("parallel",)),
    )(x, cosv, sinv)


def _mla_fwd(qc, kv, kr, chunk_shift, scale, name):
    s = qc.shape[0]
    b = min(SM_FWD_BLK, s)
    keys = min(SM_FWD_KEYS, s)
    per = keys // b
    nq = s // b

    def body(q_ref, kn_ref, v_ref, kr_ref, o_ref, lse_ref, m_s, l_s, acc_s, kc_s):
        i = pl.program_id(1)

        @pl.when(i == 0)
        def _():
            kc_s[:, :HEAD_DIM] = kn_ref[...]
            kc_s[:, HEAD_DIM:] = kr_ref[...]

        q = q_ref[...]
        m_s[...] = jnp.full_like(m_s, NEG)
        l_s[...] = jnp.zeros_like(l_s)
        acc_s[...] = jnp.zeros_like(acc_s)

        def sweep(jw, masked):
            off = pl.multiple_of(jw * keys, keys)
            z = _dot_nt(q, kc_s[pl.ds(off, keys), :]) * scale
            if masked:
                z = jnp.where(_sm_mask(i, jw, b, keys, chunk_shift), z, NEG)
            m_old = m_s[...]
            m_new = jnp.maximum(m_old, jnp.max(z, axis=1, keepdims=True))
            alpha = jnp.exp(m_old - m_new)
            p = jnp.exp(z - m_new)
            l_s[...] = alpha * l_s[...] + jnp.sum(p, axis=1, keepdims=True)
            acc_s[...] = alpha * acc_s[...] + _dot_nn(p.astype(BF16), v_ref[pl.ds(off, keys), :])
            m_s[...] = m_new

        def full(jw, carry):
            sweep(jw, False)
            return carry

        lax.fori_loop(0, i // per, full, 0)
        sweep(i // per, True)
        o_ref[...] = acc_s[...] / l_s[...]
        lse_ref[...] = m_s[...] + jnp.log(l_s[...])

    return pl.pallas_call(
        body, name=name, grid=(N_HEADS, nq),
        in_specs=[pl.BlockSpec((b, MLA_QK_PAD), lambda h, i: (i, h)),
                  pl.BlockSpec((s, HEAD_DIM), lambda h, i: (0, 2 * h)),
                  pl.BlockSpec((s, HEAD_DIM), lambda h, i: (0, 2 * h + 1)),
                  pl.BlockSpec((s, HEAD_DIM), lambda h, i: (0, 0))],
        out_specs=[pl.BlockSpec((b, HEAD_DIM), lambda h, i: (i, h)),
                   pl.BlockSpec((None, b, 1), lambda h, i: (h, i, 0))],
        out_shape=[jax.ShapeDtypeStruct((s, D_INNER), F32), jax.ShapeDtypeStruct((N_HEADS, s, 1), F32)],
        scratch_shapes=[pltpu.VMEM((b, 1), F32), pltpu.VMEM((b, 1), F32), pltpu.VMEM((b, HEAD_DIM), F32),
                        pltpu.VMEM((s, MLA_QK_PAD), BF16)],
        compiler_params=_cparams(("arbitrary", "arbitrary")),
    )(qc, kv, kv, kr)


def _mla_bwd(qc, kv, kr, do, lse, delta, chunk_shift, scale, name):
    s = qc.shape[0]
    b = min(SM_BLK, s)
    keys = min(SM_BWD_KEYS, s)
    per = keys // b
    nq = s // b
    nk = s // keys

    def body(q_ref, kn_ref, v_ref, kr_ref, do_ref, lse_ref, dl_ref, dq_ref, dkv_ref, dkr_ref,
             dq_s, dkt_s, dvt_s, dkrt_s, kc_s):
        h = pl.program_id(0)
        i = pl.program_id(1)

        @pl.when(jnp.logical_and(h == 0, i == 0))
        def _():
            dkrt_s[...] = jnp.zeros_like(dkrt_s)

        @pl.when(i == 0)
        def _():
            dkt_s[...] = jnp.zeros_like(dkt_s)
            dvt_s[...] = jnp.zeros_like(dvt_s)
            kc_s[:, :HEAD_DIM] = kn_ref[...]
            kc_s[:, HEAD_DIM:] = kr_ref[...]

        q = q_ref[...]
        dob = do_ref[...]
        q_t = _transpose_bf16(q)
        do_t = _transpose_bf16(dob)
        lse_i = lse_ref[...]
        delta_i = dl_ref[...]
        dq_s[...] = jnp.zeros_like(dq_s)

        def sweep(jw, masked):
            off = pl.multiple_of(jw * keys, keys)
            kc = kc_s[pl.ds(off, keys), :]
            p = jnp.exp(_dot_nt(q, kc) * scale - lse_i)
            if masked:
                p = jnp.where(_sm_mask(i, jw, b, keys, chunk_shift), p, 0.0)
            dvt_s[jw] += _dot_nn(do_t, p.astype(BF16))
            dz = p * (_dot_nt(dob, v_ref[pl.ds(off, keys), :]) - delta_i)
            dzs = (dz * scale).astype(BF16)
            dkc_t = _dot_nn(q_t, dzs)
            dkt_s[jw] += dkc_t[:HEAD_DIM]
            dkrt_s[jw] += dkc_t[HEAD_DIM:]
            dq_s[...] += _dot_nn(dzs, kc)

        def full(jw, carry):
            sweep(jw, False)
            return carry

        lax.fori_loop(0, i // per, full, 0)
        sweep(i // per, True)
        dq_ref[...] = dq_s[...]

        @pl.when(i == nq - 1)
        def _():
            for jw in range(nk):
                dkv_ref[jw * keys:(jw + 1) * keys, :HEAD_DIM] = dkt_s[jw].T.astype(BF16)
                dkv_ref[jw * keys:(jw + 1) * keys, HEAD_DIM:] = dvt_s[jw].T.astype(BF16)

        @pl.when(jnp.logical_and(h == N_HEADS - 1, i == nq - 1))
        def _():
            for jw in range(nk):
                dkr_ref[jw * keys:(jw + 1) * keys, :] = dkrt_s[jw].T

    vec = pl.BlockSpec((None, b, 1), lambda h, i: (h, i, 0))
    acc = pltpu.VMEM((nk, HEAD_DIM, keys), F32)
    return pl.pallas_call(
        body, name=name, grid=(N_HEADS, nq),
        in_specs=[pl.BlockSpec((b, MLA_QK_PAD), lambda h, i: (i, h)),
                  pl.BlockSpec((s, HEAD_DIM), lambda h, i: (0, 2 * h)),
                  pl.BlockSpec((s, HEAD_DIM), lambda h, i: (0, 2 * h + 1)),
                  pl.BlockSpec((s, HEAD_DIM), lambda h, i: (0, 0)),
                  pl.BlockSpec((b, HEAD_DIM), lambda h, i: (i, h)),
                  vec, vec],
        out_specs=[pl.BlockSpec((b, MLA_QK_PAD), lambda h, i: (i, h)),
                   pl.BlockSpec((s, MLA_QK_PAD), lambda h, i: (0, h)),
                   pl.BlockSpec((s, HEAD_DIM), lambda h, i: (0, 0))],
        out_shape=[jax.ShapeDtypeStruct((s, N_HEADS * MLA_QK_PAD), F32),
                   jax.ShapeDtypeStruct((s, N_HEADS * MLA_QK_PAD), BF16),
                   jax.ShapeDtypeStruct((s, HEAD_DIM), F32)],
        scratch_shapes=[pltpu.VMEM((b, MLA_QK_PAD), F32), acc, acc, acc, pltpu.VMEM((s, MLA_QK_PAD), BF16)],
        compiler_params=_cparams(("arbitrary", "arbitrary")),
    )(qc, kv, kv, kr, do, lse, delta)


def mla_core(qp, kv, kr, gate, cosv, sinv, name):
    scale = (MLA_NOPE + MLA_ROPE) ** -0.5
    shift = MLA_CHUNK.bit_length() - 1

    def run(qp, kv, kr, gate, cosv, sinv):
        qc = _mla_rope(qp, cosv, sinv, BF16, name + "_rope")
        krb = kr.astype(BF16)
        o, lse = _mla_fwd(qc, kv, krb, shift, scale, name + "_fwd")
        return _gate_fwd(o, gate, name + "_gate"), (qc, kv, krb, gate, o, lse, cosv, sinv)

    @jax.custom_vjp
    def f(qp, kv, kr, gate, cosv, sinv):
        return run(qp, kv, kr, gate, cosv, sinv)[0]

    def bwd(res, dy):
        qc, kv, krb, gate, o, lse, cosv, sinv = res
        do, dgate, delta = _gate_bwd(dy, o, gate, name + "_gate_bwd")
        dqc, dkv, dkr = _mla_bwd(qc, kv, krb, do, lse, delta, shift, scale, name + "_bwd")
        dqp = _mla_rope(dqc, cosv, -sinv, F32, name + "_rope_bwd")
        return dqp, dkv, dkr, dgate, jnp.zeros_like(cosv), jnp.zeros_like(sinv)

    f.defvjp(run, bwd)
    return f(qp, kv, kr, gate, cosv, sinv)


def _sq_loss_call(y, t, name):
    s, d = y.shape
    tm = _tile(s, 512, 8)

    def body(y_ref, t_ref, l_ref, e_ref):
        @pl.when(pl.program_id(0) == 0)
        def _():
            l_ref[...] = jnp.zeros_like(l_ref)

        e = y_ref[...] - t_ref[...]
        e_ref[...] = e * (1.0 / d)
        part = jnp.sum(jnp.sum(e * e, axis=1, keepdims=True), axis=0, keepdims=True)
        l_ref[...] += jnp.broadcast_to(part * (0.5 / d), l_ref.shape)

    row = pl.BlockSpec((tm, d), lambda i: (i, 0))
    return pl.pallas_call(
        body, name=name, grid=(s // tm,),
        in_specs=[row, row],
        out_specs=[pl.BlockSpec((8, 128), lambda i: (0, 0)), row],
        out_shape=[jax.ShapeDtypeStruct((8, 128), F32), jax.ShapeDtypeStruct((s, d), F32)],
        compiler_params=_cparams(("arbitrary",)),
    )(y, t)


@jax.custom_vjp
def sq_loss(y, t):
    return _sq_loss_call(y, t, "loss_fwd")[0][0, 0]


def _sq_loss_fwd(y, t):
    l, e = _sq_loss_call(y, t, "loss_fwd")
    return l[0, 0], e


def _sq_loss_bwd(e, g):
    return g * e, jnp.zeros_like(e)


sq_loss.defvjp(_sq_loss_fwd, _sq_loss_bwd)


def _cast_bf16(x, name):
    r, c = x.shape
    tb = _tile(r, 512, 16)

    def body(x_ref, o_ref):
        o_ref[...] = x_ref[...].astype(BF16)

    return pl.pallas_call(
        body, name=name, grid=(r // tb,),
        in_specs=[pl.BlockSpec((tb, c), lambda i: (i, 0))],
        out_specs=pl.BlockSpec((tb, c), lambda i: (i, 0)),
        out_shape=jax.ShapeDtypeStruct((r, c), BF16),
        compiler_params=_cparams(("parallel",)),
    )(x)


def _pair_sum(core_idx, g, recv, name):
    _, _, r, c = g.shape
    tb = _tile(r, 512, 16)

    def body(c_ref, g_ref, r_ref, o_ref):
        o_ref[...] = (g_ref[...] + r_ref[...]).astype(BF16)

    return pl.pallas_call(
        body, name=name,
        grid_spec=pltpu.PrefetchScalarGridSpec(
            num_scalar_prefetch=1, grid=(4, r // tb),
            in_specs=[pl.BlockSpec((None, None, tb, c), lambda q, i, c_ref: (c_ref[0], q, i, 0)),
                      pl.BlockSpec((None, tb, c), lambda q, i, c_ref: (q, i, 0))],
            out_specs=pl.BlockSpec((None, tb, c), lambda q, i, c_ref: (q, i, 0))),
        out_shape=jax.ShapeDtypeStruct((4, r, c), BF16),
        compiler_params=_cparams(("parallel", "parallel")),
    )(core_idx, g, recv)


def _adamw(w, parts, m, v, name):
    n, r, c = parts.shape
    tb = _tile(r, 256, 8)
    b1c = 1.0 - ADAM_B1 ** ADAM_STEP
    b2c = 1.0 - ADAM_B2 ** ADAM_STEP

    def body(w_ref, p_ref, m_ref, v_ref, g_ref, d_ref, nm_ref, nv_ref):
        g = p_ref[0].astype(F32)
        for k in range(1, n):
            g = g + p_ref[k].astype(F32)
        m_new = ADAM_B1 * m_ref[...] + (1.0 - ADAM_B1) * g
        v_new = ADAM_B2 * v_ref[...] + (1.0 - ADAM_B2) * (g * g)
        m_hat = m_new / b1c
        v_hat = v_new / b2c
        g_ref[...] = g
        d_ref[...] = -ADAM_LR * (m_hat / (jnp.sqrt(v_hat) + ADAM_EPS) + ADAM_WD * w_ref[...])
        nm_ref[...] = m_new
        nv_ref[...] = v_new

    row = pl.BlockSpec((tb, c), lambda i: (i, 0))
    return pl.pallas_call(
        body, name=name, grid=(r // tb,),
        in_specs=[row, pl.BlockSpec((n, tb, c), lambda i: (0, i, 0)), row, row],
        out_specs=[row] * 4,
        out_shape=[jax.ShapeDtypeStruct((r, c), F32)] * 4,
        compiler_params=_cparams(("parallel",)),
    )(w, parts, m, v)


ANY = pl.BlockSpec(memory_space=pl.ANY)


def _place():
    return lax.axis_index("x"), lax.axis_index("y"), lax.axis_index("c")


def _all_gather(shards, kinds, name):
    nm = len(shards)

    def body(*refs):
        start, finish = _gather_steps(refs[:nm], refs[nm:2 * nm], kinds, *refs[2 * nm:])
        start()
        finish()

    return pl.pallas_call(
        body, name=name,
        out_shape=_gather_out_shapes(shards, kinds),
        in_specs=[ANY] * nm, out_specs=[ANY] * nm,
        scratch_shapes=_gather_scratch(nm),
    )(*shards)


def _gather_out_shapes(shards, kinds):
    def full(sh, kind):
        a, b = sh.shape
        return {"row": (N_DEV * a, b), "col": (a, N_DEV * b), "stack": (N_DEV, a, b)}[kind]

    return [jax.ShapeDtypeStruct(full(sh, kd), sh.dtype) for sh, kd in zip(shards, kinds)]


def _gather_scratch(nm):
    return [pltpu.SemaphoreType.DMA((7 * nm,)), pltpu.SemaphoreType.DMA((7 * nm,)), pltpu.SemaphoreType.DMA((nm,))]


def _gather_steps(x_refs, out_refs, kinds, send_sems, recv_sems, local_sems):
    nm = len(x_refs)
    x, y, cc = _place()
    me, sibling = (x, y, cc), (x, y, 1 - cc)
    chips = [(1 - x, y), (x, 1 - y), (1 - x, 1 - y)]

    def slot(mi, px, py, pc):
        d = 4 * px + 2 * py + pc
        a, b = x_refs[mi].shape
        if kinds[mi] == "row":
            return out_refs[mi].at[pl.ds(pl.multiple_of(d * a, a), a), :]
        if kinds[mi] == "col":
            return out_refs[mi].at[:, pl.ds(pl.multiple_of(d * b, 128), b)]
        return out_refs[mi].at[d]

    def copy(mi, k, block, to, src=None):
        return pltpu.make_async_remote_copy(
            src_ref=slot(mi, *block) if src is None else src, dst_ref=slot(mi, *block),
            send_sem=send_sems.at[7 * mi + k], recv_sem=recv_sems.at[7 * mi + k],
            device_id=to, device_id_type=MESH)

    def own_copies():
        mine = [pltpu.make_async_copy(x_refs[mi], slot(mi, *me), local_sems.at[mi]) for mi in range(nm)]
        first = []
        for mi in range(nm):
            first.append(copy(mi, 0, me, sibling, src=x_refs[mi]))
            first += [copy(mi, 1 + j, me, (*chip, cc), src=x_refs[mi]) for j, chip in enumerate(chips)]
        return mine, first

    def start():
        mine, first = own_copies()
        for cp in mine + first:
            cp.start()

    def finish():
        mine, first = own_copies()
        passed = []
        for j, chip in enumerate(chips):
            for mi in range(nm):
                copy(mi, 1 + j, (*chip, cc), me).wait_recv()
                passed.append(copy(mi, 4 + j, (*chip, cc), sibling))
                passed[-1].start()
        for mi in range(nm):
            copy(mi, 0, sibling, me).wait_recv()
            for j, chip in enumerate(chips):
                copy(mi, 4 + j, (*chip, 1 - cc), me).wait_recv()
        for cp in first + passed:
            cp.wait_send()
        for cp in mine:
            cp.wait()

    return start, finish


def _pair_exchange(gs, name):
    nm = len(gs)

    def body(*refs):
        g_refs, recv_refs = refs[:nm], refs[nm:2 * nm]
        send_sems, recv_sems = refs[2 * nm:]
        x, y, cc = _place()
        copies = [pltpu.make_async_remote_copy(
            src_ref=g_refs[mi].at[1 - cc], dst_ref=recv_refs[mi],
            send_sem=send_sems.at[mi], recv_sem=recv_sems.at[mi], device_id=(x, y, 1 - cc), device_id_type=MESH)
            for mi in range(nm)]
        for cp in copies:
            cp.start()
        for cp in copies:
            cp.wait_recv()
        for cp in copies:
            cp.wait_send()

    return pl.pallas_call(
        body, name=name,
        out_shape=[jax.ShapeDtypeStruct(g.shape[1:], g.dtype) for g in gs],
        in_specs=[ANY] * nm, out_specs=[ANY] * nm,
        scratch_shapes=[pltpu.SemaphoreType.DMA((nm,)), pltpu.SemaphoreType.DMA((nm,))],
    )(*gs)


def _chip_exchange(parts, name):
    nm = len(parts)

    def body(*refs):
        start, finish = _exchange_steps(refs[:nm], refs[nm:2 * nm], *refs[2 * nm:])
        start()
        finish()

    return pl.pallas_call(
        body, name=name,
        out_shape=[jax.ShapeDtypeStruct(p.shape, p.dtype) for p in parts],
        in_specs=[ANY] * nm, out_specs=[ANY] * nm,
        scratch_shapes=_exchange_scratch(nm),
    )(*parts)


def _exchange_scratch(nm):
    return [pltpu.SemaphoreType.DMA((4 * nm,)), pltpu.SemaphoreType.DMA((4 * nm,)), pltpu.SemaphoreType.DMA((nm,))]


def _exchange_steps(p_refs, out_refs, send_sems, recv_sems, local_sems):
    nm = len(p_refs)
    x, y, cc = _place()
    mine = 2 * x + y
    others = [(1 - x, y), (x, 1 - y), (1 - x, 1 - y)]

    def own_copies():
        keeps = [pltpu.make_async_copy(p_refs[mi].at[mine], out_refs[mi].at[mine], local_sems.at[mi])
                 for mi in range(nm)]
        sends = []
        for px, py in others:
            q = 2 * px + py
            for mi in range(nm):
                sends.append(pltpu.make_async_remote_copy(
                    src_ref=p_refs[mi].at[q], dst_ref=out_refs[mi].at[mine],
                    send_sem=send_sems.at[4 * mi + q], recv_sem=recv_sems.at[4 * mi + mine],
                    device_id=(px, py, cc), device_id_type=MESH))
        return keeps, sends

    def start():
        keeps, sends = own_copies()
        for cp in keeps + sends:
            cp.start()

    def finish():
        keeps, sends = own_copies()
        for px, py in others:
            q = 2 * px + py
            for mi in range(nm):
                pltpu.make_async_remote_copy(
                    src_ref=p_refs[mi].at[q], dst_ref=out_refs[mi].at[q],
                    send_sem=send_sems.at[4 * mi + q], recv_sem=recv_sems.at[4 * mi + q],
                    device_id=(px, py, cc), device_id_type=MESH).wait_recv()
        for cp in sends:
            cp.wait_send()
        for cp in keeps:
            cp.wait()

    return start, finish


def _all_reduce_small(v, name):
    shape = v.shape

    def body(v_ref, out_ref, buf, send_sems, recv_sems):
        x, y, cc = _place()
        me = 4 * x + 2 * y + cc
        buf[me] = v_ref[...]
        flips = [(a, b, d) for a in (0, 1) for b in (0, 1) for d in (0, 1)][1:]
        copies = []
        for k, (a, b, d) in enumerate(flips):
            peer = (x ^ a, y ^ b, cc ^ d)
            copies.append(pltpu.make_async_remote_copy(
                src_ref=v_ref, dst_ref=buf.at[me],
                send_sem=send_sems.at[k], recv_sem=recv_sems.at[k], device_id=peer, device_id_type=MESH))
        for cp in copies:
            cp.start()
        for k, (a, b, d) in enumerate(flips):
            peer_id = 4 * (x ^ a) + 2 * (y ^ b) + (cc ^ d)
            pltpu.make_async_remote_copy(
                src_ref=v_ref, dst_ref=buf.at[peer_id],
                send_sem=send_sems.at[k], recv_sem=recv_sems.at[k], device_id=(x, y, cc), device_id_type=MESH
            ).wait_recv()
        for cp in copies:
            cp.wait_send()
        total = buf[0]
        for k in range(1, N_DEV):
            total = total + buf[k]
        out_ref[...] = total

    vm = pl.BlockSpec(memory_space=pltpu.VMEM)
    return pl.pallas_call(
        body, name=name,
        out_shape=jax.ShapeDtypeStruct(shape, F32),
        in_specs=[vm], out_specs=vm,
        scratch_shapes=[pltpu.VMEM((N_DEV,) + shape, F32), pltpu.SemaphoreType.DMA((7,)), pltpu.SemaphoreType.DMA((7,))],
    )(v)


def _gather_kind(name, shape):
    if name not in COL_SHARDED:
        return "row"
    return "col" if shape[1] % 128 == 0 else "stack"


def _slab_of(name, shape):
    if name not in COL_SHARDED:
        return ("row", shape[0])
    return ("col", shape[1]) if shape[1] % 128 == 0 else None


def _to_slabs(g, shape):
    kk, nn = shape
    return g.reshape(kk, 4, 2, nn).transpose(2, 1, 0, 3)


SMALL_ROWS = 8


def _pack_small(arrs):
    rows = [arrs[n] for n in SMALL[:5]]
    last = jnp.concatenate([arrs["q_norm1"], arrs["kv_norm1"], arrs["b_f2"]])
    rows.append(jnp.pad(last, (0, PACK_COLS - last.shape[0])))
    rows += [jnp.zeros((PACK_COLS,), F32)] * (SMALL_ROWS - len(rows))
    return jnp.stack(rows)


def _unpack_small(p):
    out = {n: p[k] for k, n in enumerate(SMALL[:5])}
    out["q_norm1"] = p[5, :MLA_Q_RANK]
    out["kv_norm1"] = p[5, MLA_Q_RANK:MLA_Q_RANK + MLA_KV_RANK]
    out["b_f2"] = p[5, MLA_Q_RANK + MLA_KV_RANK:MLA_Q_RANK + MLA_KV_RANK + N_HEADS]
    return out


N_QKV = 3 * D_INNER
N_MAIN = 4 * D_INNER


def _rope(x, pos):
    r = x.shape[-1]
    inv_freq = ROPE_BASE ** (-jnp.arange(0, r, 2, dtype=F32) / r)
    ang = pos.astype(F32)[:, None, None] * inv_freq
    cos, sin = jnp.cos(ang), jnp.sin(ang)
    x1, x2 = x[..., : r // 2], x[..., r // 2:]
    return jnp.concatenate([x1 * cos - x2 * sin, x1 * sin + x2 * cos], axis=-1)


def _forward_loss(carriers, small, x, wfull, late, late_grads, slabs, pos, target):
    s = x.shape[0]
    wfull = dict(wfull)
    carriers = dict(carriers)

    def out_proj(y, w_out, tag, x_skip):
        return mm(y, wfull[w_out], carriers[w_out], slabs[w_out], name=tag + "_out", skip=x_skip)

    def sb_layer(x, ln, w_in, w_out, tag, gather=None, late=None):
        h, x = rmsnorm(x, small[ln], tag + "_ln", skip=True)
        qkv, gate = in_proj(h, wfull[w_in], (carriers[w_in],), True, tag)
        y, gathered, late_carriers = sb_core(qkv, gate, tag, gather, late)
        return out_proj(y, w_out, tag, x), gathered, late_carriers

    names, shards, kinds, shapes = late
    car_keys, car_shapes, to_slabs = late_grads
    x, gathered, late_carriers = sb_layer(x, "ln0", "w_in0", "w_out0", "l0", (shards, kinds),
                                          (carriers["slots"], car_shapes, to_slabs))
    carriers.update(zip(car_keys, late_carriers))
    for n, kind, full in zip(names, kinds, gathered):
        wfull[n] = full.transpose(1, 0, 2).reshape(shapes[n][0], -1) if kind == "stack" else full

    h, x = rmsnorm(x, small["ln1"], "l1_ln", skip=True)
    proj = mm(h, wfull["w_in1"], carriers["w_in1"], slabs["w_in1"], name="l1_in")
    i1, i2, i3 = MLA_Q_RANK, MLA_Q_RANK + MLA_KV_RANK, MLA_Q_RANK + MLA_KV_RANK + MLA_ROPE
    w_qb = jnp.pad(wfull["w_qb1"].reshape(MLA_Q_RANK, N_HEADS, MLA_NOPE + MLA_ROPE),
                   ((0, 0), (0, 0), (0, MLA_QK_PAD - MLA_NOPE - MLA_ROPE))).reshape(MLA_Q_RANK, N_HEADS * MLA_QK_PAD)
    qp = mm(rmsnorm(proj[:, :i1], small["q_norm1"], "l1_qn"), w_qb, carriers["w_qb1"], None, name="l1_qb")
    kv = mm(rmsnorm(proj[:, i1:i2], small["kv_norm1"], "l1_kvn"), wfull["w_kvb1"], carriers["w_kvb1"],
            slabs["w_kvb1"], BF16, name="l1_kvb")
    kr = jnp.pad(_rope(proj[:, i2:i3][:, None, :], pos)[:, 0, :], ((0, 0), (0, HEAD_DIM - MLA_ROPE)))
    inv_freq = ROPE_BASE ** (-jnp.arange(0, MLA_ROPE, 2, dtype=F32) / MLA_ROPE)
    ang = pos.astype(F32)[:, None] * inv_freq
    cos, sin = jnp.cos(ang), jnp.sin(ang)
    rest = HEAD_DIM - MLA_ROPE
    cosv = jnp.concatenate([cos, cos, jnp.ones((s, rest), F32)], axis=1)
    sinv = jnp.concatenate([-sin, sin, jnp.zeros((s, rest), F32)], axis=1)
    y = mla_core(qp, kv, kr, proj[:, i3:], cosv, sinv, "l1")
    x = out_proj(y, "w_out1", "l1", x)

    h, x = rmsnorm(x, small["ln2"], "l2_ln", skip=True)
    qkv, gate = in_proj(h, wfull["w_in2"], (carriers["w_in2_qkv"], carriers["w_in2_gate"]), False, "l2")
    f_logit = mm(h, wfull["w_in2"][:, N_MAIN:], carriers["w_in2_f"], None, name="l2_f") + small["b_f2"]
    c = jnp.cumsum(jax.nn.log_sigmoid(f_logit), axis=0)
    x = out_proj(fox_core(qkv, gate, c, "l2"), "w_out2", "l2", x)

    x, _, _ = sb_layer(x, "ln3", "w_in3", "w_out3", "l3")
    return sq_loss(rmsnorm(x, small["final_norm"], "final_ln"), target)


def kernel(x, positions, ln0, w_in0, w_out0, ln1, w_in1, q_norm1, w_qb1, kv_norm1, w_kvb1, w_out1, ln2, w_in2, b_f2, w_out2, ln3, w_in3, w_out3, final_norm, loss_target, m_ln0, m_w_in0, m_w_out0, m_ln1, m_w_in1, m_q_norm1, m_w_qb1, m_kv_norm1, m_w_kvb1, m_w_out1, m_ln2, m_w_in2, m_b_f2, m_w_out2, m_ln3, m_w_in3, m_w_out3, m_final_norm, v_ln0, v_w_in0, v_w_out0, v_ln1, v_w_in1, v_q_norm1, v_w_qb1, v_kv_norm1, v_w_kvb1, v_w_out1, v_ln2, v_w_in2, v_b_f2, v_w_out2, v_ln3, v_w_in3, v_w_out3, v_final_norm):
    args = dict(locals())
    w = {n: args[n] for n in ALL_W}
    m = {n: args["m_" + n] for n in ALL_W}
    v = {n: args["v_" + n] for n in ALL_W}
    shapes = {n: w[n].shape for n in BIG}
    kinds = [_gather_kind(n, shapes[n]) for n in BIG]
    slabs = {n: _slab_of(n, shapes[n]) for n in BIG}

    first = BIG[:2]
    late_names = BIG[2:]
    gathered = _all_gather([w[n].astype(BF16) for n in first], kinds[:2], "gather_w")
    wfull = dict(zip(first, gathered))
    late = (late_names, [w[n].astype(BF16) for n in late_names], kinds[2:], shapes)

    d_model = shapes["w_in0"][0]
    car_shapes = {}
    for n in late_names:
        if n == "w_in2":
            car_shapes["w_in2_qkv"] = (d_model, N_QKV)
            car_shapes["w_in2_gate"] = (d_model, D_INNER)
            car_shapes["w_in2_f"] = (d_model, N_DEV * shapes[n][1] - N_MAIN)
        elif n == "w_qb1":
            car_shapes[n] = (MLA_Q_RANK, N_HEADS * MLA_QK_PAD)
        elif slabs[n] is None:
            car_shapes[n] = (shapes[n][0], N_DEV * shapes[n][1])
        else:
            car_shapes[n] = (2, 4) + shapes[n]
    car_keys = list(car_shapes)

    def to_slabs(cts):
        g_car = dict(zip(car_keys, cts))
        out = []
        for n in late_names:
            if n == "w_in2":
                g = jnp.concatenate([g_car["w_in2_qkv"], g_car["w_in2_gate"], g_car["w_in2_f"]], axis=1)
                out.append(_to_slabs(g, shapes[n]))
            elif n == "w_qb1":
                g = g_car[n].reshape(MLA_Q_RANK, N_HEADS, MLA_QK_PAD)[:, :, :MLA_NOPE + MLA_ROPE]
                out.append(_to_slabs(g.reshape(MLA_Q_RANK, -1), shapes[n]))
            elif slabs[n] is None:
                out.append(_to_slabs(g_car[n], shapes[n]))
            else:
                out.append(g_car[n])
        return out

    late_grads = (car_keys, [car_shapes[k] for k in car_keys], to_slabs)
    carriers = {n: jnp.zeros((2, 4) + shapes[n], F32) for n in first}
    carriers["slots"] = [jnp.zeros((4,) + shapes[n], BF16) for n in late_names]
    small = {n: w[n] for n in SMALL}

    def local_loss(carriers, small, x_seq):
        return _forward_loss(carriers, small, x_seq, wfull, late, late_grads, slabs, positions[0], loss_target[0])

    loss_local, (g_car, g_small, g_x) = jax.value_and_grad(local_loss, argnums=(0, 1, 2))(carriers, small, x[0])
    loss = lax.psum(loss_local, ("x", "y", "c"))

    core_idx = lax.axis_index("c").astype(jnp.int32).reshape(1)
    from_sibling = _pair_exchange([g_car[n] for n in first], "pair_exchange")
    chip_part = [_pair_sum(core_idx, g_car[n], r, "pair_sum_" + n) for n, r in zip(first, from_sibling)]
    by_chip = dict(zip(first, _chip_exchange(chip_part, "chip_exchange")))
    by_chip.update(zip(late_names, g_car["slots"]))
    big = [{}, {}, {}, {}]
    for n in BIG:
        for k, t in enumerate(_adamw(w[n], by_chip[n], m[n], v[n], "adamw_" + n)):
            big[k][n] = t

    g_small_sum = _all_reduce_small(_pack_small(g_small), "reduce_small")
    sm = _adamw(_pack_small(w), g_small_sum[None], _pack_small(m), _pack_small(v), "adamw_small")
    small_out = [_unpack_small(t) for t in sm]

    outs = [loss, g_x[None]]
    for k in range(4):
        outs += [small_out[k][n] if n in small_out[k] else big[k][n] for n in ALL_W]
    return tuple(outs)
```

```python
import jax
import jax.numpy as jnp
from jax import lax
from jax.experimental import pallas as pl
from jax.experimental.pallas import tpu as pltpu

F32 = jnp.float32
BF16 = jnp.bfloat16
MESH = pl.DeviceIdType.MESH

N_DEV = 8
N_HEADS = 16
HEAD_DIM = 128
D_INNER = N_HEADS * HEAD_DIM
MLA_Q_RANK = 256
MLA_KV_RANK = 128
MLA_NOPE = 128
MLA_ROPE = 64
MLA_QK_PAD = 256
MLA_CHUNK = 64
ROPE_BASE = 10000.0
EPS = 1e-6
NEG = -1e30
SB_CUT = 104.0

ADAM_LR = 0.001
ADAM_B1 = 0.9
ADAM_B2 = 0.999
ADAM_EPS = 1e-08
ADAM_WD = 0.01
ADAM_STEP = 10

PACK_COLS = 1024
VMEM_LIMIT = 56 * 1024 * 1024

BIG = ["w_in0", "w_out0", "w_in1", "w_qb1", "w_kvb1", "w_out1", "w_in2", "w_out2", "w_in3", "w_out3"]
COL_SHARDED = {"w_in0", "w_in1", "w_qb1", "w_kvb1", "w_in2", "w_in3"}
SMALL = ["ln0", "ln1", "ln2", "ln3", "final_norm", "q_norm1", "kv_norm1", "b_f2"]
ALL_W = ["ln0", "w_in0", "w_out0", "ln1", "w_in1", "q_norm1", "w_qb1", "kv_norm1", "w_kvb1", "w_out1",
         "ln2", "w_in2", "b_f2", "w_out2", "ln3", "w_in3", "w_out3", "final_norm"]


def _cparams(sem=None):
    return pltpu.CompilerParams(dimension_semantics=sem, vmem_limit_bytes=VMEM_LIMIT)


def _tile(dim, cap, align):
    if dim <= cap:
        return dim
    t = (cap // align) * align
    while t >= align:
        if dim % t == 0:
            return t
        t -= align
    return dim


def _dot(a, b, dims):
    return lax.dot_general(a, b, (dims, ((), ())), preferred_element_type=F32)


def _dot_nn(a, b):
    return _dot(a, b, ((1,), (0,)))


def _dot_nt(a, b):
    return _dot(a, b, ((1,), (1,)))


def _dot_tn(a, b):
    return _dot(a, b, ((0,), (0,)))


def _transpose_bf16(x):
    return x.astype(F32).T.astype(BF16)


def _matmul(a, b, mode, col0=0, n_cols=None, out_dtype=F32, name="mm", res=None):
    if mode == "nn":
        m, r = a.shape
        n = n_cols or b.shape[1]
        tn, tr = _tile(n, 1024, 128), _tile(r, 1024, 128)
        tm = _tile(m, 1024 if tn <= 1024 else 512, 8)
        c0 = col0 // tn
        a_spec = pl.BlockSpec((tm, tr), lambda i, j, k: (i, k))
        b_spec = pl.BlockSpec((tr, tn), lambda i, j, k: (k, j + c0))
        dims = ((1,), (0,))
        assert col0 % tn == 0
    elif mode == "nt":
        m, r = a.shape
        n = b.shape[0]
        tn, tr = _tile(n, 1024, 128), _tile(r, 1024, 128)
        tm = _tile(m, 1024 if tr <= 1024 else 512, 8)
        c0 = col0 // tr
        a_spec = pl.BlockSpec((tm, tr), lambda i, j, k: (i, k))
        b_spec = pl.BlockSpec((tn, tr), lambda i, j, k: (j, k + c0))
        dims = ((1,), (1,))
        assert col0 % tr == 0
    else:
        r, m = a.shape
        n = b.shape[1]
        tm, tn, tr = _tile(m, 1024, 128), _tile(n, 1024, 128), _tile(r, 512, 16)
        a_spec = pl.BlockSpec((tr, tm), lambda i, j, k: (k, i))
        b_spec = pl.BlockSpec((tr, tn), lambda i, j, k: (k, j))
        dims = ((0,), (0,))
    nr = r // tr
    out_spec = pl.BlockSpec((tm, tn), lambda i, j, k: (i, j))

    def body(*refs):
        a_ref, b_ref, o_ref, acc_ref = refs[0], refs[1], refs[-2], refs[-1]
        k = pl.program_id(2)

        @pl.when(k == 0)
        def _():
            acc_ref[...] = jnp.zeros_like(acc_ref)

        acc_ref[...] += _dot(a_ref[...].astype(BF16), b_ref[...].astype(BF16), dims)

        @pl.when(k == nr - 1)
        def _():
            out = acc_ref[...] if res is None else acc_ref[...] + refs[2][...]
            o_ref[...] = out.astype(out_dtype)

    return pl.pallas_call(
        body,
        name=name,
        grid=(m // tm, n // tn, nr),
        in_specs=[a_spec, b_spec] + ([] if res is None else [out_spec]),
        out_specs=out_spec,
        out_shape=jax.ShapeDtypeStruct((m, n), out_dtype),
        scratch_shapes=[pltpu.VMEM((tm, tn), F32)],
        compiler_params=_cparams(("parallel", "parallel", "arbitrary")),
    )(*([a, b] if res is None else [a, b, res]))


def _matmul_dx2(g1, g2, w, name):
    m, r1 = g1.shape
    r2 = g2.shape[1]
    n = w.shape[0]
    tm, tn, tr = _tile(m, 1024, 8), _tile(n, 1024, 128), 1024
    assert r1 % tr == 0 and r2 % tr == 0
    n1b, nr = r1 // tr, (r1 + r2) // tr

    def body(g1_ref, g2_ref, w_ref, o_ref, acc_ref):
        k = pl.program_id(2)

        @pl.when(k == 0)
        def _():
            acc_ref[...] = jnp.zeros_like(acc_ref)

        @pl.when(k < n1b)
        def _():
            acc_ref[...] += _dot_nt(g1_ref[...].astype(BF16), w_ref[...])

        @pl.when(k >= n1b)
        def _():
            acc_ref[...] += _dot_nt(g2_ref[...].astype(BF16), w_ref[...])

        @pl.when(k == nr - 1)
        def _():
            o_ref[...] = acc_ref[...]

    return pl.pallas_call(
        body, name=name, grid=(m // tm, n // tn, nr),
        in_specs=[pl.BlockSpec((tm, tr), lambda i, j, k: (i, jnp.minimum(k, n1b - 1))),
                  pl.BlockSpec((tm, tr), lambda i, j, k: (i, jnp.maximum(k - n1b, 0))),
                  pl.BlockSpec((tn, tr), lambda i, j, k: (j, k))],
        out_specs=pl.BlockSpec((tm, tn), lambda i, j, k: (i, j)),
        out_shape=jax.ShapeDtypeStruct((m, n), F32),
        scratch_shapes=[pltpu.VMEM((tm, tn), F32)],
        compiler_params=_cparams(("parallel", "parallel", "arbitrary")),
    )(g1, g2, w)


def _transpose_cast(a, name):
    r, m = a.shape
    tr = _tile(r, 512, 128)

    def body(a_ref, o_ref):
        o_ref[...] = a_ref[...].astype(F32).T.astype(BF16)

    return pl.pallas_call(
        body, name=name, grid=(r // tr,),
        in_specs=[pl.BlockSpec((tr, m), lambda i: (i, 0))],
        out_specs=pl.BlockSpec((m, tr), lambda i: (0, i)),
        out_shape=jax.ShapeDtypeStruct((m, r), BF16),
        compiler_params=_cparams(("parallel",)),
    )(a)


def _matmul_dw(a, b1, b2, slab, name):
    m, r = a.shape
    n1 = b1.shape[1]
    n = n1 + (b2.shape[1] if b2 is not None else 0)
    tr = _tile(r, 1024, 128)
    if slab is None:
        tm, tn = _tile(m, 1024, 128), _tile(n1, 1024, 128)
        out_spec = pl.BlockSpec((tm, tn), lambda i, j, k: (i, j))
        out_shape = (m, n)
    elif slab[0] == "col":
        tm, tn = _tile(m, 1024, 128), slab[1]
        out_spec = pl.BlockSpec((None, None, tm, tn), lambda i, j, k: (j % 2, j // 2, i, 0))
        out_shape = (2, 4, m, tn)
        assert n == N_DEV * tn
    else:
        tm, tn = slab[1], _tile(n, 1024, 128)
        out_spec = pl.BlockSpec((None, None, tm, tn), lambda i, j, k: (i % 2, i // 2, 0, j))
        out_shape = (2, 4, tm, n)
        assert m == N_DEV * tm
    assert n1 % tn == 0 and n % tn == 0
    if tn > 1024:
        tr = _tile(r, 256, 128)
    n1b = n1 // tn
    nr = r // tr

    def body(*refs):
        a_ref, b_refs, o_ref, acc_ref = refs[0], refs[1:-2], refs[-2], refs[-1]
        j = pl.program_id(1)
        k = pl.program_id(2)

        @pl.when(k == 0)
        def _():
            acc_ref[...] = jnp.zeros_like(acc_ref)

        at = a_ref[...]
        if b2 is None:
            acc_ref[...] += _dot_nn(at, b_refs[0][...].astype(BF16))
        else:
            @pl.when(j < n1b)
            def _():
                acc_ref[...] += _dot_nn(at, b_refs[0][...].astype(BF16))

            @pl.when(j >= n1b)
            def _():
                acc_ref[...] += _dot_nn(at, b_refs[1][...].astype(BF16))

        @pl.when(k == nr - 1)
        def _():
            o_ref[...] = acc_ref[...]

    in_specs = [pl.BlockSpec((tm, tr), lambda i, j, k: (i, k))]
    args = [a, b1]
    if b2 is None:
        in_specs.append(pl.BlockSpec((tr, tn), lambda i, j, k: (k, j)))
    else:
        in_specs.append(pl.BlockSpec((tr, tn), lambda i, j, k: (jnp.where(j < n1b, k, nr - 1), jnp.minimum(j, n1b - 1))))
        in_specs.append(pl.BlockSpec((tr, tn), lambda i, j, k: (jnp.where(j < n1b, 0, k), jnp.maximum(j - n1b, 0))))
        args.append(b2)
    return pl.pallas_call(
        body, name=name, grid=(m // tm, n // tn, nr),
        in_specs=in_specs, out_specs=out_spec,
        out_shape=jax.ShapeDtypeStruct(out_shape, F32),
        scratch_shapes=[pltpu.VMEM((tm, tn), F32)],
        compiler_params=_cparams(("parallel", "parallel", "arbitrary")),
    )(*args)


def mm(a, w, carrier, slab=None, out_dtype=F32, name="mm", skip=None):
    @jax.custom_vjp
    def f(a, w, carrier, *skip):
        return _matmul(a, w, "nn", 0, None, out_dtype, name + "_fwd", *skip)

    def fwd(a, w, carrier, *skip):
        return _matmul(a, w, "nn", 0, None, out_dtype, name + "_fwd", *skip), (a, w)

    def bwd(res, g):
        a, w = res
        da = _matmul(g, w, "nt", 0, None, F32, name + "_dx")
        dw = _matmul_dw(_transpose_cast(a, name + "_t"), g, None, slab, name + "_dw")
        return (da, jnp.zeros_like(w), dw) + ((g,) if skip is not None else ())

    f.defvjp(fwd, bwd)
    return f(a, w, carrier, *(() if skip is None else (skip,)))


def in_proj(h, w, carriers, slab, name):
    def run(h, w):
        return (_matmul(h, w, "nn", 0, N_QKV, BF16, name + "_qkv"),
                _matmul(h, w, "nn", N_QKV, D_INNER, F32, name + "_g"))

    @jax.custom_vjp
    def f(h, w, *cars):
        return run(h, w)

    def fwd(h, w, *cars):
        return run(h, w), (h, w)

    def bwd(res, g):
        h, w = res
        g_qkv, g_gate = g
        dh = _matmul_dx2(g_qkv, g_gate, w, name + "_dx")
        h_t = _transpose_cast(h, name + "_t")
        if slab:
            dws = (_matmul_dw(h_t, g_qkv, g_gate, ("col", PACK_COLS), name + "_dw"),)
        else:
            dws = (_matmul_dw(h_t, g_qkv, None, None, name + "_qkv_dw"),
                   _matmul_dw(h_t, g_gate, None, None, name + "_g_dw"))
        return (dh, jnp.zeros_like(w)) + dws

    f.defvjp(fwd, bwd)
    return f(h, w, *carriers)


def _rms_fwd(x, g, name):
    s, d = x.shape
    tm = _tile(s, 512, 8)

    def body(x_ref, g_ref, y_ref):
        x = x_ref[...]
        r = lax.rsqrt(jnp.mean(x * x, axis=-1, keepdims=True) + EPS)
        y_ref[...] = x * r * g_ref[...]

    return pl.pallas_call(
        body, name=name, grid=(s // tm,),
        in_specs=[pl.BlockSpec((tm, d), lambda i: (i, 0)), pl.BlockSpec((1, d), lambda i: (0, 0))],
        out_specs=pl.BlockSpec((tm, d), lambda i: (i, 0)),
        out_shape=jax.ShapeDtypeStruct((s, d), F32),
        compiler_params=_cparams(("parallel",)),
    )(x, g)


def _rms_bwd(x, g, dy, name, dskip=None):
    s, d = x.shape
    tm = _tile(s, 512, 8)

    def body(*refs):
        x_ref, g_ref, dy_ref, dx_ref, dg_ref = refs[0], refs[1], refs[2], refs[-2], refs[-1]

        @pl.when(pl.program_id(0) == 0)
        def _():
            dg_ref[...] = jnp.zeros_like(dg_ref)

        x = x_ref[...]
        dy = dy_ref[...]
        r = lax.rsqrt(jnp.mean(x * x, axis=-1, keepdims=True) + EPS)
        xh = x * r
        dg_ref[...] += jnp.sum(dy * xh, axis=0, keepdims=True)
        dxh = dy * g_ref[...]
        dx = r * (dxh - xh * jnp.mean(dxh * xh, axis=-1, keepdims=True))
        dx_ref[...] = dx if dskip is None else dx + refs[3][...]

    row = pl.BlockSpec((tm, d), lambda i: (i, 0))
    vec = pl.BlockSpec((1, d), lambda i: (0, 0))
    return pl.pallas_call(
        body, name=name, grid=(s // tm,),
        in_specs=[row, vec, row] + ([] if dskip is None else [row]),
        out_specs=[row, vec],
        out_shape=[jax.ShapeDtypeStruct((s, d), F32), jax.ShapeDtypeStruct((1, d), F32)],
        compiler_params=_cparams(("arbitrary",)),
    )(*([x, g, dy] if dskip is None else [x, g, dy, dskip]))


def rmsnorm(x, g, name="rms", skip=False):
    def run(x, g):
        y = _rms_fwd(x, g.reshape(1, -1), name + "_fwd")
        return (y, x) if skip else y

    @jax.custom_vjp
    def f(x, g):
        return run(x, g)

    def fwd(x, g):
        return run(x, g), (x, g)

    def bwd(res, ct):
        x, g = res
        dy, dskip = ct if skip else (ct, None)
        dx, dg = _rms_bwd(x, g.reshape(1, -1), dy, name + "_bwd", dskip)
        return dx, dg.reshape(-1)

    f.defvjp(fwd, bwd)
    return f(x, g)


def _gate_fwd(o, gate, name):
    s = o.shape[0]
    tm = 256

    def body(o_ref, g_ref, y_ref):
        g = g_ref[...]
        y_ref[...] = o_ref[...] * (g / (1.0 + jnp.exp(-g)))

    row = pl.BlockSpec((tm, D_INNER), lambda i: (i, 0))
    return pl.pallas_call(
        body, name=name, grid=(s // tm,),
        in_specs=[row, row], out_specs=row,
        out_shape=jax.ShapeDtypeStruct((s, D_INNER), F32),
        compiler_params=_cparams(("parallel",)),
    )(o, gate)


def _gate_bwd(dy, o, gate, name):
    s = o.shape[0]
    tm = 256

    def body(dy_ref, o_ref, g_ref, do_ref, dg_ref, dl_ref):
        g = g_ref[...]
        o = o_ref[...]
        dy = dy_ref[...]
        sg = 1.0 / (1.0 + jnp.exp(-g))
        do = dy * (g * sg)
        do_ref[...] = do.astype(BF16)
        dg_ref[...] = dy * o * (sg * (1.0 + g * (1.0 - sg)))
        prod = do * o
        for h in range(N_HEADS):
            dl_ref[h] = jnp.sum(prod[:, h * HEAD_DIM:(h + 1) * HEAD_DIM], axis=1, keepdims=True)

    row = pl.BlockSpec((tm, D_INNER), lambda i: (i, 0))
    return pl.pallas_call(
        body, name=name, grid=(s // tm,),
        in_specs=[row, row, row],
        out_specs=[row, row, pl.BlockSpec((N_HEADS, tm, 1), lambda i: (0, i, 0))],
        out_shape=[jax.ShapeDtypeStruct((s, D_INNER), BF16), jax.ShapeDtypeStruct((s, D_INNER), F32),
                   jax.ShapeDtypeStruct((N_HEADS, s, 1), F32)],
        compiler_params=_cparams(("parallel",)),
    )(dy, o, gate)


SB_BLK = 256


def _softplus(z):
    return jnp.maximum(z, 0.0) + jnp.log(1.0 + jnp.exp(-jnp.abs(z)))


def _tri_sum(x, tri):
    hi = x.astype(BF16)
    lo = (x - hi.astype(F32)).astype(BF16)
    return _dot_nn(hi, tri) + _dot_nn(lo, tri)


SB_WIN = 2 * SB_BLK


def _sb_tri(kind):
    row = lax.broadcasted_iota(jnp.int32, (SB_BLK, SB_BLK), 0)
    col = lax.broadcasted_iota(jnp.int32, (SB_BLK, SB_BLK), 1)
    return ((row >= col) if kind == "suffix" else (row <= col)).astype(BF16)


def _sb_bounds(i, t):
    hi = (i + 1) * SB_BLK - t * SB_WIN
    return hi, pl.multiple_of(jnp.maximum(hi - SB_WIN, 0), SB_BLK)


def _sb_mask(i, hi, start):
    row = lax.broadcasted_iota(jnp.int32, (SB_BLK, SB_WIN), 0) + i * SB_BLK
    col = lax.broadcasted_iota(jnp.int32, (SB_BLK, SB_WIN), 1) + start
    return jnp.logical_and(col < row, col < hi)


def _sb_window(q, kwin, mask, a_run, tri_suffix, scale):
    b = SB_BLK
    z = _dot_nt(q, kwin) * scale
    sp = _softplus(z)
    ls = jnp.where(mask, -sp, 0.0)
    ls_l, ls_r = ls[:, :b], ls[:, b:]
    suffix = jnp.concatenate([_tri_sum(ls_l, tri_suffix) + jnp.sum(ls_r, axis=1, keepdims=True),
                              _tri_sum(ls_r, tri_suffix)], axis=1)
    w = jnp.where(mask, jnp.exp(z + suffix + a_run), 0.0)
    return z, sp, ls, w


def _sb_fwd(qkv, name, gather=None):
    s = qkv.shape[0]
    b = SB_BLK
    nq = s // b
    scale = HEAD_DIM ** -0.5
    assert s >= SB_WIN
    shards, kinds = gather if gather else ((), ())
    nm = len(shards)

    def body(*refs):
        q_ref, k_ref, v_ref = refs[:3]
        o_ref = refs[3 + nm]
        i = pl.program_id(1)
        if nm:
            start, finish = _gather_steps(refs[3:3 + nm], refs[4 + nm:4 + 2 * nm], kinds, *refs[4 + 2 * nm:])
            first_step = jnp.logical_and(pl.program_id(0) == 0, i == 0)
            last_step = jnp.logical_and(pl.program_id(0) == N_HEADS - 1, i == nq - 1)
            pl.when(first_step)(start)
        q = q_ref[...]
        tri_suffix = _sb_tri("suffix")

        def cond(c):
            t, a_run, _ = c
            return jnp.logical_and((i + 1) * b - t * SB_WIN > 0, jnp.max(a_run) > -SB_CUT)

        def step(c):
            t, a_run, acc = c
            hi, start = _sb_bounds(i, t)
            _, _, ls, w = _sb_window(q, k_ref[pl.ds(start, SB_WIN), :], _sb_mask(i, hi, start), a_run,
                                     tri_suffix, scale)
            acc = acc + _dot_nn(w.astype(BF16), v_ref[pl.ds(start, SB_WIN), :])
            return t + 1, a_run + jnp.sum(ls, axis=1, keepdims=True), acc

        _, _, acc = lax.while_loop(cond, step, (0, jnp.zeros((b, 1), F32), jnp.zeros((b, HEAD_DIM), F32)))
        o_ref[...] = acc
        if nm:
            pl.when(last_step)(finish)

    out = pl.pallas_call(
        body, name=name, grid=(N_HEADS, nq),
        in_specs=[pl.BlockSpec((b, HEAD_DIM), lambda h, i: (i, h)),
                  pl.BlockSpec((s, HEAD_DIM), lambda h, i: (0, N_HEADS + h)),
                  pl.BlockSpec((s, HEAD_DIM), lambda h, i: (0, 2 * N_HEADS + h))] + [ANY] * nm,
        out_specs=[pl.BlockSpec((b, HEAD_DIM), lambda h, i: (i, h))] + [ANY] * nm,
        out_shape=[jax.ShapeDtypeStruct((s, D_INNER), F32)] + _gather_out_shapes(shards, kinds),
        scratch_shapes=_gather_scratch(nm) if nm else [],
        compiler_params=_cparams(("arbitrary", "arbitrary")),
    )(qkv, qkv, qkv, *shards)
    return out[0], list(out[1:])


def _sb_bwd(qkv, do, name, exchange=()):
    s = qkv.shape[0]
    b = SB_BLK
    nq = s // b
    n_win = -(-s // SB_WIN) + 1
    scale = HEAD_DIM ** -0.5
    assert s >= SB_WIN
    nm = len(exchange)

    def body(*refs):
        q_ref, k_ref, v_ref, do_ref = refs[:4]
        dq_ref, dk_ref, dv_ref = refs[4 + nm:7 + nm]
        dkt_s, dvt_s, g_buf, sig_buf = refs[7 + 2 * nm:11 + 2 * nm]
        i = pl.program_id(1)
        if nm:
            start_x, finish_x = _exchange_steps(refs[4:4 + nm], refs[7 + nm:7 + 2 * nm], *refs[11 + 2 * nm:])
            pl.when(jnp.logical_and(pl.program_id(0) == 0, i == 0))(start_x)

        @pl.when(i == 0)
        def _():
            dkt_s[...] = jnp.zeros_like(dkt_s)
            dvt_s[...] = jnp.zeros_like(dvt_s)

        q = q_ref[...]
        dob = do_ref[...]
        q_t = _transpose_bf16(q)
        do_t = _transpose_bf16(dob)
        tri_suffix = _sb_tri("suffix")
        tri_prefix = _sb_tri("prefix")

        def add_halves(acc_ref, start, upd):
            blk = start // b
            acc_ref[blk] += upd[:, :b]
            acc_ref[blk + 1] += upd[:, b:]

        def cond(c):
            t, a_run = c
            return jnp.logical_and((i + 1) * b - t * SB_WIN > 0, jnp.max(a_run) > -SB_CUT)

        def sweep(c):
            t, a_run = c
            hi, start = _sb_bounds(i, t)
            z, sp, ls, w = _sb_window(q, k_ref[pl.ds(start, SB_WIN), :], _sb_mask(i, hi, start), a_run,
                                      tri_suffix, scale)
            g_buf[t] = w * _dot_nt(dob, v_ref[pl.ds(start, SB_WIN), :])
            sig_buf[t] = jnp.exp(z - sp)
            add_halves(dvt_s, start, _dot_nn(do_t, w.astype(BF16)))
            return t + 1, a_run + jnp.sum(ls, axis=1, keepdims=True)

        n_steps, _ = lax.while_loop(cond, sweep, (0, jnp.zeros((b, 1), F32)))

        def back(u, c):
            g_run, dq = c
            t = n_steps - 1 - u
            hi, start = _sb_bounds(i, t)
            g = g_buf[t]
            g_l, g_r = g[:, :b], g[:, b:]
            g_incl = g_run + jnp.concatenate(
                [_tri_sum(g_l, tri_prefix),
                 _tri_sum(g_r, tri_prefix) + jnp.sum(g_l, axis=1, keepdims=True)], axis=1)
            dz = jnp.where(_sb_mask(i, hi, start), (g - sig_buf[t] * g_incl) * scale, 0.0).astype(BF16)
            add_halves(dkt_s, start, _dot_nn(q_t, dz))
            return g_run + jnp.sum(g, axis=1, keepdims=True), dq + _dot_nn(dz, k_ref[pl.ds(start, SB_WIN), :])

        _, dq = lax.fori_loop(0, n_steps, back, (jnp.zeros((b, 1), F32), jnp.zeros((b, HEAD_DIM), F32)))
        dq_ref[...] = dq.astype(BF16)

        @pl.when(i == nq - 1)
        def _():
            for jb in range(nq):
                dk_ref[jb * b:(jb + 1) * b, :] = dkt_s[jb].T.astype(BF16)
                dv_ref[jb * b:(jb + 1) * b, :] = dvt_s[jb].T.astype(BF16)

        if nm:
            pl.when(jnp.logical_and(pl.program_id(0) == N_HEADS - 1, i == nq - 1))(finish_x)

    blk = pl.BlockSpec((b, HEAD_DIM), lambda h, i: (i, h))
    head = pl.BlockSpec((s, HEAD_DIM), lambda h, i: (0, h))
    out = pl.pallas_call(
        body, name=name, grid=(N_HEADS, nq),
        in_specs=[blk,
                  pl.BlockSpec((s, HEAD_DIM), lambda h, i: (0, N_HEADS + h)),
                  pl.BlockSpec((s, HEAD_DIM), lambda h, i: (0, 2 * N_HEADS + h)),
                  blk] + [ANY] * nm,
        out_specs=[blk, head, head] + [ANY] * nm,
        out_shape=[jax.ShapeDtypeStruct((s, D_INNER), BF16)] * 3
        + [jax.ShapeDtypeStruct(p.shape, p.dtype) for p in exchange],
        scratch_shapes=[pltpu.VMEM((nq, HEAD_DIM, b), F32), pltpu.VMEM((nq, HEAD_DIM, b), F32),
                        pltpu.VMEM((n_win, b, SB_WIN), F32), pltpu.VMEM((n_win, b, SB_WIN), F32)]
        + (_exchange_scratch(nm) if nm else []),
        compiler_params=_cparams(("arbitrary", "arbitrary")),
    )(qkv, qkv, qkv, do, *exchange)
    return out[0], out[1], out[2], list(out[3:])


def sb_core(qkv, gate, name, gather=None, late=None):
    shards, kinds = gather if gather else ((), ())
    slots, carrier_shapes, to_slabs = late if late else ((), (), None)
    n_sh = len(shards)

    def run(qkv, gate, *shards):
        o, gathered = _sb_fwd(qkv, name + "_fwd", (shards, kinds) if shards else None)
        carriers = [jnp.zeros(sh, F32) for sh in carrier_shapes]
        return (_gate_fwd(o, gate, name + "_gate"), *gathered, *carriers), o

    @jax.custom_vjp
    def f(qkv, gate, *extra):
        return run(qkv, gate, *extra[:n_sh])[0]

    def fwd(qkv, gate, *extra):
        outs, o = run(qkv, gate, *extra[:n_sh])
        return outs, (qkv, gate, o)

    def bwd(res, cts):
        qkv, gate, o = res
        do, dgate, _ = _gate_bwd(cts[0], o, gate, name + "_gate_bwd")
        parts = []
        if slots:
            g_slabs = to_slabs(cts[1 + n_sh:])
            core_idx = lax.axis_index("c").astype(jnp.int32).reshape(1)
            from_sibling = _pair_exchange(g_slabs, name + "_pair_exchange")
            parts = [_pair_sum(core_idx, g, r, name + "_pair_sum%d" % k)
                     for k, (g, r) in enumerate(zip(g_slabs, from_sibling))]
        dq, dk, dv, by_chip = _sb_bwd(qkv, do, name + "_bwd", parts)
        zeros = tuple(jnp.zeros(sh.shape, sh.dtype) for sh in shards)
        return (jnp.concatenate([dq, dk, dv], axis=1), dgate) + zeros + tuple(by_chip)

    f.defvjp(fwd, bwd)
    outs = f(qkv, gate, *shards, *slots)
    return outs[0], list(outs[1:1 + n_sh]), list(outs[1 + n_sh:])


SM_FWD_BLK = 256
SM_BLK = 512


SM_FWD_KEYS = 1024
SM_BWD_KEYS = 512


def _sm_mask(i, jw, rows, keys, chunk_shift):
    row = lax.broadcasted_iota(jnp.int32, (rows, keys), 0) + i * rows
    col = lax.broadcasted_iota(jnp.int32, (rows, keys), 1) + jw * keys
    return (col >> chunk_shift) <= (row >> chunk_shift)


def _sm_fwd(qa, ka, va, ccol, crow, dqk, qo, ko, vo, chunk_shift, scale, name):
    s = qa.shape[0]
    b = min(SM_FWD_BLK, s)
    keys = min(SM_FWD_KEYS, s)
    per = keys // b
    nq = s // b
    has_bias = ccol is not None

    def body(*refs):
        if has_bias:
            q_ref, k_ref, v_ref, cc_ref, cr_ref, o_ref, lse_ref, m_s, l_s, acc_s = refs
        else:
            q_ref, k_ref, v_ref, o_ref, lse_ref, m_s, l_s, acc_s = refs
        i = pl.program_id(1)
        q = q_ref[...]
        m_s[...] = jnp.full_like(m_s, NEG)
        l_s[...] = jnp.zeros_like(l_s)
        acc_s[...] = jnp.zeros_like(acc_s)

        def sweep(jw, masked):
            off = pl.multiple_of(jw * keys, keys)
            z = _dot_nt(q, k_ref[pl.ds(off, keys), :]) * scale
            if has_bias:
                z = z + cc_ref[...] - cr_ref[jw]
            if masked:
                z = jnp.where(_sm_mask(i, jw, b, keys, chunk_shift), z, NEG)
            m_old = m_s[...]
            m_new = jnp.maximum(m_old, jnp.max(z, axis=1, keepdims=True))
            alpha = jnp.exp(m_old - m_new)
            p = jnp.exp(z - m_new)
            l_s[...] = alpha * l_s[...] + jnp.sum(p, axis=1, keepdims=True)
            acc_s[...] = alpha * acc_s[...] + _dot_nn(p.astype(BF16), v_ref[pl.ds(off, keys), :])
            m_s[...] = m_new

        def full(jw, carry):
            sweep(jw, False)
            return carry

        lax.fori_loop(0, i // per, full, 0)
        sweep(i // per, True)
        o_ref[...] = acc_s[...] / l_s[...]
        lse_ref[...] = m_s[...] + jnp.log(l_s[...])

    in_specs = [pl.BlockSpec((b, dqk), lambda h, i: (i, qo + h)),
                pl.BlockSpec((s, dqk), lambda h, i: (0, ko + h)),
                pl.BlockSpec((s, HEAD_DIM), lambda h, i: (0, vo + h))]
    args = [qa, ka, va]
    if has_bias:
        in_specs += [pl.BlockSpec((None, b, 1), lambda h, i: (h, i, 0)),
                     pl.BlockSpec((None, s // keys, 1, keys), lambda h, i: (h, 0, 0, 0))]
        args += [ccol, crow]
    return pl.pallas_call(
        body, name=name, grid=(N_HEADS, nq),
        in_specs=in_specs,
        out_specs=[pl.BlockSpec((b, HEAD_DIM), lambda h, i: (i, h)),
                   pl.BlockSpec((None, b, 1), lambda h, i: (h, i, 0))],
        out_shape=[jax.ShapeDtypeStruct((s, D_INNER), F32), jax.ShapeDtypeStruct((N_HEADS, s, 1), F32)],
        scratch_shapes=[pltpu.VMEM((b, 1), F32), pltpu.VMEM((b, 1), F32), pltpu.VMEM((b, HEAD_DIM), F32)],
        compiler_params=_cparams(("parallel", "arbitrary")),
    )(*args)


def _sm_bwd(qa, ka, va, do, lse, delta, ccol, crow, dqk, qo, ko, vo, chunk_shift, scale, grad_dtype, name):
    s = qa.shape[0]
    b = SM_BLK
    keys = min(SM_BWD_KEYS, s)
    per = keys // b
    nq = s // b
    nk = s // keys
    has_bias = ccol is not None

    def body(*refs):
        if has_bias:
            (q_ref, k_ref, v_ref, do_ref, lse_ref, dl_ref, cc_ref, cr_ref,
             dq_ref, dk_ref, dv_ref, dc_ref, dr_ref, dq_s, dkt_s, dvt_s, dc_s, dr_s) = refs
        else:
            (q_ref, k_ref, v_ref, do_ref, lse_ref, dl_ref,
             dq_ref, dk_ref, dv_ref, dq_s, dkt_s, dvt_s) = refs
        i = pl.program_id(1)

        @pl.when(i == 0)
        def _():
            dkt_s[...] = jnp.zeros_like(dkt_s)
            dvt_s[...] = jnp.zeros_like(dvt_s)
            if has_bias:
                dc_s[...] = jnp.zeros_like(dc_s)

        q = q_ref[...]
        dob = do_ref[...]
        q_t = _transpose_bf16(q)
        do_t = _transpose_bf16(dob)
        lse = lse_ref[...]
        delta = dl_ref[...]
        dq_s[...] = jnp.zeros_like(dq_s)
        if has_bias:
            dr_s[...] = jnp.zeros_like(dr_s)

        def sweep(jw, masked):
            off = pl.multiple_of(jw * keys, keys)
            kb = k_ref[pl.ds(off, keys), :]
            z = _dot_nt(q, kb) * scale
            if has_bias:
                z = z + cc_ref[...] - cr_ref[jw]
            p = jnp.exp(z - lse)
            if masked:
                p = jnp.where(_sm_mask(i, jw, b, keys, chunk_shift), p, 0.0)
            dvt_s[jw] += _dot_nn(do_t, p.astype(BF16))
            dz = p * (_dot_nt(dob, v_ref[pl.ds(off, keys), :]) - delta)
            if has_bias:
                dc_s[jw] += jnp.sum(dz, axis=0, keepdims=True)
                dr_s[...] += jnp.sum(dz, axis=1, keepdims=True)
            dzs = (dz * scale).astype(BF16)
            dkt_s[jw] += _dot_nn(q_t, dzs)
            dq_s[...] += _dot_nn(dzs, kb)

        def full(jw, carry):
            sweep(jw, False)
            return carry

        lax.fori_loop(0, i // per, full, 0)
        sweep(i // per, True)
        dq_ref[...] = dq_s[...].astype(grad_dtype)
        if has_bias:
            dr_ref[...] = dr_s[...]

        @pl.when(i == nq - 1)
        def _():
            for jw in range(nk):
                dk_ref[jw * keys:(jw + 1) * keys, :] = dkt_s[jw].T.astype(grad_dtype)
                dv_ref[jw * keys:(jw + 1) * keys, :] = dvt_s[jw].T.astype(grad_dtype)
            if has_bias:
                dc_ref[...] = dc_s[...]

    vec = pl.BlockSpec((None, b, 1), lambda h, i: (h, i, 0))
    in_specs = [pl.BlockSpec((b, dqk), lambda h, i: (i, qo + h)),
                pl.BlockSpec((s, dqk), lambda h, i: (0, ko + h)),
                pl.BlockSpec((s, HEAD_DIM), lambda h, i: (0, vo + h)),
                pl.BlockSpec((b, HEAD_DIM), lambda h, i: (i, h)),
                vec, vec]
    args = [qa, ka, va, do, lse, delta]
    out_specs = [pl.BlockSpec((b, dqk), lambda h, i: (i, h)),
                 pl.BlockSpec((s, dqk), lambda h, i: (0, h)),
                 pl.BlockSpec((s, HEAD_DIM), lambda h, i: (0, h))]
    out_shape = [jax.ShapeDtypeStruct((s, N_HEADS * dqk), grad_dtype),
                 jax.ShapeDtypeStruct((s, N_HEADS * dqk), grad_dtype),
                 jax.ShapeDtypeStruct((s, D_INNER), grad_dtype)]
    scratch = [pltpu.VMEM((b, dqk), F32), pltpu.VMEM((nk, dqk, keys), F32), pltpu.VMEM((nk, HEAD_DIM, keys), F32)]
    if has_bias:
        key_vec = pl.BlockSpec((None, nk, 1, keys), lambda h, i: (h, 0, 0, 0))
        in_specs += [vec, key_vec]
        args += [ccol, crow]
        out_specs += [key_vec, vec]
        out_shape += [jax.ShapeDtypeStruct((N_HEADS, nk, 1, keys), F32), jax.ShapeDtypeStruct((N_HEADS, s, 1), F32)]
        scratch += [pltpu.VMEM((nk, 1, keys), F32), pltpu.VMEM((b, 1), F32)]
    return pl.pallas_call(
        body, name=name, grid=(N_HEADS, nq),
        in_specs=in_specs, out_specs=out_specs, out_shape=out_shape, scratch_shapes=scratch,
        compiler_params=_cparams(("arbitrary", "arbitrary")),
    )(*args)


def fox_core(qkv, gate, c, name):
    s = qkv.shape[0]
    scale = HEAD_DIM ** -0.5
    cfg = dict(dqk=HEAD_DIM, qo=0, ko=N_HEADS, vo=2 * N_HEADS, chunk_shift=0, scale=scale)

    def layouts(c, keys):
        ct = c.T
        keys = min(keys, s)
        return ct.reshape(N_HEADS, s, 1), ct.reshape(N_HEADS, s // keys, 1, keys)

    def run(qkv, gate, c):
        ccol, crow = layouts(c, SM_FWD_KEYS)
        o, lse = _sm_fwd(qkv, qkv, qkv, ccol, crow, name=name + "_fwd", **cfg)
        return _gate_fwd(o, gate, name + "_gate"), o, lse

    @jax.custom_vjp
    def f(qkv, gate, c):
        return run(qkv, gate, c)[0]

    def fwd(qkv, gate, c):
        y, o, lse = run(qkv, gate, c)
        return y, (qkv, gate, c, o, lse)

    def bwd(res, dy):
        qkv, gate, c, o, lse = res
        ccol, crow = layouts(c, SM_BWD_KEYS)
        do, dgate, delta = _gate_bwd(dy, o, gate, name + "_gate_bwd")
        dq, dk, dv, colsum, rowsum = _sm_bwd(qkv, qkv, qkv, do, lse, delta, ccol, crow,
                                             grad_dtype=BF16, name=name + "_bwd", **cfg)
        dc = (rowsum.reshape(N_HEADS, s) - colsum.reshape(N_HEADS, s)).T
        return jnp.concatenate([dq, dk, dv], axis=1), dgate, dc

    f.defvjp(fwd, bwd)
    return f(qkv, gate, c)


def _mla_rope(x, cosv, sinv, out_dtype, name):
    s, width = x.shape
    tm = 256
    half = MLA_ROPE // 2

    def body(x_ref, c_ref, s_ref, o_ref):
        c = c_ref[...]
        sn = s_ref[...]
        lane = lax.broadcasted_iota(jnp.int32, (tm, HEAD_DIM), 1)
        for h in range(N_HEADS):
            lo = h * MLA_QK_PAD
            o_ref[:, lo:lo + HEAD_DIM] = x_ref[:, lo:lo + HEAD_DIM].astype(out_dtype)
            g = x_ref[:, lo + HEAD_DIM:lo + MLA_QK_PAD].astype(F32)
            swapped = jnp.where(lane < half, pltpu.roll(g, HEAD_DIM - half, 1), pltpu.roll(g, half, 1))
            o_ref[:, lo + HEAD_DIM:lo + MLA_QK_PAD] = (g * c + swapped * sn).astype(out_dtype)

    row = pl.BlockSpec((tm, width), lambda i: (i, 0))
    tab = pl.BlockSpec((tm, HEAD_DIM), lambda i: (i, 0))
    return pl.pallas_call(
        body, name=name, grid=(s // tm,),
        in_specs=[row, tab, tab], out_specs=row,
        out_shape=jax.ShapeDtypeStruct((s, width), out_dtype),
        compiler_params=_cparams(("parallel",)),
    )(x, cosv, sinv)


def _mla_fwd(qc, kv, kr, chunk_shift, scale, name):
    s = qc.shape[0]
    b = min(SM_FWD_BLK, s)
    keys = min(SM_FWD_KEYS, s)
    per = keys // b
    nq = s // b

    def body(q_ref, kn_ref, v_ref, kr_ref, o_ref, lse_ref, m_s, l_s, acc_s, kc_s):
        i = pl.program_id(1)

        @pl.when(i == 0)
        def _():
            kc_s[:, :HEAD_DIM] = kn_ref[...]
            kc_s[:, HEAD_DIM:] = kr_ref[...]

        q = q_ref[...]
        m_s[...] = jnp.full_like(m_s, NEG)
        l_s[...] = jnp.zeros_like(l_s)
        acc_s[...] = jnp.zeros_like(acc_s)

        def sweep(jw, masked):
            off = pl.multiple_of(jw * keys, keys)
            z = _dot_nt(q, kc_s[pl.ds(off, keys), :]) * scale
            if masked:
                z = jnp.where(_sm_mask(i, jw, b, keys, chunk_shift), z, NEG)
            m_old = m_s[...]
            m_new = jnp.maximum(m_old, jnp.max(z, axis=1, keepdims=True))
            alpha = jnp.exp(m_old - m_new)
            p = jnp.exp(z - m_new)
            l_s[...] = alpha * l_s[...] + jnp.sum(p, axis=1, keepdims=True)
            acc_s[...] = alpha * acc_s[...] + _dot_nn(p.astype(BF16), v_ref[pl.ds(off, keys), :])
            m_s[...] = m_new

        def full(jw, carry):
            sweep(jw, False)
            return carry

        lax.fori_loop(0, i // per, full, 0)
        sweep(i // per, True)
        o_ref[...] = acc_s[...] / l_s[...]
        lse_ref[...] = m_s[...] + jnp.log(l_s[...])

    return pl.pallas_call(
        body, name=name, grid=(N_HEADS, nq),
        in_specs=[pl.BlockSpec((b, MLA_QK_PAD), lambda h, i: (i, h)),
                  pl.BlockSpec((s, HEAD_DIM), lambda h, i: (0, 2 * h)),
                  pl.BlockSpec((s, HEAD_DIM), lambda h, i: (0, 2 * h + 1)),
                  pl.BlockSpec((s, HEAD_DIM), lambda h, i: (0, 0))],
        out_specs=[pl.BlockSpec((b, HEAD_DIM), lambda h, i: (i, h)),
                   pl.BlockSpec((None, b, 1), lambda h, i: (h, i, 0))],
        out_shape=[jax.ShapeDtypeStruct((s, D_INNER), F32), jax.ShapeDtypeStruct((N_HEADS, s, 1), F32)],
        scratch_shapes=[pltpu.VMEM((b, 1), F32), pltpu.VMEM((b, 1), F32), pltpu.VMEM((b, HEAD_DIM), F32),
                        pltpu.VMEM((s, MLA_QK_PAD), BF16)],
        compiler_params=_cparams(("arbitrary", "arbitrary")),
    )(qc, kv, kv, kr)


def _mla_bwd(qc, kv, kr, do, lse, delta, chunk_shift, scale, name):
    s = qc.shape[0]
    b = min(SM_BLK, s)
    keys = min(SM_BWD_KEYS, s)
    per = keys // b
    nq = s // b
    nk = s // keys

    def body(q_ref, kn_ref, v_ref, kr_ref, do_ref, lse_ref, dl_ref, dq_ref, dkv_ref, dkr_ref,
             dq_s, dkt_s, dvt_s, dkrt_s, kc_s):
        h = pl.program_id(0)
        i = pl.program_id(1)

        @pl.when(jnp.logical_and(h == 0, i == 0))
        def _():
            dkrt_s[...] = jnp.zeros_like(dkrt_s)

        @pl.when(i == 0)
        def _():
            dkt_s[...] = jnp.zeros_like(dkt_s)
            dvt_s[...] = jnp.zeros_like(dvt_s)
            kc_s[:, :HEAD_DIM] = kn_ref[...]
            kc_s[:, HEAD_DIM:] = kr_ref[...]

        q = q_ref[...]
        dob = do_ref[...]
        q_t = _transpose_bf16(q)
        do_t = _transpose_bf16(dob)
        lse_i = lse_ref[...]
        delta_i = dl_ref[...]
        dq_s[...] = jnp.zeros_like(dq_s)

        def sweep(jw, masked):
            off = pl.multiple_of(jw * keys, keys)
            kc = kc_s[pl.ds(off, keys), :]
            p = jnp.exp(_dot_nt(q, kc) * scale - lse_i)
            if masked:
                p = jnp.where(_sm_mask(i, jw, b, keys, chunk_shift), p, 0.0)
            dvt_s[jw] += _dot_nn(do_t, p.astype(BF16))
            dz = p * (_dot_nt(dob, v_ref[pl.ds(off, keys), :]) - delta_i)
            dzs = (dz * scale).astype(BF16)
            dkc_t = _dot_nn(q_t, dzs)
            dkt_s[jw] += dkc_t[:HEAD_DIM]
            dkrt_s[jw] += dkc_t[HEAD_DIM:]
            dq_s[...] += _dot_nn(dzs, kc)

        def full(jw, carry):
            sweep(jw, False)
            return carry

        lax.fori_loop(0, i // per, full, 0)
        sweep(i // per, True)
        dq_ref[...] = dq_s[...]

        @pl.when(i == nq - 1)
        def _():
            for jw in range(nk):
                dkv_ref[jw * keys:(jw + 1) * keys, :HEAD_DIM] = dkt_s[jw].T.astype(BF16)
                dkv_ref[jw * keys:(jw + 1) * keys, HEAD_DIM:] = dvt_s[jw].T.astype(BF16)

        @pl.when(jnp.logical_and(h == N_HEADS - 1, i == nq - 1))
        def _():
            for jw in range(nk):
                dkr_ref[jw * keys:(jw + 1) * keys, :] = dkrt_s[jw].T

    vec = pl.BlockSpec((None, b, 1), lambda h, i: (h, i, 0))
    acc = pltpu.VMEM((nk, HEAD_DIM, keys), F32)
    return pl.pallas_call(
        body, name=name, grid=(N_HEADS, nq),
        in_specs=[pl.BlockSpec((b, MLA_QK_PAD), lambda h, i: (i, h)),
                  pl.BlockSpec((s, HEAD_DIM), lambda h, i: (0, 2 * h)),
                  pl.BlockSpec((s, HEAD_DIM), lambda h, i: (0, 2 * h + 1)),
                  pl.BlockSpec((s, HEAD_DIM), lambda h, i: (0, 0)),
                  pl.BlockSpec((b, HEAD_DIM), lambda h, i: (i, h)),
                  vec, vec],
        out_specs=[pl.BlockSpec((b, MLA_QK_PAD), lambda h, i: (i, h)),
                   pl.BlockSpec((s, MLA_QK_PAD), lambda h, i: (0, h)),
                   pl.BlockSpec((s, HEAD_DIM), lambda h, i: (0, 0))],
        out_shape=[jax.ShapeDtypeStruct((s, N_HEADS * MLA_QK_PAD), F32),
                   jax.ShapeDtypeStruct((s, N_HEADS * MLA_QK_PAD), BF16),
                   jax.ShapeDtypeStruct((s, HEAD_DIM), F32)],
        scratch_shapes=[pltpu.VMEM((b, MLA_QK_PAD), F32), acc, acc, acc, pltpu.VMEM((s, MLA_QK_PAD), BF16)],
        compiler_params=_cparams(("arbitrary", "arbitrary")),
    )(qc, kv, kv, kr, do, lse, delta)


def mla_core(qp, kv, kr, gate, cosv, sinv, name):
    scale = (MLA_NOPE + MLA_ROPE) ** -0.5
    shift = MLA_CHUNK.bit_length() - 1

    def run(qp, kv, kr, gate, cosv, sinv):
        qc = _mla_rope(qp, cosv, sinv, BF16, name + "_rope")
        krb = kr.astype(BF16)
        o, lse = _mla_fwd(qc, kv, krb, shift, scale, name + "_fwd")
        return _gate_fwd(o, gate, name + "_gate"), (qc, kv, krb, gate, o, lse, cosv, sinv)

    @jax.custom_vjp
    def f(qp, kv, kr, gate, cosv, sinv):
        return run(qp, kv, kr, gate, cosv, sinv)[0]

    def bwd(res, dy):
        qc, kv, krb, gate, o, lse, cosv, sinv = res
        do, dgate, delta = _gate_bwd(dy, o, gate, name + "_gate_bwd")
        dqc, dkv, dkr = _mla_bwd(qc, kv, krb, do, lse, delta, shift, scale, name + "_bwd")
        dqp = _mla_rope(dqc, cosv, -sinv, F32, name + "_rope_bwd")
        return dqp, dkv, dkr, dgate, jnp.zeros_like(cosv), jnp.zeros_like(sinv)

    f.defvjp(run, bwd)
    return f(qp, kv, kr, gate, cosv, sinv)


def _sq_loss_call(y, t, name):
    s, d = y.shape
    tm = _tile(s, 512, 8)

    def body(y_ref, t_ref, l_ref, e_ref):
        @pl.when(pl.program_id(0) == 0)
        def _():
            l_ref[...] = jnp.zeros_like(l_ref)

        e = y_ref[...] - t_ref[...]
        e_ref[...] = e * (1.0 / d)
        part = jnp.sum(jnp.sum(e * e, axis=1, keepdims=True), axis=0, keepdims=True)
        l_ref[...] += jnp.broadcast_to(part * (0.5 / d), l_ref.shape)

    row = pl.BlockSpec((tm, d), lambda i: (i, 0))
    return pl.pallas_call(
        body, name=name, grid=(s // tm,),
        in_specs=[row, row],
        out_specs=[pl.BlockSpec((8, 128), lambda i: (0, 0)), row],
        out_shape=[jax.ShapeDtypeStruct((8, 128), F32), jax.ShapeDtypeStruct((s, d), F32)],
        compiler_params=_cparams(("arbitrary",)),
    )(y, t)


@jax.custom_vjp
def sq_loss(y, t):
    return _sq_loss_call(y, t, "loss_fwd")[0][0, 0]


def _sq_loss_fwd(y, t):
    l, e = _sq_loss_call(y, t, "loss_fwd")
    return l[0, 0], e


def _sq_loss_bwd(e, g):
    return g * e, jnp.zeros_like(e)


sq_loss.defvjp(_sq_loss_fwd, _sq_loss_bwd)


def _cast_bf16(x, name):
    r, c = x.shape
    tb = _tile(r, 512, 16)

    def body(x_ref, o_ref):
        o_ref[...] = x_ref[...].astype(BF16)

    return pl.pallas_call(
        body, name=name, grid=(r // tb,),
        in_specs=[pl.BlockSpec((tb, c), lambda i: (i, 0))],
        out_specs=pl.BlockSpec((tb, c), lambda i: (i, 0)),
        out_shape=jax.ShapeDtypeStruct((r, c), BF16),
        compiler_params=_cparams(("parallel",)),
    )(x)


def _pair_sum(core_idx, g, recv, name):
    _, _, r, c = g.shape
    tb = _tile(r, 512, 16)

    def body(c_ref, g_ref, r_ref, o_ref):
        o_ref[...] = (g_ref[...] + r_ref[...]).astype(BF16)

    return pl.pallas_call(
        body, name=name,
        grid_spec=pltpu.PrefetchScalarGridSpec(
            num_scalar_prefetch=1, grid=(4, r // tb),
            in_specs=[pl.BlockSpec((None, None, tb, c), lambda q, i, c_ref: (c_ref[0], q, i, 0)),
                      pl.BlockSpec((None, tb, c), lambda q, i, c_ref: (q, i, 0))],
            out_specs=pl.BlockSpec((None, tb, c), lambda q, i, c_ref: (q, i, 0))),
        out_shape=jax.ShapeDtypeStruct((4, r, c), BF16),
        compiler_params=_cparams(("parallel", "parallel")),
    )(core_idx, g, recv)


def _adamw(w, parts, m, v, name):
    n, r, c = parts.shape
    tb = _tile(r, 256, 8)
    b1c = 1.0 - ADAM_B1 ** ADAM_STEP
    b2c = 1.0 - ADAM_B2 ** ADAM_STEP

    def body(w_ref, p_ref, m_ref, v_ref, g_ref, d_ref, nm_ref, nv_ref):
        g = p_ref[0].astype(F32)
        for k in range(1, n):
            g = g + p_ref[k].astype(F32)
        m_new = ADAM_B1 * m_ref[...] + (1.0 - ADAM_B1) * g
        v_new = ADAM_B2 * v_ref[...] + (1.0 - ADAM_B2) * (g * g)
        m_hat = m_new / b1c
        v_hat = v_new / b2c
        g_ref[...] = g
        d_ref[...] = -ADAM_LR * (m_hat / (jnp.sqrt(v_hat) + ADAM_EPS) + ADAM_WD * w_ref[...])
        nm_ref[...] = m_new
        nv_ref[...] = v_new

    row = pl.BlockSpec((tb, c), lambda i: (i, 0))
    return pl.pallas_call(
        body, name=name, grid=(r // tb,),
        in_specs=[row, pl.BlockSpec((n, tb, c), lambda i: (0, i, 0)), row, row],
        out_specs=[row] * 4,
        out_shape=[jax.ShapeDtypeStruct((r, c), F32)] * 4,
        compiler_params=_cparams(("parallel",)),
    )(w, parts, m, v)


ANY = pl.BlockSpec(memory_space=pl.ANY)


def _place():
    return lax.axis_index("x"), lax.axis_index("y"), lax.axis_index("c")


def _all_gather(shards, kinds, name):
    nm = len(shards)

    def body(*refs):
        start, finish = _gather_steps(refs[:nm], refs[nm:2 * nm], kinds, *refs[2 * nm:])
        start()
        finish()

    return pl.pallas_call(
        body, name=name,
        out_shape=_gather_out_shapes(shards, kinds),
        in_specs=[ANY] * nm, out_specs=[ANY] * nm,
        scratch_shapes=_gather_scratch(nm),
    )(*shards)


def _gather_out_shapes(shards, kinds):
    def full(sh, kind):
        a, b = sh.shape
        return {"row": (N_DEV * a, b), "col": (a, N_DEV * b), "stack": (N_DEV, a, b)}[kind]

    return [jax.ShapeDtypeStruct(full(sh, kd), sh.dtype) for sh, kd in zip(shards, kinds)]


def _gather_scratch(nm):
    return [pltpu.SemaphoreType.DMA((7 * nm,)), pltpu.SemaphoreType.DMA((7 * nm,)), pltpu.SemaphoreType.DMA((nm,))]


def _gather_steps(x_refs, out_refs, kinds, send_sems, recv_sems, local_sems):
    nm = len(x_refs)
    x, y, cc = _place()
    me, sibling = (x, y, cc), (x, y, 1 - cc)
    chips = [(1 - x, y), (x, 1 - y), (1 - x, 1 - y)]

    def slot(mi, px, py, pc):
        d = 4 * px + 2 * py + pc
        a, b = x_refs[mi].shape
        if kinds[mi] == "row":
            return out_refs[mi].at[pl.ds(pl.multiple_of(d * a, a), a), :]
        if kinds[mi] == "col":
            return out_refs[mi].at[:, pl.ds(pl.multiple_of(d * b, 128), b)]
        return out_refs[mi].at[d]

    def copy(mi, k, block, to, src=None):
        return pltpu.make_async_remote_copy(
            src_ref=slot(mi, *block) if src is None else src, dst_ref=slot(mi, *block),
            send_sem=send_sems.at[7 * mi + k], recv_sem=recv_sems.at[7 * mi + k],
            device_id=to, device_id_type=MESH)

    def own_copies():
        mine = [pltpu.make_async_copy(x_refs[mi], slot(mi, *me), local_sems.at[mi]) for mi in range(nm)]
        first = []
        for mi in range(nm):
            first.append(copy(mi, 0, me, sibling, src=x_refs[mi]))
            first += [copy(mi, 1 + j, me, (*chip, cc), src=x_refs[mi]) for j, chip in enumerate(chips)]
        return mine, first

    def start():
        mine, first = own_copies()
        for cp in mine + first:
            cp.start()

    def finish():
        mine, first = own_copies()
        passed = []
        for j, chip in enumerate(chips):
            for mi in range(nm):
                copy(mi, 1 + j, (*chip, cc), me).wait_recv()
                passed.append(copy(mi, 4 + j, (*chip, cc), sibling))
                passed[-1].start()
        for mi in range(nm):
            copy(mi, 0, sibling, me).wait_recv()
            for j, chip in enumerate(chips):
                copy(mi, 4 + j, (*chip, 1 - cc), me).wait_recv()
        for cp in first + passed:
            cp.wait_send()
        for cp in mine:
            cp.wait()

    return start, finish


def _pair_exchange(gs, name):
    nm = len(gs)

    def body(*refs):
        g_refs, recv_refs = refs[:nm], refs[nm:2 * nm]
        send_sems, recv_sems = refs[2 * nm:]
        x, y, cc = _place()
        copies = [pltpu.make_async_remote_copy(
            src_ref=g_refs[mi].at[1 - cc], dst_ref=recv_refs[mi],
            send_sem=send_sems.at[mi], recv_sem=recv_sems.at[mi], device_id=(x, y, 1 - cc), device_id_type=MESH)
            for mi in range(nm)]
        for cp in copies:
            cp.start()
        for cp in copies:
            cp.wait_recv()
        for cp in copies:
            cp.wait_send()

    return pl.pallas_call(
        body, name=name,
        out_shape=[jax.ShapeDtypeStruct(g.shape[1:], g.dtype) for g in gs],
        in_specs=[ANY] * nm, out_specs=[ANY] * nm,
        scratch_shapes=[pltpu.SemaphoreType.DMA((nm,)), pltpu.SemaphoreType.DMA((nm,))],
    )(*gs)


def _chip_exchange(parts, name):
    nm = len(parts)

    def body(*refs):
        start, finish = _exchange_steps(refs[:nm], refs[nm:2 * nm], *refs[2 * nm:])
        start()
        finish()

    return pl.pallas_call(
        body, name=name,
        out_shape=[jax.ShapeDtypeStruct(p.shape, p.dtype) for p in parts],
        in_specs=[ANY] * nm, out_specs=[ANY] * nm,
        scratch_shapes=_exchange_scratch(nm),
    )(*parts)


def _exchange_scratch(nm):
    return [pltpu.SemaphoreType.DMA((4 * nm,)), pltpu.SemaphoreType.DMA((4 * nm,)), pltpu.SemaphoreType.DMA((nm,))]


def _exchange_steps(p_refs, out_refs, send_sems, recv_sems, local_sems):
    nm = len(p_refs)
    x, y, cc = _place()
    mine = 2 * x + y
    others = [(1 - x, y), (x, 1 - y), (1 - x, 1 - y)]

    def own_copies():
        keeps = [pltpu.make_async_copy(p_refs[mi].at[mine], out_refs[mi].at[mine], local_sems.at[mi])
                 for mi in range(nm)]
        sends = []
        for px, py in others:
            q = 2 * px + py
            for mi in range(nm):
                sends.append(pltpu.make_async_remote_copy(
                    src_ref=p_refs[mi].at[q], dst_ref=out_refs[mi].at[mine],
                    send_sem=send_sems.at[4 * mi + q], recv_sem=recv_sems.at[4 * mi + mine],
                    device_id=(px, py, cc), device_id_type=MESH))
        return keeps, sends

    def start():
        keeps, sends = own_copies()
        for cp in keeps + sends:
            cp.start()

    def finish():
        keeps, sends = own_copies()
        for px, py in others:
            q = 2 * px + py
            for mi in range(nm):
                pltpu.make_async_remote_copy(
                    src_ref=p_refs[mi].at[q], dst_ref=out_refs[mi].at[q],
                    send_sem=send_sems.at[4 * mi + q], recv_sem=recv_sems.at[4 * mi + q],
                    device_id=(px, py, cc), device_id_type=MESH).wait_recv()
        for cp in sends:
            cp.wait_send()
        for cp in keeps:
            cp.wait()

    return start, finish


def _all_reduce_small(v, name):
    shape = v.shape

    def body(v_ref, out_ref, buf, send_sems, recv_sems):
        x, y, cc = _place()
        me = 4 * x + 2 * y + cc
        buf[me] = v_ref[...]
        flips = [(a, b, d) for a in (0, 1) for b in (0, 1) for d in (0, 1)][1:]
        copies = []
        for k, (a, b, d) in enumerate(flips):
            peer = (x ^ a, y ^ b, cc ^ d)
            copies.append(pltpu.make_async_remote_copy(
                src_ref=v_ref, dst_ref=buf.at[me],
                send_sem=send_sems.at[k], recv_sem=recv_sems.at[k], device_id=peer, device_id_type=MESH))
        for cp in copies:
            cp.start()
        for k, (a, b, d) in enumerate(flips):
            peer_id = 4 * (x ^ a) + 2 * (y ^ b) + (cc ^ d)
            pltpu.make_async_remote_copy(
                src_ref=v_ref, dst_ref=buf.at[peer_id],
                send_sem=send_sems.at[k], recv_sem=recv_sems.at[k], device_id=(x, y, cc), device_id_type=MESH
            ).wait_recv()
        for cp in copies:
            cp.wait_send()
        total = buf[0]
        for k in range(1, N_DEV):
            total = total + buf[k]
        out_ref[...] = total

    vm = pl.BlockSpec(memory_space=pltpu.VMEM)
    return pl.pallas_call(
        body, name=name,
        out_shape=jax.ShapeDtypeStruct(shape, F32),
        in_specs=[vm], out_specs=vm,
        scratch_shapes=[pltpu.VMEM((N_DEV,) + shape, F32), pltpu.SemaphoreType.DMA((7,)), pltpu.SemaphoreType.DMA((7,))],
    )(v)


def _gather_kind(name, shape):
    if name not in COL_SHARDED:
        return "row"
    return "col" if shape[1] % 128 == 0 else "stack"


def _slab_of(name, shape):
    if name not in COL_SHARDED:
        return ("row", shape[0])
    return ("col", shape[1]) if shape[1] % 128 == 0 else None


def _to_slabs(g, shape):
    kk, nn = shape
    return g.reshape(kk, 4, 2, nn).transpose(2, 1, 0, 3)


SMALL_ROWS = 8


def _pack_small(arrs):
    rows = [arrs[n] for n in SMALL[:5]]
    last = jnp.concatenate([arrs["q_norm1"], arrs["kv_norm1"], arrs["b_f2"]])
    rows.append(jnp.pad(last, (0, PACK_COLS - last.shape[0])))
    rows += [jnp.zeros((PACK_COLS,), F32)] * (SMALL_ROWS - len(rows))
    return jnp.stack(rows)


def _unpack_small(p):
    out = {n: p[k] for k, n in enumerate(SMALL[:5])}
    out["q_norm1"] = p[5, :MLA_Q_RANK]
    out["kv_norm1"] = p[5, MLA_Q_RANK:MLA_Q_RANK + MLA_KV_RANK]
    out["b_f2"] = p[5, MLA_Q_RANK + MLA_KV_RANK:MLA_Q_RANK + MLA_KV_RANK + N_HEADS]
    return out


N_QKV = 3 * D_INNER
N_MAIN = 4 * D_INNER


def _rope(x, pos):
    r = x.shape[-1]
    inv_freq = ROPE_BASE ** (-jnp.arange(0, r, 2, dtype=F32) / r)
    ang = pos.astype(F32)[:, None, None] * inv_freq
    cos, sin = jnp.cos(ang), jnp.sin(ang)
    x1, x2 = x[..., : r // 2], x[..., r // 2:]
    return jnp.concatenate([x1 * cos - x2 * sin, x1 * sin + x2 * cos], axis=-1)


def _forward_loss(carriers, small, x, wfull, late, late_grads, slabs, pos, target):
    s = x.shape[0]
    wfull = dict(wfull)
    carriers = dict(carriers)

    def out_proj(y, w_out, tag, x_skip):
        return mm(y, wfull[w_out], carriers[w_out], slabs[w_out], name=tag + "_out", skip=x_skip)

    def sb_layer(x, ln, w_in, w_out, tag, gather=None, late=None):
        h, x = rmsnorm(x, small[ln], tag + "_ln", skip=True)
        qkv, gate = in_proj(h, wfull[w_in], (carriers[w_in],), True, tag)
        y, gathered, late_carriers = sb_core(qkv, gate, tag, gather, late)
        return out_proj(y, w_out, tag, x), gathered, late_carriers

    names, shards, kinds, shapes = late
    car_keys, car_shapes, to_slabs = late_grads
    x, gathered, late_carriers = sb_layer(x, "ln0", "w_in0", "w_out0", "l0", (shards, kinds),
                                          (carriers["slots"], car_shapes, to_slabs))
    carriers.update(zip(car_keys, late_carriers))
    for n, kind, full in zip(names, kinds, gathered):
        wfull[n] = full.transpose(1, 0, 2).reshape(shapes[n][0], -1) if kind == "stack" else full

    h, x = rmsnorm(x, small["ln1"], "l1_ln", skip=True)
    proj = mm(h, wfull["w_in1"], carriers["w_in1"], slabs["w_in1"], name="l1_in")
    i1, i2, i3 = MLA_Q_RANK, MLA_Q_RANK + MLA_KV_RANK, MLA_Q_RANK + MLA_KV_RANK + MLA_ROPE
    w_qb = jnp.pad(wfull["w_qb1"].reshape(MLA_Q_RANK, N_HEADS, MLA_NOPE + MLA_ROPE),
                   ((0, 0), (0, 0), (0, MLA_QK_PAD - MLA_NOPE - MLA_ROPE))).reshape(MLA_Q_RANK, N_HEADS * MLA_QK_PAD)
    qp = mm(rmsnorm(proj[:, :i1], small["q_norm1"], "l1_qn"), w_qb, carriers["w_qb1"], None, name="l1_qb")
    kv = mm(rmsnorm(proj[:, i1:i2], small["kv_norm1"], "l1_kvn"), wfull["w_kvb1"], carriers["w_kvb1"],
            slabs["w_kvb1"], BF16, name="l1_kvb")
    kr = jnp.pad(_rope(proj[:, i2:i3][:, None, :], pos)[:, 0, :], ((0, 0), (0, HEAD_DIM - MLA_ROPE)))
    inv_freq = ROPE_BASE ** (-jnp.arange(0, MLA_ROPE, 2, dtype=F32) / MLA_ROPE)
    ang = pos.astype(F32)[:, None] * inv_freq
    cos, sin = jnp.cos(ang), jnp.sin(ang)
    rest = HEAD_DIM - MLA_ROPE
    cosv = jnp.concatenate([cos, cos, jnp.ones((s, rest), F32)], axis=1)
    sinv = jnp.concatenate([-sin, sin, jnp.zeros((s, rest), F32)], axis=1)
    y = mla_core(qp, kv, kr, proj[:, i3:], cosv, sinv, "l1")
    x = out_proj(y, "w_out1", "l1", x)

    h, x = rmsnorm(x, small["ln2"], "l2_ln", skip=True)
    qkv, gate = in_proj(h, wfull["w_in2"], (carriers["w_in2_qkv"], carriers["w_in2_gate"]), False, "l2")
    f_logit = mm(h, wfull["w_in2"][:, N_MAIN:], carriers["w_in2_f"], None, name="l2_f") + small["b_f2"]
    c = jnp.cumsum(jax.nn.log_sigmoid(f_logit), axis=0)
    x = out_proj(fox_core(qkv, gate, c, "l2"), "w_out2", "l2", x)

    x, _, _ = sb_layer(x, "ln3", "w_in3", "w_out3", "l3")
    return sq_loss(rmsnorm(x, small["final_norm"], "final_ln"), target)


def kernel(x, positions, ln0, w_in0, w_out0, ln1, w_in1, q_norm1, w_qb1, kv_norm1, w_kvb1, w_out1, ln2, w_in2, b_f2, w_out2, ln3, w_in3, w_out3, final_norm, loss_target, m_ln0, m_w_in0, m_w_out0, m_ln1, m_w_in1, m_q_norm1, m_w_qb1, m_kv_norm1, m_w_kvb1, m_w_out1, m_ln2, m_w_in2, m_b_f2, m_w_out2, m_ln3, m_w_in3, m_w_out3, m_final_norm, v_ln0, v_w_in0, v_w_out0, v_ln1, v_w_in1, v_q_norm1, v_w_qb1, v_kv_norm1, v_w_kvb1, v_w_out1, v_ln2, v_w_in2, v_b_f2, v_w_out2, v_ln3, v_w_in3, v_w_out3, v_final_norm):
    args = dict(locals())
    w = {n: args[n] for n in ALL_W}
    m = {n: args["m_" + n] for n in ALL_W}
    v = {n: args["v_" + n] for n in ALL_W}
    shapes = {n: w[n].shape for n in BIG}
    kinds = [_gather_kind(n, shapes[n]) for n in BIG]
    slabs = {n: _slab_of(n, shapes[n]) for n in BIG}

    first = BIG[:2]
    late_names = BIG[2:]
    gathered = _all_gather([w[n].astype(BF16) for n in first], kinds[:2], "gather_w")
    wfull = dict(zip(first, gathered))
    late = (late_names, [w[n].astype(BF16) for n in late_names], kinds[2:], shapes)

    d_model = shapes["w_in0"][0]
    car_shapes = {}
    for n in late_names:
        if n == "w_in2":
            car_shapes["w_in2_qkv"] = (d_model, N_QKV)
            car_shapes["w_in2_gate"] = (d_model, D_INNER)
            car_shapes["w_in2_f"] = (d_model, N_DEV * shapes[n][1] - N_MAIN)
        elif n == "w_qb1":
            car_shapes[n] = (MLA_Q_RANK, N_HEADS * MLA_QK_PAD)
        elif slabs[n] is None:
            car_shapes[n] = (shapes[n][0], N_DEV * shapes[n][1])
        else:
            car_shapes[n] = (2, 4) + shapes[n]
    car_keys = list(car_shapes)

    def to_slabs(cts):
        g_car = dict(zip(car_keys, cts))
        out = []
        for n in late_names:
            if n == "w_in2":
                g = jnp.concatenate([g_car["w_in2_qkv"], g_car["w_in2_gate"], g_car["w_in2_f"]], axis=1)
                out.append(_to_slabs(g, shapes[n]))
            elif n == "w_qb1":
                g = g_car[n].reshape(MLA_Q_RANK, N_HEADS, MLA_QK_PAD)[:, :, :MLA_NOPE + MLA_ROPE]
                out.append(_to_slabs(g.reshape(MLA_Q_RANK, -1), shapes[n]))
            elif slabs[n] is None:
                out.append(_to_slabs(g_car[n], shapes[n]))
            else:
                out.append(g_car[n])
        return out

    late_grads = (car_keys, [car_shapes[k] for k in car_keys], to_slabs)
    carriers = {n: jnp.zeros((2, 4) + shapes[n], F32) for n in first}
    carriers["slots"] = [jnp.zeros((4,) + shapes[n], BF16) for n in late_names]
    small = {n: w[n] for n in SMALL}

    def local_loss(carriers, small, x_seq):
        return _forward_loss(carriers, small, x_seq, wfull, late, late_grads, slabs, positions[0], loss_target[0])

    loss_local, (g_car, g_small, g_x) = jax.value_and_grad(local_loss, argnums=(0, 1, 2))(carriers, small, x[0])
    loss = lax.psum(loss_local, ("x", "y", "c"))

    core_idx = lax.axis_index("c").astype(jnp.int32).reshape(1)
    from_sibling = _pair_exchange([g_car[n] for n in first], "pair_exchange")
    chip_part = [_pair_sum(core_idx, g_car[n], r, "pair_sum_" + n) for n, r in zip(first, from_sibling)]
    by_chip = dict(zip(first, _chip_exchange(chip_part, "chip_exchange")))
    by_chip.update(zip(late_names, g_car["slots"]))
    big = [{}, {}, {}, {}]
    for n in BIG:
        for k, t in enumerate(_adamw(w[n], by_chip[n], m[n], v[n], "adamw_" + n)):
            big[k][n] = t

    g_small_sum = _all_reduce_small(_pack_small(g_small), "reduce_small")
    sm = _adamw(_pack_small(w), g_small_sum[None], _pack_small(m), _pack_small(v), "adamw_small")
    small_out = [_unpack_small(t) for t in sm]

    outs = [loss, g_x[None]]
    for k in range(4):
        outs += [small_out[k][n] if n in small_out[k] else big[k][n] for n in ALL_W]
    return tuple(outs)
```

```python
import jax
import jax.numpy as jnp
from jax import lax
from jax.experimental import pallas as pl
from jax.experimental.pallas import tpu as pltpu

F32 = jnp.float32
BF16 = jnp.bfloat16
MESH = pl.DeviceIdType.MESH

N_DEV = 8
N_HEADS = 16
HEAD_DIM = 128
D_INNER = N_HEADS * HEAD_DIM
MLA_Q_RANK = 256
MLA_KV_RANK = 128
MLA_NOPE = 128
MLA_ROPE = 64
MLA_QK_PAD = 256
MLA_CHUNK = 64
ROPE_BASE = 10000.0
EPS = 1e-6
NEG = -1e30
SB_CUT = 104.0

ADAM_LR = 0.001
ADAM_B1 = 0.9
ADAM_B2 = 0.999
ADAM_EPS = 1e-08
ADAM_WD = 0.01
ADAM_STEP = 10

PACK_COLS = 1024
VMEM_LIMIT = 56 * 1024 * 1024

BIG = ["w_in0", "w_out0", "w_in1", "w_qb1", "w_kvb1", "w_out1", "w_in2", "w_out2", "w_in3", "w_out3"]
COL_SHARDED = {"w_in0", "w_in1", "w_qb1", "w_kvb1", "w_in2", "w_in3"}
SMALL = ["ln0", "ln1", "ln2", "ln3", "final_norm", "q_norm1", "kv_norm1", "b_f2"]
ALL_W = ["ln0", "w_in0", "w_out0", "ln1", "w_in1", "q_norm1", "w_qb1", "kv_norm1", "w_kvb1", "w_out1",
         "ln2", "w_in2", "b_f2", "w_out2", "ln3", "w_in3", "w_out3", "final_norm"]


def _cparams(sem=None):
    return pltpu.CompilerParams(dimension_semantics=sem, vmem_limit_bytes=VMEM_LIMIT)


def _tile(dim, cap, align):
    if dim <= cap:
        return dim
    t = (cap // align) * align
    while t >= align:
        if dim % t == 0:
            return t
        t -= align
    return dim


def _dot(a, b, dims):
    return lax.dot_general(a, b, (dims, ((), ())), preferred_element_type=F32)


def _dot_nn(a, b):
    return _dot(a, b, ((1,), (0,)))


def _dot_nt(a, b):
    return _dot(a, b, ((1,), (1,)))


def _transpose_bf16(x):
    return x.astype(F32).T.astype(BF16)


def _matmul(a, b, mode, col0=0, n_cols=None, out_dtype=F32, name="mm", res=None):
    assert mode in ("nn", "nt")
    if mode == "nn":
        m, r = a.shape
        n = n_cols or b.shape[1]
        tn, tr = _tile(n, 1024, 128), _tile(r, 1024, 128)
        tm = _tile(m, 1024 if tn <= 1024 else 512, 8)
        c0 = col0 // tn
        a_spec = pl.BlockSpec((tm, tr), lambda i, j, k: (i, k))
        b_spec = pl.BlockSpec((tr, tn), lambda i, j, k: (k, j + c0))
        dims = ((1,), (0,))
        assert col0 % tn == 0
    else:
        m, r = a.shape
        n = b.shape[0]
        tn, tr = _tile(n, 1024, 128), _tile(r, 1024, 128)
        tm = _tile(m, 1024 if tr <= 1024 else 512, 8)
        c0 = col0 // tr
        a_spec = pl.BlockSpec((tm, tr), lambda i, j, k: (i, k))
        b_spec = pl.BlockSpec((tn, tr), lambda i, j, k: (j, k + c0))
        dims = ((1,), (1,))
        assert col0 % tr == 0
    nr = r // tr
    out_spec = pl.BlockSpec((tm, tn), lambda i, j, k: (i, j))

    def body(*refs):
        a_ref, b_ref, o_ref, acc_ref = refs[0], refs[1], refs[-2], refs[-1]
        k = pl.program_id(2)

        @pl.when(k == 0)
        def _():
            acc_ref[...] = jnp.zeros_like(acc_ref)

        acc_ref[...] += _dot(a_ref[...].astype(BF16), b_ref[...].astype(BF16), dims)

        @pl.when(k == nr - 1)
        def _():
            out = acc_ref[...] if res is None else acc_ref[...] + refs[2][...]
            o_ref[...] = out.astype(out_dtype)

    return pl.pallas_call(
        body,
        name=name,
        grid=(m // tm, n // tn, nr),
        in_specs=[a_spec, b_spec] + ([] if res is None else [out_spec]),
        out_specs=out_spec,
        out_shape=jax.ShapeDtypeStruct((m, n), out_dtype),
        scratch_shapes=[pltpu.VMEM((tm, tn), F32)],
        compiler_params=_cparams(("parallel", "parallel", "arbitrary")),
    )(*([a, b] if res is None else [a, b, res]))


def _matmul_split(a, w, n1, n2, name):
    m, r = a.shape
    tm, tn = _tile(m, 1024, 8), 1024
    assert n1 % tn == 0 and n2 % tn == 0 and r <= 1024
    n1b = n1 // tn

    def body(a_ref, w_ref, o1_ref, o2_ref):
        j = pl.program_id(1)
        out = _dot_nn(a_ref[...].astype(BF16), w_ref[...])

        @pl.when(j < n1b)
        def _():
            o1_ref[...] = out.astype(BF16)

        @pl.when(j >= n1b)
        def _():
            o2_ref[...] = out

    return pl.pallas_call(
        body, name=name, grid=(m // tm, (n1 + n2) // tn),
        in_specs=[pl.BlockSpec((tm, r), lambda i, j: (i, 0)), pl.BlockSpec((r, tn), lambda i, j: (0, j))],
        out_specs=[pl.BlockSpec((tm, tn), lambda i, j: (i, jnp.minimum(j, n1b - 1))),
                   pl.BlockSpec((tm, tn), lambda i, j: (i, jnp.maximum(j - n1b, 0)))],
        out_shape=[jax.ShapeDtypeStruct((m, n1), BF16), jax.ShapeDtypeStruct((m, n2), F32)],
        compiler_params=_cparams(("arbitrary", "arbitrary")),
    )(a, w)


def _matmul_dx2(g1, g2, w, name):
    m, r1 = g1.shape
    r2 = g2.shape[1]
    n = w.shape[0]
    tm, tn, tr = _tile(m, 1024, 8), _tile(n, 1024, 128), 1024
    assert r1 % tr == 0 and r2 % tr == 0
    n1b, nr = r1 // tr, (r1 + r2) // tr

    def body(g1_ref, g2_ref, w_ref, o_ref, acc_ref):
        k = pl.program_id(2)

        @pl.when(k == 0)
        def _():
            acc_ref[...] = jnp.zeros_like(acc_ref)

        @pl.when(k < n1b)
        def _():
            acc_ref[...] += _dot_nt(g1_ref[...].astype(BF16), w_ref[...])

        @pl.when(k >= n1b)
        def _():
            acc_ref[...] += _dot_nt(g2_ref[...].astype(BF16), w_ref[...])

        @pl.when(k == nr - 1)
        def _():
            o_ref[...] = acc_ref[...]

    return pl.pallas_call(
        body, name=name, grid=(m // tm, n // tn, nr),
        in_specs=[pl.BlockSpec((tm, tr), lambda i, j, k: (i, jnp.minimum(k, n1b - 1))),
                  pl.BlockSpec((tm, tr), lambda i, j, k: (i, jnp.maximum(k - n1b, 0))),
                  pl.BlockSpec((tn, tr), lambda i, j, k: (j, k))],
        out_specs=pl.BlockSpec((tm, tn), lambda i, j, k: (i, j)),
        out_shape=jax.ShapeDtypeStruct((m, n), F32),
        scratch_shapes=[pltpu.VMEM((tm, tn), F32)],
        compiler_params=_cparams(("parallel", "parallel", "arbitrary")),
    )(g1, g2, w)


def _transpose_cast(a, name):
    r, m = a.shape
    tr = _tile(r, 512, 128)

    def body(a_ref, o_ref):
        o_ref[...] = a_ref[...].astype(F32).T.astype(BF16)

    return pl.pallas_call(
        body, name=name, grid=(r // tr,),
        in_specs=[pl.BlockSpec((tr, m), lambda i: (i, 0))],
        out_specs=pl.BlockSpec((m, tr), lambda i: (0, i)),
        out_shape=jax.ShapeDtypeStruct((m, r), BF16),
        compiler_params=_cparams(("parallel",)),
    )(a)


def _matmul_dw(a, b1, b2, slab, name):
    m, r = a.shape
    n1 = b1.shape[1]
    n = n1 + (b2.shape[1] if b2 is not None else 0)
    tr = _tile(r, 1024, 128)
    if slab is None:
        tm, tn = _tile(m, 1024, 128), _tile(n1, 1024, 128)
        out_spec = pl.BlockSpec((tm, tn), lambda i, j, k: (i, j))
        out_shape = (m, n)
    elif slab[0] == "col":
        tm, tn = _tile(m, 1024, 128), slab[1]
        out_spec = pl.BlockSpec((None, None, tm, tn), lambda i, j, k: (j % 2, j // 2, i, 0))
        out_shape = (2, 4, m, tn)
        assert n == N_DEV * tn
    else:
        tm, tn = slab[1], _tile(n, 1024, 128)
        out_spec = pl.BlockSpec((None, None, tm, tn), lambda i, j, k: (i % 2, i // 2, 0, j))
        out_shape = (2, 4, tm, n)
        assert m == N_DEV * tm
    assert n1 % tn == 0 and n % tn == 0
    if tn > 1024:
        tr = _tile(r, 256, 128)
    n1b = n1 // tn
    nr = r // tr

    def body(*refs):
        a_ref, b_refs, o_ref, acc_ref = refs[0], refs[1:-2], refs[-2], refs[-1]
        j = pl.program_id(1)
        k = pl.program_id(2)

        @pl.when(k == 0)
        def _():
            acc_ref[...] = jnp.zeros_like(acc_ref)

        at = a_ref[...]
        if b2 is None:
            acc_ref[...] += _dot_nn(at, b_refs[0][...].astype(BF16))
        else:
            @pl.when(j < n1b)
            def _():
                acc_ref[...] += _dot_nn(at, b_refs[0][...].astype(BF16))

            @pl.when(j >= n1b)
            def _():
                acc_ref[...] += _dot_nn(at, b_refs[1][...].astype(BF16))

        @pl.when(k == nr - 1)
        def _():
            o_ref[...] = acc_ref[...]

    in_specs = [pl.BlockSpec((tm, tr), lambda i, j, k: (i, k))]
    args = [a, b1]
    if b2 is None:
        in_specs.append(pl.BlockSpec((tr, tn), lambda i, j, k: (k, j)))
    else:
        in_specs.append(pl.BlockSpec((tr, tn), lambda i, j, k: (jnp.where(j < n1b, k, nr - 1), jnp.minimum(j, n1b - 1))))
        in_specs.append(pl.BlockSpec((tr, tn), lambda i, j, k: (jnp.where(j < n1b, 0, k), jnp.maximum(j - n1b, 0))))
        args.append(b2)
    return pl.pallas_call(
        body, name=name, grid=(m // tm, n // tn, nr),
        in_specs=in_specs, out_specs=out_spec,
        out_shape=jax.ShapeDtypeStruct(out_shape, F32),
        scratch_shapes=[pltpu.VMEM((tm, tn), F32)],
        compiler_params=_cparams(("parallel", "parallel", "arbitrary")),
    )(*args)


def mm(a, w, carrier, slab=None, out_dtype=F32, name="mm", skip=None):
    @jax.custom_vjp
    def f(a, w, carrier, *skip):
        return _matmul(a, w, "nn", 0, None, out_dtype, name + "_fwd", *skip)

    def fwd(a, w, carrier, *skip):
        return _matmul(a, w, "nn", 0, None, out_dtype, name + "_fwd", *skip), (a, w)

    def bwd(res, g):
        a, w = res
        da = _matmul(g, w, "nt", 0, None, F32, name + "_dx")
        dw = _matmul_dw(_transpose_cast(a, name + "_t"), g, None, slab, name + "_dw")
        return (da, jnp.zeros_like(w), dw) + ((g,) if skip is not None else ())

    f.defvjp(fwd, bwd)
    return f(a, w, carrier, *(() if skip is None else (skip,)))


def in_proj(h, w, carriers, slab, name):
    def run(h, w):
        return tuple(_matmul_split(h, w, N_QKV, D_INNER, name + "_proj"))

    @jax.custom_vjp
    def f(h, w, *cars):
        return run(h, w)

    def fwd(h, w, *cars):
        return run(h, w), (h, w)

    def bwd(res, g):
        h, w = res
        g_qkv, g_gate = g
        dh = _matmul_dx2(g_qkv, g_gate, w, name + "_dx")
        h_t = _transpose_cast(h, name + "_t")
        if slab:
            dws = (_matmul_dw(h_t, g_qkv, g_gate, ("col", PACK_COLS), name + "_dw"),)
        else:
            dws = (_matmul_dw(h_t, g_qkv, None, None, name + "_qkv_dw"),
                   _matmul_dw(h_t, g_gate, None, None, name + "_g_dw"))
        return (dh, jnp.zeros_like(w)) + dws

    f.defvjp(fwd, bwd)
    return f(h, w, *carriers)


def _rms_fwd(x, g, name):
    s, d = x.shape
    tm = _tile(s, 512, 8)

    def body(x_ref, g_ref, y_ref):
        x = x_ref[...]
        r = lax.rsqrt(jnp.mean(x * x, axis=-1, keepdims=True) + EPS)
        y_ref[...] = x * r * g_ref[...]

    return pl.pallas_call(
        body, name=name, grid=(s // tm,),
        in_specs=[pl.BlockSpec((tm, d), lambda i: (i, 0)), pl.BlockSpec((1, d), lambda i: (0, 0))],
        out_specs=pl.BlockSpec((tm, d), lambda i: (i, 0)),
        out_shape=jax.ShapeDtypeStruct((s, d), F32),
        compiler_params=_cparams(("parallel",)),
    )(x, g)


def _rms_bwd(x, g, dy, name, dskip=None):
    s, d = x.shape
    tm = _tile(s, 512, 8)

    def body(*refs):
        x_ref, g_ref, dy_ref, dx_ref, dg_ref = refs[0], refs[1], refs[2], refs[-2], refs[-1]

        @pl.when(pl.program_id(0) == 0)
        def _():
            dg_ref[...] = jnp.zeros_like(dg_ref)

        x = x_ref[...]
        dy = dy_ref[...]
        r = lax.rsqrt(jnp.mean(x * x, axis=-1, keepdims=True) + EPS)
        xh = x * r
        dg_ref[...] += jnp.sum(dy * xh, axis=0, keepdims=True)
        dxh = dy * g_ref[...]
        dx = r * (dxh - xh * jnp.mean(dxh * xh, axis=-1, keepdims=True))
        dx_ref[...] = dx if dskip is None else dx + refs[3][...]

    row = pl.BlockSpec((tm, d), lambda i: (i, 0))
    vec = pl.BlockSpec((1, d), lambda i: (0, 0))
    return pl.pallas_call(
        body, name=name, grid=(s // tm,),
        in_specs=[row, vec, row] + ([] if dskip is None else [row]),
        out_specs=[row, vec],
        out_shape=[jax.ShapeDtypeStruct((s, d), F32), jax.ShapeDtypeStruct((1, d), F32)],
        compiler_params=_cparams(("arbitrary",)),
    )(*([x, g, dy] if dskip is None else [x, g, dy, dskip]))


def rmsnorm(x, g, name="rms", skip=False):
    def run(x, g):
        y = _rms_fwd(x, g.reshape(1, -1), name + "_fwd")
        return (y, x) if skip else y

    @jax.custom_vjp
    def f(x, g):
        return run(x, g)

    def fwd(x, g):
        return run(x, g), (x, g)

    def bwd(res, ct):
        x, g = res
        dy, dskip = ct if skip else (ct, None)
        dx, dg = _rms_bwd(x, g.reshape(1, -1), dy, name + "_bwd", dskip)
        return dx, dg.reshape(-1)

    f.defvjp(fwd, bwd)
    return f(x, g)


def _gate_fwd(o, gate, name):
    s = o.shape[0]
    tm = 256

    def body(o_ref, g_ref, y_ref):
        g = g_ref[...]
        y_ref[...] = o_ref[...] * (g / (1.0 + jnp.exp(-g)))

    row = pl.BlockSpec((tm, D_INNER), lambda i: (i, 0))
    return pl.pallas_call(
        body, name=name, grid=(s // tm,),
        in_specs=[row, row], out_specs=row,
        out_shape=jax.ShapeDtypeStruct((s, D_INNER), F32),
        compiler_params=_cparams(("parallel",)),
    )(o, gate)


def _gate_bwd(dy, o, gate, name):
    s = o.shape[0]
    tm = 256

    def body(dy_ref, o_ref, g_ref, do_ref, dg_ref, dl_ref):
        g = g_ref[...]
        o = o_ref[...]
        dy = dy_ref[...]
        sg = 1.0 / (1.0 + jnp.exp(-g))
        do = dy * (g * sg)
        do_ref[...] = do.astype(BF16)
        dg_ref[...] = dy * o * (sg * (1.0 + g * (1.0 - sg)))
        prod = do * o
        for h in range(N_HEADS):
            dl_ref[h] = jnp.sum(prod[:, h * HEAD_DIM:(h + 1) * HEAD_DIM], axis=1, keepdims=True)

    row = pl.BlockSpec((tm, D_INNER), lambda i: (i, 0))
    return pl.pallas_call(
        body, name=name, grid=(s // tm,),
        in_specs=[row, row, row],
        out_specs=[row, row, pl.BlockSpec((N_HEADS, tm, 1), lambda i: (0, i, 0))],
        out_shape=[jax.ShapeDtypeStruct((s, D_INNER), BF16), jax.ShapeDtypeStruct((s, D_INNER), F32),
                   jax.ShapeDtypeStruct((N_HEADS, s, 1), F32)],
        compiler_params=_cparams(("parallel",)),
    )(dy, o, gate)


SB_BLK = 256


def _softplus(z):
    return jnp.maximum(z, 0.0) + jnp.log(1.0 + jnp.exp(-jnp.abs(z)))


def _tri_sum(x, tri):
    hi = x.astype(BF16)
    lo = (x - hi.astype(F32)).astype(BF16)
    return _dot_nn(hi, tri) + _dot_nn(lo, tri)


SB_WIN = 2 * SB_BLK


def _sb_tri(kind):
    row = lax.broadcasted_iota(jnp.int32, (SB_BLK, SB_BLK), 0)
    col = lax.broadcasted_iota(jnp.int32, (SB_BLK, SB_BLK), 1)
    return ((row >= col) if kind == "suffix" else (row <= col)).astype(BF16)


def _sb_bounds(i, t):
    hi = (i + 1) * SB_BLK - t * SB_WIN
    return hi, pl.multiple_of(jnp.maximum(hi - SB_WIN, 0), SB_BLK)


def _sb_mask(i, hi, start):
    row = lax.broadcasted_iota(jnp.int32, (SB_BLK, SB_WIN), 0) + i * SB_BLK
    col = lax.broadcasted_iota(jnp.int32, (SB_BLK, SB_WIN), 1) + start
    return jnp.logical_and(col < row, col < hi)


def _sb_window(q, kwin, mask, a_run, tri_suffix, scale):
    b = SB_BLK
    z = _dot_nt(q, kwin) * scale
    sp = _softplus(z)
    ls = jnp.where(mask, -sp, 0.0)
    ls_l, ls_r = ls[:, :b], ls[:, b:]
    suffix = jnp.concatenate([_tri_sum(ls_l, tri_suffix) + jnp.sum(ls_r, axis=1, keepdims=True),
                              _tri_sum(ls_r, tri_suffix)], axis=1)
    w = jnp.where(mask, jnp.exp(z + suffix + a_run), 0.0)
    return z, sp, ls, w


def _sb_fwd(qkv, name, gather=None):
    s = qkv.shape[0]
    b = SB_BLK
    nq = s // b
    scale = HEAD_DIM ** -0.5
    assert s >= SB_WIN
    shards, kinds = gather if gather else ((), ())
    nm = len(shards)

    def body(*refs):
        q_ref, k_ref, v_ref = refs[:3]
        o_ref = refs[3 + nm]
        i = pl.program_id(1)
        if nm:
            start, finish = _gather_steps(refs[3:3 + nm], refs[4 + nm:4 + 2 * nm], kinds, *refs[4 + 2 * nm:])
            first_step = jnp.logical_and(pl.program_id(0) == 0, i == 0)
            last_step = jnp.logical_and(pl.program_id(0) == N_HEADS - 1, i == nq - 1)
            pl.when(first_step)(start)
        q = q_ref[...]
        tri_suffix = _sb_tri("suffix")

        def cond(c):
            t, a_run, _ = c
            return jnp.logical_and((i + 1) * b - t * SB_WIN > 0, jnp.max(a_run) > -SB_CUT)

        def step(c):
            t, a_run, acc = c
            hi, start = _sb_bounds(i, t)
            _, _, ls, w = _sb_window(q, k_ref[pl.ds(start, SB_WIN), :], _sb_mask(i, hi, start), a_run,
                                     tri_suffix, scale)
            acc = acc + _dot_nn(w.astype(BF16), v_ref[pl.ds(start, SB_WIN), :])
            return t + 1, a_run + jnp.sum(ls, axis=1, keepdims=True), acc

        _, _, acc = lax.while_loop(cond, step, (0, jnp.zeros((b, 1), F32), jnp.zeros((b, HEAD_DIM), F32)))
        o_ref[...] = acc
        if nm:
            pl.when(last_step)(finish)

    out = pl.pallas_call(
        body, name=name, grid=(N_HEADS, nq),
        in_specs=[pl.BlockSpec((b, HEAD_DIM), lambda h, i: (i, h)),
                  pl.BlockSpec((s, HEAD_DIM), lambda h, i: (0, N_HEADS + h)),
                  pl.BlockSpec((s, HEAD_DIM), lambda h, i: (0, 2 * N_HEADS + h))] + [ANY] * nm,
        out_specs=[pl.BlockSpec((b, HEAD_DIM), lambda h, i: (i, h))] + [ANY] * nm,
        out_shape=[jax.ShapeDtypeStruct((s, D_INNER), F32)] + _gather_out_shapes(shards, kinds),
        scratch_shapes=_gather_scratch(nm) if nm else [],
        compiler_params=_cparams(("arbitrary", "arbitrary")),
    )(qkv, qkv, qkv, *shards)
    return out[0], list(out[1:])


def _sb_bwd(qkv, do, name, exchange=()):
    s = qkv.shape[0]
    b = SB_BLK
    nq = s // b
    n_win = -(-s // SB_WIN) + 1
    scale = HEAD_DIM ** -0.5
    assert s >= SB_WIN
    nm = len(exchange)

    def body(*refs):
        q_ref, k_ref, v_ref, do_ref = refs[:4]
        dq_ref, dk_ref, dv_ref = refs[4 + nm:7 + nm]
        dkt_s, dvt_s, g_buf, sig_buf = refs[7 + 2 * nm:11 + 2 * nm]
        i = pl.program_id(1)
        if nm:
            start_x, finish_x = _exchange_steps(refs[4:4 + nm], refs[7 + nm:7 + 2 * nm], *refs[11 + 2 * nm:])
            pl.when(jnp.logical_and(pl.program_id(0) == 0, i == 0))(start_x)

        @pl.when(i == 0)
        def _():
            dkt_s[...] = jnp.zeros_like(dkt_s)
            dvt_s[...] = jnp.zeros_like(dvt_s)

        q = q_ref[...]
        dob = do_ref[...]
        q_t = _transpose_bf16(q)
        do_t = _transpose_bf16(dob)
        tri_suffix = _sb_tri("suffix")
        tri_prefix = _sb_tri("prefix")

        def add_halves(acc_ref, start, upd):
            blk = start // b
            acc_ref[blk] += upd[:, :b]
            acc_ref[blk + 1] += upd[:, b:]

        def cond(c):
            t, a_run = c
            return jnp.logical_and((i + 1) * b - t * SB_WIN > 0, jnp.max(a_run) > -SB_CUT)

        def sweep(c):
            t, a_run = c
            hi, start = _sb_bounds(i, t)
            z, sp, ls, w = _sb_window(q, k_ref[pl.ds(start, SB_WIN), :], _sb_mask(i, hi, start), a_run,
                                      tri_suffix, scale)
            g_buf[t] = w * _dot_nt(dob, v_ref[pl.ds(start, SB_WIN), :])
            sig_buf[t] = jnp.exp(z - sp)
            add_halves(dvt_s, start, _dot_nn(do_t, w.astype(BF16)))
            return t + 1, a_run + jnp.sum(ls, axis=1, keepdims=True)

        n_steps, _ = lax.while_loop(cond, sweep, (0, jnp.zeros((b, 1), F32)))

        def back(u, c):
            g_run, dq = c
            t = n_steps - 1 - u
            hi, start = _sb_bounds(i, t)
            g = g_buf[t]
            g_l, g_r = g[:, :b], g[:, b:]
            g_incl = g_run + jnp.concatenate(
                [_tri_sum(g_l, tri_prefix),
                 _tri_sum(g_r, tri_prefix) + jnp.sum(g_l, axis=1, keepdims=True)], axis=1)
            dz = jnp.where(_sb_mask(i, hi, start), (g - sig_buf[t] * g_incl) * scale, 0.0).astype(BF16)
            add_halves(dkt_s, start, _dot_nn(q_t, dz))
            return g_run + jnp.sum(g, axis=1, keepdims=True), dq + _dot_nn(dz, k_ref[pl.ds(start, SB_WIN), :])

        _, dq = lax.fori_loop(0, n_steps, back, (jnp.zeros((b, 1), F32), jnp.zeros((b, HEAD_DIM), F32)))
        dq_ref[...] = dq.astype(BF16)

        @pl.when(i == nq - 1)
        def _():
            for jb in range(nq):
                dk_ref[jb * b:(jb + 1) * b, :] = dkt_s[jb].T.astype(BF16)
                dv_ref[jb * b:(jb + 1) * b, :] = dvt_s[jb].T.astype(BF16)

        if nm:
            pl.when(jnp.logical_and(pl.program_id(0) == N_HEADS - 1, i == nq - 1))(finish_x)

    blk = pl.BlockSpec((b, HEAD_DIM), lambda h, i: (i, h))
    head = pl.BlockSpec((s, HEAD_DIM), lambda h, i: (0, h))
    out = pl.pallas_call(
        body, name=name, grid=(N_HEADS, nq),
        in_specs=[blk,
                  pl.BlockSpec((s, HEAD_DIM), lambda h, i: (0, N_HEADS + h)),
                  pl.BlockSpec((s, HEAD_DIM), lambda h, i: (0, 2 * N_HEADS + h)),
                  blk] + [ANY] * nm,
        out_specs=[blk, head, head] + [ANY] * nm,
        out_shape=[jax.ShapeDtypeStruct((s, D_INNER), BF16)] * 3
        + [jax.ShapeDtypeStruct(p.shape, p.dtype) for p in exchange],
        scratch_shapes=[pltpu.VMEM((nq, HEAD_DIM, b), F32), pltpu.VMEM((nq, HEAD_DIM, b), F32),
                        pltpu.VMEM((n_win, b, SB_WIN), F32), pltpu.VMEM((n_win, b, SB_WIN), F32)]
        + (_exchange_scratch(nm) if nm else []),
        compiler_params=_cparams(("arbitrary", "arbitrary")),
    )(qkv, qkv, qkv, do, *exchange)
    return out[0], out[1], out[2], list(out[3:])


def sb_core(qkv, gate, name, gather=None, late=None):
    shards, kinds = gather if gather else ((), ())
    slots, carrier_shapes, to_slabs = late if late else ((), (), None)
    n_sh = len(shards)

    def run(qkv, gate, *shards):
        o, gathered = _sb_fwd(qkv, name + "_fwd", (shards, kinds) if shards else None)
        carriers = [jnp.zeros(sh, F32) for sh in carrier_shapes]
        return (_gate_fwd(o, gate, name + "_gate"), *gathered, *carriers), o

    @jax.custom_vjp
    def f(qkv, gate, *extra):
        return run(qkv, gate, *extra[:n_sh])[0]

    def fwd(qkv, gate, *extra):
        outs, o = run(qkv, gate, *extra[:n_sh])
        return outs, (qkv, gate, o)

    def bwd(res, cts):
        qkv, gate, o = res
        do, dgate, _ = _gate_bwd(cts[0], o, gate, name + "_gate_bwd")
        parts = []
        if slots:
            g_slabs = to_slabs(cts[1 + n_sh:])
            core_idx = lax.axis_index("c").astype(jnp.int32).reshape(1)
            from_sibling = _pair_exchange(g_slabs, name + "_pair_exchange")
            parts = [_pair_sum(core_idx, g, r, name + "_pair_sum%d" % k)
                     for k, (g, r) in enumerate(zip(g_slabs, from_sibling))]
        dq, dk, dv, by_chip = _sb_bwd(qkv, do, name + "_bwd", parts)
        zeros = tuple(jnp.zeros(sh.shape, sh.dtype) for sh in shards)
        return (jnp.concatenate([dq, dk, dv], axis=1), dgate) + zeros + tuple(by_chip)

    f.defvjp(fwd, bwd)
    outs = f(qkv, gate, *shards, *slots)
    return outs[0], list(outs[1:1 + n_sh]), list(outs[1 + n_sh:])


SM_FWD_BLK = 256
SM_BLK = 512


SM_FWD_KEYS = 1024
SM_BWD_KEYS = 512


def _sm_mask(i, jw, rows, keys, chunk_shift):
    row = lax.broadcasted_iota(jnp.int32, (rows, keys), 0) + i * rows
    col = lax.broadcasted_iota(jnp.int32, (rows, keys), 1) + jw * keys
    return (col >> chunk_shift) <= (row >> chunk_shift)


def _sm_fwd(qa, ka, va, ccol, crow, dqk, qo, ko, vo, chunk_shift, scale, name):
    s = qa.shape[0]
    b = min(SM_FWD_BLK, s)
    keys = min(SM_FWD_KEYS, s)
    per = keys // b
    nq = s // b
    has_bias = ccol is not None

    def body(*refs):
        if has_bias:
            q_ref, k_ref, v_ref, cc_ref, cr_ref, o_ref, lse_ref, m_s, l_s, acc_s = refs
        else:
            q_ref, k_ref, v_ref, o_ref, lse_ref, m_s, l_s, acc_s = refs
        i = pl.program_id(1)
        q = q_ref[...]
        m_s[...] = jnp.full_like(m_s, NEG)
        l_s[...] = jnp.zeros_like(l_s)
        acc_s[...] = jnp.zeros_like(acc_s)

        def sweep(jw, masked):
            off = pl.multiple_of(jw * keys, keys)
            z = _dot_nt(q, k_ref[pl.ds(off, keys), :]) * scale
            if has_bias:
                z = z + cc_ref[...] - cr_ref[jw]
            if masked:
                z = jnp.where(_sm_mask(i, jw, b, keys, chunk_shift), z, NEG)
            m_old = m_s[...]
            m_new = jnp.maximum(m_old, jnp.max(z, axis=1, keepdims=True))
            alpha = jnp.exp(m_old - m_new)
            p = jnp.exp(z - m_new)
            l_s[...] = alpha * l_s[...] + jnp.sum(p, axis=1, keepdims=True)
            acc_s[...] = alpha * acc_s[...] + _dot_nn(p.astype(BF16), v_ref[pl.ds(off, keys), :])
            m_s[...] = m_new

        def full(jw, carry):
            sweep(jw, False)
            return carry

        lax.fori_loop(0, i // per, full, 0)
        sweep(i // per, True)
        o_ref[...] = acc_s[...] / l_s[...]
        lse_ref[...] = m_s[...] + jnp.log(l_s[...])

    in_specs = [pl.BlockSpec((b, dqk), lambda h, i: (i, qo + h)),
                pl.BlockSpec((s, dqk), lambda h, i: (0, ko + h)),
                pl.BlockSpec((s, HEAD_DIM), lambda h, i: (0, vo + h))]
    args = [qa, ka, va]
    if has_bias:
        in_specs += [pl.BlockSpec((None, b, 1), lambda h, i: (h, i, 0)),
                     pl.BlockSpec((None, s // keys, 1, keys), lambda h, i: (h, 0, 0, 0))]
        args += [ccol, crow]
    return pl.pallas_call(
        body, name=name, grid=(N_HEADS, nq),
        in_specs=in_specs,
        out_specs=[pl.BlockSpec((b, HEAD_DIM), lambda h, i: (i, h)),
                   pl.BlockSpec((None, b, 1), lambda h, i: (h, i, 0))],
        out_shape=[jax.ShapeDtypeStruct((s, D_INNER), F32), jax.ShapeDtypeStruct((N_HEADS, s, 1), F32)],
        scratch_shapes=[pltpu.VMEM((b, 1), F32), pltpu.VMEM((b, 1), F32), pltpu.VMEM((b, HEAD_DIM), F32)],
        compiler_params=_cparams(("parallel", "arbitrary")),
    )(*args)


def _sm_bwd(qa, ka, va, do, lse, delta, ccol, crow, dqk, qo, ko, vo, chunk_shift, scale, grad_dtype, name):
    s = qa.shape[0]
    b = SM_BLK
    keys = min(SM_BWD_KEYS, s)
    per = keys // b
    nq = s // b
    nk = s // keys
    has_bias = ccol is not None

    def body(*refs):
        if has_bias:
            (q_ref, k_ref, v_ref, do_ref, lse_ref, dl_ref, cc_ref, cr_ref,
             dq_ref, dk_ref, dv_ref, dc_ref, dr_ref, dq_s, dkt_s, dvt_s, dc_s, dr_s) = refs
        else:
            (q_ref, k_ref, v_ref, do_ref, lse_ref, dl_ref,
             dq_ref, dk_ref, dv_ref, dq_s, dkt_s, dvt_s) = refs
        i = pl.program_id(1)

        @pl.when(i == 0)
        def _():
            dkt_s[...] = jnp.zeros_like(dkt_s)
            dvt_s[...] = jnp.zeros_like(dvt_s)
            if has_bias:
                dc_s[...] = jnp.zeros_like(dc_s)

        q = q_ref[...]
        dob = do_ref[...]
        q_t = _transpose_bf16(q)
        do_t = _transpose_bf16(dob)
        lse = lse_ref[...]
        delta = dl_ref[...]
        dq_s[...] = jnp.zeros_like(dq_s)
        if has_bias:
            dr_s[...] = jnp.zeros_like(dr_s)

        def sweep(jw, masked):
            off = pl.multiple_of(jw * keys, keys)
            kb = k_ref[pl.ds(off, keys), :]
            z = _dot_nt(q, kb) * scale
            if has_bias:
                z = z + cc_ref[...] - cr_ref[jw]
            p = jnp.exp(z - lse)
            if masked:
                p = jnp.where(_sm_mask(i, jw, b, keys, chunk_shift), p, 0.0)
            dvt_s[jw] += _dot_nn(do_t, p.astype(BF16))
            dz = p * (_dot_nt(dob, v_ref[pl.ds(off, keys), :]) - delta)
            if has_bias:
                dc_s[jw] += jnp.sum(dz, axis=0, keepdims=True)
                dr_s[...] += jnp.sum(dz, axis=1, keepdims=True)
            dzs = (dz * scale).astype(BF16)
            dkt_s[jw] += _dot_nn(q_t, dzs)
            dq_s[...] += _dot_nn(dzs, kb)

        def full(jw, carry):
            sweep(jw, False)
            return carry

        lax.fori_loop(0, i // per, full, 0)
        sweep(i // per, True)
        dq_ref[...] = dq_s[...].astype(grad_dtype)
        if has_bias:
            dr_ref[...] = dr_s[...]

        @pl.when(i == nq - 1)
        def _():
            for jw in range(nk):
                dk_ref[jw * keys:(jw + 1) * keys, :] = dkt_s[jw].T.astype(grad_dtype)
                dv_ref[jw * keys:(jw + 1) * keys, :] = dvt_s[jw].T.astype(grad_dtype)
            if has_bias:
                dc_ref[...] = dc_s[...]

    vec = pl.BlockSpec((None, b, 1), lambda h, i: (h, i, 0))
    in_specs = [pl.BlockSpec((b, dqk), lambda h, i: (i, qo + h)),
                pl.BlockSpec((s, dqk), lambda h, i: (0, ko + h)),
                pl.BlockSpec((s, HEAD_DIM), lambda h, i: (0, vo + h)),
                pl.BlockSpec((b, HEAD_DIM), lambda h, i: (i, h)),
                vec, vec]
    args = [qa, ka, va, do, lse, delta]
    out_specs = [pl.BlockSpec((b, dqk), lambda h, i: (i, h)),
                 pl.BlockSpec((s, dqk), lambda h, i: (0, h)),
                 pl.BlockSpec((s, HEAD_DIM), lambda h, i: (0, h))]
    out_shape = [jax.ShapeDtypeStruct((s, N_HEADS * dqk), grad_dtype),
                 jax.ShapeDtypeStruct((s, N_HEADS * dqk), grad_dtype),
                 jax.ShapeDtypeStruct((s, D_INNER), grad_dtype)]
    scratch = [pltpu.VMEM((b, dqk), F32), pltpu.VMEM((nk, dqk, keys), F32), pltpu.VMEM((nk, HEAD_DIM, keys), F32)]
    if has_bias:
        key_vec = pl.BlockSpec((None, nk, 1, keys), lambda h, i: (h, 0, 0, 0))
        in_specs += [vec, key_vec]
        args += [ccol, crow]
        out_specs += [key_vec, vec]
        out_shape += [jax.ShapeDtypeStruct((N_HEADS, nk, 1, keys), F32), jax.ShapeDtypeStruct((N_HEADS, s, 1), F32)]
        scratch += [pltpu.VMEM((nk, 1, keys), F32), pltpu.VMEM((b, 1), F32)]
    return pl.pallas_call(
        body, name=name, grid=(N_HEADS, nq),
        in_specs=in_specs, out_specs=out_specs, out_shape=out_shape, scratch_shapes=scratch,
        compiler_params=_cparams(("arbitrary", "arbitrary")),
    )(*args)


def fox_core(qkv, gate, c, name):
    s = qkv.shape[0]
    scale = HEAD_DIM ** -0.5
    cfg = dict(dqk=HEAD_DIM, qo=0, ko=N_HEADS, vo=2 * N_HEADS, chunk_shift=0, scale=scale)

    def layouts(c, keys):
        ct = c.T
        keys = min(keys, s)
        return ct.reshape(N_HEADS, s, 1), ct.reshape(N_HEADS, s // keys, 1, keys)

    def run(qkv, gate, c):
        ccol, crow = layouts(c, SM_FWD_KEYS)
        o, lse = _sm_fwd(qkv, qkv, qkv, ccol, crow, name=name + "_fwd", **cfg)
        return _gate_fwd(o, gate, name + "_gate"), o, lse

    @jax.custom_vjp
    def f(qkv, gate, c):
        return run(qkv, gate, c)[0]

    def fwd(qkv, gate, c):
        y, o, lse = run(qkv, gate, c)
        return y, (qkv, gate, c, o, lse)

    def bwd(res, dy):
        qkv, gate, c, o, lse = res
        ccol, crow = layouts(c, SM_BWD_KEYS)
        do, dgate, delta = _gate_bwd(dy, o, gate, name + "_gate_bwd")
        dq, dk, dv, colsum, rowsum = _sm_bwd(qkv, qkv, qkv, do, lse, delta, ccol, crow,
                                             grad_dtype=BF16, name=name + "_bwd", **cfg)
        dc = (rowsum.reshape(N_HEADS, s) - colsum.reshape(N_HEADS, s)).T
        return jnp.concatenate([dq, dk, dv], axis=1), dgate, dc

    f.defvjp(fwd, bwd)
    return f(qkv, gate, c)


def _mla_rope(x, cosv, sinv, out_dtype, name):
    s, width = x.shape
    tm = 256
    half = MLA_ROPE // 2

    def body(x_ref, c_ref, s_ref, o_ref):
        c = c_ref[...]
        sn = s_ref[...]
        lane = lax.broadcasted_iota(jnp.int32, (tm, HEAD_DIM), 1)
        for h in range(N_HEADS):
            lo = h * MLA_QK_PAD
            o_ref[:, lo:lo + HEAD_DIM] = x_ref[:, lo:lo + HEAD_DIM].astype(out_dtype)
            g = x_ref[:, lo + HEAD_DIM:lo + MLA_QK_PAD].astype(F32)
            swapped = jnp.where(lane < half, pltpu.roll(g, HEAD_DIM - half, 1), pltpu.roll(g, half, 1))
            o_ref[:, lo + HEAD_DIM:lo + MLA_QK_PAD] = (g * c + swapped * sn).astype(out_dtype)

    row = pl.BlockSpec((tm, width), lambda i: (i, 0))
    tab = pl.BlockSpec((tm, HEAD_DIM), lambda i: (i, 0))
    return pl.pallas_call(
        body, name=name, grid=(s // tm,),
        in_specs=[row, tab, tab], out_specs=row,
        out_shape=jax.ShapeDtypeStruct((s, width), out_dtype),
        compiler_params=_cparams(("parallel",)),
    )(x, cosv, sinv)


def _mla_fwd(qc, kv, kr, chunk_shift, scale, name):
    s = qc.shape[0]
    b = min(SM_FWD_BLK, s)
    keys = min(SM_FWD_KEYS, s)
    per = keys // b
    nq = s // b

    def body(q_ref, kn_ref, v_ref, kr_ref, o_ref, lse_ref, m_s, l_s, acc_s, kc_s):
        i = pl.program_id(1)

        @pl.when(i == 0)
        def _():
            kc_s[:, :HEAD_DIM] = kn_ref[...]
            kc_s[:, HEAD_DIM:] = kr_ref[...]

        q = q_ref[...]
        m_s[...] = jnp.full_like(m_s, NEG)
        l_s[...] = jnp.zeros_like(l_s)
        acc_s[...] = jnp.zeros_like(acc_s)

        def sweep(jw, masked):
            off = pl.multiple_of(jw * keys, keys)
            z = _dot_nt(q, kc_s[pl.ds(off, keys), :]) * scale
            if masked:
                z = jnp.where(_sm_mask(i, jw, b, keys, chunk_shift), z, NEG)
            m_old = m_s[...]
            m_new = jnp.maximum(m_old, jnp.max(z, axis=1, keepdims=True))
            alpha = jnp.exp(m_old - m_new)
            p = jnp.exp(z - m_new)
            l_s[...] = alpha * l_s[...] + jnp.sum(p, axis=1, keepdims=True)
            acc_s[...] = alpha * acc_s[...] + _dot_nn(p.astype(BF16), v_ref[pl.ds(off, keys), :])
            m_s[...] = m_new

        def full(jw, carry):
            sweep(jw, False)
            return carry

        lax.fori_loop(0, i // per, full, 0)
        sweep(i // per, True)
        o_ref[...] = acc_s[...] / l_s[...]
        lse_ref[...] = m_s[...] + jnp.log(l_s[...])

    return pl.pallas_call(
        body, name=name, grid=(N_HEADS, nq),
        in_specs=[pl.BlockSpec((b, MLA_QK_PAD), lambda h, i: (i, h)),
                  pl.BlockSpec((s, HEAD_DIM), lambda h, i: (0, 2 * h)),
                  pl.BlockSpec((s, HEAD_DIM), lambda h, i: (0, 2 * h + 1)),
                  pl.BlockSpec((s, HEAD_DIM), lambda h, i: (0, 0))],
        out_specs=[pl.BlockSpec((b, HEAD_DIM), lambda h, i: (i, h)),
                   pl.BlockSpec((None, b, 1), lambda h, i: (h, i, 0))],
        out_shape=[jax.ShapeDtypeStruct((s, D_INNER), F32), jax.ShapeDtypeStruct((N_HEADS, s, 1), F32)],
        scratch_shapes=[pltpu.VMEM((b, 1), F32), pltpu.VMEM((b, 1), F32), pltpu.VMEM((b, HEAD_DIM), F32),
                        pltpu.VMEM((s, MLA_QK_PAD), BF16)],
        compiler_params=_cparams(("arbitrary", "arbitrary")),
    )(qc, kv, kv, kr)


def _mla_bwd(qc, kv, kr, do, lse, delta, chunk_shift, scale, name):
    s = qc.shape[0]
    b = min(SM_BLK, s)
    keys = min(SM_BWD_KEYS, s)
    per = keys // b
    nq = s // b
    nk = s // keys

    def body(q_ref, kn_ref, v_ref, kr_ref, do_ref, lse_ref, dl_ref, dq_ref, dkv_ref, dkr_ref,
             dq_s, dkt_s, dvt_s, dkrt_s, kc_s):
        h = pl.program_id(0)
        i = pl.program_id(1)

        @pl.when(jnp.logical_and(h == 0, i == 0))
        def _():
            dkrt_s[...] = jnp.zeros_like(dkrt_s)

        @pl.when(i == 0)
        def _():
            dkt_s[...] = jnp.zeros_like(dkt_s)
            dvt_s[...] = jnp.zeros_like(dvt_s)
            kc_s[:, :HEAD_DIM] = kn_ref[...]
            kc_s[:, HEAD_DIM:] = kr_ref[...]

        q = q_ref[...]
        dob = do_ref[...]
        q_t = _transpose_bf16(q)
        do_t = _transpose_bf16(dob)
        lse_i = lse_ref[...]
        delta_i = dl_ref[...]
        dq_s[...] = jnp.zeros_like(dq_s)

        def sweep(jw, masked):
            off = pl.multiple_of(jw * keys, keys)
            kc = kc_s[pl.ds(off, keys), :]
            p = jnp.exp(_dot_nt(q, kc) * scale - lse_i)
            if masked:
                p = jnp.where(_sm_mask(i, jw, b, keys, chunk_shift), p, 0.0)
            dvt_s[jw] += _dot_nn(do_t, p.astype(BF16))
            dz = p * (_dot_nt(dob, v_ref[pl.ds(off, keys), :]) - delta_i)
            dzs = (dz * scale).astype(BF16)
            dkc_t = _dot_nn(q_t, dzs)
            dkt_s[jw] += dkc_t[:HEAD_DIM]
            dkrt_s[jw] += dkc_t[HEAD_DIM:]
            dq_s[...] += _dot_nn(dzs, kc)

        def full(jw, carry):
            sweep(jw, False)
            return carry

        lax.fori_loop(0, i // per, full, 0)
        sweep(i // per, True)
        dq_ref[...] = dq_s[...]

        @pl.when(i == nq - 1)
        def _():
            for jw in range(nk):
                dkv_ref[jw * keys:(jw + 1) * keys, :HEAD_DIM] = dkt_s[jw].T.astype(BF16)
                dkv_ref[jw * keys:(jw + 1) * keys, HEAD_DIM:] = dvt_s[jw].T.astype(BF16)

        @pl.when(jnp.logical_and(h == N_HEADS - 1, i == nq - 1))
        def _():
            for jw in range(nk):
                dkr_ref[jw * keys:(jw + 1) * keys, :] = dkrt_s[jw].T

    vec = pl.BlockSpec((None, b, 1), lambda h, i: (h, i, 0))
    acc = pltpu.VMEM((nk, HEAD_DIM, keys), F32)
    return pl.pallas_call(
        body, name=name, grid=(N_HEADS, nq),
        in_specs=[pl.BlockSpec((b, MLA_QK_PAD), lambda h, i: (i, h)),
                  pl.BlockSpec((s, HEAD_DIM), lambda h, i: (0, 2 * h)),
                  pl.BlockSpec((s, HEAD_DIM), lambda h, i: (0, 2 * h + 1)),
                  pl.BlockSpec((s, HEAD_DIM), lambda h, i: (0, 0)),
                  pl.BlockSpec((b, HEAD_DIM), lambda h, i: (i, h)),
                  vec, vec],
        out_specs=[pl.BlockSpec((b, MLA_QK_PAD), lambda h, i: (i, h)),
                   pl.BlockSpec((s, MLA_QK_PAD), lambda h, i: (0, h)),
                   pl.BlockSpec((s, HEAD_DIM), lambda h, i: (0, 0))],
        out_shape=[jax.ShapeDtypeStruct((s, N_HEADS * MLA_QK_PAD), F32),
                   jax.ShapeDtypeStruct((s, N_HEADS * MLA_QK_PAD), BF16),
                   jax.ShapeDtypeStruct((s, HEAD_DIM), F32)],
        scratch_shapes=[pltpu.VMEM((b, MLA_QK_PAD), F32), acc, acc, acc, pltpu.VMEM((s, MLA_QK_PAD), BF16)],
        compiler_params=_cparams(("arbitrary", "arbitrary")),
    )(qc, kv, kv, kr, do, lse, delta)


def mla_core(qp, kv, kr, gate, cosv, sinv, name):
    scale = (MLA_NOPE + MLA_ROPE) ** -0.5
    shift = MLA_CHUNK.bit_length() - 1

    def run(qp, kv, kr, gate, cosv, sinv):
        qc = _mla_rope(qp, cosv, sinv, BF16, name + "_rope")
        krb = kr.astype(BF16)
        o, lse = _mla_fwd(qc, kv, krb, shift, scale, name + "_fwd")
        return _gate_fwd(o, gate, name + "_gate"), (qc, kv, krb, gate, o, lse, cosv, sinv)

    @jax.custom_vjp
    def f(qp, kv, kr, gate, cosv, sinv):
        return run(qp, kv, kr, gate, cosv, sinv)[0]

    def bwd(res, dy):
        qc, kv, krb, gate, o, lse, cosv, sinv = res
        do, dgate, delta = _gate_bwd(dy, o, gate, name + "_gate_bwd")
        dqc, dkv, dkr = _mla_bwd(qc, kv, krb, do, lse, delta, shift, scale, name + "_bwd")
        dqp = _mla_rope(dqc, cosv, -sinv, F32, name + "_rope_bwd")
        return dqp, dkv, dkr, dgate, jnp.zeros_like(cosv), jnp.zeros_like(sinv)

    f.defvjp(run, bwd)
    return f(qp, kv, kr, gate, cosv, sinv)


def _sq_loss_call(y, t, name):
    s, d = y.shape
    tm = _tile(s, 512, 8)

    def body(y_ref, t_ref, l_ref, e_ref):
        @pl.when(pl.program_id(0) == 0)
        def _():
            l_ref[...] = jnp.zeros_like(l_ref)

        e = y_ref[...] - t_ref[...]
        e_ref[...] = e * (1.0 / d)
        part = jnp.sum(jnp.sum(e * e, axis=1, keepdims=True), axis=0, keepdims=True)
        l_ref[...] += jnp.broadcast_to(part * (0.5 / d), l_ref.shape)

    row = pl.BlockSpec((tm, d), lambda i: (i, 0))
    return pl.pallas_call(
        body, name=name, grid=(s // tm,),
        in_specs=[row, row],
        out_specs=[pl.BlockSpec((8, 128), lambda i: (0, 0)), row],
        out_shape=[jax.ShapeDtypeStruct((8, 128), F32), jax.ShapeDtypeStruct((s, d), F32)],
        compiler_params=_cparams(("arbitrary",)),
    )(y, t)


@jax.custom_vjp
def sq_loss(y, t):
    return _sq_loss_call(y, t, "loss_fwd")[0][0, 0]


def _sq_loss_fwd(y, t):
    l, e = _sq_loss_call(y, t, "loss_fwd")
    return l[0, 0], e


def _sq_loss_bwd(e, g):
    return g * e, jnp.zeros_like(e)


sq_loss.defvjp(_sq_loss_fwd, _sq_loss_bwd)


def _pair_sum(core_idx, g, recv, name):
    _, _, r, c = g.shape
    tb = _tile(r, 512, 16)

    def body(c_ref, g_ref, r_ref, o_ref):
        o_ref[...] = (g_ref[...] + r_ref[...]).astype(BF16)

    return pl.pallas_call(
        body, name=name,
        grid_spec=pltpu.PrefetchScalarGridSpec(
            num_scalar_prefetch=1, grid=(4, r // tb),
            in_specs=[pl.BlockSpec((None, None, tb, c), lambda q, i, c_ref: (c_ref[0], q, i, 0)),
                      pl.BlockSpec((None, tb, c), lambda q, i, c_ref: (q, i, 0))],
            out_specs=pl.BlockSpec((None, tb, c), lambda q, i, c_ref: (q, i, 0))),
        out_shape=jax.ShapeDtypeStruct((4, r, c), BF16),
        compiler_params=_cparams(("parallel", "parallel")),
    )(core_idx, g, recv)


def _adamw(w, parts, m, v, name):
    n, r, c = parts.shape
    tb = _tile(r, 256, 8)
    b1c = 1.0 - ADAM_B1 ** ADAM_STEP
    b2c = 1.0 - ADAM_B2 ** ADAM_STEP

    def body(w_ref, p_ref, m_ref, v_ref, g_ref, d_ref, nm_ref, nv_ref):
        g = p_ref[0].astype(F32)
        for k in range(1, n):
            g = g + p_ref[k].astype(F32)
        m_new = ADAM_B1 * m_ref[...] + (1.0 - ADAM_B1) * g
        v_new = ADAM_B2 * v_ref[...] + (1.0 - ADAM_B2) * (g * g)
        m_hat = m_new / b1c
        v_hat = v_new / b2c
        g_ref[...] = g
        d_ref[...] = -ADAM_LR * (m_hat / (jnp.sqrt(v_hat) + ADAM_EPS) + ADAM_WD * w_ref[...])
        nm_ref[...] = m_new
        nv_ref[...] = v_new

    row = pl.BlockSpec((tb, c), lambda i: (i, 0))
    return pl.pallas_call(
        body, name=name, grid=(r // tb,),
        in_specs=[row, pl.BlockSpec((n, tb, c), lambda i: (0, i, 0)), row, row],
        out_specs=[row] * 4,
        out_shape=[jax.ShapeDtypeStruct((r, c), F32)] * 4,
        compiler_params=_cparams(("parallel",)),
    )(w, parts, m, v)


ANY = pl.BlockSpec(memory_space=pl.ANY)


def _place():
    return lax.axis_index("x"), lax.axis_index("y"), lax.axis_index("c")


def _all_gather(shards, kinds, name):
    nm = len(shards)

    def body(*refs):
        start, finish = _gather_steps(refs[:nm], refs[nm:2 * nm], kinds, *refs[2 * nm:])
        start()
        finish()

    return pl.pallas_call(
        body, name=name,
        out_shape=_gather_out_shapes(shards, kinds),
        in_specs=[ANY] * nm, out_specs=[ANY] * nm,
        scratch_shapes=_gather_scratch(nm),
    )(*shards)


def _gather_out_shapes(shards, kinds):
    def full(sh, kind):
        a, b = sh.shape
        return {"row": (N_DEV * a, b), "col": (a, N_DEV * b), "stack": (N_DEV, a, b)}[kind]

    return [jax.ShapeDtypeStruct(full(sh, kd), sh.dtype) for sh, kd in zip(shards, kinds)]


def _gather_scratch(nm):
    return [pltpu.SemaphoreType.DMA((7 * nm,)), pltpu.SemaphoreType.DMA((7 * nm,)), pltpu.SemaphoreType.DMA((nm,))]


def _gather_steps(x_refs, out_refs, kinds, send_sems, recv_sems, local_sems):
    nm = len(x_refs)
    x, y, cc = _place()
    me, sibling = (x, y, cc), (x, y, 1 - cc)
    chips = [(1 - x, y), (x, 1 - y), (1 - x, 1 - y)]

    def slot(mi, px, py, pc):
        d = 4 * px + 2 * py + pc
        a, b = x_refs[mi].shape
        if kinds[mi] == "row":
            return out_refs[mi].at[pl.ds(pl.multiple_of(d * a, a), a), :]
        if kinds[mi] == "col":
            return out_refs[mi].at[:, pl.ds(pl.multiple_of(d * b, 128), b)]
        return out_refs[mi].at[d]

    def copy(mi, k, block, to, src=None):
        return pltpu.make_async_remote_copy(
            src_ref=slot(mi, *block) if src is None else src, dst_ref=slot(mi, *block),
            send_sem=send_sems.at[7 * mi + k], recv_sem=recv_sems.at[7 * mi + k],
            device_id=to, device_id_type=MESH)

    def own_copies():
        mine = [pltpu.make_async_copy(x_refs[mi], slot(mi, *me), local_sems.at[mi]) for mi in range(nm)]
        first = []
        for mi in range(nm):
            first.append(copy(mi, 0, me, sibling, src=x_refs[mi]))
            first += [copy(mi, 1 + j, me, (*chip, cc), src=x_refs[mi]) for j, chip in enumerate(chips)]
        return mine, first

    def start():
        mine, first = own_copies()
        for cp in mine + first:
            cp.start()

    def finish():
        mine, first = own_copies()
        passed = []
        for j, chip in enumerate(chips):
            for mi in range(nm):
                copy(mi, 1 + j, (*chip, cc), me).wait_recv()
                passed.append(copy(mi, 4 + j, (*chip, cc), sibling))
                passed[-1].start()
        for mi in range(nm):
            copy(mi, 0, sibling, me).wait_recv()
            for j, chip in enumerate(chips):
                copy(mi, 4 + j, (*chip, 1 - cc), me).wait_recv()
        for cp in first + passed:
            cp.wait_send()
        for cp in mine:
            cp.wait()

    return start, finish


def _pair_exchange(gs, name):
    nm = len(gs)

    def body(*refs):
        g_refs, recv_refs = refs[:nm], refs[nm:2 * nm]
        send_sems, recv_sems = refs[2 * nm:]
        x, y, cc = _place()
        copies = [pltpu.make_async_remote_copy(
            src_ref=g_refs[mi].at[1 - cc], dst_ref=recv_refs[mi],
            send_sem=send_sems.at[mi], recv_sem=recv_sems.at[mi], device_id=(x, y, 1 - cc), device_id_type=MESH)
            for mi in range(nm)]
        for cp in copies:
            cp.start()
        for cp in copies:
            cp.wait_recv()
        for cp in copies:
            cp.wait_send()

    return pl.pallas_call(
        body, name=name,
        out_shape=[jax.ShapeDtypeStruct(g.shape[1:], g.dtype) for g in gs],
        in_specs=[ANY] * nm, out_specs=[ANY] * nm,
        scratch_shapes=[pltpu.SemaphoreType.DMA((nm,)), pltpu.SemaphoreType.DMA((nm,))],
    )(*gs)


def _chip_exchange(parts, name):
    nm = len(parts)

    def body(*refs):
        start, finish = _exchange_steps(refs[:nm], refs[nm:2 * nm], *refs[2 * nm:])
        start()
        finish()

    return pl.pallas_call(
        body, name=name,
        out_shape=[jax.ShapeDtypeStruct(p.shape, p.dtype) for p in parts],
        in_specs=[ANY] * nm, out_specs=[ANY] * nm,
        scratch_shapes=_exchange_scratch(nm),
    )(*parts)


def _exchange_scratch(nm):
    return [pltpu.SemaphoreType.DMA((4 * nm,)), pltpu.SemaphoreType.DMA((4 * nm,)), pltpu.SemaphoreType.DMA((nm,))]


def _exchange_steps(p_refs, out_refs, send_sems, recv_sems, local_sems):
    nm = len(p_refs)
    x, y, cc = _place()
    mine = 2 * x + y
    others = [(1 - x, y), (x, 1 - y), (1 - x, 1 - y)]

    def own_copies():
        keeps = [pltpu.make_async_copy(p_refs[mi].at[mine], out_refs[mi].at[mine], local_sems.at[mi])
                 for mi in range(nm)]
        sends = []
        for px, py in others:
            q = 2 * px + py
            for mi in range(nm):
                sends.append(pltpu.make_async_remote_copy(
                    src_ref=p_refs[mi].at[q], dst_ref=out_refs[mi].at[mine],
                    send_sem=send_sems.at[4 * mi + q], recv_sem=recv_sems.at[4 * mi + mine],
                    device_id=(px, py, cc), device_id_type=MESH))
        return keeps, sends

    def start():
        keeps, sends = own_copies()
        for cp in keeps + sends:
            cp.start()

    def finish():
        keeps, sends = own_copies()
        for px, py in others:
            q = 2 * px + py
            for mi in range(nm):
                pltpu.make_async_remote_copy(
                    src_ref=p_refs[mi].at[q], dst_ref=out_refs[mi].at[q],
                    send_sem=send_sems.at[4 * mi + q], recv_sem=recv_sems.at[4 * mi + q],
                    device_id=(px, py, cc), device_id_type=MESH).wait_recv()
        for cp in sends:
            cp.wait_send()
        for cp in keeps:
            cp.wait()

    return start, finish


def _all_reduce_small(v, name):
    shape = v.shape

    def body(v_ref, out_ref, buf, send_sems, recv_sems):
        x, y, cc = _place()
        me = 4 * x + 2 * y + cc
        buf[me] = v_ref[...]
        flips = [(a, b, d) for a in (0, 1) for b in (0, 1) for d in (0, 1)][1:]
        copies = []
        for k, (a, b, d) in enumerate(flips):
            peer = (x ^ a, y ^ b, cc ^ d)
            copies.append(pltpu.make_async_remote_copy(
                src_ref=v_ref, dst_ref=buf.at[me],
                send_sem=send_sems.at[k], recv_sem=recv_sems.at[k], device_id=peer, device_id_type=MESH))
        for cp in copies:
            cp.start()
        for k, (a, b, d) in enumerate(flips):
            peer_id = 4 * (x ^ a) + 2 * (y ^ b) + (cc ^ d)
            pltpu.make_async_remote_copy(
                src_ref=v_ref, dst_ref=buf.at[peer_id],
                send_sem=send_sems.at[k], recv_sem=recv_sems.at[k], device_id=(x, y, cc), device_id_type=MESH
            ).wait_recv()
        for cp in copies:
            cp.wait_send()
        total = buf[0]
        for k in range(1, N_DEV):
            total = total + buf[k]
        out_ref[...] = total

    vm = pl.BlockSpec(memory_space=pltpu.VMEM)
    return pl.pallas_call(
        body, name=name,
        out_shape=jax.ShapeDtypeStruct(shape, F32),
        in_specs=[vm], out_specs=vm,
        scratch_shapes=[pltpu.VMEM((N_DEV,) + shape, F32), pltpu.SemaphoreType.DMA((7,)), pltpu.SemaphoreType.DMA((7,))],
    )(v)


def _gather_kind(name, shape):
    if name not in COL_SHARDED:
        return "row"
    return "col" if shape[1] % 128 == 0 else "stack"


def _slab_of(name, shape):
    if name not in COL_SHARDED:
        return ("row", shape[0])
    return ("col", shape[1]) if shape[1] % 128 == 0 else None


def _to_slabs(g, shape):
    kk, nn = shape
    return g.reshape(kk, 4, 2, nn).transpose(2, 1, 0, 3)


SMALL_ROWS = 8


def _pack_small(arrs):
    rows = [arrs[n] for n in SMALL[:5]]
    last = jnp.concatenate([arrs["q_norm1"], arrs["kv_norm1"], arrs["b_f2"]])
    rows.append(jnp.pad(last, (0, PACK_COLS - last.shape[0])))
    rows += [jnp.zeros((PACK_COLS,), F32)] * (SMALL_ROWS - len(rows))
    return jnp.stack(rows)


def _unpack_small(p):
    out = {n: p[k] for k, n in enumerate(SMALL[:5])}
    out["q_norm1"] = p[5, :MLA_Q_RANK]
    out["kv_norm1"] = p[5, MLA_Q_RANK:MLA_Q_RANK + MLA_KV_RANK]
    out["b_f2"] = p[5, MLA_Q_RANK + MLA_KV_RANK:MLA_Q_RANK + MLA_KV_RANK + N_HEADS]
    return out


N_QKV = 3 * D_INNER
N_MAIN = 4 * D_INNER


def _rope(x, pos):
    r = x.shape[-1]
    inv_freq = ROPE_BASE ** (-jnp.arange(0, r, 2, dtype=F32) / r)
    ang = pos.astype(F32)[:, None, None] * inv_freq
    cos, sin = jnp.cos(ang), jnp.sin(ang)
    x1, x2 = x[..., : r // 2], x[..., r // 2:]
    return jnp.concatenate([x1 * cos - x2 * sin, x1 * sin + x2 * cos], axis=-1)


def _forward_loss(carriers, small, x, wfull, late, late_grads, slabs, pos, target):
    s = x.shape[0]
    wfull = dict(wfull)
    carriers = dict(carriers)

    def out_proj(y, w_out, tag, x_skip):
        return mm(y, wfull[w_out], carriers[w_out], slabs[w_out], name=tag + "_out", skip=x_skip)

    def sb_layer(x, ln, w_in, w_out, tag, gather=None, late=None):
        h, x = rmsnorm(x, small[ln], tag + "_ln", skip=True)
        qkv, gate = in_proj(h, wfull[w_in], (carriers[w_in],), True, tag)
        y, gathered, late_carriers = sb_core(qkv, gate, tag, gather, late)
        return out_proj(y, w_out, tag, x), gathered, late_carriers

    names, shards, kinds, shapes = late
    car_keys, car_shapes, to_slabs = late_grads
    x, gathered, late_carriers = sb_layer(x, "ln0", "w_in0", "w_out0", "l0", (shards, kinds),
                                          (carriers["slots"], car_shapes, to_slabs))
    carriers.update(zip(car_keys, late_carriers))
    for n, kind, full in zip(names, kinds, gathered):
        wfull[n] = full.transpose(1, 0, 2).reshape(shapes[n][0], -1) if kind == "stack" else full

    h, x = rmsnorm(x, small["ln1"], "l1_ln", skip=True)
    proj = mm(h, wfull["w_in1"], carriers["w_in1"], slabs["w_in1"], name="l1_in")
    i1, i2, i3 = MLA_Q_RANK, MLA_Q_RANK + MLA_KV_RANK, MLA_Q_RANK + MLA_KV_RANK + MLA_ROPE
    w_qb = jnp.pad(wfull["w_qb1"].reshape(MLA_Q_RANK, N_HEADS, MLA_NOPE + MLA_ROPE),
                   ((0, 0), (0, 0), (0, MLA_QK_PAD - MLA_NOPE - MLA_ROPE))).reshape(MLA_Q_RANK, N_HEADS * MLA_QK_PAD)
    qp = mm(rmsnorm(proj[:, :i1], small["q_norm1"], "l1_qn"), w_qb, carriers["w_qb1"], None, name="l1_qb")
    kv = mm(rmsnorm(proj[:, i1:i2], small["kv_norm1"], "l1_kvn"), wfull["w_kvb1"], carriers["w_kvb1"],
            slabs["w_kvb1"], BF16, name="l1_kvb")
    kr = jnp.pad(_rope(proj[:, i2:i3][:, None, :], pos)[:, 0, :], ((0, 0), (0, HEAD_DIM - MLA_ROPE)))
    inv_freq = ROPE_BASE ** (-jnp.arange(0, MLA_ROPE, 2, dtype=F32) / MLA_ROPE)
    ang = pos.astype(F32)[:, None] * inv_freq
    cos, sin = jnp.cos(ang), jnp.sin(ang)
    rest = HEAD_DIM - MLA_ROPE
    cosv = jnp.concatenate([cos, cos, jnp.ones((s, rest), F32)], axis=1)
    sinv = jnp.concatenate([-sin, sin, jnp.zeros((s, rest), F32)], axis=1)
    y = mla_core(qp, kv, kr, proj[:, i3:], cosv, sinv, "l1")
    x = out_proj(y, "w_out1", "l1", x)

    h, x = rmsnorm(x, small["ln2"], "l2_ln", skip=True)
    qkv, gate = in_proj(h, wfull["w_in2"], (carriers["w_in2_qkv"], carriers["w_in2_gate"]), False, "l2")
    f_logit = mm(h, wfull["w_in2"][:, N_MAIN:], carriers["w_in2_f"], None, name="l2_f") + small["b_f2"]
    c = jnp.cumsum(jax.nn.log_sigmoid(f_logit), axis=0)
    x = out_proj(fox_core(qkv, gate, c, "l2"), "w_out2", "l2", x)

    x, _, _ = sb_layer(x, "ln3", "w_in3", "w_out3", "l3")
    return sq_loss(rmsnorm(x, small["final_norm"], "final_ln"), target)


def kernel(x, positions, ln0, w_in0, w_out0, ln1, w_in1, q_norm1, w_qb1, kv_norm1, w_kvb1, w_out1, ln2, w_in2, b_f2, w_out2, ln3, w_in3, w_out3, final_norm, loss_target, m_ln0, m_w_in0, m_w_out0, m_ln1, m_w_in1, m_q_norm1, m_w_qb1, m_kv_norm1, m_w_kvb1, m_w_out1, m_ln2, m_w_in2, m_b_f2, m_w_out2, m_ln3, m_w_in3, m_w_out3, m_final_norm, v_ln0, v_w_in0, v_w_out0, v_ln1, v_w_in1, v_q_norm1, v_w_qb1, v_kv_norm1, v_w_kvb1, v_w_out1, v_ln2, v_w_in2, v_b_f2, v_w_out2, v_ln3, v_w_in3, v_w_out3, v_final_norm):
    args = dict(locals())
    w = {n: args[n] for n in ALL_W}
    m = {n: args["m_" + n] for n in ALL_W}
    v = {n: args["v_" + n] for n in ALL_W}
    shapes = {n: w[n].shape for n in BIG}
    kinds = [_gather_kind(n, shapes[n]) for n in BIG]
    slabs = {n: _slab_of(n, shapes[n]) for n in BIG}

    first = BIG[:2]
    late_names = BIG[2:]
    gathered = _all_gather([w[n].astype(BF16) for n in first], kinds[:2], "gather_w")
    wfull = dict(zip(first, gathered))
    late = (late_names, [w[n].astype(BF16) for n in late_names], kinds[2:], shapes)

    d_model = shapes["w_in0"][0]
    car_shapes = {}
    for n in late_names:
        if n == "w_in2":
            car_shapes["w_in2_qkv"] = (d_model, N_QKV)
            car_shapes["w_in2_gate"] = (d_model, D_INNER)
            car_shapes["w_in2_f"] = (d_model, N_DEV * shapes[n][1] - N_MAIN)
        elif n == "w_qb1":
            car_shapes[n] = (MLA_Q_RANK, N_HEADS * MLA_QK_PAD)
        elif slabs[n] is None:
            car_shapes[n] = (shapes[n][0], N_DEV * shapes[n][1])
        else:
            car_shapes[n] = (2, 4) + shapes[n]
    car_keys = list(car_shapes)

    def to_slabs(cts):
        g_car = dict(zip(car_keys, cts))
        out = []
        for n in late_names:
            if n == "w_in2":
                g = jnp.concatenate([g_car["w_in2_qkv"], g_car["w_in2_gate"], g_car["w_in2_f"]], axis=1)
                out.append(_to_slabs(g, shapes[n]))
            elif n == "w_qb1":
                g = g_car[n].reshape(MLA_Q_RANK, N_HEADS, MLA_QK_PAD)[:, :, :MLA_NOPE + MLA_ROPE]
                out.append(_to_slabs(g.reshape(MLA_Q_RANK, -1), shapes[n]))
            elif slabs[n] is None:
                out.append(_to_slabs(g_car[n], shapes[n]))
            else:
                out.append(g_car[n])
        return out

    late_grads = (car_keys, [car_shapes[k] for k in car_keys], to_slabs)
    carriers = {n: jnp.zeros((2, 4) + shapes[n], F32) for n in first}
    carriers["slots"] = [jnp.zeros((4,) + shapes[n], BF16) for n in late_names]
    small = {n: w[n] for n in SMALL}

    def local_loss(carriers, small, x_seq):
        return _forward_loss(carriers, small, x_seq, wfull, late, late_grads, slabs, positions[0], loss_target[0])

    loss_local, (g_car, g_small, g_x) = jax.value_and_grad(local_loss, argnums=(0, 1, 2))(carriers, small, x[0])
    loss = lax.psum(loss_local, ("x", "y", "c"))

    core_idx = lax.axis_index("c").astype(jnp.int32).reshape(1)
    from_sibling = _pair_exchange([g_car[n] for n in first], "pair_exchange")
    chip_part = [_pair_sum(core_idx, g_car[n], r, "pair_sum_" + n) for n, r in zip(first, from_sibling)]
    by_chip = dict(zip(first, _chip_exchange(chip_part, "chip_exchange")))
    by_chip.update(zip(late_names, g_car["slots"]))
    big = [{}, {}, {}, {}]
    for n in BIG:
        for k, t in enumerate(_adamw(w[n], by_chip[n], m[n], v[n], "adamw_" + n)):
            big[k][n] = t

    g_small_sum = _all_reduce_small(_pack_small(g_small), "reduce_small")
    sm = _adamw(_pack_small(w), g_small_sum[None], _pack_small(m), _pack_small(v), "adamw_small")
    small_out = [_unpack_small(t) for t in sm]

    outs = [loss, g_x[None]]
    for k in range(4):
        outs += [small_out[k][n] if n in small_out[k] else big[k][n] for n in ALL_W]
    return tuple(outs)
```

```python
import jax
import jax.numpy as jnp
from jax import lax
from jax.experimental import pallas as pl
from jax.experimental.pallas import tpu as pltpu

F32 = jnp.float32
BF16 = jnp.bfloat16
MESH = pl.DeviceIdType.MESH

N_DEV = 8
N_HEADS = 16
HEAD_DIM = 128
D_INNER = N_HEADS * HEAD_DIM
MLA_Q_RANK = 256
MLA_KV_RANK = 128
MLA_NOPE = 128
MLA_ROPE = 64
MLA_QK_PAD = 256
MLA_CHUNK = 64
ROPE_BASE = 10000.0
EPS = 1e-6
NEG = -1e30
SB_CUT = 104.0

ADAM_LR = 0.001
ADAM_B1 = 0.9
ADAM_B2 = 0.999
ADAM_EPS = 1e-08
ADAM_WD = 0.01
ADAM_STEP = 10

PACK_COLS = 1024
VMEM_LIMIT = 56 * 1024 * 1024

BIG = ["w_in0", "w_out0", "w_in1", "w_qb1", "w_kvb1", "w_out1", "w_in2", "w_out2", "w_in3", "w_out3"]
COL_SHARDED = {"w_in0", "w_in1", "w_qb1", "w_kvb1", "w_in2", "w_in3"}
SMALL = ["ln0", "ln1", "ln2", "ln3", "final_norm", "q_norm1", "kv_norm1", "b_f2"]
ALL_W = ["ln0", "w_in0", "w_out0", "ln1", "w_in1", "q_norm1", "w_qb1", "kv_norm1", "w_kvb1", "w_out1",
         "ln2", "w_in2", "b_f2", "w_out2", "ln3", "w_in3", "w_out3", "final_norm"]


def _cparams(sem=None):
    return pltpu.CompilerParams(dimension_semantics=sem, vmem_limit_bytes=VMEM_LIMIT)


def _tile(dim, cap, align):
    if dim <= cap:
        return dim
    t = (cap // align) * align
    while t >= align:
        if dim % t == 0:
            return t
        t -= align
    return dim


def _dot(a, b, dims):
    return lax.dot_general(a, b, (dims, ((), ())), preferred_element_type=F32)


def _dot_nn(a, b):
    return _dot(a, b, ((1,), (0,)))


def _dot_nt(a, b):
    return _dot(a, b, ((1,), (1,)))


def _transpose_bf16(x):
    return x.astype(F32).T.astype(BF16)


def _matmul(a, b, mode, col0=0, n_cols=None, out_dtype=F32, name="mm", res=None):
    assert mode in ("nn", "nt")
    if mode == "nn":
        m, r = a.shape
        n = n_cols or b.shape[1]
        tn, tr = _tile(n, 1024, 128), _tile(r, 1024, 128)
        tm = _tile(m, 1024 if tn <= 1024 else 512, 8)
        c0 = col0 // tn
        a_spec = pl.BlockSpec((tm, tr), lambda i, j, k: (i, k))
        b_spec = pl.BlockSpec((tr, tn), lambda i, j, k: (k, j + c0))
        dims = ((1,), (0,))
        assert col0 % tn == 0
    else:
        m, r = a.shape
        n = b.shape[0]
        tn, tr = _tile(n, 1024, 128), _tile(r, 1024, 128)
        tm = _tile(m, 1024 if tr <= 1024 else 512, 8)
        c0 = col0 // tr
        a_spec = pl.BlockSpec((tm, tr), lambda i, j, k: (i, k))
        b_spec = pl.BlockSpec((tn, tr), lambda i, j, k: (j, k + c0))
        dims = ((1,), (1,))
        assert col0 % tr == 0
    nr = r // tr
    out_spec = pl.BlockSpec((tm, tn), lambda i, j, k: (i, j))

    def body(*refs):
        a_ref, b_ref, o_ref, acc_ref = refs[0], refs[1], refs[-2], refs[-1]
        k = pl.program_id(2)

        @pl.when(k == 0)
        def _():
            acc_ref[...] = jnp.zeros_like(acc_ref)

        acc_ref[...] += _dot(a_ref[...].astype(BF16), b_ref[...].astype(BF16), dims)

        @pl.when(k == nr - 1)
        def _():
            out = acc_ref[...] if res is None else acc_ref[...] + refs[2][...]
            o_ref[...] = out.astype(out_dtype)

    return pl.pallas_call(
        body,
        name=name,
        grid=(m // tm, n // tn, nr),
        in_specs=[a_spec, b_spec] + ([] if res is None else [out_spec]),
        out_specs=out_spec,
        out_shape=jax.ShapeDtypeStruct((m, n), out_dtype),
        scratch_shapes=[pltpu.VMEM((tm, tn), F32)],
        compiler_params=_cparams(("parallel", "parallel", "arbitrary")),
    )(*([a, b] if res is None else [a, b, res]))


def _matmul_dx2(g1, g2, w, name):
    m, r1 = g1.shape
    r2 = g2.shape[1]
    n = w.shape[0]
    tm, tn, tr = _tile(m, 1024, 8), _tile(n, 1024, 128), 1024
    assert r1 % tr == 0 and r2 % tr == 0
    n1b, nr = r1 // tr, (r1 + r2) // tr

    def body(g1_ref, g2_ref, w_ref, o_ref, acc_ref):
        k = pl.program_id(2)

        @pl.when(k == 0)
        def _():
            acc_ref[...] = jnp.zeros_like(acc_ref)

        @pl.when(k < n1b)
        def _():
            acc_ref[...] += _dot_nt(g1_ref[...].astype(BF16), w_ref[...])

        @pl.when(k >= n1b)
        def _():
            acc_ref[...] += _dot_nt(g2_ref[...].astype(BF16), w_ref[...])

        @pl.when(k == nr - 1)
        def _():
            o_ref[...] = acc_ref[...]

    return pl.pallas_call(
        body, name=name, grid=(m // tm, n // tn, nr),
        in_specs=[pl.BlockSpec((tm, tr), lambda i, j, k: (i, jnp.minimum(k, n1b - 1))),
                  pl.BlockSpec((tm, tr), lambda i, j, k: (i, jnp.maximum(k - n1b, 0))),
                  pl.BlockSpec((tn, tr), lambda i, j, k: (j, k))],
        out_specs=pl.BlockSpec((tm, tn), lambda i, j, k: (i, j)),
        out_shape=jax.ShapeDtypeStruct((m, n), F32),
        scratch_shapes=[pltpu.VMEM((tm, tn), F32)],
        compiler_params=_cparams(("parallel", "parallel", "arbitrary")),
    )(g1, g2, w)


def _transpose_cast(a, name):
    r, m = a.shape
    tr = _tile(r, 512, 128)

    def body(a_ref, o_ref):
        o_ref[...] = a_ref[...].astype(F32).T.astype(BF16)

    return pl.pallas_call(
        body, name=name, grid=(r // tr,),
        in_specs=[pl.BlockSpec((tr, m), lambda i: (i, 0))],
        out_specs=pl.BlockSpec((m, tr), lambda i: (0, i)),
        out_shape=jax.ShapeDtypeStruct((m, r), BF16),
        compiler_params=_cparams(("parallel",)),
    )(a)


def _matmul_dw(a, b1, b2, slab, name):
    m, r = a.shape
    n1 = b1.shape[1]
    n = n1 + (b2.shape[1] if b2 is not None else 0)
    tr = _tile(r, 1024, 128)
    if slab is None:
        tm, tn = _tile(m, 1024, 128), _tile(n1, 1024, 128)
        out_spec = pl.BlockSpec((tm, tn), lambda i, j, k: (i, j))
        out_shape = (m, n)
    elif slab[0] == "col":
        tm, tn = _tile(m, 1024, 128), slab[1]
        out_spec = pl.BlockSpec((None, None, tm, tn), lambda i, j, k: (j % 2, j // 2, i, 0))
        out_shape = (2, 4, m, tn)
        assert n == N_DEV * tn
    else:
        tm, tn = slab[1], _tile(n, 1024, 128)
        out_spec = pl.BlockSpec((None, None, tm, tn), lambda i, j, k: (i % 2, i // 2, 0, j))
        out_shape = (2, 4, tm, n)
        assert m == N_DEV * tm
    assert n1 % tn == 0 and n % tn == 0
    if tn > 1024:
        tr = _tile(r, 256, 128)
    n1b = n1 // tn
    nr = r // tr

    def body(*refs):
        a_ref, b_refs, o_ref, acc_ref = refs[0], refs[1:-2], refs[-2], refs[-1]
        j = pl.program_id(1)
        k = pl.program_id(2)

        @pl.when(k == 0)
        def _():
            acc_ref[...] = jnp.zeros_like(acc_ref)

        at = a_ref[...]
        if b2 is None:
            acc_ref[...] += _dot_nn(at, b_refs[0][...].astype(BF16))
        else:
            @pl.when(j < n1b)
            def _():
                acc_ref[...] += _dot_nn(at, b_refs[0][...].astype(BF16))

            @pl.when(j >= n1b)
            def _():
                acc_ref[...] += _dot_nn(at, b_refs[1][...].astype(BF16))

        @pl.when(k == nr - 1)
        def _():
            o_ref[...] = acc_ref[...]

    in_specs = [pl.BlockSpec((tm, tr), lambda i, j, k: (i, k))]
    args = [a, b1]
    if b2 is None:
        in_specs.append(pl.BlockSpec((tr, tn), lambda i, j, k: (k, j)))
    else:
        in_specs.append(pl.BlockSpec((tr, tn), lambda i, j, k: (jnp.where(j < n1b, k, nr - 1), jnp.minimum(j, n1b - 1))))
        in_specs.append(pl.BlockSpec((tr, tn), lambda i, j, k: (jnp.where(j < n1b, 0, k), jnp.maximum(j - n1b, 0))))
        args.append(b2)
    return pl.pallas_call(
        body, name=name, grid=(m // tm, n // tn, nr),
        in_specs=in_specs, out_specs=out_spec,
        out_shape=jax.ShapeDtypeStruct(out_shape, F32),
        scratch_shapes=[pltpu.VMEM((tm, tn), F32)],
        compiler_params=_cparams(("parallel", "parallel", "arbitrary")),
    )(*args)


def mm(a, w, carrier, slab=None, out_dtype=F32, name="mm", skip=None):
    @jax.custom_vjp
    def f(a, w, carrier, *skip):
        return _matmul(a, w, "nn", 0, None, out_dtype, name + "_fwd", *skip)

    def fwd(a, w, carrier, *skip):
        return _matmul(a, w, "nn", 0, None, out_dtype, name + "_fwd", *skip), (a, w)

    def bwd(res, g):
        a, w = res
        da = _matmul(g, w, "nt", 0, None, F32, name + "_dx")
        dw = _matmul_dw(_transpose_cast(a, name + "_t"), g, None, slab, name + "_dw")
        return (da, jnp.zeros_like(w), dw) + ((g,) if skip is not None else ())

    f.defvjp(fwd, bwd)
    return f(a, w, carrier, *(() if skip is None else (skip,)))


def in_proj(h, w, carriers, slab, name):
    def run(h, w):
        return (_matmul(h, w, "nn", 0, N_QKV, BF16, name + "_qkv"),
                _matmul(h, w, "nn", N_QKV, D_INNER, F32, name + "_g"))

    @jax.custom_vjp
    def f(h, w, *cars):
        return run(h, w)

    def fwd(h, w, *cars):
        return run(h, w), (h, w)

    def bwd(res, g):
        h, w = res
        g_qkv, g_gate = g
        dh = _matmul_dx2(g_qkv, g_gate, w, name + "_dx")
        h_t = _transpose_cast(h, name + "_t")
        if slab:
            dws = (_matmul_dw(h_t, g_qkv, g_gate, ("col", PACK_COLS), name + "_dw"),)
        else:
            dws = (_matmul_dw(h_t, g_qkv, None, None, name + "_qkv_dw"),
                   _matmul_dw(h_t, g_gate, None, None, name + "_g_dw"))
        return (dh, jnp.zeros_like(w)) + dws

    f.defvjp(fwd, bwd)
    return f(h, w, *carriers)


def _rms_fwd(x, g, name):
    s, d = x.shape
    tm = _tile(s, 512, 8)

    def body(x_ref, g_ref, y_ref):
        x = x_ref[...]
        r = lax.rsqrt(jnp.mean(x * x, axis=-1, keepdims=True) + EPS)
        y_ref[...] = x * r * g_ref[...]

    return pl.pallas_call(
        body, name=name, grid=(s // tm,),
        in_specs=[pl.BlockSpec((tm, d), lambda i: (i, 0)), pl.BlockSpec((1, d), lambda i: (0, 0))],
        out_specs=pl.BlockSpec((tm, d), lambda i: (i, 0)),
        out_shape=jax.ShapeDtypeStruct((s, d), F32),
        compiler_params=_cparams(("parallel",)),
    )(x, g)


def _rms_bwd(x, g, dy, name, dskip=None):
    s, d = x.shape
    tm = _tile(s, 512, 8)

    def body(*refs):
        x_ref, g_ref, dy_ref, dx_ref, dg_ref = refs[0], refs[1], refs[2], refs[-2], refs[-1]

        @pl.when(pl.program_id(0) == 0)
        def _():
            dg_ref[...] = jnp.zeros_like(dg_ref)

        x = x_ref[...]
        dy = dy_ref[...]
        r = lax.rsqrt(jnp.mean(x * x, axis=-1, keepdims=True) + EPS)
        xh = x * r
        dg_ref[...] += jnp.sum(dy * xh, axis=0, keepdims=True)
        dxh = dy * g_ref[...]
        dx = r * (dxh - xh * jnp.mean(dxh * xh, axis=-1, keepdims=True))
        dx_ref[...] = dx if dskip is None else dx + refs[3][...]

    row = pl.BlockSpec((tm, d), lambda i: (i, 0))
    vec = pl.BlockSpec((1, d), lambda i: (0, 0))
    return pl.pallas_call(
        body, name=name, grid=(s // tm,),
        in_specs=[row, vec, row] + ([] if dskip is None else [row]),
        out_specs=[row, vec],
        out_shape=[jax.ShapeDtypeStruct((s, d), F32), jax.ShapeDtypeStruct((1, d), F32)],
        compiler_params=_cparams(("arbitrary",)),
    )(*([x, g, dy] if dskip is None else [x, g, dy, dskip]))


def rmsnorm(x, g, name="rms", skip=False):
    def run(x, g):
        y = _rms_fwd(x, g.reshape(1, -1), name + "_fwd")
        return (y, x) if skip else y

    @jax.custom_vjp
    def f(x, g):
        return run(x, g)

    def fwd(x, g):
        return run(x, g), (x, g)

    def bwd(res, ct):
        x, g = res
        dy, dskip = ct if skip else (ct, None)
        dx, dg = _rms_bwd(x, g.reshape(1, -1), dy, name + "_bwd", dskip)
        return dx, dg.reshape(-1)

    f.defvjp(fwd, bwd)
    return f(x, g)


def _gate_bwd(dy, o, gate, name):
    s = o.shape[0]
    tm = 256

    def body(dy_ref, o_ref, g_ref, do_ref, dg_ref, dl_ref):
        g = g_ref[...]
        o = o_ref[...]
        dy = dy_ref[...]
        sg = 1.0 / (1.0 + jnp.exp(-g))
        do = dy * (g * sg)
        do_ref[...] = do.astype(BF16)
        dg_ref[...] = dy * o * (sg * (1.0 + g * (1.0 - sg)))
        prod = do * o
        for h in range(N_HEADS):
            dl_ref[h] = jnp.sum(prod[:, h * HEAD_DIM:(h + 1) * HEAD_DIM], axis=1, keepdims=True)

    row = pl.BlockSpec((tm, D_INNER), lambda i: (i, 0))
    return pl.pallas_call(
        body, name=name, grid=(s // tm,),
        in_specs=[row, row, row],
        out_specs=[row, row, pl.BlockSpec((N_HEADS, tm, 1), lambda i: (0, i, 0))],
        out_shape=[jax.ShapeDtypeStruct((s, D_INNER), BF16), jax.ShapeDtypeStruct((s, D_INNER), F32),
                   jax.ShapeDtypeStruct((N_HEADS, s, 1), F32)],
        compiler_params=_cparams(("parallel",)),
    )(dy, o, gate)


SB_BLK = 256


def _softplus(z):
    return jnp.maximum(z, 0.0) + jnp.log(1.0 + jnp.exp(-jnp.abs(z)))


def _tri_sum(x, tri):
    hi = x.astype(BF16)
    lo = (x - hi.astype(F32)).astype(BF16)
    return _dot_nn(hi, tri) + _dot_nn(lo, tri)


SB_WIN = 2 * SB_BLK


def _sb_tri(kind):
    row = lax.broadcasted_iota(jnp.int32, (SB_BLK, SB_BLK), 0)
    col = lax.broadcasted_iota(jnp.int32, (SB_BLK, SB_BLK), 1)
    return ((row >= col) if kind == "suffix" else (row <= col)).astype(BF16)


def _sb_bounds(i, t):
    hi = (i + 1) * SB_BLK - t * SB_WIN
    return hi, pl.multiple_of(jnp.maximum(hi - SB_WIN, 0), SB_BLK)


def _sb_mask(i, hi, start):
    row = lax.broadcasted_iota(jnp.int32, (SB_BLK, SB_WIN), 0) + i * SB_BLK
    col = lax.broadcasted_iota(jnp.int32, (SB_BLK, SB_WIN), 1) + start
    return jnp.logical_and(col < row, col < hi)


def _sb_window(q, kwin, mask, a_run, tri_suffix, scale):
    b = SB_BLK
    z = _dot_nt(q, kwin) * scale
    sp = _softplus(z)
    ls = jnp.where(mask, -sp, 0.0)
    ls_l, ls_r = ls[:, :b], ls[:, b:]
    suffix = jnp.concatenate([_tri_sum(ls_l, tri_suffix) + jnp.sum(ls_r, axis=1, keepdims=True),
                              _tri_sum(ls_r, tri_suffix)], axis=1)
    w = jnp.where(mask, jnp.exp(z + suffix + a_run), 0.0)
    return z, sp, ls, w


def _silu_gate(o, g):
    return o * (g / (1.0 + jnp.exp(-g)))


def _sb_fwd(qkv, gate, name, gather=None):
    s = qkv.shape[0]
    b = SB_BLK
    nq = s // b
    scale = HEAD_DIM ** -0.5
    assert s >= SB_WIN
    shards, kinds = gather if gather else ((), ())
    nm = len(shards)

    def body(*refs):
        q_ref, k_ref, v_ref, g_ref = refs[:4]
        o_ref, y_ref = refs[4 + nm], refs[5 + nm]
        i = pl.program_id(1)
        if nm:
            start, finish = _gather_steps(refs[4:4 + nm], refs[6 + nm:6 + 2 * nm], kinds, *refs[6 + 2 * nm:])
            first_step = jnp.logical_and(pl.program_id(0) == 0, i == 0)
            last_step = jnp.logical_and(pl.program_id(0) == N_HEADS - 1, i == nq - 1)
            pl.when(first_step)(start)
        q = q_ref[...]
        tri_suffix = _sb_tri("suffix")

        def cond(c):
            t, a_run, _ = c
            return jnp.logical_and((i + 1) * b - t * SB_WIN > 0, jnp.max(a_run) > -SB_CUT)

        def step(c):
            t, a_run, acc = c
            hi, start = _sb_bounds(i, t)
            _, _, ls, w = _sb_window(q, k_ref[pl.ds(start, SB_WIN), :], _sb_mask(i, hi, start), a_run,
                                     tri_suffix, scale)
            acc = acc + _dot_nn(w.astype(BF16), v_ref[pl.ds(start, SB_WIN), :])
            return t + 1, a_run + jnp.sum(ls, axis=1, keepdims=True), acc

        _, _, acc = lax.while_loop(cond, step, (0, jnp.zeros((b, 1), F32), jnp.zeros((b, HEAD_DIM), F32)))
        o_ref[...] = acc
        y_ref[...] = _silu_gate(acc, g_ref[...])
        if nm:
            pl.when(last_step)(finish)

    blk = pl.BlockSpec((b, HEAD_DIM), lambda h, i: (i, h))
    out = pl.pallas_call(
        body, name=name, grid=(N_HEADS, nq),
        in_specs=[blk,
                  pl.BlockSpec((s, HEAD_DIM), lambda h, i: (0, N_HEADS + h)),
                  pl.BlockSpec((s, HEAD_DIM), lambda h, i: (0, 2 * N_HEADS + h)),
                  blk] + [ANY] * nm,
        out_specs=[blk, blk] + [ANY] * nm,
        out_shape=[jax.ShapeDtypeStruct((s, D_INNER), F32)] * 2 + _gather_out_shapes(shards, kinds),
        scratch_shapes=_gather_scratch(nm) if nm else [],
        compiler_params=_cparams(("arbitrary", "arbitrary")),
    )(qkv, qkv, qkv, gate, *shards)
    return out[0], out[1], list(out[2:])


def _sb_bwd(qkv, do, name, exchange=()):
    s = qkv.shape[0]
    b = SB_BLK
    nq = s // b
    n_win = -(-s // SB_WIN) + 1
    scale = HEAD_DIM ** -0.5
    assert s >= SB_WIN
    nm = len(exchange)

    def body(*refs):
        q_ref, k_ref, v_ref, do_ref = refs[:4]
        dq_ref, dk_ref, dv_ref = refs[4 + nm:7 + nm]
        dkt_s, dvt_s, g_buf, sig_buf = refs[7 + 2 * nm:11 + 2 * nm]
        i = pl.program_id(1)
        if nm:
            start_x, finish_x = _exchange_steps(refs[4:4 + nm], refs[7 + nm:7 + 2 * nm], *refs[11 + 2 * nm:])
            pl.when(jnp.logical_and(pl.program_id(0) == 0, i == 0))(start_x)

        @pl.when(i == 0)
        def _():
            dkt_s[...] = jnp.zeros_like(dkt_s)
            dvt_s[...] = jnp.zeros_like(dvt_s)

        q = q_ref[...]
        dob = do_ref[...]
        q_t = _transpose_bf16(q)
        do_t = _transpose_bf16(dob)
        tri_suffix = _sb_tri("suffix")
        tri_prefix = _sb_tri("prefix")

        def add_halves(acc_ref, start, upd):
            blk = start // b
            acc_ref[blk] += upd[:, :b]
            acc_ref[blk + 1] += upd[:, b:]

        def cond(c):
            t, a_run = c
            return jnp.logical_and((i + 1) * b - t * SB_WIN > 0, jnp.max(a_run) > -SB_CUT)

        def sweep(c):
            t, a_run = c
            hi, start = _sb_bounds(i, t)
            z, sp, ls, w = _sb_window(q, k_ref[pl.ds(start, SB_WIN), :], _sb_mask(i, hi, start), a_run,
                                      tri_suffix, scale)
            g_buf[t] = w * _dot_nt(dob, v_ref[pl.ds(start, SB_WIN), :])
            sig_buf[t] = jnp.exp(z - sp)
            add_halves(dvt_s, start, _dot_nn(do_t, w.astype(BF16)))
            return t + 1, a_run + jnp.sum(ls, axis=1, keepdims=True)

        n_steps, _ = lax.while_loop(cond, sweep, (0, jnp.zeros((b, 1), F32)))

        def back(u, c):
            g_run, dq = c
            t = n_steps - 1 - u
            hi, start = _sb_bounds(i, t)
            g = g_buf[t]
            g_l, g_r = g[:, :b], g[:, b:]
            g_incl = g_run + jnp.concatenate(
                [_tri_sum(g_l, tri_prefix),
                 _tri_sum(g_r, tri_prefix) + jnp.sum(g_l, axis=1, keepdims=True)], axis=1)
            dz = jnp.where(_sb_mask(i, hi, start), (g - sig_buf[t] * g_incl) * scale, 0.0).astype(BF16)
            add_halves(dkt_s, start, _dot_nn(q_t, dz))
            return g_run + jnp.sum(g, axis=1, keepdims=True), dq + _dot_nn(dz, k_ref[pl.ds(start, SB_WIN), :])

        _, dq = lax.fori_loop(0, n_steps, back, (jnp.zeros((b, 1), F32), jnp.zeros((b, HEAD_DIM), F32)))
        dq_ref[...] = dq.astype(BF16)

        @pl.when(i == nq - 1)
        def _():
            for jb in range(nq):
                dk_ref[jb * b:(jb + 1) * b, :] = dkt_s[jb].T.astype(BF16)
                dv_ref[jb * b:(jb + 1) * b, :] = dvt_s[jb].T.astype(BF16)

        if nm:
            pl.when(jnp.logical_and(pl.program_id(0) == N_HEADS - 1, i == nq - 1))(finish_x)

    blk = pl.BlockSpec((b, HEAD_DIM), lambda h, i: (i, h))
    head = pl.BlockSpec((s, HEAD_DIM), lambda h, i: (0, h))
    out = pl.pallas_call(
        body, name=name, grid=(N_HEADS, nq),
        in_specs=[blk,
                  pl.BlockSpec((s, HEAD_DIM), lambda h, i: (0, N_HEADS + h)),
                  pl.BlockSpec((s, HEAD_DIM), lambda h, i: (0, 2 * N_HEADS + h)),
                  blk] + [ANY] * nm,
        out_specs=[blk, head, head] + [ANY] * nm,
        out_shape=[jax.ShapeDtypeStruct((s, D_INNER), BF16)] * 3
        + [jax.ShapeDtypeStruct(p.shape, p.dtype) for p in exchange],
        scratch_shapes=[pltpu.VMEM((nq, HEAD_DIM, b), F32), pltpu.VMEM((nq, HEAD_DIM, b), F32),
                        pltpu.VMEM((n_win, b, SB_WIN), F32), pltpu.VMEM((n_win, b, SB_WIN), F32)]
        + (_exchange_scratch(nm) if nm else []),
        compiler_params=_cparams(("arbitrary", "arbitrary")),
    )(qkv, qkv, qkv, do, *exchange)
    return out[0], out[1], out[2], list(out[3:])


def sb_core(qkv, gate, name, gather=None, late=None):
    shards, kinds = gather if gather else ((), ())
    slots, carrier_shapes, to_slabs = late if late else ((), (), None)
    n_sh = len(shards)

    def run(qkv, gate, *shards):
        o, y, gathered = _sb_fwd(qkv, gate, name + "_fwd", (shards, kinds) if shards else None)
        carriers = [jnp.zeros(sh, F32) for sh in carrier_shapes]
        return (y, *gathered, *carriers), o

    @jax.custom_vjp
    def f(qkv, gate, *extra):
        return run(qkv, gate, *extra[:n_sh])[0]

    def fwd(qkv, gate, *extra):
        outs, o = run(qkv, gate, *extra[:n_sh])
        return outs, (qkv, gate, o)

    def bwd(res, cts):
        qkv, gate, o = res
        do, dgate, _ = _gate_bwd(cts[0], o, gate, name + "_gate_bwd")
        parts = []
        if slots:
            g_slabs = to_slabs(cts[1 + n_sh:])
            core_idx = lax.axis_index("c").astype(jnp.int32).reshape(1)
            from_sibling = _pair_exchange(g_slabs, name + "_pair_exchange")
            parts = [_pair_sum(core_idx, g, r, name + "_pair_sum%d" % k)
                     for k, (g, r) in enumerate(zip(g_slabs, from_sibling))]
        dq, dk, dv, by_chip = _sb_bwd(qkv, do, name + "_bwd", parts)
        zeros = tuple(jnp.zeros(sh.shape, sh.dtype) for sh in shards)
        return (jnp.concatenate([dq, dk, dv], axis=1), dgate) + zeros + tuple(by_chip)

    f.defvjp(fwd, bwd)
    outs = f(qkv, gate, *shards, *slots)
    return outs[0], list(outs[1:1 + n_sh]), list(outs[1 + n_sh:])


SM_FWD_BLK = 256
SM_BLK = 512


SM_FWD_KEYS = 1024
SM_BWD_KEYS = 512


def _sm_mask(i, jw, rows, keys, chunk_shift):
    row = lax.broadcasted_iota(jnp.int32, (rows, keys), 0) + i * rows
    col = lax.broadcasted_iota(jnp.int32, (rows, keys), 1) + jw * keys
    return (col >> chunk_shift) <= (row >> chunk_shift)


def _sm_fwd(qa, ka, va, ccol, crow, gate, dqk, qo, ko, vo, chunk_shift, scale, name):
    s = qa.shape[0]
    b = min(SM_FWD_BLK, s)
    keys = min(SM_FWD_KEYS, s)
    per = keys // b
    nq = s // b
    has_bias = ccol is not None

    def body(*refs):
        if has_bias:
            q_ref, k_ref, v_ref, cc_ref, cr_ref, g_ref, o_ref, lse_ref, y_ref, m_s, l_s, acc_s = refs
        else:
            q_ref, k_ref, v_ref, g_ref, o_ref, lse_ref, y_ref, m_s, l_s, acc_s = refs
        i = pl.program_id(1)
        q = q_ref[...]
        m_s[...] = jnp.full_like(m_s, NEG)
        l_s[...] = jnp.zeros_like(l_s)
        acc_s[...] = jnp.zeros_like(acc_s)

        def sweep(jw, masked):
            off = pl.multiple_of(jw * keys, keys)
            z = _dot_nt(q, k_ref[pl.ds(off, keys), :]) * scale
            if has_bias:
                z = z + cc_ref[...] - cr_ref[jw]
            if masked:
                z = jnp.where(_sm_mask(i, jw, b, keys, chunk_shift), z, NEG)
            m_old = m_s[...]
            m_new = jnp.maximum(m_old, jnp.max(z, axis=1, keepdims=True))
            alpha = jnp.exp(m_old - m_new)
            p = jnp.exp(z - m_new)
            l_s[...] = alpha * l_s[...] + jnp.sum(p, axis=1, keepdims=True)
            acc_s[...] = alpha * acc_s[...] + _dot_nn(p.astype(BF16), v_ref[pl.ds(off, keys), :])
            m_s[...] = m_new

        def full(jw, carry):
            sweep(jw, False)
            return carry

        lax.fori_loop(0, i // per, full, 0)
        sweep(i // per, True)
        o = acc_s[...] / l_s[...]
        o_ref[...] = o
        y_ref[...] = _silu_gate(o, g_ref[...])
        lse_ref[...] = m_s[...] + jnp.log(l_s[...])

    blk = pl.BlockSpec((b, HEAD_DIM), lambda h, i: (i, h))
    in_specs = [pl.BlockSpec((b, dqk), lambda h, i: (i, qo + h)),
                pl.BlockSpec((s, dqk), lambda h, i: (0, ko + h)),
                pl.BlockSpec((s, HEAD_DIM), lambda h, i: (0, vo + h))]
    args = [qa, ka, va]
    if has_bias:
        in_specs += [pl.BlockSpec((None, b, 1), lambda h, i: (h, i, 0)),
                     pl.BlockSpec((None, s // keys, 1, keys), lambda h, i: (h, 0, 0, 0))]
        args += [ccol, crow]
    return pl.pallas_call(
        body, name=name, grid=(N_HEADS, nq),
        in_specs=in_specs + [blk],
        out_specs=[blk, pl.BlockSpec((None, b, 1), lambda h, i: (h, i, 0)), blk],
        out_shape=[jax.ShapeDtypeStruct((s, D_INNER), F32), jax.ShapeDtypeStruct((N_HEADS, s, 1), F32),
                   jax.ShapeDtypeStruct((s, D_INNER), F32)],
        scratch_shapes=[pltpu.VMEM((b, 1), F32), pltpu.VMEM((b, 1), F32), pltpu.VMEM((b, HEAD_DIM), F32)],
        compiler_params=_cparams(("parallel", "arbitrary")),
    )(*args, gate)


def _sm_bwd(qa, ka, va, do, lse, delta, ccol, crow, dqk, qo, ko, vo, chunk_shift, scale, grad_dtype, name):
    s = qa.shape[0]
    b = SM_BLK
    keys = min(SM_BWD_KEYS, s)
    per = keys // b
    nq = s // b
    nk = s // keys
    has_bias = ccol is not None

    def body(*refs):
        if has_bias:
            (q_ref, k_ref, v_ref, do_ref, lse_ref, dl_ref, cc_ref, cr_ref,
             dq_ref, dk_ref, dv_ref, dc_ref, dr_ref, dq_s, dkt_s, dvt_s, dc_s, dr_s) = refs
        else:
            (q_ref, k_ref, v_ref, do_ref, lse_ref, dl_ref,
             dq_ref, dk_ref, dv_ref, dq_s, dkt_s, dvt_s) = refs
        i = pl.program_id(1)

        @pl.when(i == 0)
        def _():
            dkt_s[...] = jnp.zeros_like(dkt_s)
            dvt_s[...] = jnp.zeros_like(dvt_s)
            if has_bias:
                dc_s[...] = jnp.zeros_like(dc_s)

        q = q_ref[...]
        dob = do_ref[...]
        q_t = _transpose_bf16(q)
        do_t = _transpose_bf16(dob)
        lse = lse_ref[...]
        delta = dl_ref[...]
        dq_s[...] = jnp.zeros_like(dq_s)
        if has_bias:
            dr_s[...] = jnp.zeros_like(dr_s)

        def sweep(jw, masked):
            off = pl.multiple_of(jw * keys, keys)
            kb = k_ref[pl.ds(off, keys), :]
            z = _dot_nt(q, kb) * scale
            if has_bias:
                z = z + cc_ref[...] - cr_ref[jw]
            p = jnp.exp(z - lse)
            if masked:
                p = jnp.where(_sm_mask(i, jw, b, keys, chunk_shift), p, 0.0)
            dvt_s[jw] += _dot_nn(do_t, p.astype(BF16))
            dz = p * (_dot_nt(dob, v_ref[pl.ds(off, keys), :]) - delta)
            if has_bias:
                dc_s[jw] += jnp.sum(dz, axis=0, keepdims=True)
                dr_s[...] += jnp.sum(dz, axis=1, keepdims=True)
            dzs = (dz * scale).astype(BF16)
            dkt_s[jw] += _dot_nn(q_t, dzs)
            dq_s[...] += _dot_nn(dzs, kb)

        def full(jw, carry):
            sweep(jw, False)
            return carry

        lax.fori_loop(0, i // per, full, 0)
        sweep(i // per, True)
        dq_ref[...] = dq_s[...].astype(grad_dtype)
        if has_bias:
            dr_ref[...] = dr_s[...]

        @pl.when(i == nq - 1)
        def _():
            for jw in range(nk):
                dk_ref[jw * keys:(jw + 1) * keys, :] = dkt_s[jw].T.astype(grad_dtype)
                dv_ref[jw * keys:(jw + 1) * keys, :] = dvt_s[jw].T.astype(grad_dtype)
            if has_bias:
                dc_ref[...] = dc_s[...]

    vec = pl.BlockSpec((None, b, 1), lambda h, i: (h, i, 0))
    in_specs = [pl.BlockSpec((b, dqk), lambda h, i: (i, qo + h)),
                pl.BlockSpec((s, dqk), lambda h, i: (0, ko + h)),
                pl.BlockSpec((s, HEAD_DIM), lambda h, i: (0, vo + h)),
                pl.BlockSpec((b, HEAD_DIM), lambda h, i: (i, h)),
                vec, vec]
    args = [qa, ka, va, do, lse, delta]
    out_specs = [pl.BlockSpec((b, dqk), lambda h, i: (i, h)),
                 pl.BlockSpec((s, dqk), lambda h, i: (0, h)),
                 pl.BlockSpec((s, HEAD_DIM), lambda h, i: (0, h))]
    out_shape = [jax.ShapeDtypeStruct((s, N_HEADS * dqk), grad_dtype),
                 jax.ShapeDtypeStruct((s, N_HEADS * dqk), grad_dtype),
                 jax.ShapeDtypeStruct((s, D_INNER), grad_dtype)]
    scratch = [pltpu.VMEM((b, dqk), F32), pltpu.VMEM((nk, dqk, keys), F32), pltpu.VMEM((nk, HEAD_DIM, keys), F32)]
    if has_bias:
        key_vec = pl.BlockSpec((None, nk, 1, keys), lambda h, i: (h, 0, 0, 0))
        in_specs += [vec, key_vec]
        args += [ccol, crow]
        out_specs += [key_vec, vec]
        out_shape += [jax.ShapeDtypeStruct((N_HEADS, nk, 1, keys), F32), jax.ShapeDtypeStruct((N_HEADS, s, 1), F32)]
        scratch += [pltpu.VMEM((nk, 1, keys), F32), pltpu.VMEM((b, 1), F32)]
    return pl.pallas_call(
        body, name=name, grid=(N_HEADS, nq),
        in_specs=in_specs, out_specs=out_specs, out_shape=out_shape, scratch_shapes=scratch,
        compiler_params=_cparams(("arbitrary", "arbitrary")),
    )(*args)


def fox_core(qkv, gate, c, name):
    s = qkv.shape[0]
    scale = HEAD_DIM ** -0.5
    cfg = dict(dqk=HEAD_DIM, qo=0, ko=N_HEADS, vo=2 * N_HEADS, chunk_shift=0, scale=scale)

    def layouts(c, keys):
        ct = c.T
        keys = min(keys, s)
        return ct.reshape(N_HEADS, s, 1), ct.reshape(N_HEADS, s // keys, 1, keys)

    def run(qkv, gate, c):
        ccol, crow = layouts(c, SM_FWD_KEYS)
        o, lse, y = _sm_fwd(qkv, qkv, qkv, ccol, crow, gate, name=name + "_fwd", **cfg)
        return y, o, lse

    @jax.custom_vjp
    def f(qkv, gate, c):
        return run(qkv, gate, c)[0]

    def fwd(qkv, gate, c):
        y, o, lse = run(qkv, gate, c)
        return y, (qkv, gate, c, o, lse)

    def bwd(res, dy):
        qkv, gate, c, o, lse = res
        ccol, crow = layouts(c, SM_BWD_KEYS)
        do, dgate, delta = _gate_bwd(dy, o, gate, name + "_gate_bwd")
        dq, dk, dv, colsum, rowsum = _sm_bwd(qkv, qkv, qkv, do, lse, delta, ccol, crow,
                                             grad_dtype=BF16, name=name + "_bwd", **cfg)
        dc = (rowsum.reshape(N_HEADS, s) - colsum.reshape(N_HEADS, s)).T
        return jnp.concatenate([dq, dk, dv], axis=1), dgate, dc

    f.defvjp(fwd, bwd)
    return f(qkv, gate, c)


def _mla_rope(x, cosv, sinv, out_dtype, name):
    s, width = x.shape
    tm = 256
    half = MLA_ROPE // 2

    def body(x_ref, c_ref, s_ref, o_ref):
        c = c_ref[...]
        sn = s_ref[...]
        lane = lax.broadcasted_iota(jnp.int32, (tm, HEAD_DIM), 1)
        for h in range(N_HEADS):
            lo = h * MLA_QK_PAD
            o_ref[:, lo:lo + HEAD_DIM] = x_ref[:, lo:lo + HEAD_DIM].astype(out_dtype)
            g = x_ref[:, lo + HEAD_DIM:lo + MLA_QK_PAD].astype(F32)
            swapped = jnp.where(lane < half, pltpu.roll(g, HEAD_DIM - half, 1), pltpu.roll(g, half, 1))
            o_ref[:, lo + HEAD_DIM:lo + MLA_QK_PAD] = (g * c + swapped * sn).astype(out_dtype)

    row = pl.BlockSpec((tm, width), lambda i: (i, 0))
    tab = pl.BlockSpec((tm, HEAD_DIM), lambda i: (i, 0))
    return pl.pallas_call(
        body, name=name, grid=(s // tm,),
        in_specs=[row, tab, tab], out_specs=row,
        out_shape=jax.ShapeDtypeStruct((s, width), out_dtype),
        compiler_params=_cparams(("parallel",)),
    )(x, cosv, sinv)


def _mla_fwd(qc, kv, kr, gate, chunk_shift, scale, name):
    s = qc.shape[0]
    b = min(SM_FWD_BLK, s)
    keys = min(SM_FWD_KEYS, s)
    per = keys // b
    nq = s // b

    def body(q_ref, kn_ref, v_ref, kr_ref, g_ref, o_ref, lse_ref, y_ref, m_s, l_s, acc_s, kc_s):
        i = pl.program_id(1)

        @pl.when(i == 0)
        def _():
            kc_s[:, :HEAD_DIM] = kn_ref[...]
            kc_s[:, HEAD_DIM:] = kr_ref[...]

        q = q_ref[...]
        m_s[...] = jnp.full_like(m_s, NEG)
        l_s[...] = jnp.zeros_like(l_s)
        acc_s[...] = jnp.zeros_like(acc_s)

        def sweep(jw, masked):
            off = pl.multiple_of(jw * keys, keys)
            z = _dot_nt(q, kc_s[pl.ds(off, keys), :]) * scale
            if masked:
                z = jnp.where(_sm_mask(i, jw, b, keys, chunk_shift), z, NEG)
            m_old = m_s[...]
            m_new = jnp.maximum(m_old, jnp.max(z, axis=1, keepdims=True))
            alpha = jnp.exp(m_old - m_new)
            p = jnp.exp(z - m_new)
            l_s[...] = alpha * l_s[...] + jnp.sum(p, axis=1, keepdims=True)
            acc_s[...] = alpha * acc_s[...] + _dot_nn(p.astype(BF16), v_ref[pl.ds(off, keys), :])
            m_s[...] = m_new

        def full(jw, carry):
            sweep(jw, False)
            return carry

        lax.fori_loop(0, i // per, full, 0)
        sweep(i // per, True)
        o = acc_s[...] / l_s[...]
        o_ref[...] = o
        y_ref[...] = _silu_gate(o, g_ref[...])
        lse_ref[...] = m_s[...] + jnp.log(l_s[...])

    blk = pl.BlockSpec((b, HEAD_DIM), lambda h, i: (i, h))
    return pl.pallas_call(
        body, name=name, grid=(N_HEADS, nq),
        in_specs=[pl.BlockSpec((b, MLA_QK_PAD), lambda h, i: (i, h)),
                  pl.BlockSpec((s, HEAD_DIM), lambda h, i: (0, 2 * h)),
                  pl.BlockSpec((s, HEAD_DIM), lambda h, i: (0, 2 * h + 1)),
                  pl.BlockSpec((s, HEAD_DIM), lambda h, i: (0, 0)),
                  blk],
        out_specs=[blk, pl.BlockSpec((None, b, 1), lambda h, i: (h, i, 0)), blk],
        out_shape=[jax.ShapeDtypeStruct((s, D_INNER), F32), jax.ShapeDtypeStruct((N_HEADS, s, 1), F32),
                   jax.ShapeDtypeStruct((s, D_INNER), F32)],
        scratch_shapes=[pltpu.VMEM((b, 1), F32), pltpu.VMEM((b, 1), F32), pltpu.VMEM((b, HEAD_DIM), F32),
                        pltpu.VMEM((s, MLA_QK_PAD), BF16)],
        compiler_params=_cparams(("arbitrary", "arbitrary")),
    )(qc, kv, kv, kr, gate)


def _mla_bwd(qc, kv, kr, do, lse, delta, chunk_shift, scale, name):
    s = qc.shape[0]
    b = min(SM_BLK, s)
    keys = min(SM_BWD_KEYS, s)
    per = keys // b
    nq = s // b
    nk = s // keys

    def body(q_ref, kn_ref, v_ref, kr_ref, do_ref, lse_ref, dl_ref, dq_ref, dkv_ref, dkr_ref,
             dq_s, dkt_s, dvt_s, dkrt_s, kc_s):
        h = pl.program_id(0)
        i = pl.program_id(1)

        @pl.when(jnp.logical_and(h == 0, i == 0))
        def _():
            dkrt_s[...] = jnp.zeros_like(dkrt_s)

        @pl.when(i == 0)
        def _():
            dkt_s[...] = jnp.zeros_like(dkt_s)
            dvt_s[...] = jnp.zeros_like(dvt_s)
            kc_s[:, :HEAD_DIM] = kn_ref[...]
            kc_s[:, HEAD_DIM:] = kr_ref[...]

        q = q_ref[...]
        dob = do_ref[...]
        q_t = _transpose_bf16(q)
        do_t = _transpose_bf16(dob)
        lse_i = lse_ref[...]
        delta_i = dl_ref[...]
        dq_s[...] = jnp.zeros_like(dq_s)

        def sweep(jw, masked):
            off = pl.multiple_of(jw * keys, keys)
            kc = kc_s[pl.ds(off, keys), :]
            p = jnp.exp(_dot_nt(q, kc) * scale - lse_i)
            if masked:
                p = jnp.where(_sm_mask(i, jw, b, keys, chunk_shift), p, 0.0)
            dvt_s[jw] += _dot_nn(do_t, p.astype(BF16))
            dz = p * (_dot_nt(dob, v_ref[pl.ds(off, keys), :]) - delta_i)
            dzs = (dz * scale).astype(BF16)
            dkc_t = _dot_nn(q_t, dzs)
            dkt_s[jw] += dkc_t[:HEAD_DIM]
            dkrt_s[jw] += dkc_t[HEAD_DIM:]
            dq_s[...] += _dot_nn(dzs, kc)

        def full(jw, carry):
            sweep(jw, False)
            return carry

        lax.fori_loop(0, i // per, full, 0)
        sweep(i // per, True)
        dq_ref[...] = dq_s[...]

        @pl.when(i == nq - 1)
        def _():
            for jw in range(nk):
                dkv_ref[jw * keys:(jw + 1) * keys, :HEAD_DIM] = dkt_s[jw].T.astype(BF16)
                dkv_ref[jw * keys:(jw + 1) * keys, HEAD_DIM:] = dvt_s[jw].T.astype(BF16)

        @pl.when(jnp.logical_and(h == N_HEADS - 1, i == nq - 1))
        def _():
            for jw in range(nk):
                dkr_ref[jw * keys:(jw + 1) * keys, :] = dkrt_s[jw].T

    vec = pl.BlockSpec((None, b, 1), lambda h, i: (h, i, 0))
    acc = pltpu.VMEM((nk, HEAD_DIM, keys), F32)
    return pl.pallas_call(
        body, name=name, grid=(N_HEADS, nq),
        in_specs=[pl.BlockSpec((b, MLA_QK_PAD), lambda h, i: (i, h)),
                  pl.BlockSpec((s, HEAD_DIM), lambda h, i: (0, 2 * h)),
                  pl.BlockSpec((s, HEAD_DIM), lambda h, i: (0, 2 * h + 1)),
                  pl.BlockSpec((s, HEAD_DIM), lambda h, i: (0, 0)),
                  pl.BlockSpec((b, HEAD_DIM), lambda h, i: (i, h)),
                  vec, vec],
        out_specs=[pl.BlockSpec((b, MLA_QK_PAD), lambda h, i: (i, h)),
                   pl.BlockSpec((s, MLA_QK_PAD), lambda h, i: (0, h)),
                   pl.BlockSpec((s, HEAD_DIM), lambda h, i: (0, 0))],
        out_shape=[jax.ShapeDtypeStruct((s, N_HEADS * MLA_QK_PAD), F32),
                   jax.ShapeDtypeStruct((s, N_HEADS * MLA_QK_PAD), BF16),
                   jax.ShapeDtypeStruct((s, HEAD_DIM), F32)],
        scratch_shapes=[pltpu.VMEM((b, MLA_QK_PAD), F32), acc, acc, acc, pltpu.VMEM((s, MLA_QK_PAD), BF16)],
        compiler_params=_cparams(("arbitrary", "arbitrary")),
    )(qc, kv, kv, kr, do, lse, delta)


def mla_core(qp, kv, kr, gate, cosv, sinv, name):
    scale = (MLA_NOPE + MLA_ROPE) ** -0.5
    shift = MLA_CHUNK.bit_length() - 1

    def run(qp, kv, kr, gate, cosv, sinv):
        qc = _mla_rope(qp, cosv, sinv, BF16, name + "_rope")
        krb = kr.astype(BF16)
        o, lse, y = _mla_fwd(qc, kv, krb, gate, shift, scale, name + "_fwd")
        return y, (qc, kv, krb, gate, o, lse, cosv, sinv)

    @jax.custom_vjp
    def f(qp, kv, kr, gate, cosv, sinv):
        return run(qp, kv, kr, gate, cosv, sinv)[0]

    def bwd(res, dy):
        qc, kv, krb, gate, o, lse, cosv, sinv = res
        do, dgate, delta = _gate_bwd(dy, o, gate, name + "_gate_bwd")
        dqc, dkv, dkr = _mla_bwd(qc, kv, krb, do, lse, delta, shift, scale, name + "_bwd")
        dqp = _mla_rope(dqc, cosv, -sinv, F32, name + "_rope_bwd")
        return dqp, dkv, dkr, dgate, jnp.zeros_like(cosv), jnp.zeros_like(sinv)

    f.defvjp(run, bwd)
    return f(qp, kv, kr, gate, cosv, sinv)


def _sq_loss_call(y, t, name):
    s, d = y.shape
    tm = _tile(s, 512, 8)

    def body(y_ref, t_ref, l_ref, e_ref):
        @pl.when(pl.program_id(0) == 0)
        def _():
            l_ref[...] = jnp.zeros_like(l_ref)

        e = y_ref[...] - t_ref[...]
        e_ref[...] = e * (1.0 / d)
        part = jnp.sum(jnp.sum(e * e, axis=1, keepdims=True), axis=0, keepdims=True)
        l_ref[...] += jnp.broadcast_to(part * (0.5 / d), l_ref.shape)

    row = pl.BlockSpec((tm, d), lambda i: (i, 0))
    return pl.pallas_call(
        body, name=name, grid=(s // tm,),
        in_specs=[row, row],
        out_specs=[pl.BlockSpec((8, 128), lambda i: (0, 0)), row],
        out_shape=[jax.ShapeDtypeStruct((8, 128), F32), jax.ShapeDtypeStruct((s, d), F32)],
        compiler_params=_cparams(("arbitrary",)),
    )(y, t)


@jax.custom_vjp
def sq_loss(y, t):
    return _sq_loss_call(y, t, "loss_fwd")[0][0, 0]


def _sq_loss_fwd(y, t):
    l, e = _sq_loss_call(y, t, "loss_fwd")
    return l[0, 0], e


def _sq_loss_bwd(e, g):
    return g * e, jnp.zeros_like(e)


sq_loss.defvjp(_sq_loss_fwd, _sq_loss_bwd)


def _pair_sum(core_idx, g, recv, name):
    _, _, r, c = g.shape
    tb = _tile(r, 512, 16)

    def body(c_ref, g_ref, r_ref, o_ref):
        o_ref[...] = (g_ref[...] + r_ref[...]).astype(BF16)

    return pl.pallas_call(
        body, name=name,
        grid_spec=pltpu.PrefetchScalarGridSpec(
            num_scalar_prefetch=1, grid=(4, r // tb),
            in_specs=[pl.BlockSpec((None, None, tb, c), lambda q, i, c_ref: (c_ref[0], q, i, 0)),
                      pl.BlockSpec((None, tb, c), lambda q, i, c_ref: (q, i, 0))],
            out_specs=pl.BlockSpec((None, tb, c), lambda q, i, c_ref: (q, i, 0))),
        out_shape=jax.ShapeDtypeStruct((4, r, c), BF16),
        compiler_params=_cparams(("parallel", "parallel")),
    )(core_idx, g, recv)


def _adamw(w, parts, m, v, name):
    n, r, c = parts.shape
    tb = _tile(r, 256, 8)
    b1c = 1.0 - ADAM_B1 ** ADAM_STEP
    b2c = 1.0 - ADAM_B2 ** ADAM_STEP

    def body(w_ref, p_ref, m_ref, v_ref, g_ref, d_ref, nm_ref, nv_ref):
        g = p_ref[0].astype(F32)
        for k in range(1, n):
            g = g + p_ref[k].astype(F32)
        m_new = ADAM_B1 * m_ref[...] + (1.0 - ADAM_B1) * g
        v_new = ADAM_B2 * v_ref[...] + (1.0 - ADAM_B2) * (g * g)
        m_hat = m_new / b1c
        v_hat = v_new / b2c
        g_ref[...] = g
        d_ref[...] = -ADAM_LR * (m_hat / (jnp.sqrt(v_hat) + ADAM_EPS) + ADAM_WD * w_ref[...])
        nm_ref[...] = m_new
        nv_ref[...] = v_new

    row = pl.BlockSpec((tb, c), lambda i: (i, 0))
    return pl.pallas_call(
        body, name=name, grid=(r // tb,),
        in_specs=[row, pl.BlockSpec((n, tb, c), lambda i: (0, i, 0)), row, row],
        out_specs=[row] * 4,
        out_shape=[jax.ShapeDtypeStruct((r, c), F32)] * 4,
        compiler_params=_cparams(("parallel",)),
    )(w, parts, m, v)


ANY = pl.BlockSpec(memory_space=pl.ANY)


def _place():
    return lax.axis_index("x"), lax.axis_index("y"), lax.axis_index("c")


def _all_gather(shards, kinds, name):
    nm = len(shards)

    def body(*refs):
        start, finish = _gather_steps(refs[:nm], refs[nm:2 * nm], kinds, *refs[2 * nm:])
        start()
        finish()

    return pl.pallas_call(
        body, name=name,
        out_shape=_gather_out_shapes(shards, kinds),
        in_specs=[ANY] * nm, out_specs=[ANY] * nm,
        scratch_shapes=_gather_scratch(nm),
    )(*shards)


def _gather_out_shapes(shards, kinds):
    def full(sh, kind):
        a, b = sh.shape
        return {"row": (N_DEV * a, b), "col": (a, N_DEV * b), "stack": (N_DEV, a, b)}[kind]

    return [jax.ShapeDtypeStruct(full(sh, kd), sh.dtype) for sh, kd in zip(shards, kinds)]


def _gather_scratch(nm):
    return [pltpu.SemaphoreType.DMA((7 * nm,)), pltpu.SemaphoreType.DMA((7 * nm,)), pltpu.SemaphoreType.DMA((nm,))]


def _gather_steps(x_refs, out_refs, kinds, send_sems, recv_sems, local_sems):
    nm = len(x_refs)
    x, y, cc = _place()
    me, sibling = (x, y, cc), (x, y, 1 - cc)
    chips = [(1 - x, y), (x, 1 - y), (1 - x, 1 - y)]

    def slot(mi, px, py, pc):
        d = 4 * px + 2 * py + pc
        a, b = x_refs[mi].shape
        if kinds[mi] == "row":
            return out_refs[mi].at[pl.ds(pl.multiple_of(d * a, a), a), :]
        if kinds[mi] == "col":
            return out_refs[mi].at[:, pl.ds(pl.multiple_of(d * b, 128), b)]
        return out_refs[mi].at[d]

    def copy(mi, k, block, to, src=None):
        return pltpu.make_async_remote_copy(
            src_ref=slot(mi, *block) if src is None else src, dst_ref=slot(mi, *block),
            send_sem=send_sems.at[7 * mi + k], recv_sem=recv_sems.at[7 * mi + k],
            device_id=to, device_id_type=MESH)

    def own_copies():
        mine = [pltpu.make_async_copy(x_refs[mi], slot(mi, *me), local_sems.at[mi]) for mi in range(nm)]
        first = []
        for mi in range(nm):
            first.append(copy(mi, 0, me, sibling, src=x_refs[mi]))
            first += [copy(mi, 1 + j, me, (*chip, cc), src=x_refs[mi]) for j, chip in enumerate(chips)]
        return mine, first

    def start():
        mine, first = own_copies()
        for cp in mine + first:
            cp.start()

    def finish():
        mine, first = own_copies()
        passed = []
        for j, chip in enumerate(chips):
            for mi in range(nm):
                copy(mi, 1 + j, (*chip, cc), me).wait_recv()
                passed.append(copy(mi, 4 + j, (*chip, cc), sibling))
                passed[-1].start()
        for mi in range(nm):
            copy(mi, 0, sibling, me).wait_recv()
            for j, chip in enumerate(chips):
                copy(mi, 4 + j, (*chip, 1 - cc), me).wait_recv()
        for cp in first + passed:
            cp.wait_send()
        for cp in mine:
            cp.wait()

    return start, finish


def _pair_exchange(gs, name):
    nm = len(gs)

    def body(*refs):
        g_refs, recv_refs = refs[:nm], refs[nm:2 * nm]
        send_sems, recv_sems = refs[2 * nm:]
        x, y, cc = _place()
        copies = [pltpu.make_async_remote_copy(
            src_ref=g_refs[mi].at[1 - cc], dst_ref=recv_refs[mi],
            send_sem=send_sems.at[mi], recv_sem=recv_sems.at[mi], device_id=(x, y, 1 - cc), device_id_type=MESH)
            for mi in range(nm)]
        for cp in copies:
            cp.start()
        for cp in copies:
            cp.wait_recv()
        for cp in copies:
            cp.wait_send()

    return pl.pallas_call(
        body, name=name,
        out_shape=[jax.ShapeDtypeStruct(g.shape[1:], g.dtype) for g in gs],
        in_specs=[ANY] * nm, out_specs=[ANY] * nm,
        scratch_shapes=[pltpu.SemaphoreType.DMA((nm,)), pltpu.SemaphoreType.DMA((nm,))],
    )(*gs)


def _chip_exchange(parts, name):
    nm = len(parts)

    def body(*refs):
        start, finish = _exchange_steps(refs[:nm], refs[nm:2 * nm], *refs[2 * nm:])
        start()
        finish()

    return pl.pallas_call(
        body, name=name,
        out_shape=[jax.ShapeDtypeStruct(p.shape, p.dtype) for p in parts],
        in_specs=[ANY] * nm, out_specs=[ANY] * nm,
        scratch_shapes=_exchange_scratch(nm),
    )(*parts)


def _exchange_scratch(nm):
    return [pltpu.SemaphoreType.DMA((4 * nm,)), pltpu.SemaphoreType.DMA((4 * nm,)), pltpu.SemaphoreType.DMA((nm,))]


def _exchange_steps(p_refs, out_refs, send_sems, recv_sems, local_sems):
    nm = len(p_refs)
    x, y, cc = _place()
    mine = 2 * x + y
    others = [(1 - x, y), (x, 1 - y), (1 - x, 1 - y)]

    def own_copies():
        keeps = [pltpu.make_async_copy(p_refs[mi].at[mine], out_refs[mi].at[mine], local_sems.at[mi])
                 for mi in range(nm)]
        sends = []
        for px, py in others:
            q = 2 * px + py
            for mi in range(nm):
                sends.append(pltpu.make_async_remote_copy(
                    src_ref=p_refs[mi].at[q], dst_ref=out_refs[mi].at[mine],
                    send_sem=send_sems.at[4 * mi + q], recv_sem=recv_sems.at[4 * mi + mine],
                    device_id=(px, py, cc), device_id_type=MESH))
        return keeps, sends

    def start():
        keeps, sends = own_copies()
        for cp in keeps + sends:
            cp.start()

    def finish():
        keeps, sends = own_copies()
        for px, py in others:
            q = 2 * px + py
            for mi in range(nm):
                pltpu.make_async_remote_copy(
                    src_ref=p_refs[mi].at[q], dst_ref=out_refs[mi].at[q],
                    send_sem=send_sems.at[4 * mi + q], recv_sem=recv_sems.at[4 * mi + q],
                    device_id=(px, py, cc), device_id_type=MESH).wait_recv()
        for cp in sends:
            cp.wait_send()
        for cp in keeps:
            cp.wait()

    return start, finish


def _all_reduce_small(v, name):
    shape = v.shape

    def body(v_ref, out_ref, buf, send_sems, recv_sems):
        x, y, cc = _place()
        me = 4 * x + 2 * y + cc
        buf[me] = v_ref[...]
        flips = [(a, b, d) for a in (0, 1) for b in (0, 1) for d in (0, 1)][1:]
        copies = []
        for k, (a, b, d) in enumerate(flips):
            peer = (x ^ a, y ^ b, cc ^ d)
            copies.append(pltpu.make_async_remote_copy(
                src_ref=v_ref, dst_ref=buf.at[me],
                send_sem=send_sems.at[k], recv_sem=recv_sems.at[k], device_id=peer, device_id_type=MESH))
        for cp in copies:
            cp.start()
        for k, (a, b, d) in enumerate(flips):
            peer_id = 4 * (x ^ a) + 2 * (y ^ b) + (cc ^ d)
            pltpu.make_async_remote_copy(
                src_ref=v_ref, dst_ref=buf.at[peer_id],
                send_sem=send_sems.at[k], recv_sem=recv_sems.at[k], device_id=(x, y, cc), device_id_type=MESH
            ).wait_recv()
        for cp in copies:
            cp.wait_send()
        total = buf[0]
        for k in range(1, N_DEV):
            total = total + buf[k]
        out_ref[...] = total

    vm = pl.BlockSpec(memory_space=pltpu.VMEM)
    return pl.pallas_call(
        body, name=name,
        out_shape=jax.ShapeDtypeStruct(shape, F32),
        in_specs=[vm], out_specs=vm,
        scratch_shapes=[pltpu.VMEM((N_DEV,) + shape, F32), pltpu.SemaphoreType.DMA((7,)), pltpu.SemaphoreType.DMA((7,))],
    )(v)


def _gather_kind(name, shape):
    if name not in COL_SHARDED:
        return "row"
    return "col" if shape[1] % 128 == 0 else "stack"


def _slab_of(name, shape):
    if name not in COL_SHARDED:
        return ("row", shape[0])
    return ("col", shape[1]) if shape[1] % 128 == 0 else None


def _to_slabs(g, shape):
    kk, nn = shape
    return g.reshape(kk, 4, 2, nn).transpose(2, 1, 0, 3)


SMALL_ROWS = 8


def _pack_small(arrs):
    rows = [arrs[n] for n in SMALL[:5]]
    last = jnp.concatenate([arrs["q_norm1"], arrs["kv_norm1"], arrs["b_f2"]])
    rows.append(jnp.pad(last, (0, PACK_COLS - last.shape[0])))
    rows += [jnp.zeros((PACK_COLS,), F32)] * (SMALL_ROWS - len(rows))
    return jnp.stack(rows)


def _unpack_small(p):
    out = {n: p[k] for k, n in enumerate(SMALL[:5])}
    out["q_norm1"] = p[5, :MLA_Q_RANK]
    out["kv_norm1"] = p[5, MLA_Q_RANK:MLA_Q_RANK + MLA_KV_RANK]
    out["b_f2"] = p[5, MLA_Q_RANK + MLA_KV_RANK:MLA_Q_RANK + MLA_KV_RANK + N_HEADS]
    return out


N_QKV = 3 * D_INNER
N_MAIN = 4 * D_INNER


def _rope(x, pos):
    r = x.shape[-1]
    inv_freq = ROPE_BASE ** (-jnp.arange(0, r, 2, dtype=F32) / r)
    ang = pos.astype(F32)[:, None, None] * inv_freq
    cos, sin = jnp.cos(ang), jnp.sin(ang)
    x1, x2 = x[..., : r // 2], x[..., r // 2:]
    return jnp.concatenate([x1 * cos - x2 * sin, x1 * sin + x2 * cos], axis=-1)


def _forward_loss(carriers, small, x, wfull, late, late_grads, slabs, pos, target):
    s = x.shape[0]
    wfull = dict(wfull)
    carriers = dict(carriers)

    def out_proj(y, w_out, tag, x_skip):
        return mm(y, wfull[w_out], carriers[w_out], slabs[w_out], name=tag + "_out", skip=x_skip)

    def sb_layer(x, ln, w_in, w_out, tag, gather=None, late=None):
        h, x = rmsnorm(x, small[ln], tag + "_ln", skip=True)
        qkv, gate = in_proj(h, wfull[w_in], (carriers[w_in],), True, tag)
        y, gathered, late_carriers = sb_core(qkv, gate, tag, gather, late)
        return out_proj(y, w_out, tag, x), gathered, late_carriers

    names, shards, kinds, shapes = late
    car_keys, car_shapes, to_slabs = late_grads
    x, gathered, late_carriers = sb_layer(x, "ln0", "w_in0", "w_out0", "l0", (shards, kinds),
                                          (carriers["slots"], car_shapes, to_slabs))
    carriers.update(zip(car_keys, late_carriers))
    for n, kind, full in zip(names, kinds, gathered):
        wfull[n] = full.transpose(1, 0, 2).reshape(shapes[n][0], -1) if kind == "stack" else full

    h, x = rmsnorm(x, small["ln1"], "l1_ln", skip=True)
    proj = mm(h, wfull["w_in1"], carriers["w_in1"], slabs["w_in1"], name="l1_in")
    i1, i2, i3 = MLA_Q_RANK, MLA_Q_RANK + MLA_KV_RANK, MLA_Q_RANK + MLA_KV_RANK + MLA_ROPE
    w_qb = jnp.pad(wfull["w_qb1"].reshape(MLA_Q_RANK, N_HEADS, MLA_NOPE + MLA_ROPE),
                   ((0, 0), (0, 0), (0, MLA_QK_PAD - MLA_NOPE - MLA_ROPE))).reshape(MLA_Q_RANK, N_HEADS * MLA_QK_PAD)
    qp = mm(rmsnorm(proj[:, :i1], small["q_norm1"], "l1_qn"), w_qb, carriers["w_qb1"], None, name="l1_qb")
    kv = mm(rmsnorm(proj[:, i1:i2], small["kv_norm1"], "l1_kvn"), wfull["w_kvb1"], carriers["w_kvb1"],
            slabs["w_kvb1"], BF16, name="l1_kvb")
    kr = jnp.pad(_rope(proj[:, i2:i3][:, None, :], pos)[:, 0, :], ((0, 0), (0, HEAD_DIM - MLA_ROPE)))
    inv_freq = ROPE_BASE ** (-jnp.arange(0, MLA_ROPE, 2, dtype=F32) / MLA_ROPE)
    ang = pos.astype(F32)[:, None] * inv_freq
    cos, sin = jnp.cos(ang), jnp.sin(ang)
    rest = HEAD_DIM - MLA_ROPE
    cosv = jnp.concatenate([cos, cos, jnp.ones((s, rest), F32)], axis=1)
    sinv = jnp.concatenate([-sin, sin, jnp.zeros((s, rest), F32)], axis=1)
    y = mla_core(qp, kv, kr, proj[:, i3:], cosv, sinv, "l1")
    x = out_proj(y, "w_out1", "l1", x)

    h, x = rmsnorm(x, small["ln2"], "l2_ln", skip=True)
    qkv, gate = in_proj(h, wfull["w_in2"], (carriers["w_in2_qkv"], carriers["w_in2_gate"]), False, "l2")
    f_logit = mm(h, wfull["w_in2"][:, N_MAIN:], carriers["w_in2_f"], None, name="l2_f") + small["b_f2"]
    c = jnp.cumsum(jax.nn.log_sigmoid(f_logit), axis=0)
    x = out_proj(fox_core(qkv, gate, c, "l2"), "w_out2", "l2", x)

    x, _, _ = sb_layer(x, "ln3", "w_in3", "w_out3", "l3")
    return sq_loss(rmsnorm(x, small["final_norm"], "final_ln"), target)


def kernel(x, positions, ln0, w_in0, w_out0, ln1, w_in1, q_norm1, w_qb1, kv_norm1, w_kvb1, w_out1, ln2, w_in2, b_f2, w_out2, ln3, w_in3, w_out3, final_norm, loss_target, m_ln0, m_w_in0, m_w_out0, m_ln1, m_w_in1, m_q_norm1, m_w_qb1, m_kv_norm1, m_w_kvb1, m_w_out1, m_ln2, m_w_in2, m_b_f2, m_w_out2, m_ln3, m_w_in3, m_w_out3, m_final_norm, v_ln0, v_w_in0, v_w_out0, v_ln1, v_w_in1, v_q_norm1, v_w_qb1, v_kv_norm1, v_w_kvb1, v_w_out1, v_ln2, v_w_in2, v_b_f2, v_w_out2, v_ln3, v_w_in3, v_w_out3, v_final_norm):
    args = dict(locals())
    w = {n: args[n] for n in ALL_W}
    m = {n: args["m_" + n] for n in ALL_W}
    v = {n: args["v_" + n] for n in ALL_W}
    shapes = {n: w[n].shape for n in BIG}
    kinds = [_gather_kind(n, shapes[n]) for n in BIG]
    slabs = {n: _slab_of(n, shapes[n]) for n in BIG}

    first = BIG[:2]
    late_names = BIG[2:]
    gathered = _all_gather([w[n].astype(BF16) for n in first], kinds[:2], "gather_w")
    wfull = dict(zip(first, gathered))
    late = (late_names, [w[n].astype(BF16) for n in late_names], kinds[2:], shapes)

    d_model = shapes["w_in0"][0]
    car_shapes = {}
    for n in late_names:
        if n == "w_in2":
            car_shapes["w_in2_qkv"] = (d_model, N_QKV)
            car_shapes["w_in2_gate"] = (d_model, D_INNER)
            car_shapes["w_in2_f"] = (d_model, N_DEV * shapes[n][1] - N_MAIN)
        elif n == "w_qb1":
            car_shapes[n] = (MLA_Q_RANK, N_HEADS * MLA_QK_PAD)
        elif slabs[n] is None:
            car_shapes[n] = (shapes[n][0], N_DEV * shapes[n][1])
        else:
            car_shapes[n] = (2, 4) + shapes[n]
    car_keys = list(car_shapes)

    def to_slabs(cts):
        g_car = dict(zip(car_keys, cts))
        out = []
        for n in late_names:
            if n == "w_in2":
                g = jnp.concatenate([g_car["w_in2_qkv"], g_car["w_in2_gate"], g_car["w_in2_f"]], axis=1)
                out.append(_to_slabs(g, shapes[n]))
            elif n == "w_qb1":
                g = g_car[n].reshape(MLA_Q_RANK, N_HEADS, MLA_QK_PAD)[:, :, :MLA_NOPE + MLA_ROPE]
                out.append(_to_slabs(g.reshape(MLA_Q_RANK, -1), shapes[n]))
            elif slabs[n] is None:
                out.append(_to_slabs(g_car[n], shapes[n]))
            else:
                out.append(g_car[n])
        return out

    late_grads = (car_keys, [car_shapes[k] for k in car_keys], to_slabs)
    carriers = {n: jnp.zeros((2, 4) + shapes[n], F32) for n in first}
    carriers["slots"] = [jnp.zeros((4,) + shapes[n], BF16) for n in late_names]
    small = {n: w[n] for n in SMALL}

    def local_loss(carriers, small, x_seq):
        return _forward_loss(carriers, small, x_seq, wfull, late, late_grads, slabs, positions[0], loss_target[0])

    loss_local, (g_car, g_small, g_x) = jax.value_and_grad(local_loss, argnums=(0, 1, 2))(carriers, small, x[0])
    loss = lax.psum(loss_local, ("x", "y", "c"))

    core_idx = lax.axis_index("c").astype(jnp.int32).reshape(1)
    from_sibling = _pair_exchange([g_car[n] for n in first], "pair_exchange")
    chip_part = [_pair_sum(core_idx, g_car[n], r, "pair_sum_" + n) for n, r in zip(first, from_sibling)]
    by_chip = dict(zip(first, _chip_exchange(chip_part, "chip_exchange")))
    by_chip.update(zip(late_names, g_car["slots"]))
    big = [{}, {}, {}, {}]
    for n in BIG:
        for k, t in enumerate(_adamw(w[n], by_chip[n], m[n], v[n], "adamw_" + n)):
            big[k][n] = t

    g_small_sum = _all_reduce_small(_pack_small(g_small), "reduce_small")
    sm = _adamw(_pack_small(w), g_small_sum[None], _pack_small(m), _pack_small(v), "adamw_small")
    small_out = [_unpack_small(t) for t in sm]

    outs = [loss, g_x[None]]
    for k in range(4):
        outs += [small_out[k][n] if n in small_out[k] else big[k][n] for n in ALL_W]
    return tuple(outs)
```
